```python
import jax, jax.numpy as jnp
from jax import lax
import numpy as np

D_MODEL = 2048
BATCH = 2
SEQ = 8192
DEPTH = 1

NSA_HEADS = 16
NSA_KV_GROUPS = 4
NSA_HPG = NSA_HEADS // NSA_KV_GROUPS
NSA_HEAD_DIM = 64
CMP_BLOCK = 32
CMP_STRIDE = 16
CMP_HIDDEN = 256
SLC_BLOCK = 64
N_SELECT = 16
WINDOW = 512
Q_BLOCK = 128
NSA_WIDTH = NSA_HEADS * NSA_HEAD_DIM
NSA_KV_WIDTH = NSA_KV_GROUPS * NSA_HEAD_DIM
GLA_HEADS = 4
GLA_KEY_DIM = 128
GLA_VAL_DIM = 256
GLA_GATE_RANK = 16
GLA_TAU = 16.0
GLA_CHUNK = 64
GLA_KEY_WIDTH = GLA_HEADS * GLA_KEY_DIM
GLA_WIDTH = GLA_HEADS * GLA_VAL_DIM
IN_WIDTHS = (NSA_WIDTH,
             NSA_KV_WIDTH, NSA_KV_WIDTH,
             NSA_KV_WIDTH, NSA_KV_WIDTH,
             NSA_KV_WIDTH, NSA_KV_WIDTH,
             3 * NSA_HEADS,
             NSA_WIDTH,
             GLA_KEY_WIDTH, GLA_KEY_WIDTH, GLA_WIDTH,
             GLA_GATE_RANK,
             GLA_WIDTH,
             D_MODEL, D_MODEL)
IN_WIDTH = sum(IN_WIDTHS)
EPS = 1e-6
NEG = -1e30

kernel_name = "hybrid_nsa_gla_gated_merge"


def rmsnorm(x, g):
    xf = x.astype(jnp.float32)
    y = xf * lax.rsqrt(jnp.mean(xf * xf, axis=-1, keepdims=True) + EPS)
    return (y * g.astype(jnp.float32)).astype(x.dtype)


def split_cols(a, widths):
    offs = np.cumsum(np.array(widths))[:-1].tolist()
    return jnp.split(a, offs, axis=-1)


def alibi_slopes(n):
    return 2.0 ** (-8.0 * jnp.arange(1, n + 1, dtype=jnp.float32) / n)


def masked_softmax(s, valid):
    p = jax.nn.softmax(jnp.where(valid, s, NEG), axis=-1)
    return jnp.where(jnp.any(valid, axis=-1, keepdims=True), p, 0.0)


def compress(kv, pos_emb, w1, w2):
    B, G, T, DH = kv.shape
    n_sub = CMP_BLOCK // CMP_STRIDE
    n_chunks = T // CMP_STRIDE
    nc = n_chunks - n_sub + 1
    chunks = kv.reshape(B, G, n_chunks, CMP_STRIDE, DH)
    blocks = jnp.concatenate([chunks[:, :, j:j + nc] for j in range(n_sub)], axis=3)
    blocks = blocks + pos_emb
    flat = blocks.reshape(B, G, nc, CMP_BLOCK * DH)
    return jax.nn.silu(flat @ w1) @ w2


def window_bands(a):
    B, G, T, DH = a.shape
    nq = T // Q_BLOCK
    nw = WINDOW // Q_BLOCK
    ap = jnp.pad(a, ((0, 0), (0, 0), (WINDOW, 0), (0, 0))).reshape(B, G, nq + nw, Q_BLOCK, DH)
    band = jnp.stack([ap[:, :, j:j + nq] for j in range(nw + 1)], axis=3)
    return jnp.moveaxis(band.reshape(B, G, nq, (nw + 1) * Q_BLOCK, DH), 2, 0)


def to_query_blocks(a):
    B, G, HPG, T, X = a.shape
    return jnp.moveaxis(a.reshape(B, G, HPG, T // Q_BLOCK, Q_BLOCK, X), 3, 0)


def nsa_attention(q, k_cmp, v_cmp, k_slc, v_slc, k_win, v_win, gates,
                  pos_k, pos_v, w1_k, w2_k, w1_v, w2_v):
    B, G, HPG, T, DH = q.shape
    nq = T // Q_BLOCK
    scale = DH ** -0.5
    slopes = alibi_slopes(NSA_HEADS).reshape(G, HPG)[None, :, :, None, None]
    Kc = compress(k_cmp, pos_k, w1_k, w2_k)
    Vc = compress(v_cmp, pos_v, w1_v, w2_v)
    nc = Kc.shape[2]
    cmp_end = jnp.arange(nc) * CMP_STRIDE + CMP_BLOCK - 1
    nb = T // SLC_BLOCK
    n_sel = min(N_SELECT, nb)
    Ks = k_slc.reshape(B, G, nb, SLC_BLOCK, DH)
    Vs = v_slc.reshape(B, G, nb, SLC_BLOCK, DH)
    ratio = SLC_BLOCK // CMP_STRIDE
    n_sub = CMP_BLOCK // CMP_STRIDE
    gather = jax.vmap(jax.vmap(lambda blk, ix: blk[ix]))
    blk_ids = jnp.arange(nb)

    def block_fn(args):
        i, qi, gi, kwi, vwi = args
        t = i * Q_BLOCK + jnp.arange(Q_BLOCK)
        tf = t.astype(jnp.float32)
        dist = tf[:, None] - cmp_end[None, :].astype(jnp.float32)
        s = jnp.einsum('bghqd,bgnd->bghqn', qi, Kc).astype(jnp.float32) * scale - slopes * dist
        p_cmp = masked_softmax(s, cmp_end[None, :] <= t[:, None])
        o_cmp = jnp.einsum('bghqn,bgnd->bghqd', p_cmp.astype(Vc.dtype), Vc)
        P = jnp.sum(p_cmp, axis=2)
        Ppad = jnp.pad(P, ((0, 0), (0, 0), (0, 0), (n_sub - 1, ratio * nb - nc)))
        imp = jnp.zeros(P.shape[:3] + (nb,), jnp.float32)
        for m in range(ratio):
            for n in range(n_sub):
                st = m - n + n_sub - 1
                imp = imp + lax.slice_in_dim(Ppad, st, st + ratio * (nb - 1) + 1, stride=ratio, axis=3)
        cur = t // SLC_BLOCK
        forced = (blk_ids[None, :] == 0) | (blk_ids[None, :] == cur[:, None]) | (blk_ids[None, :] == cur[:, None] - 1)
        blk_valid = blk_ids[None, :] * SLC_BLOCK <= t[:, None]
        score = jnp.where(forced, jnp.inf, jnp.where(blk_valid, imp, -jnp.inf))
        _, idx = lax.top_k(score, n_sel)
        ksel = gather(Ks, idx).reshape(B, G, Q_BLOCK, n_sel * SLC_BLOCK, DH)
        vsel = gather(Vs, idx).reshape(B, G, Q_BLOCK, n_sel * SLC_BLOCK, DH)
        spos = (idx[..., None] * SLC_BLOCK + jnp.arange(SLC_BLOCK)).reshape(B, G, Q_BLOCK, n_sel * SLC_BLOCK)
        dist = (tf[:, None] - spos.astype(jnp.float32))[:, :, None]
        s = jnp.einsum('bghqd,bgqkd->bghqk', qi, ksel).astype(jnp.float32) * scale - slopes * dist
        p = masked_softmax(s, (spos <= t[:, None])[:, :, None])
        o_slc = jnp.einsum('bghqk,bgqkd->bghqd', p.astype(vsel.dtype), vsel)
        wpos = i * Q_BLOCK - WINDOW + jnp.arange(Q_BLOCK + WINDOW)
        diff = t[:, None] - wpos[None, :]
        valid = (diff >= 0) & (diff < WINDOW) & (wpos[None, :] >= 0)
        s = jnp.einsum('bghqd,bgkd->bghqk', qi, kwi).astype(jnp.float32) * scale - slopes * diff.astype(jnp.float32)
        p = masked_softmax(s, valid)
        o_win = jnp.einsum('bghqk,bgkd->bghqd', p.astype(vwi.dtype), vwi)
        return gi[..., 0:1] * o_cmp + gi[..., 1:2] * o_slc + gi[..., 2:3] * o_win

    out = lax.map(block_fn, (jnp.arange(nq), to_query_blocks(q), to_query_blocks(gates),
                             window_bands(k_win), window_bands(v_win)))
    return out.transpose(1, 0, 4, 2, 3, 5).reshape(B, T, G * HPG * DH)


def gla_attention(q, k, v, log_a):
    B, T, H, DK = q.shape
    DV = v.shape[-1]
    n = T // GLA_CHUNK
    causal = jnp.tril(jnp.ones((GLA_CHUNK, GLA_CHUNK), dtype=bool))[..., None]

    def chunks(a):
        return a.reshape(B, n, GLA_CHUNK, H, a.shape[-1]).transpose(1, 0, 3, 2, 4)

    def step(S, inp):
        qc, kc, vc, gc = inp
        b = jnp.cumsum(gc, axis=2)
        rel = jnp.where(causal, b[:, :, :, None, :] - b[:, :, None, :, :], -jnp.inf)
        A = jnp.einsum('bhik,bhjk,bhijk->bhij', qc, kc, jnp.exp(rel))
        o = jnp.einsum('bhij,bhjv->bhiv', A, vc) + jnp.einsum('bhik,bhkv->bhiv', qc * jnp.exp(b), S)
        b_last = b[:, :, -1:, :]
        S = jnp.exp(b_last)[:, :, 0, :, None] * S + jnp.einsum('bhjk,bhjv->bhkv', kc * jnp.exp(b_last - b), vc)
        return S, o

    S0 = jnp.zeros((B, H, DK, DV), jnp.float32)
    _, o = lax.scan(step, S0, (chunks(q * (DK ** -0.5)), chunks(k), chunks(v), chunks(log_a)))
    return o.transpose(1, 0, 3, 2, 4).reshape(B, T, H, DV)


def setup_inputs(seed: int = 0) -> dict:
    key = jax.random.key(seed)
    ks = jax.random.split(key, 22)
    L = DEPTH

    def nrm(k, shape, s):
        return jax.random.normal(k, shape, jnp.float32) * s

    return {
        "x": nrm(ks[0], (BATCH, SEQ, D_MODEL), 1.0),
        "c": nrm(ks[1], (BATCH, D_MODEL), 1.0),
        "w_ada": nrm(ks[2], (L, D_MODEL, 3 * D_MODEL), 0.3 * D_MODEL ** -0.5),
        "b_ada": nrm(ks[3], (L, 3 * D_MODEL), 0.01),
        "norm_gain": 1.0 + nrm(ks[4], (L, D_MODEL), 0.02),
        "w_in": nrm(ks[5], (L, D_MODEL, IN_WIDTH), D_MODEL ** -0.5),
        "b_in": nrm(ks[6], (L, IN_WIDTH), 0.01),
        "cmp_pos_k": nrm(ks[7], (L, CMP_BLOCK, NSA_HEAD_DIM), 0.02),
        "cmp_pos_v": nrm(ks[8], (L, CMP_BLOCK, NSA_HEAD_DIM), 0.02),
        "cmp_w1_k": nrm(ks[9], (L, CMP_BLOCK * NSA_HEAD_DIM, CMP_HIDDEN), (CMP_BLOCK * NSA_HEAD_DIM) ** -0.5),
        "cmp_w2_k": nrm(ks[10], (L, CMP_HIDDEN, NSA_HEAD_DIM), CMP_HIDDEN ** -0.5),
        "cmp_w1_v": nrm(ks[11], (L, CMP_BLOCK * NSA_HEAD_DIM, CMP_HIDDEN), (CMP_BLOCK * NSA_HEAD_DIM) ** -0.5),
        "cmp_w2_v": nrm(ks[12], (L, CMP_HIDDEN, NSA_HEAD_DIM), CMP_HIDDEN ** -0.5),
        "gla_w_alpha": nrm(ks[13], (L, GLA_GATE_RANK, GLA_KEY_WIDTH), GLA_GATE_RANK ** -0.5),
        "gla_b_alpha": nrm(ks[14], (L, GLA_KEY_WIDTH), 0.01),
        "gla_norm_gain": 1.0 + nrm(ks[15], (L, GLA_VAL_DIM), 0.02),
        "w_br_nsa": nrm(ks[16], (L, NSA_WIDTH, D_MODEL), NSA_WIDTH ** -0.5),
        "w_br_gla": nrm(ks[17], (L, GLA_WIDTH, D_MODEL), GLA_WIDTH ** -0.5),
        "w_out": nrm(ks[18], (L, D_MODEL, D_MODEL), D_MODEL ** -0.5),
        "final_norm_gain": 1.0 + nrm(ks[19], (D_MODEL,), 0.02),
    }


def reference(x, c, w_ada, b_ada, norm_gain, w_in, b_in, cmp_pos_k, cmp_pos_v,
              cmp_w1_k, cmp_w2_k, cmp_w1_v, cmp_w2_v, gla_w_alpha, gla_b_alpha,
              gla_norm_gain, w_br_nsa, w_br_gla, w_out, final_norm_gain):
    B, T, D = x.shape
    G, HPG, DH = NSA_KV_GROUPS, NSA_HPG, NSA_HEAD_DIM

    def kv_heads(a):
        return a.reshape(B, T, G, DH).transpose(0, 2, 1, 3)

    for l in range(DEPTH):
        shift, scale, gate = jnp.split(c @ w_ada[l] + b_ada[l], 3, axis=-1)
        h = rmsnorm(x, norm_gain[l]) * (1.0 + scale[:, None, :]) + shift[:, None, :]
        (nsa_q, ck, cv, sk, sv, wk, wv, nsa_g, nsa_z,
         gq, gk, gv, ga, gla_z, mg_nsa, mg_gla) = split_cols(h @ w_in[l] + b_in[l], IN_WIDTHS)
        q = nsa_q.reshape(B, T, G, HPG, DH).transpose(0, 2, 3, 1, 4)
        bgates = jax.nn.sigmoid(nsa_g).reshape(B, T, G, HPG, 3).transpose(0, 2, 3, 1, 4)
        o_nsa = nsa_attention(q, kv_heads(ck), kv_heads(cv), kv_heads(sk), kv_heads(sv),
                              kv_heads(wk), kv_heads(wv), bgates,
                              cmp_pos_k[l], cmp_pos_v[l], cmp_w1_k[l], cmp_w2_k[l],
                              cmp_w1_v[l], cmp_w2_v[l])
        y_nsa = o_nsa * jax.nn.silu(nsa_z)
        log_a = jax.nn.log_sigmoid((ga @ gla_w_alpha[l] + gla_b_alpha[l]).astype(jnp.float32)) / GLA_TAU
        o_gla = gla_attention(gq.reshape(B, T, GLA_HEADS, GLA_KEY_DIM),
                              gk.reshape(B, T, GLA_HEADS, GLA_KEY_DIM),
                              gv.reshape(B, T, GLA_HEADS, GLA_VAL_DIM),
                              log_a.reshape(B, T, GLA_HEADS, GLA_KEY_DIM)).astype(x.dtype)
        y_gla = rmsnorm(o_gla, gla_norm_gain[l]).reshape(B, T, GLA_WIDTH) * jax.nn.silu(gla_z)
        merged = jax.nn.sigmoid(mg_nsa) * (y_nsa @ w_br_nsa[l]) + jax.nn.sigmoid(mg_gla) * (y_gla @ w_br_gla[l])
        x = x + gate[:, None, :] * (merged @ w_out[l])
    return rmsnorm(x, final_norm_gain)
```

```python
import functools

import numpy as np
import jax
import jax.numpy as jnp
from jax import lax
from jax.experimental import pallas as pl
from jax.experimental.pallas import tpu as pltpu

D_MODEL = 2048
DEPTH = 1
NSA_HEADS = 16
NSA_KV_GROUPS = 4
NSA_HPG = NSA_HEADS // NSA_KV_GROUPS
NSA_HEAD_DIM = 64
CMP_BLOCK = 32
CMP_STRIDE = 16
CMP_HIDDEN = 256
SLC_BLOCK = 64
N_SELECT = 16
WINDOW = 512
NSA_WIDTH = NSA_HEADS * NSA_HEAD_DIM
NSA_KV_WIDTH = NSA_KV_GROUPS * NSA_HEAD_DIM
GLA_HEADS = 4
GLA_KEY_DIM = 128
GLA_VAL_DIM = 256
GLA_GATE_RANK = 16
GLA_TAU = 16.0
GLA_KEY_WIDTH = GLA_HEADS * GLA_KEY_DIM
GLA_WIDTH = GLA_HEADS * GLA_VAL_DIM
EPS = 1e-6
NEG = -1e30

F32 = jnp.float32
BF16 = jnp.bfloat16

_OLD = {}
_off = 0
for _name, _w in (("nsa_q", NSA_WIDTH), ("ck", NSA_KV_WIDTH), ("cv", NSA_KV_WIDTH), ("sk", NSA_KV_WIDTH),
                  ("sv", NSA_KV_WIDTH), ("wk", NSA_KV_WIDTH), ("wv", NSA_KV_WIDTH), ("nsa_g", 3 * NSA_HEADS),
                  ("nsa_z", NSA_WIDTH), ("gq", GLA_KEY_WIDTH), ("gk", GLA_KEY_WIDTH), ("gv", GLA_WIDTH),
                  ("ga", GLA_GATE_RANK), ("gla_z", GLA_WIDTH), ("mg_nsa", D_MODEL), ("mg_gla", D_MODEL)):
    _OLD[_name] = (_off, _w)
    _off += _w

_MAIN_ORDER = ("nsa_q", "nsa_z", "gla_z", "gv", "mg_nsa", "mg_gla", "gq", "gk", "ck", "cv", "sk", "sv", "wk", "wv")
_NEW = {}
_off = 0
for _name in _MAIN_ORDER:
    _NEW[_name] = _off
    _off += _OLD[_name][1]
MAIN_WIDTH = _off
SMALL_WIDTH = 128
GA_LANE = 3 * NSA_HEADS

LANE = 128
GLA_CHUNK = 64


def _dot(a, b):
    return jnp.dot(a, b, preferred_element_type=F32)


def _dot_nt(a, b):
    return lax.dot_general(a, b, (((1,), (1,)), ((), ())), preferred_element_type=F32)


def _split3(x):
    hi = x.astype(BF16)
    r = x - hi.astype(F32)
    mid = r.astype(BF16)
    lo = (r - mid.astype(F32)).astype(BF16)
    return hi, mid, lo


def _sigmoid(x):
    return 1.0 / (1.0 + jnp.exp(-x))


def _ada_kernel(c_ref, w_ref, b_ref, o_ref):
    ch, cm, cl = _split3(c_ref[...])
    wh, wm, wl = _split3(w_ref[...])
    acc = _dot(ch, wh) + _dot(ch, wm) + _dot(cm, wh) + _dot(ch, wl) + _dot(cl, wh) + _dot(cm, wm)
    o_ref[...] = acc + b_ref[...]


def _ada(c8, w, b, *, tn=768):
    m, d = c8.shape
    n = w.shape[1]
    return pl.pallas_call(
        _ada_kernel,
        grid=(n // tn,),
        in_specs=[pl.BlockSpec((m, d), lambda j: (0, 0)),
                  pl.BlockSpec((d, tn), lambda j: (0, j)),
                  pl.BlockSpec((1, tn), lambda j: (0, j))],
        out_specs=pl.BlockSpec((m, tn), lambda j: (0, j)),
        out_shape=jax.ShapeDtypeStruct((m, n), F32),
        compiler_params=pltpu.CompilerParams(dimension_semantics=("arbitrary",),
                                             vmem_limit_bytes=40 * 1024 * 1024),
        name="ada",
    )(c8, w, b)


def _in_proj_kernel(x_ref, gain_ref, shift_ref, scale_ref, w_ref, b_ref, ws_ref, bs_ref,
                    o_ref, os_ref, h_ref):
    j = pl.program_id(1)

    @pl.when(j == 0)
    def _():
        x = x_ref[...]
        ms = jnp.mean(x * x, axis=-1, keepdims=True)
        y = x * lax.rsqrt(ms + EPS) * gain_ref[...]
        y = y * (1.0 + scale_ref[0]) + shift_ref[0]
        h = y.astype(BF16)
        h_ref[...] = h
        os_ref[...] = _dot(h, ws_ref[...]) + bs_ref[...]

    o_ref[...] = (_dot(h_ref[...], w_ref[...]) + b_ref[...]).astype(o_ref.dtype)


def _in_proj(x2, gain, shift, scale, w_main, b_main, w_small, b_small, *, T, tm=1024, tn=1536):
    n, d = x2.shape
    nm = w_main.shape[1]
    nt = T // tm
    return pl.pallas_call(
        _in_proj_kernel,
        grid=(n // tm, nm // tn),
        in_specs=[pl.BlockSpec((tm, d), lambda i, j: (i, 0)),
                  pl.BlockSpec((1, d), lambda i, j: (0, 0)),
                  pl.BlockSpec((1, 1, d), lambda i, j: (i // nt, 0, 0)),
                  pl.BlockSpec((1, 1, d), lambda i, j: (i // nt, 0, 0)),
                  pl.BlockSpec((d, tn), lambda i, j: (0, j)),
                  pl.BlockSpec((1, tn), lambda i, j: (0, j)),
                  pl.BlockSpec((d, SMALL_WIDTH), lambda i, j: (0, 0)),
                  pl.BlockSpec((1, SMALL_WIDTH), lambda i, j: (0, 0))],
        out_specs=[pl.BlockSpec((tm, tn), lambda i, j: (i, j)),
                   pl.BlockSpec((tm, SMALL_WIDTH), lambda i, j: (i, 0))],
        out_shape=[jax.ShapeDtypeStruct((n, nm), BF16),
                   jax.ShapeDtypeStruct((n, SMALL_WIDTH), F32)],
        scratch_shapes=[pltpu.VMEM((tm, d), BF16)],
        compiler_params=pltpu.CompilerParams(dimension_semantics=("parallel", "arbitrary"),
                                             vmem_limit_bytes=56 * 1024 * 1024),
        name="in_proj",
    )(x2, gain, shift, scale, w_main, b_main, w_small, b_small)


def _compress_kernel(kv_ref, pos_ref, w1_ref, w2_ref, o_ref):
    y = kv_ref[0, 0, 0]
    w1 = w1_ref[0]
    half = y.shape[1]
    nch = y.shape[0]
    z1 = _dot(y, w1[:half])
    z2 = _dot(y, w1[half:])
    posb = _dot(pos_ref[0], w1)[0:1]
    pre = z1 + pltpu.roll(z2, nch - 1, 0) + posb
    hid = pre * _sigmoid(pre)
    o_ref[0, 0, 0] = _dot(hid.astype(BF16), w2_ref[0]).astype(o_ref.dtype)


def _compress(kvf, posf, w1, w2):
    two, b, g, nch, half = kvf.shape
    hidden = w1.shape[2]
    dh = w2.shape[2]
    return pl.pallas_call(
        _compress_kernel,
        grid=(two, b, g),
        in_specs=[pl.BlockSpec((1, 1, 1, nch, half), lambda s, i, j: (s, i, j, 0, 0)),
                  pl.BlockSpec((1, 8, 2 * half), lambda s, i, j: (s, 0, 0)),
                  pl.BlockSpec((1, 2 * half, hidden), lambda s, i, j: (s, 0, 0)),
                  pl.BlockSpec((1, hidden, dh), lambda s, i, j: (s, 0, 0))],
        out_specs=pl.BlockSpec((1, 1, 1, nch, dh), lambda s, i, j: (s, i, j, 0, 0)),
        out_shape=jax.ShapeDtypeStruct((two, b, g, nch, dh), BF16),
        compiler_params=pltpu.CompilerParams(dimension_semantics=("parallel", "parallel", "parallel")),
        name="compress",
    )(kvf, posf, w1, w2)


def _cmp_attn_kernel(slopes_ref, q_ref, kc_ref, vc_ref, wimp_ref, o_ref, sel_ref, *, tq, nc, n_sel):
    g = pl.program_id(1)
    qi = pl.program_id(2)
    kc = kc_ref[0, 0]
    vc = vc_ref[0, 0]
    ncp = kc.shape[0]
    scale = NSA_HEAD_DIM ** -0.5
    t_i = qi * tq + lax.broadcasted_iota(jnp.int32, (tq, 1), 0)
    n_i = lax.broadcasted_iota(jnp.int32, (1, ncp), 1)
    ce_i = n_i * CMP_STRIDE + (CMP_BLOCK - 1)
    valid = (ce_i <= t_i) & (n_i < nc)
    dist = (t_i - ce_i).astype(F32)
    psum = jnp.zeros((tq, ncp), F32)
    for hh in range(NSA_HPG):
        slope = slopes_ref[g * NSA_HPG + hh]
        s = _dot_nt(q_ref[0, 0, hh], kc) * scale - slope * dist
        sm = jnp.where(valid, s, NEG)
        m = jnp.max(sm, axis=-1, keepdims=True)
        p = jnp.where(valid, jnp.exp(sm - m), 0.0)
        l = jnp.sum(p, axis=-1, keepdims=True)
        p = p / jnp.where(l > 0.0, l, 1.0)
        o_ref[0, 0, hh] = _dot(p.astype(BF16), vc).astype(o_ref.dtype)
        psum = psum + p
    wimp = wimp_ref[...]
    ph, pm, plo = _split3(psum)
    imp = _dot_nt(wimp, ph) + _dot_nt(wimp, pm) + _dot_nt(wimp, plo)
    nbp = imp.shape[0]
    blk = lax.broadcasted_iota(jnp.int32, (nbp, 1), 0)
    t_l = qi * tq + lax.broadcasted_iota(jnp.int32, (1, tq), 1)
    cur = lax.shift_right_logical(t_l, 6)
    forced = (blk == 0) | (blk == cur) | (blk == cur - 1)
    bvalid = blk * SLC_BLOCK <= t_l
    score = jnp.where(forced, 1e30, jnp.where(bvalid, imp, -1.0))
    blk_f = blk.astype(F32)
    sel = jnp.zeros((nbp, tq), F32)
    for _ in range(n_sel):
        m = jnp.max(score, axis=0, keepdims=True)
        first = jnp.min(jnp.where(score == m, blk_f, float(nbp)), axis=0, keepdims=True)
        hit = blk_f == first
        sel = jnp.where(hit, 1.0, sel)
        score = jnp.where(hit, -1.0, score)
    sel_ref[0, 0] = sel.T.astype(sel_ref.dtype)


def _cmp_attn(slopes, q5, kc, vc, wimp, *, tq, nc, n_sel):
    b, g, hpg, t, dh = q5.shape
    ncp = kc.shape[2]
    nbp = wimp.shape[0]
    kern = functools.partial(_cmp_attn_kernel, tq=tq, nc=nc, n_sel=n_sel)
    return pl.pallas_call(
        kern,
        grid=(b, g, t // tq),
        in_specs=[pl.BlockSpec(memory_space=pltpu.SMEM),
                  pl.BlockSpec((1, 1, hpg, tq, dh), lambda i, j, k: (i, j, 0, k, 0)),
                  pl.BlockSpec((1, 1, ncp, dh), lambda i, j, k: (i, j, 0, 0)),
                  pl.BlockSpec((1, 1, ncp, dh), lambda i, j, k: (i, j, 0, 0)),
                  pl.BlockSpec((nbp, ncp), lambda i, j, k: (0, 0))],
        out_specs=[pl.BlockSpec((1, 1, hpg, tq, dh), lambda i, j, k: (i, j, 0, k, 0)),
                   pl.BlockSpec((1, 1, tq, nbp), lambda i, j, k: (i, j, k, 0))],
        out_shape=[jax.ShapeDtypeStruct((b, g, hpg, t, dh), BF16),
                   jax.ShapeDtypeStruct((b, g, t, nbp), BF16)],
        compiler_params=pltpu.CompilerParams(dimension_semantics=("parallel", "parallel", "parallel"),
                                             vmem_limit_bytes=48 * 1024 * 1024),
        name="cmp_attn",
    )(slopes, q5, kc, vc, wimp)


def _flash_kernel(*refs, tq, tk, nkv, selected):
    if selected:
        slopes_ref, q_ref, k_ref, v_ref, sel_ref, et_ref, o_ref, m_sc, l_sc, acc_sc = refs
    else:
        slopes_ref, q_ref, k_ref, v_ref, o_ref, m_sc, l_sc, acc_sc = refs
    g = pl.program_id(1)
    qi = pl.program_id(2)
    kj = pl.program_id(3)
    scale = NSA_HEAD_DIM ** -0.5

    @pl.when(kj == 0)
    def _():
        m_sc[...] = jnp.full(m_sc.shape, NEG, F32)
        l_sc[...] = jnp.zeros(l_sc.shape, F32)
        acc_sc[...] = jnp.zeros(acc_sc.shape, F32)

    if selected:
        kv_idx = kj
        active = kj * tk <= qi * tq + (tq - 1)
    else:
        kv_idx = qi - (nkv - 1) + kj
        active = kv_idx >= 0

    @pl.when(active)
    def _():
        t_i = qi * tq + lax.broadcasted_iota(jnp.int32, (tq, 1), 0)
        j_i = kv_idx * tk + lax.broadcasted_iota(jnp.int32, (1, tk), 1)
        diff = t_i - j_i
        if selected:
            picked = _dot(sel_ref[0, 0], et_ref[...])
            valid = (diff >= 0) & (picked > 0.5)
        else:
            valid = (diff >= 0) & (diff < WINDOW)
        dist = diff.astype(F32)
        k = k_ref[0, 0]
        v = v_ref[0, 0]
        for hh in range(NSA_HPG):
            slope = slopes_ref[g * NSA_HPG + hh]
            s = _dot_nt(q_ref[0, 0, hh], k) * scale - slope * dist
            sm = jnp.where(valid, s, NEG)
            m_prev = m_sc[hh]
            m_new = jnp.maximum(m_prev, jnp.max(sm, axis=-1, keepdims=True))
            alpha = jnp.exp(m_prev - m_new)
            p = jnp.where(valid, jnp.exp(sm - m_new), 0.0)
            l_sc[hh] = alpha * l_sc[hh] + jnp.sum(p, axis=-1, keepdims=True)
            acc_sc[hh] = alpha * acc_sc[hh] + _dot(p.astype(BF16), v)
            m_sc[hh] = m_new

    @pl.when(kj == nkv - 1)
    def _():
        for hh in range(NSA_HPG):
            o_ref[0, 0, hh] = (acc_sc[hh] / l_sc[hh]).astype(o_ref.dtype)


def _flash(slopes, q5, k4, v4, sel=None, et=None, *, tq, tk):
    b, g, hpg, t, dh = q5.shape
    selected = sel is not None
    if selected:
        nkv = t // tk

        def kv_block(qi, kj):
            return jnp.minimum(kj, (qi * tq + (tq - 1)) // tk)
    else:
        assert tk == tq and WINDOW % tk == 0
        nkv = WINDOW // tk + 1

        def kv_block(qi, kj):
            return jnp.maximum(qi - (nkv - 1) + kj, 0)

    in_specs = [pl.BlockSpec(memory_space=pltpu.SMEM),
                pl.BlockSpec((1, 1, hpg, tq, dh), lambda i, j, qi, kj: (i, j, 0, qi, 0)),
                pl.BlockSpec((1, 1, tk, dh), lambda i, j, qi, kj: (i, j, kv_block(qi, kj), 0)),
                pl.BlockSpec((1, 1, tk, dh), lambda i, j, qi, kj: (i, j, kv_block(qi, kj), 0))]
    args = [slopes, q5, k4, v4]
    if selected:
        nbp = sel.shape[3]
        in_specs += [pl.BlockSpec((1, 1, tq, nbp), lambda i, j, qi, kj: (i, j, qi, 0)),
                     pl.BlockSpec((nbp, tk), lambda i, j, qi, kj: (0, kv_block(qi, kj)))]
        args += [sel, et]
    kern = functools.partial(_flash_kernel, tq=tq, tk=tk, nkv=nkv, selected=selected)
    return pl.pallas_call(
        kern,
        grid=(b, g, t // tq, nkv),
        in_specs=in_specs,
        out_specs=pl.BlockSpec((1, 1, hpg, tq, dh), lambda i, j, qi, kj: (i, j, 0, qi, 0)),
        out_shape=jax.ShapeDtypeStruct((b, g, hpg, t, dh), BF16),
        scratch_shapes=[pltpu.VMEM((hpg, tq, 1), F32), pltpu.VMEM((hpg, tq, 1), F32),
                        pltpu.VMEM((hpg, tq, dh), F32)],
        compiler_params=pltpu.CompilerParams(
            dimension_semantics=("parallel", "parallel", "parallel", "arbitrary"),
            vmem_limit_bytes=48 * 1024 * 1024),
        name="flash_sel" if selected else "flash_win",
    )(*args)


def _gla_kernel(q_ref, k_ref, v_ref, z_ref, ps_ref, wal_ref, bal_ref, gain_ref, tri_ref, o_ref,
                st_sc, b_sc, k_sc, q_sc, *, ts):
    ti = pl.program_id(2)
    c = GLA_CHUNK

    @pl.when(ti == 0)
    def _():
        st_sc[...] = jnp.zeros(st_sc.shape, F32)

    ps = ps_ref[...]
    ph, pm, _ = _split3(ps)
    wh, wm, _ = _split3(wal_ref[...])
    zz = _dot(ph, wh) + _dot(ph, wm) + _dot(pm, wh) + bal_ref[...]
    log_a = (jnp.minimum(zz, 0.0) - jnp.log(1.0 + jnp.exp(-jnp.abs(zz)))) / GLA_TAU
    row = lax.broadcasted_iota(jnp.int32, (c, GLA_KEY_DIM), 0)
    lane = lax.broadcasted_iota(jnp.int32, (c, GLA_KEY_DIM), 1)
    tri = tri_ref[...]
    qscale = GLA_KEY_DIM ** -0.5
    for ci in range(ts // c):
        sl = slice(ci * c, (ci + 1) * c)
        gh, gm, gl = _split3(log_a[sl])
        bcum = _dot(tri, gh) + _dot(tri, gm) + _dot(tri, gl)
        q = q_ref[sl, :].astype(F32) * qscale
        k = k_ref[sl, :].astype(F32)
        v = v_ref[sl, :]
        b_sc[...] = bcum
        k_sc[...] = k
        q_sc[...] = q

        def body(j, a_acc):
            bj = b_sc[pl.ds(j, 1), :]
            kj = k_sc[pl.ds(j, 1), :]
            w = jnp.exp(jnp.where(row >= j, b_sc[...] - bj, NEG))
            a = jnp.sum(q_sc[...] * kj * w, axis=-1, keepdims=True)
            return jnp.where(lane == j, a, a_acc)

        a_mat = lax.fori_loop(0, c, body, jnp.zeros((c, GLA_KEY_DIM), F32))
        st = st_sc[...]
        o = _dot(a_mat[:, :c].astype(BF16), v) + _dot_nt((q * jnp.exp(bcum)).astype(BF16), st.astype(BF16))
        b_last = bcum[c - 1:c, :]
        kd = k * jnp.exp(b_last - bcum)
        st_sc[...] = st * jnp.exp(b_last) + _dot(v.astype(F32).T.astype(BF16), kd.astype(BF16))
        ms = jnp.mean(o * o, axis=-1, keepdims=True)
        y = o * lax.rsqrt(ms + EPS) * gain_ref[...]
        z = z_ref[sl, :].astype(F32)
        o_ref[sl, :] = (y * (z * _sigmoid(z))).astype(o_ref.dtype)


def _gla(pm, ps, wal, bal, gain, tri, *, B, T, ts=256):
    n = pm.shape[0]
    nts = T // ts
    dk, dv = GLA_KEY_DIM, GLA_VAL_DIM
    qb, kb = _NEW["gq"] // dk, _NEW["gk"] // dk
    vb, zb = _NEW["gv"] // dv, _NEW["gla_z"] // dv
    kern = functools.partial(_gla_kernel, ts=ts)
    return pl.pallas_call(
        kern,
        grid=(B, GLA_HEADS, nts),
        in_specs=[pl.BlockSpec((ts, dk), lambda b, h, i: (b * nts + i, qb + h)),
                  pl.BlockSpec((ts, dk), lambda b, h, i: (b * nts + i, kb + h)),
                  pl.BlockSpec((ts, dv), lambda b, h, i: (b * nts + i, vb + h)),
                  pl.BlockSpec((ts, dv), lambda b, h, i: (b * nts + i, zb + h)),
                  pl.BlockSpec((ts, SMALL_WIDTH), lambda b, h, i: (b * nts + i, 0)),
                  pl.BlockSpec((SMALL_WIDTH, dk), lambda b, h, i: (0, h)),
                  pl.BlockSpec((1, dk), lambda b, h, i: (0, h)),
                  pl.BlockSpec((1, dv), lambda b, h, i: (0, 0)),
                  pl.BlockSpec((GLA_CHUNK, GLA_CHUNK), lambda b, h, i: (0, 0))],
        out_specs=pl.BlockSpec((ts, dv), lambda b, h, i: (b * nts + i, h)),
        out_shape=jax.ShapeDtypeStruct((n, GLA_WIDTH), BF16),
        scratch_shapes=[pltpu.VMEM((dv, dk), F32), pltpu.VMEM((GLA_CHUNK, dk), F32),
                        pltpu.VMEM((GLA_CHUNK, dk), F32), pltpu.VMEM((GLA_CHUNK, dk), F32)],
        compiler_params=pltpu.CompilerParams(dimension_semantics=("parallel", "parallel", "arbitrary")),
        name="gla",
    )(pm, pm, pm, pm, ps, wal, bal, gain, tri)


def _out_proj_kernel(x_ref, gate_ref, oc_ref, os_ref, ow_ref, ps_ref, nz_ref, mgn_ref, mgg_ref, yg_ref,
                     ex_ref, wn_ref, wg_ref, wo_ref, fg_ref, o_ref):
    sg = _sigmoid(ps_ref[...]).astype(BF16)
    nz = nz_ref[...].astype(F32)
    o_nsa = (_dot(sg, ex_ref[0]) * oc_ref[...].astype(F32)
             + _dot(sg, ex_ref[1]) * os_ref[...].astype(F32)
             + _dot(sg, ex_ref[2]) * ow_ref[...].astype(F32))
    y_nsa = o_nsa * (nz * _sigmoid(nz))
    a = _dot(y_nsa.astype(BF16), wn_ref[...])
    bm = _dot(yg_ref[...], wg_ref[...])
    merged = _sigmoid(mgn_ref[...].astype(F32)) * a + _sigmoid(mgg_ref[...].astype(F32)) * bm
    xn = x_ref[...] + gate_ref[0] * _dot(merged.astype(BF16), wo_ref[...])
    ms = jnp.mean(xn * xn, axis=-1, keepdims=True)
    o_ref[...] = xn * lax.rsqrt(ms + EPS) * fg_ref[...]


def _out_proj(x2, gate, oc, osl, ow, ps, pm, yg, ex, wn, wg, wo, fg, *, T, tm=256):
    n, d = x2.shape
    nt = T // tm
    nzb = _NEW["nsa_z"] // NSA_WIDTH
    mnb = _NEW["mg_nsa"] // D_MODEL
    mgb = _NEW["mg_gla"] // D_MODEL
    row = lambda i: (i, 0)
    const2 = lambda i: (0, 0)
    return pl.pallas_call(
        _out_proj_kernel,
        grid=(n // tm,),
        in_specs=[pl.BlockSpec((tm, d), row),
                  pl.BlockSpec((1, 1, d), lambda i: (i // nt, 0, 0)),
                  pl.BlockSpec((tm, NSA_WIDTH), row),
                  pl.BlockSpec((tm, NSA_WIDTH), row),
                  pl.BlockSpec((tm, NSA_WIDTH), row),
                  pl.BlockSpec((tm, SMALL_WIDTH), row),
                  pl.BlockSpec((tm, NSA_WIDTH), lambda i: (i, nzb)),
                  pl.BlockSpec((tm, D_MODEL), lambda i: (i, mnb)),
                  pl.BlockSpec((tm, D_MODEL), lambda i: (i, mgb)),
                  pl.BlockSpec((tm, GLA_WIDTH), row),
                  pl.BlockSpec((3, SMALL_WIDTH, NSA_WIDTH), lambda i: (0, 0, 0)),
                  pl.BlockSpec((NSA_WIDTH, d), const2),
                  pl.BlockSpec((GLA_WIDTH, d), const2),
                  pl.BlockSpec((d, d), const2),
                  pl.BlockSpec((1, d), const2)],
        out_specs=pl.BlockSpec((tm, d), row),
        out_shape=jax.ShapeDtypeStruct((n, d), F32),
        compiler_params=pltpu.CompilerParams(dimension_semantics=("parallel",),
                                             vmem_limit_bytes=60 * 1024 * 1024),
        name="out_proj",
    )(x2, gate, oc, osl, ow, ps, pm, pm, pm, yg, ex, wn, wg, wo, fg)


def _importance_matrix(nbp, ncp):
    ratio = SLC_BLOCK // CMP_STRIDE
    n_sub = CMP_BLOCK // CMP_STRIDE
    w = np.zeros((nbp, ncp), np.float32)
    for blk in range(nbp):
        for m in range(ratio):
            for s in range(n_sub):
                n = ratio * blk + m - s
                if 0 <= n < ncp:
                    w[blk, n] += 1.0
    return w


def _gate_expansion():
    ex = np.zeros((3, SMALL_WIDTH, NSA_WIDTH), np.float32)
    for h in range(NSA_HEADS):
        for br in range(3):
            ex[br, h * 3 + br, h * NSA_HEAD_DIM:(h + 1) * NSA_HEAD_DIM] = 1.0
    return ex


def _main_cols(a):
    return jnp.concatenate([a[..., _OLD[nm][0]:_OLD[nm][0] + _OLD[nm][1]] for nm in _MAIN_ORDER], axis=-1)


def _small_cols(a):
    pad = SMALL_WIDTH - 3 * NSA_HEADS - GLA_GATE_RANK
    parts = [a[..., _OLD["nsa_g"][0]:_OLD["nsa_g"][0] + 3 * NSA_HEADS],
             a[..., _OLD["ga"][0]:_OLD["ga"][0] + GLA_GATE_RANK],
             jnp.zeros(a.shape[:-1] + (pad,), a.dtype)]
    return jnp.concatenate(parts, axis=-1)


def kernel(x, c, w_ada, b_ada, norm_gain, w_in, b_in, cmp_pos_k, cmp_pos_v, cmp_w1_k, cmp_w2_k, cmp_w1_v, cmp_w2_v,
           gla_w_alpha, gla_b_alpha, gla_norm_gain, w_br_nsa, w_br_gla, w_out, final_norm_gain):
    assert DEPTH == 1, "the final rmsnorm is fused into the single layer's output kernel"
    B, T, D = x.shape
    G, HPG, DH = NSA_KV_GROUPS, NSA_HPG, NSA_HEAD_DIM
    N = B * T
    nch = T // CMP_STRIDE
    nc = nch - CMP_BLOCK // CMP_STRIDE + 1
    nb = T // SLC_BLOCK
    nbp = max(LANE, nb)
    n_sel = min(N_SELECT, nb)
    tq = 256

    slopes = 2.0 ** (-8.0 * jnp.arange(1, NSA_HEADS + 1, dtype=F32) / NSA_HEADS)
    wimp = jnp.asarray(_importance_matrix(nbp, nch), BF16)
    et = (jnp.arange(nbp, dtype=jnp.int32)[:, None] == (jnp.arange(T, dtype=jnp.int32)[None, :] // SLC_BLOCK)).astype(BF16)
    ex = jnp.asarray(_gate_expansion(), BF16)
    tri = jnp.asarray(np.tril(np.ones((GLA_CHUNK, GLA_CHUNK), np.float32)), BF16)
    bp = ((B + 7) // 8) * 8
    c8 = jnp.zeros((bp, D), F32).at[:B].set(c)

    x2 = x.reshape(N, D)
    for l in range(DEPTH):
        mod = _ada(c8, w_ada[l], b_ada[l][None, :])
        shift = mod[:B, :D].reshape(B, 1, D)
        scale = mod[:B, D:2 * D].reshape(B, 1, D)
        gate = mod[:B, 2 * D:].reshape(B, 1, D)

        pm, ps = _in_proj(x2, norm_gain[l][None, :], shift, scale,
                          _main_cols(w_in[l]).astype(BF16), _main_cols(b_in[l])[None, :],
                          _small_cols(w_in[l]).astype(BF16), _small_cols(b_in[l])[None, :], T=T)

        q5 = pm[:, :NSA_WIDTH].reshape(B, T, G, HPG, DH).transpose(0, 2, 3, 1, 4)
        c0 = _NEW["ck"]
        kvf = pm[:, c0:c0 + 2 * NSA_KV_WIDTH].reshape(B, nch, CMP_STRIDE, 2, G, DH)
        kvf = kvf.transpose(3, 0, 4, 1, 2, 5).reshape(2, B, G, nch, CMP_STRIDE * DH)
        s0 = _NEW["sk"]
        kv4 = pm[:, s0:s0 + 4 * NSA_KV_WIDTH].reshape(B, T, 4, G, DH).transpose(2, 0, 3, 1, 4)
        posf = jnp.stack([cmp_pos_k[l], cmp_pos_v[l]]).reshape(2, 1, CMP_BLOCK * DH)
        posf = jnp.broadcast_to(posf, (2, 8, CMP_BLOCK * DH)).astype(BF16)
        w1 = jnp.stack([cmp_w1_k[l], cmp_w1_v[l]]).astype(BF16)
        w2 = jnp.stack([cmp_w2_k[l], cmp_w2_v[l]]).astype(BF16)
        kvc = _compress(kvf, posf, w1, w2)
        o_cmp, sel = _cmp_attn(slopes, q5, kvc[0], kvc[1], wimp, tq=tq, nc=nc, n_sel=n_sel)
        o_slc = _flash(slopes, q5, kv4[0], kv4[1], sel, et, tq=tq, tk=512)
        o_win = _flash(slopes, q5, kv4[2], kv4[3], tq=tq, tk=tq)

        def tok(o):
            return o.transpose(0, 3, 1, 2, 4).reshape(N, NSA_WIDTH)

        wal = jnp.zeros((SMALL_WIDTH, GLA_KEY_WIDTH), F32).at[GA_LANE:GA_LANE + GLA_GATE_RANK].set(gla_w_alpha[l])
        y_gla = _gla(pm, ps, wal, gla_b_alpha[l][None, :], gla_norm_gain[l][None, :], tri, B=B, T=T)

        x2 = _out_proj(x2, gate, tok(o_cmp), tok(o_slc), tok(o_win), ps, pm, y_gla, ex,
                       w_br_nsa[l].astype(BF16), w_br_gla[l].astype(BF16), w_out[l].astype(BF16),
                       final_norm_gain[None, :], T=T)
    return x2.reshape(B, T, D)
```

```python
import functools

import numpy as np
import jax
import jax.numpy as jnp
from jax import lax
from jax.experimental import pallas as pl
from jax.experimental.pallas import tpu as pltpu

D_MODEL = 2048
DEPTH = 1
NSA_HEADS = 16
NSA_KV_GROUPS = 4
NSA_HPG = NSA_HEADS // NSA_KV_GROUPS
NSA_HEAD_DIM = 64
CMP_BLOCK = 32
CMP_STRIDE = 16
CMP_HIDDEN = 256
SLC_BLOCK = 64
N_SELECT = 16
WINDOW = 512
NSA_WIDTH = NSA_HEADS * NSA_HEAD_DIM
NSA_KV_WIDTH = NSA_KV_GROUPS * NSA_HEAD_DIM
GLA_HEADS = 4
GLA_KEY_DIM = 128
GLA_VAL_DIM = 256
GLA_GATE_RANK = 16
GLA_TAU = 16.0
GLA_KEY_WIDTH = GLA_HEADS * GLA_KEY_DIM
GLA_WIDTH = GLA_HEADS * GLA_VAL_DIM
EPS = 1e-6
NEG = -1e30

F32 = jnp.float32
BF16 = jnp.bfloat16

_OLD = {}
_off = 0
for _name, _w in (("nsa_q", NSA_WIDTH), ("ck", NSA_KV_WIDTH), ("cv", NSA_KV_WIDTH), ("sk", NSA_KV_WIDTH),
                  ("sv", NSA_KV_WIDTH), ("wk", NSA_KV_WIDTH), ("wv", NSA_KV_WIDTH), ("nsa_g", 3 * NSA_HEADS),
                  ("nsa_z", NSA_WIDTH), ("gq", GLA_KEY_WIDTH), ("gk", GLA_KEY_WIDTH), ("gv", GLA_WIDTH),
                  ("ga", GLA_GATE_RANK), ("gla_z", GLA_WIDTH), ("mg_nsa", D_MODEL), ("mg_gla", D_MODEL)):
    _OLD[_name] = (_off, _w)
    _off += _w

_MAIN_ORDER = ("nsa_q", "nsa_z", "gla_z", "gv", "mg_nsa", "mg_gla", "gq", "gk", "ck", "cv", "sk", "sv", "wk", "wv")
_NEW = {}
_off = 0
for _name in _MAIN_ORDER:
    _NEW[_name] = _off
    _off += _OLD[_name][1]
MAIN_WIDTH = _off
SMALL_WIDTH = 128
GA_LANE = 3 * NSA_HEADS

LANE = 128
GLA_CHUNK = 64
GLA_SUB = 16


def _dot(a, b):
    return jnp.dot(a, b, preferred_element_type=F32)


def _dot_nt(a, b):
    return lax.dot_general(a, b, (((1,), (1,)), ((), ())), preferred_element_type=F32)


def _split3(x):
    hi = x.astype(BF16)
    r = x - hi.astype(F32)
    mid = r.astype(BF16)
    lo = (r - mid.astype(F32)).astype(BF16)
    return hi, mid, lo


def _sigmoid(x):
    return 1.0 / (1.0 + jnp.exp(-x))


def _ada_kernel(c_ref, w_ref, b_ref, o_ref):
    ch, cm, cl = _split3(c_ref[...])
    wh, wm, wl = _split3(w_ref[...])
    acc = _dot(ch, wh) + _dot(ch, wm) + _dot(cm, wh) + _dot(ch, wl) + _dot(cl, wh) + _dot(cm, wm)
    o_ref[...] = acc + b_ref[...]


def _ada(c8, w, b, *, tn=768):
    m, d = c8.shape
    n = w.shape[1]
    return pl.pallas_call(
        _ada_kernel,
        grid=(n // tn,),
        in_specs=[pl.BlockSpec((m, d), lambda j: (0, 0)),
                  pl.BlockSpec((d, tn), lambda j: (0, j)),
                  pl.BlockSpec((1, tn), lambda j: (0, j))],
        out_specs=pl.BlockSpec((m, tn), lambda j: (0, j)),
        out_shape=jax.ShapeDtypeStruct((m, n), F32),
        compiler_params=pltpu.CompilerParams(dimension_semantics=("arbitrary",),
                                             vmem_limit_bytes=40 * 1024 * 1024),
        name="ada",
    )(c8, w, b)


def _in_proj_kernel(x_ref, gain_ref, shift_ref, scale_ref, w_ref, b_ref, ws_ref, bs_ref,
                    o_ref, os_ref, h_ref):
    j = pl.program_id(1)

    @pl.when(j == 0)
    def _():
        x = x_ref[...]
        ms = jnp.mean(x * x, axis=-1, keepdims=True)
        y = x * lax.rsqrt(ms + EPS) * gain_ref[...]
        y = y * (1.0 + scale_ref[0]) + shift_ref[0]
        h = y.astype(BF16)
        h_ref[...] = h
        os_ref[...] = _dot(h, ws_ref[...]) + bs_ref[...]

    o_ref[...] = (_dot(h_ref[...], w_ref[...]) + b_ref[...]).astype(o_ref.dtype)


def _in_proj(x2, gain, shift, scale, w_main, b_main, w_small, b_small, *, T, tm=1024, tn=1536):
    n, d = x2.shape
    nm = w_main.shape[1]
    nt = T // tm
    return pl.pallas_call(
        _in_proj_kernel,
        grid=(n // tm, nm // tn),
        in_specs=[pl.BlockSpec((tm, d), lambda i, j: (i, 0)),
                  pl.BlockSpec((1, d), lambda i, j: (0, 0)),
                  pl.BlockSpec((1, 1, d), lambda i, j: (i // nt, 0, 0)),
                  pl.BlockSpec((1, 1, d), lambda i, j: (i // nt, 0, 0)),
                  pl.BlockSpec((d, tn), lambda i, j: (0, j)),
                  pl.BlockSpec((1, tn), lambda i, j: (0, j)),
                  pl.BlockSpec((d, SMALL_WIDTH), lambda i, j: (0, 0)),
                  pl.BlockSpec((1, SMALL_WIDTH), lambda i, j: (0, 0))],
        out_specs=[pl.BlockSpec((tm, tn), lambda i, j: (i, j)),
                   pl.BlockSpec((tm, SMALL_WIDTH), lambda i, j: (i, 0))],
        out_shape=[jax.ShapeDtypeStruct((n, nm), BF16),
                   jax.ShapeDtypeStruct((n, SMALL_WIDTH), F32)],
        scratch_shapes=[pltpu.VMEM((tm, d), BF16)],
        compiler_params=pltpu.CompilerParams(dimension_semantics=("parallel", "arbitrary"),
                                             vmem_limit_bytes=56 * 1024 * 1024),
        name="in_proj",
    )(x2, gain, shift, scale, w_main, b_main, w_small, b_small)


def _compress_kernel(kv_ref, pos_ref, w1_ref, w2_ref, o_ref):
    y = kv_ref[0, 0, 0]
    w1 = w1_ref[0]
    half = y.shape[1]
    nch = y.shape[0]
    z1 = _dot(y, w1[:half])
    z2 = _dot(y, w1[half:])
    posb = _dot(pos_ref[0], w1)[0:1]
    pre = z1 + pltpu.roll(z2, nch - 1, 0) + posb
    hid = pre * _sigmoid(pre)
    o_ref[0, 0, 0] = _dot(hid.astype(BF16), w2_ref[0]).astype(o_ref.dtype)


def _compress(kvf, posf, w1, w2):
    two, b, g, nch, half = kvf.shape
    hidden = w1.shape[2]
    dh = w2.shape[2]
    return pl.pallas_call(
        _compress_kernel,
        grid=(two, b, g),
        in_specs=[pl.BlockSpec((1, 1, 1, nch, half), lambda s, i, j: (s, i, j, 0, 0)),
                  pl.BlockSpec((1, 8, 2 * half), lambda s, i, j: (s, 0, 0)),
                  pl.BlockSpec((1, 2 * half, hidden), lambda s, i, j: (s, 0, 0)),
                  pl.BlockSpec((1, hidden, dh), lambda s, i, j: (s, 0, 0))],
        out_specs=pl.BlockSpec((1, 1, 1, nch, dh), lambda s, i, j: (s, i, j, 0, 0)),
        out_shape=jax.ShapeDtypeStruct((two, b, g, nch, dh), BF16),
        compiler_params=pltpu.CompilerParams(dimension_semantics=("parallel", "parallel", "parallel")),
        name="compress",
    )(kvf, posf, w1, w2)


def _cmp_attn_kernel(slopes_ref, q_ref, kc_ref, vc_ref, wimp_ref, o_ref, sel_ref, *, tq, nc, n_sel):
    g = pl.program_id(1)
    qi = pl.program_id(2)
    kc = kc_ref[0, 0]
    vc = vc_ref[0, 0]
    ncp = kc.shape[0]
    scale = NSA_HEAD_DIM ** -0.5
    t_i = qi * tq + lax.broadcasted_iota(jnp.int32, (tq, 1), 0)
    n_i = lax.broadcasted_iota(jnp.int32, (1, ncp), 1)
    ce_i = n_i * CMP_STRIDE + (CMP_BLOCK - 1)
    valid = (ce_i <= t_i) & (n_i < nc)
    dist = (t_i - ce_i).astype(F32)
    psum = jnp.zeros((tq, ncp), F32)
    for hh in range(NSA_HPG):
        slope = slopes_ref[g * NSA_HPG + hh]
        s = _dot_nt(q_ref[0, 0, hh], kc) * scale - slope * dist
        sm = jnp.where(valid, s, NEG)
        m = jnp.max(sm, axis=-1, keepdims=True)
        p = jnp.where(valid, jnp.exp(sm - m), 0.0)
        l = jnp.sum(p, axis=-1, keepdims=True)
        p = p / jnp.where(l > 0.0, l, 1.0)
        o_ref[0, 0, hh] = _dot(p.astype(BF16), vc).astype(o_ref.dtype)
        psum = psum + p
    wimp = wimp_ref[...]
    ph, pm, plo = _split3(psum)
    imp = _dot_nt(wimp, ph) + _dot_nt(wimp, pm) + _dot_nt(wimp, plo)
    nbp = imp.shape[0]
    blk = lax.broadcasted_iota(jnp.int32, (nbp, 1), 0)
    t_l = qi * tq + lax.broadcasted_iota(jnp.int32, (1, tq), 1)
    cur = lax.shift_right_logical(t_l, 6)
    forced = (blk == 0) | (blk == cur) | (blk == cur - 1)
    bvalid = blk * SLC_BLOCK <= t_l
    score = jnp.where(forced, 1e30, jnp.where(bvalid, imp, -1.0))
    blk_f = blk.astype(F32)
    sel = jnp.zeros((nbp, tq), F32)
    for _ in range(n_sel):
        m = jnp.max(score, axis=0, keepdims=True)
        first = jnp.min(jnp.where(score == m, blk_f, float(nbp)), axis=0, keepdims=True)
        hit = blk_f == first
        sel = jnp.where(hit, 1.0, sel)
        score = jnp.where(hit, -1.0, score)
    sel_ref[0, 0] = sel.T.astype(sel_ref.dtype)


def _cmp_attn(slopes, q5, kc, vc, wimp, *, tq, nc, n_sel):
    b, g, hpg, t, dh = q5.shape
    ncp = kc.shape[2]
    nbp = wimp.shape[0]
    kern = functools.partial(_cmp_attn_kernel, tq=tq, nc=nc, n_sel=n_sel)
    return pl.pallas_call(
        kern,
        grid=(b, g, t // tq),
        in_specs=[pl.BlockSpec(memory_space=pltpu.SMEM),
                  pl.BlockSpec((1, 1, hpg, tq, dh), lambda i, j, k: (i, j, 0, k, 0)),
                  pl.BlockSpec((1, 1, ncp, dh), lambda i, j, k: (i, j, 0, 0)),
                  pl.BlockSpec((1, 1, ncp, dh), lambda i, j, k: (i, j, 0, 0)),
                  pl.BlockSpec((nbp, ncp), lambda i, j, k: (0, 0))],
        out_specs=[pl.BlockSpec((1, 1, hpg, tq, dh), lambda i, j, k: (i, j, 0, k, 0)),
                   pl.BlockSpec((1, 1, tq, nbp), lambda i, j, k: (i, j, k, 0))],
        out_shape=[jax.ShapeDtypeStruct((b, g, hpg, t, dh), BF16),
                   jax.ShapeDtypeStruct((b, g, t, nbp), BF16)],
        compiler_params=pltpu.CompilerParams(dimension_semantics=("parallel", "parallel", "parallel"),
                                             vmem_limit_bytes=48 * 1024 * 1024),
        name="cmp_attn",
    )(slopes, q5, kc, vc, wimp)


def _flash_kernel(*refs, tq, tk, nkv, selected):
    if selected:
        slopes_ref, q_ref, k_ref, v_ref, sel_ref, et_ref, o_ref, m_sc, l_sc, acc_sc = refs
    else:
        slopes_ref, q_ref, k_ref, v_ref, o_ref, m_sc, l_sc, acc_sc = refs
    g = pl.program_id(1)
    qi = pl.program_id(2)
    kj = pl.program_id(3)
    scale = NSA_HEAD_DIM ** -0.5

    @pl.when(kj == 0)
    def _():
        m_sc[...] = jnp.full(m_sc.shape, NEG, F32)
        l_sc[...] = jnp.zeros(l_sc.shape, F32)
        acc_sc[...] = jnp.zeros(acc_sc.shape, F32)

    if selected:
        kv_idx = kj
        active = kj * tk <= qi * tq + (tq - 1)
    else:
        kv_idx = qi - (nkv - 1) + kj
        active = kv_idx >= 0

    @pl.when(active)
    def _():
        t_i = qi * tq + lax.broadcasted_iota(jnp.int32, (tq, 1), 0)
        j_i = kv_idx * tk + lax.broadcasted_iota(jnp.int32, (1, tk), 1)
        diff = t_i - j_i
        if selected:
            picked = _dot(sel_ref[0, 0], et_ref[...])
            valid = (diff >= 0) & (picked > 0.5)
        else:
            valid = (diff >= 0) & (diff < WINDOW)
        dist = diff.astype(F32)
        k = k_ref[0, 0]
        v = v_ref[0, 0]
        for hh in range(NSA_HPG):
            slope = slopes_ref[g * NSA_HPG + hh]
            s = _dot_nt(q_ref[0, 0, hh], k) * scale - slope * dist
            sm = jnp.where(valid, s, NEG)
            m_prev = m_sc[hh]
            m_new = jnp.maximum(m_prev, jnp.max(sm, axis=-1, keepdims=True))
            alpha = jnp.exp(m_prev - m_new)
            p = jnp.where(valid, jnp.exp(sm - m_new), 0.0)
            l_sc[hh] = alpha * l_sc[hh] + jnp.sum(p, axis=-1, keepdims=True)
            acc_sc[hh] = alpha * acc_sc[hh] + _dot(p.astype(BF16), v)
            m_sc[hh] = m_new

    @pl.when(kj == nkv - 1)
    def _():
        for hh in range(NSA_HPG):
            o_ref[0, 0, hh] = (acc_sc[hh] / l_sc[hh]).astype(o_ref.dtype)


def _flash(slopes, q5, k4, v4, sel=None, et=None, *, tq, tk):
    b, g, hpg, t, dh = q5.shape
    selected = sel is not None
    if selected:
        nkv = t // tk

        def kv_block(qi, kj):
            return jnp.minimum(kj, (qi * tq + (tq - 1)) // tk)
    else:
        assert tk == tq and WINDOW % tk == 0
        nkv = WINDOW // tk + 1

        def kv_block(qi, kj):
            return jnp.maximum(qi - (nkv - 1) + kj, 0)

    in_specs = [pl.BlockSpec(memory_space=pltpu.SMEM),
                pl.BlockSpec((1, 1, hpg, tq, dh), lambda i, j, qi, kj: (i, j, 0, qi, 0)),
                pl.BlockSpec((1, 1, tk, dh), lambda i, j, qi, kj: (i, j, kv_block(qi, kj), 0)),
                pl.BlockSpec((1, 1, tk, dh), lambda i, j, qi, kj: (i, j, kv_block(qi, kj), 0))]
    args = [slopes, q5, k4, v4]
    if selected:
        nbp = sel.shape[3]
        in_specs += [pl.BlockSpec((1, 1, tq, nbp), lambda i, j, qi, kj: (i, j, qi, 0)),
                     pl.BlockSpec((nbp, tk), lambda i, j, qi, kj: (0, kv_block(qi, kj)))]
        args += [sel, et]
    kern = functools.partial(_flash_kernel, tq=tq, tk=tk, nkv=nkv, selected=selected)
    return pl.pallas_call(
        kern,
        grid=(b, g, t // tq, nkv),
        in_specs=in_specs,
        out_specs=pl.BlockSpec((1, 1, hpg, tq, dh), lambda i, j, qi, kj: (i, j, 0, qi, 0)),
        out_shape=jax.ShapeDtypeStruct((b, g, hpg, t, dh), BF16),
        scratch_shapes=[pltpu.VMEM((hpg, tq, 1), F32), pltpu.VMEM((hpg, tq, 1), F32),
                        pltpu.VMEM((hpg, tq, dh), F32)],
        compiler_params=pltpu.CompilerParams(
            dimension_semantics=("parallel", "parallel", "parallel", "arbitrary"),
            vmem_limit_bytes=48 * 1024 * 1024),
        name="flash_sel" if selected else "flash_win",
    )(*args)


def _gla_intra_scores(q, k, bcum):
    c, dk = q.shape
    sb = GLA_SUB
    row_c = lax.broadcasted_iota(jnp.int32, (c, dk), 0)
    row_s = lax.broadcasted_iota(jnp.int32, (sb, dk), 0)
    lane_s = lax.broadcasted_iota(jnp.int32, (sb, c), 1)
    blocks = []
    for i0 in range(0, c, sb):
        bi = bcum[i0:i0 + sb]
        qi = q[i0:i0 + sb]
        a_blk = jnp.zeros((sb, c), F32)
        for jl in range(sb):
            j = i0 + jl
            w = jnp.exp(jnp.where(row_s >= jl, bi - bcum[j:j + 1], NEG))
            a = jnp.sum(qi * k[j:j + 1] * w, axis=-1, keepdims=True)
            a_blk = jnp.where(lane_s == j, a, a_blk)
        if i0 > 0:
            r = bcum[i0 - 1:i0]
            qh = qi * jnp.exp(bi - r)
            kh = k * jnp.exp(jnp.where(row_c < i0, r - bcum, NEG))
            a_blk = a_blk + _dot_nt(qh.astype(BF16), kh.astype(BF16))
        blocks.append(a_blk)
    return jnp.concatenate(blocks, axis=0)


def _gla_kernel(q_ref, k_ref, v_ref, z_ref, ps_ref, wal_ref, bal_ref, gain_ref, tri_ref, o_ref,
                st_sc, *, ts):
    ti = pl.program_id(2)
    c = GLA_CHUNK

    @pl.when(ti == 0)
    def _():
        st_sc[...] = jnp.zeros(st_sc.shape, F32)

    ps = ps_ref[...]
    ph, pm, _ = _split3(ps)
    wh, wm, _ = _split3(wal_ref[...])
    zz = _dot(ph, wh) + _dot(ph, wm) + _dot(pm, wh) + bal_ref[...]
    log_a = (jnp.minimum(zz, 0.0) - jnp.log(1.0 + jnp.exp(-jnp.abs(zz)))) / GLA_TAU
    tri = tri_ref[...]
    qscale = GLA_KEY_DIM ** -0.5
    for ci in range(ts // c):
        sl = slice(ci * c, (ci + 1) * c)
        gh, gm, gl = _split3(log_a[sl])
        bcum = _dot(tri, gh) + _dot(tri, gm) + _dot(tri, gl)
        q = q_ref[sl, :].astype(F32) * qscale
        k = k_ref[sl, :].astype(F32)
        v = v_ref[sl, :]
        a_mat = _gla_intra_scores(q, k, bcum)
        st = st_sc[...]
        o = _dot(a_mat.astype(BF16), v) + _dot_nt((q * jnp.exp(bcum)).astype(BF16), st.astype(BF16))
        b_last = bcum[c - 1:c, :]
        kd = k * jnp.exp(b_last - bcum)
        st_sc[...] = st * jnp.exp(b_last) + _dot(v.astype(F32).T.astype(BF16), kd.astype(BF16))
        ms = jnp.mean(o * o, axis=-1, keepdims=True)
        y = o * lax.rsqrt(ms + EPS) * gain_ref[...]
        z = z_ref[sl, :].astype(F32)
        o_ref[sl, :] = (y * (z * _sigmoid(z))).astype(o_ref.dtype)


def _gla(pm, ps, wal, bal, gain, tri, *, B, T, ts=256):
    n = pm.shape[0]
    nts = T // ts
    dk, dv = GLA_KEY_DIM, GLA_VAL_DIM
    qb, kb = _NEW["gq"] // dk, _NEW["gk"] // dk
    vb, zb = _NEW["gv"] // dv, _NEW["gla_z"] // dv
    kern = functools.partial(_gla_kernel, ts=ts)
    return pl.pallas_call(
        kern,
        grid=(B, GLA_HEADS, nts),
        in_specs=[pl.BlockSpec((ts, dk), lambda b, h, i: (b * nts + i, qb + h)),
                  pl.BlockSpec((ts, dk), lambda b, h, i: (b * nts + i, kb + h)),
                  pl.BlockSpec((ts, dv), lambda b, h, i: (b * nts + i, vb + h)),
                  pl.BlockSpec((ts, dv), lambda b, h, i: (b * nts + i, zb + h)),
                  pl.BlockSpec((ts, SMALL_WIDTH), lambda b, h, i: (b * nts + i, 0)),
                  pl.BlockSpec((SMALL_WIDTH, dk), lambda b, h, i: (0, h)),
                  pl.BlockSpec((1, dk), lambda b, h, i: (0, h)),
                  pl.BlockSpec((1, dv), lambda b, h, i: (0, 0)),
                  pl.BlockSpec((GLA_CHUNK, GLA_CHUNK), lambda b, h, i: (0, 0))],
        out_specs=pl.BlockSpec((ts, dv), lambda b, h, i: (b * nts + i, h)),
        out_shape=jax.ShapeDtypeStruct((n, GLA_WIDTH), BF16),
        scratch_shapes=[pltpu.VMEM((dv, dk), F32)],
        compiler_params=pltpu.CompilerParams(dimension_semantics=("parallel", "parallel", "arbitrary")),
        name="gla",
    )(pm, pm, pm, pm, ps, wal, bal, gain, tri)


def _out_proj_kernel(x_ref, gate_ref, oc_ref, os_ref, ow_ref, ps_ref, nz_ref, mgn_ref, mgg_ref, yg_ref,
                     ex_ref, wn_ref, wg_ref, wo_ref, fg_ref, o_ref):
    sg = _sigmoid(ps_ref[...]).astype(BF16)
    nz = nz_ref[...].astype(F32)
    o_nsa = (_dot(sg, ex_ref[0]) * oc_ref[...].astype(F32)
             + _dot(sg, ex_ref[1]) * os_ref[...].astype(F32)
             + _dot(sg, ex_ref[2]) * ow_ref[...].astype(F32))
    y_nsa = o_nsa * (nz * _sigmoid(nz))
    a = _dot(y_nsa.astype(BF16), wn_ref[...])
    bm = _dot(yg_ref[...], wg_ref[...])
    merged = _sigmoid(mgn_ref[...].astype(F32)) * a + _sigmoid(mgg_ref[...].astype(F32)) * bm
    xn = x_ref[...] + gate_ref[0] * _dot(merged.astype(BF16), wo_ref[...])
    ms = jnp.mean(xn * xn, axis=-1, keepdims=True)
    o_ref[...] = xn * lax.rsqrt(ms + EPS) * fg_ref[...]


def _out_proj(x2, gate, oc, osl, ow, ps, pm, yg, ex, wn, wg, wo, fg, *, T, tm=256):
    n, d = x2.shape
    nt = T // tm
    nzb = _NEW["nsa_z"] // NSA_WIDTH
    mnb = _NEW["mg_nsa"] // D_MODEL
    mgb = _NEW["mg_gla"] // D_MODEL
    row = lambda i: (i, 0)
    const2 = lambda i: (0, 0)
    return pl.pallas_call(
        _out_proj_kernel,
        grid=(n // tm,),
        in_specs=[pl.BlockSpec((tm, d), row),
                  pl.BlockSpec((1, 1, d), lambda i: (i // nt, 0, 0)),
                  pl.BlockSpec((tm, NSA_WIDTH), row),
                  pl.BlockSpec((tm, NSA_WIDTH), row),
                  pl.BlockSpec((tm, NSA_WIDTH), row),
                  pl.BlockSpec((tm, SMALL_WIDTH), row),
                  pl.BlockSpec((tm, NSA_WIDTH), lambda i: (i, nzb)),
                  pl.BlockSpec((tm, D_MODEL), lambda i: (i, mnb)),
                  pl.BlockSpec((tm, D_MODEL), lambda i: (i, mgb)),
                  pl.BlockSpec((tm, GLA_WIDTH), row),
                  pl.BlockSpec((3, SMALL_WIDTH, NSA_WIDTH), lambda i: (0, 0, 0)),
                  pl.BlockSpec((NSA_WIDTH, d), const2),
                  pl.BlockSpec((GLA_WIDTH, d), const2),
                  pl.BlockSpec((d, d), const2),
                  pl.BlockSpec((1, d), const2)],
        out_specs=pl.BlockSpec((tm, d), row),
        out_shape=jax.ShapeDtypeStruct((n, d), F32),
        compiler_params=pltpu.CompilerParams(dimension_semantics=("parallel",),
                                             vmem_limit_bytes=60 * 1024 * 1024),
        name="out_proj",
    )(x2, gate, oc, osl, ow, ps, pm, pm, pm, yg, ex, wn, wg, wo, fg)


def _importance_matrix(nbp, ncp):
    ratio = SLC_BLOCK // CMP_STRIDE
    n_sub = CMP_BLOCK // CMP_STRIDE
    w = np.zeros((nbp, ncp), np.float32)
    for blk in range(nbp):
        for m in range(ratio):
            for s in range(n_sub):
                n = ratio * blk + m - s
                if 0 <= n < ncp:
                    w[blk, n] += 1.0
    return w


def _gate_expansion():
    ex = np.zeros((3, SMALL_WIDTH, NSA_WIDTH), np.float32)
    for h in range(NSA_HEADS):
        for br in range(3):
            ex[br, h * 3 + br, h * NSA_HEAD_DIM:(h + 1) * NSA_HEAD_DIM] = 1.0
    return ex


def _main_cols(a):
    return jnp.concatenate([a[..., _OLD[nm][0]:_OLD[nm][0] + _OLD[nm][1]] for nm in _MAIN_ORDER], axis=-1)


def _small_cols(a):
    pad = SMALL_WIDTH - 3 * NSA_HEADS - GLA_GATE_RANK
    parts = [a[..., _OLD["nsa_g"][0]:_OLD["nsa_g"][0] + 3 * NSA_HEADS],
             a[..., _OLD["ga"][0]:_OLD["ga"][0] + GLA_GATE_RANK],
             jnp.zeros(a.shape[:-1] + (pad,), a.dtype)]
    return jnp.concatenate(parts, axis=-1)


def kernel(x, c, w_ada, b_ada, norm_gain, w_in, b_in, cmp_pos_k, cmp_pos_v, cmp_w1_k, cmp_w2_k, cmp_w1_v, cmp_w2_v,
           gla_w_alpha, gla_b_alpha, gla_norm_gain, w_br_nsa, w_br_gla, w_out, final_norm_gain):
    assert DEPTH == 1, "the final rmsnorm is fused into the single layer's output kernel"
    B, T, D = x.shape
    G, HPG, DH = NSA_KV_GROUPS, NSA_HPG, NSA_HEAD_DIM
    N = B * T
    nch = T // CMP_STRIDE
    nc = nch - CMP_BLOCK // CMP_STRIDE + 1
    nb = T // SLC_BLOCK
    nbp = max(LANE, nb)
    n_sel = min(N_SELECT, nb)
    tq = 256

    slopes = 2.0 ** (-8.0 * jnp.arange(1, NSA_HEADS + 1, dtype=F32) / NSA_HEADS)
    wimp = jnp.asarray(_importance_matrix(nbp, nch), BF16)
    et = (jnp.arange(nbp, dtype=jnp.int32)[:, None] == (jnp.arange(T, dtype=jnp.int32)[None, :] // SLC_BLOCK)).astype(BF16)
    ex = jnp.asarray(_gate_expansion(), BF16)
    tri = jnp.asarray(np.tril(np.ones((GLA_CHUNK, GLA_CHUNK), np.float32)), BF16)
    bp = ((B + 7) // 8) * 8
    c8 = jnp.zeros((bp, D), F32).at[:B].set(c)

    x2 = x.reshape(N, D)
    for l in range(DEPTH):
        mod = _ada(c8, w_ada[l], b_ada[l][None, :])
        shift = mod[:B, :D].reshape(B, 1, D)
        scale = mod[:B, D:2 * D].reshape(B, 1, D)
        gate = mod[:B, 2 * D:].reshape(B, 1, D)

        pm, ps = _in_proj(x2, norm_gain[l][None, :], shift, scale,
                          _main_cols(w_in[l]).astype(BF16), _main_cols(b_in[l])[None, :],
                          _small_cols(w_in[l]).astype(BF16), _small_cols(b_in[l])[None, :], T=T)

        q5 = pm[:, :NSA_WIDTH].reshape(B, T, G, HPG, DH).transpose(0, 2, 3, 1, 4)
        c0 = _NEW["ck"]
        kvf = pm[:, c0:c0 + 2 * NSA_KV_WIDTH].reshape(B, nch, CMP_STRIDE, 2, G, DH)
        kvf = kvf.transpose(3, 0, 4, 1, 2, 5).reshape(2, B, G, nch, CMP_STRIDE * DH)
        s0 = _NEW["sk"]
        kv4 = pm[:, s0:s0 + 4 * NSA_KV_WIDTH].reshape(B, T, 4, G, DH).transpose(2, 0, 3, 1, 4)
        posf = jnp.stack([cmp_pos_k[l], cmp_pos_v[l]]).reshape(2, 1, CMP_BLOCK * DH)
        posf = jnp.broadcast_to(posf, (2, 8, CMP_BLOCK * DH)).astype(BF16)
        w1 = jnp.stack([cmp_w1_k[l], cmp_w1_v[l]]).astype(BF16)
        w2 = jnp.stack([cmp_w2_k[l], cmp_w2_v[l]]).astype(BF16)
        kvc = _compress(kvf, posf, w1, w2)
        o_cmp, sel = _cmp_attn(slopes, q5, kvc[0], kvc[1], wimp, tq=tq, nc=nc, n_sel=n_sel)
        o_slc = _flash(slopes, q5, kv4[0], kv4[1], sel, et, tq=tq, tk=512)
        o_win = _flash(slopes, q5, kv4[2], kv4[3], tq=tq, tk=tq)

        def tok(o):
            return o.transpose(0, 3, 1, 2, 4).reshape(N, NSA_WIDTH)

        wal = jnp.zeros((SMALL_WIDTH, GLA_KEY_WIDTH), F32).at[GA_LANE:GA_LANE + GLA_GATE_RANK].set(gla_w_alpha[l])
        y_gla = _gla(pm, ps, wal, gla_b_alpha[l][None, :], gla_norm_gain[l][None, :], tri, B=B, T=T)

        x2 = _out_proj(x2, gate, tok(o_cmp), tok(o_slc), tok(o_win), ps, pm, y_gla, ex,
                       w_br_nsa[l].astype(BF16), w_br_gla[l].astype(BF16), w_out[l].astype(BF16),
                       final_norm_gain[None, :], T=T)
    return x2.reshape(B, T, D)
```

```python
import functools

import numpy as np
import jax
import jax.numpy as jnp
from jax import lax
from jax.experimental import pallas as pl
from jax.experimental.pallas import tpu as pltpu

D_MODEL = 2048
DEPTH = 1
NSA_HEADS = 16
NSA_KV_GROUPS = 4
NSA_HPG = NSA_HEADS // NSA_KV_GROUPS
NSA_HEAD_DIM = 64
CMP_BLOCK = 32
CMP_STRIDE = 16
CMP_HIDDEN = 256
SLC_BLOCK = 64
N_SELECT = 16
WINDOW = 512
NSA_WIDTH = NSA_HEADS * NSA_HEAD_DIM
NSA_KV_WIDTH = NSA_KV_GROUPS * NSA_HEAD_DIM
GLA_HEADS = 4
GLA_KEY_DIM = 128
GLA_VAL_DIM = 256
GLA_GATE_RANK = 16
GLA_TAU = 16.0
GLA_KEY_WIDTH = GLA_HEADS * GLA_KEY_DIM
GLA_WIDTH = GLA_HEADS * GLA_VAL_DIM
EPS = 1e-6
NEG = -1e30

F32 = jnp.float32
BF16 = jnp.bfloat16

_OLD = {}
_off = 0
for _name, _w in (("nsa_q", NSA_WIDTH), ("ck", NSA_KV_WIDTH), ("cv", NSA_KV_WIDTH), ("sk", NSA_KV_WIDTH),
                  ("sv", NSA_KV_WIDTH), ("wk", NSA_KV_WIDTH), ("wv", NSA_KV_WIDTH), ("nsa_g", 3 * NSA_HEADS),
                  ("nsa_z", NSA_WIDTH), ("gq", GLA_KEY_WIDTH), ("gk", GLA_KEY_WIDTH), ("gv", GLA_WIDTH),
                  ("ga", GLA_GATE_RANK), ("gla_z", GLA_WIDTH), ("mg_nsa", D_MODEL), ("mg_gla", D_MODEL)):
    _OLD[_name] = (_off, _w)
    _off += _w

_MAIN_ORDER = ("nsa_q", "nsa_z", "gla_z", "gv", "mg_nsa", "mg_gla", "gq", "gk", "ck", "cv", "sk", "sv", "wk", "wv")
_NEW = {}
_off = 0
for _name in _MAIN_ORDER:
    _NEW[_name] = _off
    _off += _OLD[_name][1]
MAIN_WIDTH = _off
SMALL_WIDTH = 128
GA_LANE = 3 * NSA_HEADS

LANE = 128
GLA_CHUNK = 64
GLA_SUB = 16
NSA_TILE = 256
QFEAT = 2 * NSA_HEAD_DIM
KSEL_FEAT = QFEAT + LANE
N_SLOPE_TERMS = 3
SEL_MASK = 16384.0


def _dot(a, b):
    return jnp.dot(a, b, preferred_element_type=F32)


def _dot_nt(a, b):
    return lax.dot_general(a, b, (((1,), (1,)), ((), ())), preferred_element_type=F32)


def _split3(x):
    hi = x.astype(BF16)
    r = x - hi.astype(F32)
    mid = r.astype(BF16)
    lo = (r - mid.astype(F32)).astype(BF16)
    return hi, mid, lo


def _sigmoid(x):
    return 1.0 / (1.0 + jnp.exp(-x))


def _ada_kernel(c_ref, w_ref, b_ref, o_ref):
    ch, cm, cl = _split3(c_ref[...])
    wh, wm, wl = _split3(w_ref[...])
    acc = _dot(ch, wh) + _dot(ch, wm) + _dot(cm, wh) + _dot(ch, wl) + _dot(cl, wh) + _dot(cm, wm)
    o_ref[...] = acc + b_ref[...]


def _ada(c8, w, b, *, tn=768):
    m, d = c8.shape
    n = w.shape[1]
    return pl.pallas_call(
        _ada_kernel,
        grid=(n // tn,),
        in_specs=[pl.BlockSpec((m, d), lambda j: (0, 0)),
                  pl.BlockSpec((d, tn), lambda j: (0, j)),
                  pl.BlockSpec((1, tn), lambda j: (0, j))],
        out_specs=pl.BlockSpec((m, tn), lambda j: (0, j)),
        out_shape=jax.ShapeDtypeStruct((m, n), F32),
        compiler_params=pltpu.CompilerParams(dimension_semantics=("arbitrary",),
                                             vmem_limit_bytes=40 * 1024 * 1024),
        name="ada",
    )(c8, w, b)


def _in_proj_kernel(x_ref, gain_ref, shift_ref, scale_ref, w_ref, b_ref, ws_ref, bs_ref,
                    o_ref, os_ref, h_ref):
    j = pl.program_id(1)

    @pl.when(j == 0)
    def _():
        x = x_ref[...]
        ms = jnp.mean(x * x, axis=-1, keepdims=True)
        y = x * lax.rsqrt(ms + EPS) * gain_ref[...]
        y = y * (1.0 + scale_ref[0]) + shift_ref[0]
        h = y.astype(BF16)
        h_ref[...] = h
        os_ref[...] = _dot(h, ws_ref[...]) + bs_ref[...]

    o_ref[...] = (_dot(h_ref[...], w_ref[...]) + b_ref[...]).astype(o_ref.dtype)


def _in_proj(x2, gain, shift, scale, w_main, b_main, w_small, b_small, *, T, tm=1024, tn=1536):
    n, d = x2.shape
    nm = w_main.shape[1]
    nt = T // tm
    return pl.pallas_call(
        _in_proj_kernel,
        grid=(n // tm, nm // tn),
        in_specs=[pl.BlockSpec((tm, d), lambda i, j: (i, 0)),
                  pl.BlockSpec((1, d), lambda i, j: (0, 0)),
                  pl.BlockSpec((1, 1, d), lambda i, j: (i // nt, 0, 0)),
                  pl.BlockSpec((1, 1, d), lambda i, j: (i // nt, 0, 0)),
                  pl.BlockSpec((d, tn), lambda i, j: (0, j)),
                  pl.BlockSpec((1, tn), lambda i, j: (0, j)),
                  pl.BlockSpec((d, SMALL_WIDTH), lambda i, j: (0, 0)),
                  pl.BlockSpec((1, SMALL_WIDTH), lambda i, j: (0, 0))],
        out_specs=[pl.BlockSpec((tm, tn), lambda i, j: (i, j)),
                   pl.BlockSpec((tm, SMALL_WIDTH), lambda i, j: (i, 0))],
        out_shape=[jax.ShapeDtypeStruct((n, nm), BF16),
                   jax.ShapeDtypeStruct((n, SMALL_WIDTH), F32)],
        scratch_shapes=[pltpu.VMEM((tm, d), BF16)],
        compiler_params=pltpu.CompilerParams(dimension_semantics=("parallel", "arbitrary"),
                                             vmem_limit_bytes=56 * 1024 * 1024),
        name="in_proj",
    )(x2, gain, shift, scale, w_main, b_main, w_small, b_small)


def _compress_kernel(kv_ref, pos_ref, w1_ref, w2_ref, o_ref):
    y = kv_ref[0, 0, 0]
    w1 = w1_ref[0]
    half = y.shape[1]
    nch = y.shape[0]
    z1 = _dot(y, w1[:half])
    z2 = _dot(y, w1[half:])
    posb = _dot(pos_ref[0], w1)[0:1]
    pre = z1 + pltpu.roll(z2, nch - 1, 0) + posb
    hid = pre * _sigmoid(pre)
    o_ref[0, 0, 0] = _dot(hid.astype(BF16), w2_ref[0]).astype(o_ref.dtype)


def _compress(kvf, posf, w1, w2):
    two, b, g, nch, half = kvf.shape
    hidden = w1.shape[2]
    dh = w2.shape[2]
    return pl.pallas_call(
        _compress_kernel,
        grid=(two, b, g),
        in_specs=[pl.BlockSpec((1, 1, 1, nch, half), lambda s, i, j: (s, i, j, 0, 0)),
                  pl.BlockSpec((1, 8, 2 * half), lambda s, i, j: (s, 0, 0)),
                  pl.BlockSpec((1, 2 * half, hidden), lambda s, i, j: (s, 0, 0)),
                  pl.BlockSpec((1, hidden, dh), lambda s, i, j: (s, 0, 0))],
        out_specs=pl.BlockSpec((1, 1, 1, nch, dh), lambda s, i, j: (s, i, j, 0, 0)),
        out_shape=jax.ShapeDtypeStruct((two, b, g, nch, dh), BF16),
        compiler_params=pltpu.CompilerParams(dimension_semantics=("parallel", "parallel", "parallel")),
        name="compress",
    )(kvf, posf, w1, w2)


def _select_blocks(imp, t_row, n_sel):
    nbp = imp.shape[0]
    blk = lax.broadcasted_iota(jnp.int32, (nbp, 1), 0)
    cur = lax.shift_right_logical(t_row, 6)
    forced = (blk == 0) | (blk == cur) | (blk == cur - 1)
    bvalid = blk * SLC_BLOCK <= t_row
    score = jnp.where(forced, 1e30, jnp.where(bvalid, imp, -1.0))
    blk_f = blk.astype(F32)
    sel = jnp.zeros(imp.shape, F32)
    for _ in range(n_sel):
        m = jnp.max(score, axis=0, keepdims=True)
        first = jnp.min(jnp.where(score == m, blk_f, float(nbp)), axis=0, keepdims=True)
        hit = blk_f == first
        sel = jnp.where(hit, 1.0, sel)
        score = jnp.where(hit, -1.0, score)
    return sel


def _attend_tile(k, vt, q, slope_row, base, mask, m_sc, l_sc, acc_sc):
    s = _dot(k, q)
    if mask is not None:
        s = jnp.where(mask, s, NEG)
    d = slope_row * base
    m_prev = m_sc[...]
    m_new = jnp.maximum(m_prev, jnp.max(s, axis=0, keepdims=True) - d)
    alpha = jnp.exp(m_prev - m_new)
    p = jnp.exp(s - (m_new + d))
    l_sc[...] = alpha * l_sc[...] + jnp.sum(p, axis=0, keepdims=True)
    acc_sc[...] = alpha * acc_sc[...] + _dot(vt, p.astype(BF16))
    m_sc[...] = m_new


def _nsa_kernel(slopes_ref, qt_ref, kc_ref, vct_ref, ks_ref, vst_ref, kw_ref, vwt_ref, gt_ref, wimp_ref,
                o_ref, q2_sc, m_sc, l_sc, acc_sc, br_sc, *, tq, nc, n_sel):
    g = pl.program_id(1)
    qi = pl.program_id(2)
    tk = tq
    dh = NSA_HEAD_DIM
    hq = NSA_HPG * tq
    lane = lax.broadcasted_iota(jnp.int32, (1, hq), 1)
    slope_row = jnp.zeros((1, hq), F32)
    for hh in range(NSA_HPG):
        slope_row = jnp.where(lane >= hh * tq, slopes_ref[g * NSA_HPG + hh], slope_row)
    t_one = qi * tq + lax.broadcasted_iota(jnp.int32, (1, tq), 1)
    t_row = jnp.concatenate([t_one] * NSA_HPG, axis=1)
    t_f = t_row.astype(F32)
    j_loc = lax.broadcasted_iota(jnp.int32, (tk, 1), 0)

    def reset_stats():
        m_sc[...] = jnp.full(m_sc.shape, NEG, F32)
        l_sc[...] = jnp.zeros(l_sc.shape, F32)
        acc_sc[...] = jnp.zeros(acc_sc.shape, F32)

    def per_head(a):
        return [a[:, hh * tq:(hh + 1) * tq] for hh in range(NSA_HPG)]

    kc = kc_ref[0, 0]
    ncp = kc.shape[0]
    n_col = lax.broadcasted_iota(jnp.int32, (ncp, 1), 0)
    ce = n_col * CMP_STRIDE + (CMP_BLOCK - 1)
    valid_c = (ce <= t_row) & (n_col < nc)
    s = _dot(kc, qt_ref[0, 0, 0, 0:dh, :]) - slope_row * (t_row - ce).astype(F32)
    sm = jnp.where(valid_c, s, NEG)
    m = jnp.max(sm, axis=0, keepdims=True)
    p = jnp.where(valid_c, jnp.exp(sm - m), 0.0)
    l = jnp.sum(p, axis=0, keepdims=True)
    p = p * (1.0 / jnp.where(l > 0.0, l, 1.0))
    br_sc[0] = _dot(vct_ref[0, 0], p.astype(BF16))
    ps4 = per_head(p)
    psum = (ps4[0] + ps4[1]) + (ps4[2] + ps4[3])
    wimp = wimp_ref[...]
    ph, pm, plo = _split3(psum)
    imp = _dot(wimp, ph) + _dot(wimp, pm) + _dot(wimp, plo)
    sel = _select_blocks(imp, t_one, n_sel)
    penalty = ((sel - 1.0) * SEL_MASK).astype(BF16)
    q2_sc[0:QFEAT, :] = qt_ref[0, 0, 0]
    q2_sc[QFEAT:KSEL_FEAT, :] = jnp.concatenate([penalty] * NSA_HPG, axis=1)

    reset_stats()

    def sel_body(kj, carry):
        base = t_f - (kj * tk).astype(F32)
        _attend_tile(ks_ref[0, 0, kj], vst_ref[0, 0, kj], q2_sc[...], slope_row, base, None, m_sc, l_sc, acc_sc)
        return carry

    lax.fori_loop(0, qi, sel_body, 0)
    diag_base = t_f - (qi * tk).astype(F32)
    causal = (qi * tk + j_loc) <= t_row
    _attend_tile(ks_ref[0, 0, qi], vst_ref[0, 0, qi], q2_sc[...], slope_row, diag_base, causal, m_sc, l_sc, acc_sc)
    br_sc[1] = acc_sc[...] * (1.0 / l_sc[...])

    reset_stats()
    _attend_tile(kw_ref[0, 0, qi], vwt_ref[0, 0, qi], qt_ref[0, 0, 0], slope_row, diag_base, causal,
                 m_sc, l_sc, acc_sc)
    n_back = WINDOW // tk
    for w in range(1, n_back + 1):
        @pl.when(qi >= w)
        def _(w=w):
            kv = qi - w
            base = t_f - (kv * tk).astype(F32)
            inside = (t_row - (kv * tk + j_loc)) < WINDOW if w == n_back else None
            _attend_tile(kw_ref[0, 0, kv], vwt_ref[0, 0, kv], qt_ref[0, 0, 0], slope_row, base, inside,
                         m_sc, l_sc, acc_sc)

    sg = _sigmoid(gt_ref[0, 0])
    gates = [jnp.concatenate([sg[3 * hh + br:3 * hh + br + 1] for hh in range(NSA_HPG)], axis=1)
             for br in range(3)]
    o_all = gates[0] * br_sc[0] + gates[1] * br_sc[1] + gates[2] * (acc_sc[...] * (1.0 / l_sc[...]))
    o_ref[...] = jnp.concatenate(per_head(o_all), axis=0).T.astype(o_ref.dtype)


def _nsa(slopes, qt, kc, vct, ks, vst, kw, vwt, gt, wimp, *, nc, n_sel):
    b, g, nt, qfeat, hq = qt.shape
    hpg = NSA_HPG
    tq = hq // hpg
    t = nt * tq
    ncp = kc.shape[2]
    nbp = wimp.shape[0]
    dh = NSA_HEAD_DIM
    kern = functools.partial(_nsa_kernel, tq=tq, nc=nc, n_sel=n_sel)
    per_group = lambda i, j, k: (i, j, 0, 0, 0)
    return pl.pallas_call(
        kern,
        grid=(b, g, nt),
        in_specs=[pl.BlockSpec(memory_space=pltpu.SMEM),
                  pl.BlockSpec((1, 1, 1, qfeat, hq), lambda i, j, k: (i, j, k, 0, 0)),
                  pl.BlockSpec((1, 1, ncp, dh), lambda i, j, k: (i, j, 0, 0)),
                  pl.BlockSpec((1, 1, dh, ncp), lambda i, j, k: (i, j, 0, 0)),
                  pl.BlockSpec((1, 1, nt, tq, KSEL_FEAT), per_group),
                  pl.BlockSpec((1, 1, nt, dh, tq), per_group),
                  pl.BlockSpec((1, 1, nt, tq, QFEAT), per_group),
                  pl.BlockSpec((1, 1, nt, dh, tq), per_group),
                  pl.BlockSpec((1, 1, 3 * hpg, tq), lambda i, j, k: (i, j, 0, k)),
                  pl.BlockSpec((nbp, ncp), lambda i, j, k: (0, 0))],
        out_specs=pl.BlockSpec((tq, hpg * dh), lambda i, j, k: (i * nt + k, j)),
        out_shape=jax.ShapeDtypeStruct((b * t, g * hpg * dh), BF16),
        scratch_shapes=[pltpu.VMEM((KSEL_FEAT, hq), BF16),
                        pltpu.VMEM((1, hq), F32), pltpu.VMEM((1, hq), F32),
                        pltpu.VMEM((dh, hq), F32),
                        pltpu.VMEM((2, dh, hq), F32)],
        compiler_params=pltpu.CompilerParams(dimension_semantics=("parallel", "parallel", "arbitrary"),
                                             vmem_limit_bytes=48 * 1024 * 1024),
        name="nsa",
    )(slopes, qt, kc, vct, ks, vst, kw, vwt, gt, wimp)


def _gla_intra_scores(q, k, bcum):
    c, dk = q.shape
    sb = GLA_SUB
    row_c = lax.broadcasted_iota(jnp.int32, (c, dk), 0)
    row_s = lax.broadcasted_iota(jnp.int32, (sb, dk), 0)
    lane_s = lax.broadcasted_iota(jnp.int32, (sb, c), 1)
    blocks = []
    for i0 in range(0, c, sb):
        bi = bcum[i0:i0 + sb]
        qi = q[i0:i0 + sb]
        a_blk = jnp.zeros((sb, c), F32)
        for jl in range(sb):
            j = i0 + jl
            w = jnp.exp(jnp.where(row_s >= jl, bi - bcum[j:j + 1], NEG))
            a = jnp.sum(qi * k[j:j + 1] * w, axis=-1, keepdims=True)
            a_blk = jnp.where(lane_s == j, a, a_blk)
        if i0 > 0:
            r = bcum[i0 - 1:i0]
            qh = qi * jnp.exp(bi - r)
            kh = k * jnp.exp(jnp.where(row_c < i0, r - bcum, NEG))
            a_blk = a_blk + _dot_nt(qh.astype(BF16), kh.astype(BF16))
        blocks.append(a_blk)
    return jnp.concatenate(blocks, axis=0)


def _gla_kernel(q_ref, k_ref, v_ref, z_ref, ps_ref, wal_ref, bal_ref, gain_ref, tri_ref, o_ref,
                st_sc, *, ts):
    ti = pl.program_id(2)
    c = GLA_CHUNK

    @pl.when(ti == 0)
    def _():
        st_sc[...] = jnp.zeros(st_sc.shape, F32)

    ps = ps_ref[...]
    ph, pm, _ = _split3(ps)
    wh, wm, _ = _split3(wal_ref[...])
    zz = _dot(ph, wh) + _dot(ph, wm) + _dot(pm, wh) + bal_ref[...]
    log_a = (jnp.minimum(zz, 0.0) - jnp.log(1.0 + jnp.exp(-jnp.abs(zz)))) / GLA_TAU
    tri = tri_ref[...]
    qscale = GLA_KEY_DIM ** -0.5
    for ci in range(ts // c):
        sl = slice(ci * c, (ci + 1) * c)
        gh, gm, gl = _split3(log_a[sl])
        bcum = _dot(tri, gh) + _dot(tri, gm) + _dot(tri, gl)
        q = q_ref[sl, :].astype(F32) * qscale
        k = k_ref[sl, :].astype(F32)
        v = v_ref[sl, :]
        a_mat = _gla_intra_scores(q, k, bcum)
        st = st_sc[...]
        o = _dot(a_mat.astype(BF16), v) + _dot_nt((q * jnp.exp(bcum)).astype(BF16), st.astype(BF16))
        b_last = bcum[c - 1:c, :]
        kd = k * jnp.exp(b_last - bcum)
        st_sc[...] = st * jnp.exp(b_last) + _dot(v.astype(F32).T.astype(BF16), kd.astype(BF16))
        ms = jnp.mean(o * o, axis=-1, keepdims=True)
        y = o * lax.rsqrt(ms + EPS) * gain_ref[...]
        z = z_ref[sl, :].astype(F32)
        o_ref[sl, :] = (y * (z * _sigmoid(z))).astype(o_ref.dtype)


def _gla(pm, ps, wal, bal, gain, tri, *, B, T, ts=256):
    n = pm.shape[0]
    nts = T // ts
    dk, dv = GLA_KEY_DIM, GLA_VAL_DIM
    qb, kb = _NEW["gq"] // dk, _NEW["gk"] // dk
    vb, zb = _NEW["gv"] // dv, _NEW["gla_z"] // dv
    kern = functools.partial(_gla_kernel, ts=ts)
    return pl.pallas_call(
        kern,
        grid=(B, GLA_HEADS, nts),
        in_specs=[pl.BlockSpec((ts, dk), lambda b, h, i: (b * nts + i, qb + h)),
                  pl.BlockSpec((ts, dk), lambda b, h, i: (b * nts + i, kb + h)),
                  pl.BlockSpec((ts, dv), lambda b, h, i: (b * nts + i, vb + h)),
                  pl.BlockSpec((ts, dv), lambda b, h, i: (b * nts + i, zb + h)),
                  pl.BlockSpec((ts, SMALL_WIDTH), lambda b, h, i: (b * nts + i, 0)),
                  pl.BlockSpec((SMALL_WIDTH, dk), lambda b, h, i: (0, h)),
                  pl.BlockSpec((1, dk), lambda b, h, i: (0, h)),
                  pl.BlockSpec((1, dv), lambda b, h, i: (0, 0)),
                  pl.BlockSpec((GLA_CHUNK, GLA_CHUNK), lambda b, h, i: (0, 0))],
        out_specs=pl.BlockSpec((ts, dv), lambda b, h, i: (b * nts + i, h)),
        out_shape=jax.ShapeDtypeStruct((n, GLA_WIDTH), BF16),
        scratch_shapes=[pltpu.VMEM((dv, dk), F32)],
        compiler_params=pltpu.CompilerParams(dimension_semantics=("parallel", "parallel", "arbitrary")),
        name="gla",
    )(pm, pm, pm, pm, ps, wal, bal, gain, tri)


def _out_proj_kernel(x_ref, gate_ref, on_ref, nz_ref, mgn_ref, mgg_ref, yg_ref,
                     wn_ref, wg_ref, wo_ref, fg_ref, o_ref):
    nz = nz_ref[...].astype(F32)
    y_nsa = on_ref[...].astype(F32) * (nz * _sigmoid(nz))
    a = _dot(y_nsa.astype(BF16), wn_ref[...])
    bm = _dot(yg_ref[...], wg_ref[...])
    merged = _sigmoid(mgn_ref[...].astype(F32)) * a + _sigmoid(mgg_ref[...].astype(F32)) * bm
    xn = x_ref[...] + gate_ref[0] * _dot(merged.astype(BF16), wo_ref[...])
    ms = jnp.mean(xn * xn, axis=-1, keepdims=True)
    o_ref[...] = xn * lax.rsqrt(ms + EPS) * fg_ref[...]


def _out_proj(x2, gate, o_nsa, pm, yg, wn, wg, wo, fg, *, T, tm=256):
    n, d = x2.shape
    nt = T // tm
    nzb = _NEW["nsa_z"] // NSA_WIDTH
    mnb = _NEW["mg_nsa"] // D_MODEL
    mgb = _NEW["mg_gla"] // D_MODEL
    row = lambda i: (i, 0)
    const2 = lambda i: (0, 0)
    return pl.pallas_call(
        _out_proj_kernel,
        grid=(n // tm,),
        in_specs=[pl.BlockSpec((tm, d), row),
                  pl.BlockSpec((1, 1, d), lambda i: (i // nt, 0, 0)),
                  pl.BlockSpec((tm, NSA_WIDTH), row),
                  pl.BlockSpec((tm, NSA_WIDTH), lambda i: (i, nzb)),
                  pl.BlockSpec((tm, D_MODEL), lambda i: (i, mnb)),
                  pl.BlockSpec((tm, D_MODEL), lambda i: (i, mgb)),
                  pl.BlockSpec((tm, GLA_WIDTH), row),
                  pl.BlockSpec((NSA_WIDTH, d), const2),
                  pl.BlockSpec((GLA_WIDTH, d), const2),
                  pl.BlockSpec((d, d), const2),
                  pl.BlockSpec((1, d), const2)],
        out_specs=pl.BlockSpec((tm, d), row),
        out_shape=jax.ShapeDtypeStruct((n, d), F32),
        compiler_params=pltpu.CompilerParams(dimension_semantics=("parallel",),
                                             vmem_limit_bytes=60 * 1024 * 1024),
        name="out_proj",
    )(x2, gate, o_nsa, pm, pm, pm, yg, wn, wg, wo, fg)


def _importance_matrix(nbp, ncp):
    ratio = SLC_BLOCK // CMP_STRIDE
    n_sub = CMP_BLOCK // CMP_STRIDE
    w = np.zeros((nbp, ncp), np.float32)
    for blk in range(nbp):
        for m in range(ratio):
            for s in range(n_sub):
                n = ratio * blk + m - s
                if 0 <= n < ncp:
                    w[blk, n] += 1.0
    return w


def _main_cols(a):
    return jnp.concatenate([a[..., _OLD[nm][0]:_OLD[nm][0] + _OLD[nm][1]] for nm in _MAIN_ORDER], axis=-1)


def _small_cols(a):
    pad = SMALL_WIDTH - 3 * NSA_HEADS - GLA_GATE_RANK
    parts = [a[..., _OLD["nsa_g"][0]:_OLD["nsa_g"][0] + 3 * NSA_HEADS],
             a[..., _OLD["ga"][0]:_OLD["ga"][0] + GLA_GATE_RANK],
             jnp.zeros(a.shape[:-1] + (pad,), a.dtype)]
    return jnp.concatenate(parts, axis=-1)


def _slope_terms(slopes):
    s1 = slopes.astype(BF16)
    s2 = (slopes - s1.astype(F32)).astype(BF16)
    s3 = (slopes - s1.astype(F32) - s2.astype(F32)).astype(BF16)
    return jnp.stack([s1, s2, s3], axis=-1)


def _query_features(q, slopes, B, T):
    G, HPG, DH, tq = NSA_KV_GROUPS, NSA_HPG, NSA_HEAD_DIM, NSA_TILE
    nt = T // tq
    qt = (q.reshape(B, nt, tq, G, HPG, DH) * (DH ** -0.5)).astype(BF16).transpose(0, 3, 1, 5, 4, 2)
    terms = _slope_terms(slopes).reshape(G, HPG, N_SLOPE_TERMS).transpose(0, 2, 1)[None, :, None, :, :, None]
    feat = jnp.concatenate([jnp.broadcast_to(terms, (B, G, nt, N_SLOPE_TERMS, HPG, tq)),
                            jnp.zeros((B, G, nt, QFEAT - DH - N_SLOPE_TERMS, HPG, tq), BF16)], axis=3)
    return jnp.concatenate([qt, feat], axis=3).reshape(B, G, nt, QFEAT, HPG * tq)


def _key_features(k, B, T, tk, with_blocks):
    G, DH = NSA_KV_GROUPS, NSA_HEAD_DIM
    kk = k.reshape(B, T, G, DH).transpose(0, 2, 1, 3)
    pos = (jnp.arange(T, dtype=jnp.int32) % tk).astype(BF16)
    cols = [kk, jnp.broadcast_to(pos[None, None, :, None], (B, G, T, N_SLOPE_TERMS)),
            jnp.zeros((B, G, T, QFEAT - DH - N_SLOPE_TERMS), BF16)]
    if with_blocks:
        onehot = (jnp.arange(T, dtype=jnp.int32)[:, None] // SLC_BLOCK
                  == jnp.arange(KSEL_FEAT - QFEAT, dtype=jnp.int32)[None, :]).astype(BF16)
        cols.append(jnp.broadcast_to(onehot[None, None], (B, G, T, KSEL_FEAT - QFEAT)))
    out = jnp.concatenate(cols, axis=-1)
    return out.reshape(B, G, T // tk, tk, out.shape[-1])


def _value_tiles(v, B, T, tk):
    G, DH = NSA_KV_GROUPS, NSA_HEAD_DIM
    return v.reshape(B, T // tk, tk, G, DH).transpose(0, 3, 1, 4, 2)


def kernel(x, c, w_ada, b_ada, norm_gain, w_in, b_in, cmp_pos_k, cmp_pos_v, cmp_w1_k, cmp_w2_k, cmp_w1_v, cmp_w2_v,
           gla_w_alpha, gla_b_alpha, gla_norm_gain, w_br_nsa, w_br_gla, w_out, final_norm_gain):
    assert DEPTH == 1, "the final rmsnorm is fused into the single layer's output kernel"
    B, T, D = x.shape
    G, HPG, DH = NSA_KV_GROUPS, NSA_HPG, NSA_HEAD_DIM
    N = B * T
    nch = T // CMP_STRIDE
    nc = nch - CMP_BLOCK // CMP_STRIDE + 1
    nb = T // SLC_BLOCK
    nbp = KSEL_FEAT - QFEAT
    assert nb <= nbp and T % NSA_TILE == 0
    n_sel = min(N_SELECT, nb)
    tk = NSA_TILE
    l = 0

    slopes = 2.0 ** (-8.0 * jnp.arange(1, NSA_HEADS + 1, dtype=F32) / NSA_HEADS)
    wimp = jnp.asarray(_importance_matrix(nbp, nch), BF16)
    tri = jnp.asarray(np.tril(np.ones((GLA_CHUNK, GLA_CHUNK), np.float32)), BF16)
    bp = ((B + 7) // 8) * 8
    c8 = jnp.zeros((bp, D), F32).at[:B].set(c)
    x2 = x.reshape(N, D)

    mod = _ada(c8, w_ada[l], b_ada[l][None, :])
    shift = mod[:B, :D].reshape(B, 1, D)
    scale = mod[:B, D:2 * D].reshape(B, 1, D)
    gate = mod[:B, 2 * D:].reshape(B, 1, D)

    pm, ps = _in_proj(x2, norm_gain[l][None, :], shift, scale,
                      _main_cols(w_in[l]).astype(BF16), _main_cols(b_in[l])[None, :],
                      _small_cols(w_in[l]).astype(BF16), _small_cols(b_in[l])[None, :], T=T)

    def cols(name, width=NSA_KV_WIDTH):
        return pm[:, _NEW[name]:_NEW[name] + width]

    kvf = cols("ck", 2 * NSA_KV_WIDTH).reshape(B, nch, CMP_STRIDE, 2, G, DH)
    kvf = kvf.transpose(3, 0, 4, 1, 2, 5).reshape(2, B, G, nch, CMP_STRIDE * DH)
    posf = jnp.stack([cmp_pos_k[l], cmp_pos_v[l]]).reshape(2, 1, CMP_BLOCK * DH)
    posf = jnp.broadcast_to(posf, (2, 8, CMP_BLOCK * DH)).astype(BF16)
    w1 = jnp.stack([cmp_w1_k[l], cmp_w1_v[l]]).astype(BF16)
    w2 = jnp.stack([cmp_w2_k[l], cmp_w2_v[l]]).astype(BF16)
    kvc = _compress(kvf, posf, w1, w2)
    gt = ps[:, :3 * NSA_HEADS].reshape(B, T, G, 3 * HPG).transpose(0, 2, 3, 1)
    o_nsa = _nsa(slopes, _query_features(cols("nsa_q", NSA_WIDTH), slopes, B, T),
                 kvc[0], kvc[1].transpose(0, 1, 3, 2),
                 _key_features(cols("sk"), B, T, tk, True), _value_tiles(cols("sv"), B, T, tk),
                 _key_features(cols("wk"), B, T, tk, False), _value_tiles(cols("wv"), B, T, tk),
                 gt, wimp, nc=nc, n_sel=n_sel)

    wal = jnp.zeros((SMALL_WIDTH, GLA_KEY_WIDTH), F32).at[GA_LANE:GA_LANE + GLA_GATE_RANK].set(gla_w_alpha[l])
    y_gla = _gla(pm, ps, wal, gla_b_alpha[l][None, :], gla_norm_gain[l][None, :], tri, B=B, T=T)

    out = _out_proj(x2, gate, o_nsa, pm, y_gla,
                    w_br_nsa[l].astype(BF16), w_br_gla[l].astype(BF16), w_out[l].astype(BF16),
                    final_norm_gain[None, :], T=T)
    return out.reshape(B, T, D)
```

```python
import functools

import numpy as np
import jax
import jax.numpy as jnp
from jax import lax
from jax.experimental import pallas as pl
from jax.experimental.pallas import tpu as pltpu

D_MODEL = 2048
DEPTH = 1
NSA_HEADS = 16
NSA_KV_GROUPS = 4
NSA_HPG = NSA_HEADS // NSA_KV_GROUPS
NSA_HEAD_DIM = 64
CMP_BLOCK = 32
CMP_STRIDE = 16
CMP_HIDDEN = 256
SLC_BLOCK = 64
N_SELECT = 16
WINDOW = 512
NSA_WIDTH = NSA_HEADS * NSA_HEAD_DIM
NSA_KV_WIDTH = NSA_KV_GROUPS * NSA_HEAD_DIM
GLA_HEADS = 4
GLA_KEY_DIM = 128
GLA_VAL_DIM = 256
GLA_GATE_RANK = 16
GLA_TAU = 16.0
GLA_KEY_WIDTH = GLA_HEADS * GLA_KEY_DIM
GLA_WIDTH = GLA_HEADS * GLA_VAL_DIM
EPS = 1e-6
NEG = -1e30

F32 = jnp.float32
BF16 = jnp.bfloat16

_OLD = {}
_off = 0
for _name, _w in (("nsa_q", NSA_WIDTH), ("ck", NSA_KV_WIDTH), ("cv", NSA_KV_WIDTH), ("sk", NSA_KV_WIDTH),
                  ("sv", NSA_KV_WIDTH), ("wk", NSA_KV_WIDTH), ("wv", NSA_KV_WIDTH), ("nsa_g", 3 * NSA_HEADS),
                  ("nsa_z", NSA_WIDTH), ("gq", GLA_KEY_WIDTH), ("gk", GLA_KEY_WIDTH), ("gv", GLA_WIDTH),
                  ("ga", GLA_GATE_RANK), ("gla_z", GLA_WIDTH), ("mg_nsa", D_MODEL), ("mg_gla", D_MODEL)):
    _OLD[_name] = (_off, _w)
    _off += _w

_MAIN_ORDER = ("nsa_q", "nsa_z", "gla_z", "gv", "mg_nsa", "mg_gla", "gq", "gk", "ck", "cv", "sk", "sv", "wk", "wv")
_NEW = {}
_off = 0
for _name in _MAIN_ORDER:
    _NEW[_name] = _off
    _off += _OLD[_name][1]
MAIN_WIDTH = _off
SMALL_WIDTH = 128
GA_LANE = 3 * NSA_HEADS

LANE = 128
GLA_CHUNK = 64
GLA_SUB = 16
NSA_TILE = 256
QFEAT = 2 * NSA_HEAD_DIM
KSEL_FEAT = QFEAT + LANE
N_SLOPE_TERMS = 3
SEL_MASK = 16384.0
LOG2E = 1.4426950408889634
Q_SCALE = LOG2E * NSA_HEAD_DIM ** -0.5


def _dot(a, b):
    return jnp.dot(a, b, preferred_element_type=F32)


def _dot_nt(a, b):
    return lax.dot_general(a, b, (((1,), (1,)), ((), ())), preferred_element_type=F32)


def _split3(x):
    hi = x.astype(BF16)
    r = x - hi.astype(F32)
    mid = r.astype(BF16)
    lo = (r - mid.astype(F32)).astype(BF16)
    return hi, mid, lo


def _sigmoid(x):
    return 1.0 / (1.0 + jnp.exp(-x))


def _ada_kernel(c_ref, w_ref, b_ref, o_ref):
    ch, cm, cl = _split3(c_ref[...])
    wh, wm, wl = _split3(w_ref[...])
    acc = _dot(ch, wh) + _dot(ch, wm) + _dot(cm, wh) + _dot(ch, wl) + _dot(cl, wh) + _dot(cm, wm)
    o_ref[...] = acc + b_ref[...]


def _ada(c8, w, b, *, tn=768):
    m, d = c8.shape
    n = w.shape[1]
    return pl.pallas_call(
        _ada_kernel,
        grid=(n // tn,),
        in_specs=[pl.BlockSpec((m, d), lambda j: (0, 0)),
                  pl.BlockSpec((d, tn), lambda j: (0, j)),
                  pl.BlockSpec((1, tn), lambda j: (0, j))],
        out_specs=pl.BlockSpec((m, tn), lambda j: (0, j)),
        out_shape=jax.ShapeDtypeStruct((m, n), F32),
        compiler_params=pltpu.CompilerParams(dimension_semantics=("arbitrary",),
                                             vmem_limit_bytes=40 * 1024 * 1024),
        name="ada",
    )(c8, w, b)


def _in_proj_kernel(x_ref, gain_ref, shift_ref, scale_ref, w_ref, b_ref, ws_ref, bs_ref,
                    o_ref, os_ref, h_ref):
    j = pl.program_id(1)

    @pl.when(j == 0)
    def _():
        x = x_ref[...]
        ms = jnp.mean(x * x, axis=-1, keepdims=True)
        y = x * lax.rsqrt(ms + EPS) * gain_ref[...]
        y = y * (1.0 + scale_ref[0]) + shift_ref[0]
        h = y.astype(BF16)
        h_ref[...] = h
        os_ref[...] = _dot(h, ws_ref[...]) + bs_ref[...]

    o_ref[...] = (_dot(h_ref[...], w_ref[...]) + b_ref[...]).astype(o_ref.dtype)


def _in_proj(x2, gain, shift, scale, w_main, b_main, w_small, b_small, *, T, tm=1024, tn=1536):
    n, d = x2.shape
    nm = w_main.shape[1]
    nt = T // tm
    return pl.pallas_call(
        _in_proj_kernel,
        grid=(n // tm, nm // tn),
        in_specs=[pl.BlockSpec((tm, d), lambda i, j: (i, 0)),
                  pl.BlockSpec((1, d), lambda i, j: (0, 0)),
                  pl.BlockSpec((1, 1, d), lambda i, j: (i // nt, 0, 0)),
                  pl.BlockSpec((1, 1, d), lambda i, j: (i // nt, 0, 0)),
                  pl.BlockSpec((d, tn), lambda i, j: (0, j)),
                  pl.BlockSpec((1, tn), lambda i, j: (0, j)),
                  pl.BlockSpec((d, SMALL_WIDTH), lambda i, j: (0, 0)),
                  pl.BlockSpec((1, SMALL_WIDTH), lambda i, j: (0, 0))],
        out_specs=[pl.BlockSpec((tm, tn), lambda i, j: (i, j)),
                   pl.BlockSpec((tm, SMALL_WIDTH), lambda i, j: (i, 0))],
        out_shape=[jax.ShapeDtypeStruct((n, nm), BF16),
                   jax.ShapeDtypeStruct((n, SMALL_WIDTH), F32)],
        scratch_shapes=[pltpu.VMEM((tm, d), BF16)],
        compiler_params=pltpu.CompilerParams(dimension_semantics=("parallel", "arbitrary"),
                                             vmem_limit_bytes=56 * 1024 * 1024),
        name="in_proj",
    )(x2, gain, shift, scale, w_main, b_main, w_small, b_small)


def _compress_kernel(kv_ref, pos_ref, w1_ref, w2_ref, o_ref):
    y = kv_ref[0, 0, 0]
    w1 = w1_ref[0]
    half = y.shape[1]
    nch = y.shape[0]
    z1 = _dot(y, w1[:half])
    z2 = _dot(y, w1[half:])
    posb = _dot(pos_ref[0], w1)[0:1]
    pre = z1 + pltpu.roll(z2, nch - 1, 0) + posb
    hid = pre * _sigmoid(pre)
    o_ref[0, 0, 0] = _dot(hid.astype(BF16), w2_ref[0]).astype(o_ref.dtype)


def _compress(kvf, posf, w1, w2):
    two, b, g, nch, half = kvf.shape
    hidden = w1.shape[2]
    dh = w2.shape[2]
    return pl.pallas_call(
        _compress_kernel,
        grid=(two, b, g),
        in_specs=[pl.BlockSpec((1, 1, 1, nch, half), lambda s, i, j: (s, i, j, 0, 0)),
                  pl.BlockSpec((1, 8, 2 * half), lambda s, i, j: (s, 0, 0)),
                  pl.BlockSpec((1, 2 * half, hidden), lambda s, i, j: (s, 0, 0)),
                  pl.BlockSpec((1, hidden, dh), lambda s, i, j: (s, 0, 0))],
        out_specs=pl.BlockSpec((1, 1, 1, nch, dh), lambda s, i, j: (s, i, j, 0, 0)),
        out_shape=jax.ShapeDtypeStruct((two, b, g, nch, dh), BF16),
        compiler_params=pltpu.CompilerParams(dimension_semantics=("parallel", "parallel", "parallel")),
        name="compress",
    )(kvf, posf, w1, w2)


def _select_blocks(imp, t_row, n_sel):
    nbp = imp.shape[0]
    blk = lax.broadcasted_iota(jnp.int32, (nbp, 1), 0)
    cur = lax.shift_right_logical(t_row, 6)
    forced = (blk == 0) | (blk == cur) | (blk == cur - 1)
    bvalid = blk * SLC_BLOCK <= t_row
    score = jnp.where(forced, 1e30, jnp.where(bvalid, imp, -1.0))
    blk_f = blk.astype(F32)
    sel = jnp.zeros(imp.shape, F32)
    for _ in range(n_sel):
        m = jnp.max(score, axis=0, keepdims=True)
        first = jnp.min(jnp.where(score == m, blk_f, float(nbp)), axis=0, keepdims=True)
        hit = blk_f == first
        sel = jnp.where(hit, 1.0, sel)
        score = jnp.where(hit, -1.0, score)
    return sel


def _softmax_step(s, vt, d, m_sc, l_sc, acc_sc):
    m_prev = m_sc[...]
    m_new = jnp.maximum(m_prev, jnp.max(s, axis=0, keepdims=True) - d)
    alpha = jnp.exp2(m_prev - m_new)
    p = jnp.exp2(s - (m_new + d))
    l_sc[...] = alpha * l_sc[...] + jnp.sum(p, axis=0, keepdims=True)
    acc_sc[...] = alpha * acc_sc[...] + _dot(vt, p.astype(BF16))
    m_sc[...] = m_new


def _nsa_kernel(slopes_ref, qt_ref, kc_ref, vct_ref, ks_ref, vst_ref, kw_ref, vwt_ref, gt_ref, wimp_ref,
                o_ref, q2_sc, sa_sc, sb_sc, m_sc, l_sc, acc_sc, br_sc, list_sc, *, tq, nt, nc, n_sel):
    g = pl.program_id(1)
    qi = pl.program_id(2)
    tk = tq
    dh = NSA_HEAD_DIM
    hq = NSA_HPG * tq
    lane = lax.broadcasted_iota(jnp.int32, (1, hq), 1)
    slope_row = jnp.zeros((1, hq), F32)
    for hh in range(NSA_HPG):
        slope_row = jnp.where(lane >= hh * tq, slopes_ref[g * NSA_HPG + hh], slope_row)
    t_one = qi * tq + lax.broadcasted_iota(jnp.int32, (1, tq), 1)
    t_row = jnp.concatenate([t_one] * NSA_HPG, axis=1)
    t_f = t_row.astype(F32)
    j_loc = lax.broadcasted_iota(jnp.int32, (tk, 1), 0)

    def reset_stats():
        m_sc[...] = jnp.full(m_sc.shape, NEG, F32)
        l_sc[...] = jnp.zeros(l_sc.shape, F32)
        acc_sc[...] = jnp.zeros(acc_sc.shape, F32)

    def per_head(a):
        return [a[:, hh * tq:(hh + 1) * tq] for hh in range(NSA_HPG)]

    kc = kc_ref[0, 0]
    ncp = kc.shape[0]
    n_col = lax.broadcasted_iota(jnp.int32, (ncp, 1), 0)
    ce = n_col * CMP_STRIDE + (CMP_BLOCK - 1)
    valid_c = (ce <= t_row) & (n_col < nc)
    s = _dot(kc, qt_ref[0, 0, 0, 0:dh, :]) - slope_row * (t_row - ce).astype(F32)
    sm = jnp.where(valid_c, s, NEG)
    m = jnp.max(sm, axis=0, keepdims=True)
    p = jnp.where(valid_c, jnp.exp2(sm - m), 0.0)
    l = jnp.sum(p, axis=0, keepdims=True)
    p = p * (1.0 / jnp.where(l > 0.0, l, 1.0))
    br_sc[0] = _dot(vct_ref[0, 0], p.astype(BF16))
    ps4 = per_head(p)
    psum = (ps4[0] + ps4[1]) + (ps4[2] + ps4[3])
    wimp = wimp_ref[...]
    ph, pm, plo = _split3(psum)
    imp = _dot(wimp, ph) + _dot(wimp, pm) + _dot(wimp, plo)

    def shift_of(kv, off=None):
        d = slope_row * (t_f - (kv * tk).astype(F32))
        return d if off is None else d + jnp.where(off, -NEG, 0.0)

    reset_stats()
    n_back = WINDOW // tk
    qw = qt_ref[0, 0, 0]
    tiles = [jnp.maximum(qi - w, 0) for w in range(n_back + 1)]
    logits = [_dot(kw_ref[0, 0, kv], qw) for kv in tiles]
    for w, (kv, s) in enumerate(zip(tiles, logits)):
        if w == 0:
            s = jnp.where((kv * tk + j_loc) <= t_row, s, NEG)
        elif w == n_back:
            s = jnp.where((t_row - (kv * tk + j_loc)) < WINDOW, s, NEG)
        _softmax_step(s, vwt_ref[0, 0, kv], shift_of(kv, qi < w), m_sc, l_sc, acc_sc)
    br_sc[1] = acc_sc[...] * (1.0 / l_sc[...])

    sel = _select_blocks(imp, t_one, n_sel)
    penalty = ((sel - 1.0) * SEL_MASK).astype(BF16)
    q2_sc[0:QFEAT, :] = qt_ref[0, 0, 0]
    q2_sc[QFEAT:KSEL_FEAT, :] = jnp.concatenate([penalty] * NSA_HPG, axis=1)

    per_tile = tk // SLC_BLOCK
    any_tok = jnp.max(sel, axis=1, keepdims=True)
    for kt in range(nt + 2):
        list_sc[kt] = qi
    n_act = jnp.int32(0)
    for kt in range(nt):
        list_sc[n_act] = kt
        hit = jnp.max(any_tok[kt * per_tile:(kt + 1) * per_tile]) > 0.5
        n_act = n_act + hit.astype(jnp.int32)
    list_sc[n_act] = qi
    n_plain = n_act - 1

    reset_stats()

    def sel_logits(idx, s_ref):
        s_ref[...] = _dot(ks_ref[0, 0, list_sc[idx]], q2_sc[...])

    def sel_update(idx, s_ref):
        kv = list_sc[idx]
        _softmax_step(s_ref[...], vst_ref[0, 0, kv], shift_of(kv, idx >= n_plain), m_sc, l_sc, acc_sc)

    sel_logits(0, sa_sc)

    def pair_body(i, carry):
        sel_logits(2 * i + 1, sb_sc)
        sel_update(2 * i, sa_sc)
        sel_logits(2 * i + 2, sa_sc)
        sel_update(2 * i + 1, sb_sc)
        return carry

    lax.fori_loop(0, lax.shift_right_logical(n_plain + 1, 1), pair_body, 0)
    causal = (qi * tk + j_loc) <= t_row
    _softmax_step(jnp.where(causal, sa_sc[...], NEG), vst_ref[0, 0, qi], shift_of(qi), m_sc, l_sc, acc_sc)
    o_sel = acc_sc[...] * (1.0 / l_sc[...])

    sg = _sigmoid(gt_ref[0, 0])
    gates = [jnp.concatenate([sg[3 * hh + br:3 * hh + br + 1] for hh in range(NSA_HPG)], axis=1)
             for br in range(3)]
    o_all = gates[0] * br_sc[0] + gates[1] * o_sel + gates[2] * br_sc[1]
    o_ref[...] = jnp.concatenate(per_head(o_all), axis=0).T.astype(o_ref.dtype)


def _nsa(slopes, qt, kc, vct, ks, vst, kw, vwt, gt, wimp, *, nc, n_sel):
    b, g, nt, qfeat, hq = qt.shape
    hpg = NSA_HPG
    tq = hq // hpg
    t = nt * tq
    ncp = kc.shape[2]
    nbp = wimp.shape[0]
    dh = NSA_HEAD_DIM
    kern = functools.partial(_nsa_kernel, tq=tq, nt=nt, nc=nc, n_sel=n_sel)
    per_group = lambda i, j, k: (i, j, 0, 0, 0)
    return pl.pallas_call(
        kern,
        grid=(b, g, nt),
        in_specs=[pl.BlockSpec(memory_space=pltpu.SMEM),
                  pl.BlockSpec((1, 1, 1, qfeat, hq), lambda i, j, k: (i, j, k, 0, 0)),
                  pl.BlockSpec((1, 1, ncp, dh), lambda i, j, k: (i, j, 0, 0)),
                  pl.BlockSpec((1, 1, dh, ncp), lambda i, j, k: (i, j, 0, 0)),
                  pl.BlockSpec((1, 1, nt, tq, KSEL_FEAT), per_group),
                  pl.BlockSpec((1, 1, nt, dh, tq), per_group),
                  pl.BlockSpec((1, 1, nt, tq, QFEAT), per_group),
                  pl.BlockSpec((1, 1, nt, dh, tq), per_group),
                  pl.BlockSpec((1, 1, 3 * hpg, tq), lambda i, j, k: (i, j, 0, k)),
                  pl.BlockSpec((nbp, ncp), lambda i, j, k: (0, 0))],
        out_specs=pl.BlockSpec((tq, hpg * dh), lambda i, j, k: (i * nt + k, j)),
        out_shape=jax.ShapeDtypeStruct((b * t, g * hpg * dh), BF16),
        scratch_shapes=[pltpu.VMEM((KSEL_FEAT, hq), BF16),
                        pltpu.VMEM((tq, hq), F32), pltpu.VMEM((tq, hq), F32),
                        pltpu.VMEM((1, hq), F32), pltpu.VMEM((1, hq), F32),
                        pltpu.VMEM((dh, hq), F32),
                        pltpu.VMEM((2, dh, hq), F32),
                        pltpu.SMEM((nt + 2,), jnp.int32)],
        compiler_params=pltpu.CompilerParams(dimension_semantics=("parallel", "parallel", "arbitrary"),
                                             vmem_limit_bytes=48 * 1024 * 1024),
        name="nsa",
    )(slopes, qt, kc, vct, ks, vst, kw, vwt, gt, wimp)


def _gla_intra_scores(q, k, bcum):
    c, dk = q.shape
    sb = GLA_SUB
    row_c = lax.broadcasted_iota(jnp.int32, (c, dk), 0)
    row_s = lax.broadcasted_iota(jnp.int32, (sb, dk), 0)
    lane_s = lax.broadcasted_iota(jnp.int32, (sb, c), 1)
    blocks = []
    for i0 in range(0, c, sb):
        bi = bcum[i0:i0 + sb]
        qi = q[i0:i0 + sb]
        a_blk = jnp.zeros((sb, c), F32)
        for jl in range(sb):
            j = i0 + jl
            w = jnp.exp(jnp.where(row_s >= jl, bi - bcum[j:j + 1], NEG))
            a = jnp.sum(qi * k[j:j + 1] * w, axis=-1, keepdims=True)
            a_blk = jnp.where(lane_s == j, a, a_blk)
        if i0 > 0:
            r = bcum[i0 - 1:i0]
            qh = qi * jnp.exp(bi - r)
            kh = k * jnp.exp(jnp.where(row_c < i0, r - bcum, NEG))
            a_blk = a_blk + _dot_nt(qh.astype(BF16), kh.astype(BF16))
        blocks.append(a_blk)
    return jnp.concatenate(blocks, axis=0)


def _gla_kernel(q_ref, k_ref, v_ref, z_ref, ps_ref, wal_ref, bal_ref, gain_ref, tri_ref, o_ref,
                st_sc, *, ts):
    ti = pl.program_id(2)
    c = GLA_CHUNK

    @pl.when(ti == 0)
    def _():
        st_sc[...] = jnp.zeros(st_sc.shape, F32)

    ps = ps_ref[...]
    ph, pm, _ = _split3(ps)
    wh, wm, _ = _split3(wal_ref[...])
    zz = _dot(ph, wh) + _dot(ph, wm) + _dot(pm, wh) + bal_ref[...]
    log_a = (jnp.minimum(zz, 0.0) - jnp.log(1.0 + jnp.exp(-jnp.abs(zz)))) / GLA_TAU
    tri = tri_ref[...]
    qscale = GLA_KEY_DIM ** -0.5
    for ci in range(ts // c):
        sl = slice(ci * c, (ci + 1) * c)
        gh, gm, gl = _split3(log_a[sl])
        bcum = _dot(tri, gh) + _dot(tri, gm) + _dot(tri, gl)
        q = q_ref[sl, :].astype(F32) * qscale
        k = k_ref[sl, :].astype(F32)
        v = v_ref[sl, :]
        a_mat = _gla_intra_scores(q, k, bcum)
        st = st_sc[...]
        o = _dot(a_mat.astype(BF16), v) + _dot_nt((q * jnp.exp(bcum)).astype(BF16), st.astype(BF16))
        b_last = bcum[c - 1:c, :]
        kd = k * jnp.exp(b_last - bcum)
        st_sc[...] = st * jnp.exp(b_last) + _dot(v.astype(F32).T.astype(BF16), kd.astype(BF16))
        ms = jnp.mean(o * o, axis=-1, keepdims=True)
        y = o * lax.rsqrt(ms + EPS) * gain_ref[...]
        z = z_ref[sl, :].astype(F32)
        o_ref[sl, :] = (y * (z * _sigmoid(z))).astype(o_ref.dtype)


def _gla(pm, ps, wal, bal, gain, tri, *, B, T, ts=256):
    n = pm.shape[0]
    nts = T // ts
    dk, dv = GLA_KEY_DIM, GLA_VAL_DIM
    qb, kb = _NEW["gq"] // dk, _NEW["gk"] // dk
    vb, zb = _NEW["gv"] // dv, _NEW["gla_z"] // dv
    kern = functools.partial(_gla_kernel, ts=ts)
    return pl.pallas_call(
        kern,
        grid=(B, GLA_HEADS, nts),
        in_specs=[pl.BlockSpec((ts, dk), lambda b, h, i: (b * nts + i, qb + h)),
                  pl.BlockSpec((ts, dk), lambda b, h, i: (b * nts + i, kb + h)),
                  pl.BlockSpec((ts, dv), lambda b, h, i: (b * nts + i, vb + h)),
                  pl.BlockSpec((ts, dv), lambda b, h, i: (b * nts + i, zb + h)),
                  pl.BlockSpec((ts, SMALL_WIDTH), lambda b, h, i: (b * nts + i, 0)),
                  pl.BlockSpec((SMALL_WIDTH, dk), lambda b, h, i: (0, h)),
                  pl.BlockSpec((1, dk), lambda b, h, i: (0, h)),
                  pl.BlockSpec((1, dv), lambda b, h, i: (0, 0)),
                  pl.BlockSpec((GLA_CHUNK, GLA_CHUNK), lambda b, h, i: (0, 0))],
        out_specs=pl.BlockSpec((ts, dv), lambda b, h, i: (b * nts + i, h)),
        out_shape=jax.ShapeDtypeStruct((n, GLA_WIDTH), BF16),
        scratch_shapes=[pltpu.VMEM((dv, dk), F32)],
        compiler_params=pltpu.CompilerParams(dimension_semantics=("parallel", "parallel", "arbitrary")),
        name="gla",
    )(pm, pm, pm, pm, ps, wal, bal, gain, tri)


def _out_proj_kernel(x_ref, gate_ref, on_ref, nz_ref, mgn_ref, mgg_ref, yg_ref,
                     wn_ref, wg_ref, wo_ref, fg_ref, o_ref):
    nz = nz_ref[...].astype(F32)
    y_nsa = on_ref[...].astype(F32) * (nz * _sigmoid(nz))
    a = _dot(y_nsa.astype(BF16), wn_ref[...])
    bm = _dot(yg_ref[...], wg_ref[...])
    merged = _sigmoid(mgn_ref[...].astype(F32)) * a + _sigmoid(mgg_ref[...].astype(F32)) * bm
    xn = x_ref[...] + gate_ref[0] * _dot(merged.astype(BF16), wo_ref[...])
    ms = jnp.mean(xn * xn, axis=-1, keepdims=True)
    o_ref[...] = xn * lax.rsqrt(ms + EPS) * fg_ref[...]


def _out_proj(x2, gate, o_nsa, pm, yg, wn, wg, wo, fg, *, T, tm=256):
    n, d = x2.shape
    nt = T // tm
    nzb = _NEW["nsa_z"] // NSA_WIDTH
    mnb = _NEW["mg_nsa"] // D_MODEL
    mgb = _NEW["mg_gla"] // D_MODEL
    row = lambda i: (i, 0)
    const2 = lambda i: (0, 0)
    return pl.pallas_call(
        _out_proj_kernel,
        grid=(n // tm,),
        in_specs=[pl.BlockSpec((tm, d), row),
                  pl.BlockSpec((1, 1, d), lambda i: (i // nt, 0, 0)),
                  pl.BlockSpec((tm, NSA_WIDTH), row),
                  pl.BlockSpec((tm, NSA_WIDTH), lambda i: (i, nzb)),
                  pl.BlockSpec((tm, D_MODEL), lambda i: (i, mnb)),
                  pl.BlockSpec((tm, D_MODEL), lambda i: (i, mgb)),
                  pl.BlockSpec((tm, GLA_WIDTH), row),
                  pl.BlockSpec((NSA_WIDTH, d), const2),
                  pl.BlockSpec((GLA_WIDTH, d), const2),
                  pl.BlockSpec((d, d), const2),
                  pl.BlockSpec((1, d), const2)],
        out_specs=pl.BlockSpec((tm, d), row),
        out_shape=jax.ShapeDtypeStruct((n, d), F32),
        compiler_params=pltpu.CompilerParams(dimension_semantics=("parallel",),
                                             vmem_limit_bytes=60 * 1024 * 1024),
        name="out_proj",
    )(x2, gate, o_nsa, pm, pm, pm, yg, wn, wg, wo, fg)


def _importance_matrix(nbp, ncp):
    ratio = SLC_BLOCK // CMP_STRIDE
    n_sub = CMP_BLOCK // CMP_STRIDE
    w = np.zeros((nbp, ncp), np.float32)
    for blk in range(nbp):
        for m in range(ratio):
            for s in range(n_sub):
                n = ratio * blk + m - s
                if 0 <= n < ncp:
                    w[blk, n] += 1.0
    return w


def _main_cols(a):
    parts = [a[..., _OLD[nm][0]:_OLD[nm][0] + _OLD[nm][1]] * (Q_SCALE if nm == "nsa_q" else 1.0)
             for nm in _MAIN_ORDER]
    return jnp.concatenate(parts, axis=-1)


def _small_cols(a):
    pad = SMALL_WIDTH - 3 * NSA_HEADS - GLA_GATE_RANK
    parts = [a[..., _OLD["nsa_g"][0]:_OLD["nsa_g"][0] + 3 * NSA_HEADS],
             a[..., _OLD["ga"][0]:_OLD["ga"][0] + GLA_GATE_RANK],
             jnp.zeros(a.shape[:-1] + (pad,), a.dtype)]
    return jnp.concatenate(parts, axis=-1)


def _slope_terms(slopes):
    s1 = slopes.astype(BF16)
    s2 = (slopes - s1.astype(F32)).astype(BF16)
    s3 = (slopes - s1.astype(F32) - s2.astype(F32)).astype(BF16)
    return jnp.stack([s1, s2, s3], axis=-1)


def _query_features(q, slopes, B, T):
    G, HPG, DH, tq = NSA_KV_GROUPS, NSA_HPG, NSA_HEAD_DIM, NSA_TILE
    nt = T // tq
    qt = q.reshape(B, nt, tq, G, HPG, DH).transpose(0, 3, 1, 5, 4, 2)
    terms = _slope_terms(slopes).reshape(G, HPG, N_SLOPE_TERMS).transpose(0, 2, 1)[None, :, None, :, :, None]
    feat = jnp.concatenate([jnp.broadcast_to(terms, (B, G, nt, N_SLOPE_TERMS, HPG, tq)),
                            jnp.zeros((B, G, nt, QFEAT - DH - N_SLOPE_TERMS, HPG, tq), BF16)], axis=3)
    return jnp.concatenate([qt, feat], axis=3).reshape(B, G, nt, QFEAT, HPG * tq)


def _key_features(k, B, T, tk, with_blocks):
    G, DH = NSA_KV_GROUPS, NSA_HEAD_DIM
    kk = k.reshape(B, T, G, DH).transpose(0, 2, 1, 3)
    pos = (jnp.arange(T, dtype=jnp.int32) % tk).astype(BF16)
    cols = [kk, jnp.broadcast_to(pos[None, None, :, None], (B, G, T, N_SLOPE_TERMS)),
            jnp.zeros((B, G, T, QFEAT - DH - N_SLOPE_TERMS), BF16)]
    if with_blocks:
        onehot = (jnp.arange(T, dtype=jnp.int32)[:, None] // SLC_BLOCK
                  == jnp.arange(KSEL_FEAT - QFEAT, dtype=jnp.int32)[None, :]).astype(BF16)
        cols.append(jnp.broadcast_to(onehot[None, None], (B, G, T, KSEL_FEAT - QFEAT)))
    out = jnp.concatenate(cols, axis=-1)
    return out.reshape(B, G, T // tk, tk, out.shape[-1])


def _value_tiles(v, B, T, tk):
    G, DH = NSA_KV_GROUPS, NSA_HEAD_DIM
    return v.reshape(B, T // tk, tk, G, DH).transpose(0, 3, 1, 4, 2)


def kernel(x, c, w_ada, b_ada, norm_gain, w_in, b_in, cmp_pos_k, cmp_pos_v, cmp_w1_k, cmp_w2_k, cmp_w1_v, cmp_w2_v,
           gla_w_alpha, gla_b_alpha, gla_norm_gain, w_br_nsa, w_br_gla, w_out, final_norm_gain):
    assert DEPTH == 1, "the final rmsnorm is fused into the single layer's output kernel"
    B, T, D = x.shape
    G, HPG, DH = NSA_KV_GROUPS, NSA_HPG, NSA_HEAD_DIM
    N = B * T
    nch = T // CMP_STRIDE
    nc = nch - CMP_BLOCK // CMP_STRIDE + 1
    nb = T // SLC_BLOCK
    nbp = KSEL_FEAT - QFEAT
    assert nb <= nbp and T % NSA_TILE == 0
    n_sel = min(N_SELECT, nb)
    tk = NSA_TILE
    l = 0

    slopes = 2.0 ** (-8.0 * jnp.arange(1, NSA_HEADS + 1, dtype=F32) / NSA_HEADS)
    wimp = jnp.asarray(_importance_matrix(nbp, nch), BF16)
    tri = jnp.asarray(np.tril(np.ones((GLA_CHUNK, GLA_CHUNK), np.float32)), BF16)
    bp = ((B + 7) // 8) * 8
    c8 = jnp.zeros((bp, D), F32).at[:B].set(c)
    x2 = x.reshape(N, D)

    mod = _ada(c8, w_ada[l], b_ada[l][None, :])
    shift = mod[:B, :D].reshape(B, 1, D)
    scale = mod[:B, D:2 * D].reshape(B, 1, D)
    gate = mod[:B, 2 * D:].reshape(B, 1, D)

    pm, ps = _in_proj(x2, norm_gain[l][None, :], shift, scale,
                      _main_cols(w_in[l]).astype(BF16), _main_cols(b_in[l])[None, :],
                      _small_cols(w_in[l]).astype(BF16), _small_cols(b_in[l])[None, :], T=T)

    def cols(name, width=NSA_KV_WIDTH):
        return pm[:, _NEW[name]:_NEW[name] + width]

    kvf = cols("ck", 2 * NSA_KV_WIDTH).reshape(B, nch, CMP_STRIDE, 2, G, DH)
    kvf = kvf.transpose(3, 0, 4, 1, 2, 5).reshape(2, B, G, nch, CMP_STRIDE * DH)
    posf = jnp.stack([cmp_pos_k[l], cmp_pos_v[l]]).reshape(2, 1, CMP_BLOCK * DH)
    posf = jnp.broadcast_to(posf, (2, 8, CMP_BLOCK * DH)).astype(BF16)
    w1 = jnp.stack([cmp_w1_k[l], cmp_w1_v[l]]).astype(BF16)
    w2 = jnp.stack([cmp_w2_k[l], cmp_w2_v[l]]).astype(BF16)
    kvc = _compress(kvf, posf, w1, w2)
    gt = ps[:, :3 * NSA_HEADS].reshape(B, T, G, 3 * HPG).transpose(0, 2, 3, 1)
    slopes2 = slopes * LOG2E
    o_nsa = _nsa(slopes2, _query_features(cols("nsa_q", NSA_WIDTH), slopes2, B, T),
                 kvc[0], kvc[1].transpose(0, 1, 3, 2),
                 _key_features(cols("sk"), B, T, tk, True), _value_tiles(cols("sv"), B, T, tk),
                 _key_features(cols("wk"), B, T, tk, False), _value_tiles(cols("wv"), B, T, tk),
                 gt, wimp, nc=nc, n_sel=n_sel)

    wal = jnp.zeros((SMALL_WIDTH, GLA_KEY_WIDTH), F32).at[GA_LANE:GA_LANE + GLA_GATE_RANK].set(gla_w_alpha[l])
    y_gla = _gla(pm, ps, wal, gla_b_alpha[l][None, :], gla_norm_gain[l][None, :], tri, B=B, T=T)

    out = _out_proj(x2, gate, o_nsa, pm, y_gla,
                    w_br_nsa[l].astype(BF16), w_br_gla[l].astype(BF16), w_out[l].astype(BF16),
                    final_norm_gain[None, :], T=T)
    return out.reshape(B, T, D)
```

```python
import functools

import numpy as np
import jax
import jax.numpy as jnp
from jax import lax
from jax.experimental import pallas as pl
from jax.experimental.pallas import tpu as pltpu

D_MODEL = 2048
DEPTH = 1
NSA_HEADS = 16
NSA_KV_GROUPS = 4
NSA_HPG = NSA_HEADS // NSA_KV_GROUPS
NSA_HEAD_DIM = 64
CMP_BLOCK = 32
CMP_STRIDE = 16
CMP_HIDDEN = 256
SLC_BLOCK = 64
N_SELECT = 16
WINDOW = 512
NSA_WIDTH = NSA_HEADS * NSA_HEAD_DIM
NSA_KV_WIDTH = NSA_KV_GROUPS * NSA_HEAD_DIM
GLA_HEADS = 4
GLA_KEY_DIM = 128
GLA_VAL_DIM = 256
GLA_GATE_RANK = 16
GLA_TAU = 16.0
GLA_KEY_WIDTH = GLA_HEADS * GLA_KEY_DIM
GLA_WIDTH = GLA_HEADS * GLA_VAL_DIM
EPS = 1e-6
NEG = -1e30

F32 = jnp.float32
BF16 = jnp.bfloat16

_OLD = {}
_off = 0
for _name, _w in (("nsa_q", NSA_WIDTH), ("ck", NSA_KV_WIDTH), ("cv", NSA_KV_WIDTH), ("sk", NSA_KV_WIDTH),
                  ("sv", NSA_KV_WIDTH), ("wk", NSA_KV_WIDTH), ("wv", NSA_KV_WIDTH), ("nsa_g", 3 * NSA_HEADS),
                  ("nsa_z", NSA_WIDTH), ("gq", GLA_KEY_WIDTH), ("gk", GLA_KEY_WIDTH), ("gv", GLA_WIDTH),
                  ("ga", GLA_GATE_RANK), ("gla_z", GLA_WIDTH), ("mg_nsa", D_MODEL), ("mg_gla", D_MODEL)):
    _OLD[_name] = (_off, _w)
    _off += _w

def _pair_by_group(a, b):
    return [(_OLD[nm][0] + g * NSA_HEAD_DIM, NSA_HEAD_DIM) for g in range(NSA_KV_GROUPS) for nm in (a, b)]


_MAIN_PIECES = [(nm, [_OLD[nm]]) for nm in
                ("nsa_q", "nsa_z", "gla_z", "gv", "mg_nsa", "mg_gla", "gq", "gk", "ck", "cv")]
_MAIN_PIECES += [("sel_win_k", _pair_by_group("sk", "wk")), ("sel_win_v", _pair_by_group("sv", "wv"))]
_NEW = {}
_off = 0
for _name, _pieces in _MAIN_PIECES:
    _NEW[_name] = _off
    _off += sum(w for _, w in _pieces)
MAIN_WIDTH = _off
SMALL_WIDTH = 128
GA_LANE = 3 * NSA_HEADS

LANE = 128
GLA_CHUNK = 64
GLA_SUB = 16
NSA_TILE = 256
QFEAT = 2 * NSA_HEAD_DIM
KSEL_FEAT = QFEAT + LANE
N_SLOPE_TERMS = 3
SEL_MASK = 16384.0
LOG2E = 1.4426950408889634
Q_SCALE = LOG2E * NSA_HEAD_DIM ** -0.5


def _dot(a, b):
    return jnp.dot(a, b, preferred_element_type=F32)


def _dot_nt(a, b):
    return lax.dot_general(a, b, (((1,), (1,)), ((), ())), preferred_element_type=F32)


def _split3(x):
    hi = x.astype(BF16)
    r = x - hi.astype(F32)
    mid = r.astype(BF16)
    lo = (r - mid.astype(F32)).astype(BF16)
    return hi, mid, lo


def _sigmoid(x):
    return 1.0 / (1.0 + jnp.exp(-x))


def _ada_kernel(c_ref, w_ref, b_ref, o_ref):
    ch, cm, cl = _split3(c_ref[...])
    wh, wm, wl = _split3(w_ref[...])
    acc = _dot(ch, wh) + _dot(ch, wm) + _dot(cm, wh) + _dot(ch, wl) + _dot(cl, wh) + _dot(cm, wm)
    o_ref[...] = acc + b_ref[...]


def _ada(c8, w, b, *, tn=768):
    m, d = c8.shape
    n = w.shape[1]
    return pl.pallas_call(
        _ada_kernel,
        grid=(n // tn,),
        in_specs=[pl.BlockSpec((m, d), lambda j: (0, 0)),
                  pl.BlockSpec((d, tn), lambda j: (0, j)),
                  pl.BlockSpec((1, tn), lambda j: (0, j))],
        out_specs=pl.BlockSpec((m, tn), lambda j: (0, j)),
        out_shape=jax.ShapeDtypeStruct((m, n), F32),
        compiler_params=pltpu.CompilerParams(dimension_semantics=("arbitrary",),
                                             vmem_limit_bytes=40 * 1024 * 1024),
        name="ada",
    )(c8, w, b)


def _in_proj_kernel(x_ref, gain_ref, shift_ref, scale_ref, w_ref, b_ref, ws_ref, bs_ref,
                    o_ref, os_ref, h_ref):
    j = pl.program_id(1)

    @pl.when(j == 0)
    def _():
        x = x_ref[...]
        ms = jnp.mean(x * x, axis=-1, keepdims=True)
        y = x * lax.rsqrt(ms + EPS) * gain_ref[...]
        y = y * (1.0 + scale_ref[0]) + shift_ref[0]
        h = y.astype(BF16)
        h_ref[...] = h
        os_ref[...] = _dot(h, ws_ref[...]) + bs_ref[...]

    o_ref[...] = (_dot(h_ref[...], w_ref[...]) + b_ref[...]).astype(o_ref.dtype)


def _in_proj(x2, gain, shift, scale, w_main, b_main, w_small, b_small, *, T, tm=1024, tn=1536):
    n, d = x2.shape
    nm = w_main.shape[1]
    nt = T // tm
    return pl.pallas_call(
        _in_proj_kernel,
        grid=(n // tm, nm // tn),
        in_specs=[pl.BlockSpec((tm, d), lambda i, j: (i, 0)),
                  pl.BlockSpec((1, d), lambda i, j: (0, 0)),
                  pl.BlockSpec((1, 1, d), lambda i, j: (i // nt, 0, 0)),
                  pl.BlockSpec((1, 1, d), lambda i, j: (i // nt, 0, 0)),
                  pl.BlockSpec((d, tn), lambda i, j: (0, j)),
                  pl.BlockSpec((1, tn), lambda i, j: (0, j)),
                  pl.BlockSpec((d, SMALL_WIDTH), lambda i, j: (0, 0)),
                  pl.BlockSpec((1, SMALL_WIDTH), lambda i, j: (0, 0))],
        out_specs=[pl.BlockSpec((tm, tn), lambda i, j: (i, j)),
                   pl.BlockSpec((tm, SMALL_WIDTH), lambda i, j: (i, 0))],
        out_shape=[jax.ShapeDtypeStruct((n, nm), BF16),
                   jax.ShapeDtypeStruct((n, SMALL_WIDTH), F32)],
        scratch_shapes=[pltpu.VMEM((tm, d), BF16)],
        compiler_params=pltpu.CompilerParams(dimension_semantics=("parallel", "arbitrary"),
                                             vmem_limit_bytes=56 * 1024 * 1024),
        name="in_proj",
    )(x2, gain, shift, scale, w_main, b_main, w_small, b_small)


def _compress_kernel(kv_ref, pos_ref, w1_ref, w2_ref, o_ref):
    y = kv_ref[0, 0, 0]
    w1 = w1_ref[0]
    half = y.shape[1]
    nch = y.shape[0]
    z1 = _dot(y, w1[:half])
    z2 = _dot(y, w1[half:])
    posb = _dot(pos_ref[0], w1)[0:1]
    pre = z1 + pltpu.roll(z2, nch - 1, 0) + posb
    hid = pre * _sigmoid(pre)
    o_ref[0, 0, 0] = _dot(hid.astype(BF16), w2_ref[0]).astype(o_ref.dtype)


def _compress(kvf, posf, w1, w2):
    two, b, g, nch, half = kvf.shape
    hidden = w1.shape[2]
    dh = w2.shape[2]
    return pl.pallas_call(
        _compress_kernel,
        grid=(two, b, g),
        in_specs=[pl.BlockSpec((1, 1, 1, nch, half), lambda s, i, j: (s, i, j, 0, 0)),
                  pl.BlockSpec((1, 8, 2 * half), lambda s, i, j: (s, 0, 0)),
                  pl.BlockSpec((1, 2 * half, hidden), lambda s, i, j: (s, 0, 0)),
                  pl.BlockSpec((1, hidden, dh), lambda s, i, j: (s, 0, 0))],
        out_specs=pl.BlockSpec((1, 1, 1, nch, dh), lambda s, i, j: (s, i, j, 0, 0)),
        out_shape=jax.ShapeDtypeStruct((two, b, g, nch, dh), BF16),
        compiler_params=pltpu.CompilerParams(dimension_semantics=("parallel", "parallel", "parallel")),
        name="compress",
    )(kvf, posf, w1, w2)


def _select_blocks(imp, t_row, n_sel):
    nbp = imp.shape[0]
    blk = lax.broadcasted_iota(jnp.int32, (nbp, 1), 0)
    cur = lax.shift_right_logical(t_row, 6)
    forced = (blk == 0) | (blk == cur) | (blk == cur - 1)
    bvalid = blk * SLC_BLOCK <= t_row
    score = jnp.where(forced, 1e30, jnp.where(bvalid, imp, -1.0))
    blk_f = blk.astype(F32)
    sel = jnp.zeros(imp.shape, F32)
    for _ in range(n_sel):
        m = jnp.max(score, axis=0, keepdims=True)
        first = jnp.min(jnp.where(score == m, blk_f, float(nbp)), axis=0, keepdims=True)
        hit = blk_f == first
        sel = jnp.where(hit, 1.0, sel)
        score = jnp.where(hit, -1.0, score)
    return sel


def _softmax_step(s, vt, d, m_sc, l_sc, acc_sc):
    m_prev = m_sc[...]
    m_new = jnp.maximum(m_prev, jnp.max(s, axis=0, keepdims=True) - d)
    alpha = jnp.exp2(m_prev - m_new)
    p = jnp.exp2(s - (m_new + d))
    l_sc[...] = alpha * l_sc[...] + jnp.sum(p, axis=0, keepdims=True)
    acc_sc[...] = alpha * acc_sc[...] + _dot(vt, p.astype(BF16))
    m_sc[...] = m_new


def _stage_keys_values(kslab_ref, vslab_ref, ks_sc, kw_sc, vt_sc, nt, tk):
    dh = NSA_HEAD_DIM
    lane = lax.broadcasted_iota(jnp.int32, (tk, LANE), 1)
    row = lax.broadcasted_iota(jnp.int32, (tk, LANE), 0)
    pos = row.astype(F32)
    feat_sel = jnp.where((lane >= dh) & (lane < dh + N_SLOPE_TERMS), pos, 0.0)
    feat_win = jnp.where(lane < N_SLOPE_TERMS, pos, 0.0)
    blk_in_tile = lax.shift_right_logical(row, 6)

    def body(kt, carry):
        start = pl.multiple_of(kt * tk, tk)
        kk = kslab_ref[pl.ds(start, tk), :].astype(F32)
        onehot = jnp.where(kt * (tk // SLC_BLOCK) + blk_in_tile == lane, 1.0, 0.0)
        ks_sc[kt] = jnp.concatenate([jnp.where(lane < dh, kk, feat_sel), onehot], axis=1).astype(BF16)
        kw_sc[kt] = jnp.where(lane >= dh, kk, feat_win).astype(BF16)
        vt_sc[kt] = vslab_ref[pl.ds(start, tk), :].astype(F32).T.astype(BF16)
        return carry

    lax.fori_loop(0, nt, body, 0)


def _nsa_kernel(slopes_ref, q_ref, kslab_ref, vslab_ref, kc_ref, vct_ref, gt_ref, wimp_ref,
                o_ref, q2_sc, qw_sc, ks_sc, kw_sc, vt_sc, sa_sc, sb_sc, m_sc, l_sc, acc_sc, br_sc, list_sc,
                *, tq, nt, nc, n_sel):
    g = pl.program_id(1)
    qi = pl.program_id(2)
    tk = tq
    dh = NSA_HEAD_DIM
    hq = NSA_HPG * tq

    @pl.when(qi == 0)
    def _():
        _stage_keys_values(kslab_ref, vslab_ref, ks_sc, kw_sc, vt_sc, nt, tk)

    lane = lax.broadcasted_iota(jnp.int32, (1, hq), 1)
    slope_row = jnp.zeros((1, hq), F32)
    for hh in range(NSA_HPG):
        slope_row = jnp.where(lane >= hh * tq, slopes_ref[g * NSA_HPG + hh], slope_row)
    t_one = qi * tq + lax.broadcasted_iota(jnp.int32, (1, tq), 1)
    t_row = jnp.concatenate([t_one] * NSA_HPG, axis=1)
    t_f = t_row.astype(F32)
    j_loc = lax.broadcasted_iota(jnp.int32, (tk, 1), 0)

    x_t = q_ref[...].astype(F32).T
    q_t = jnp.concatenate([x_t[hh * dh:(hh + 1) * dh] for hh in range(NSA_HPG)], axis=1).astype(BF16)
    feat_row = lax.broadcasted_iota(jnp.int32, (dh, hq), 0)
    terms = [t.astype(F32) for t in _split3(slope_row)]
    feat = jnp.zeros((dh, hq), F32)
    for i, term in enumerate(terms):
        feat = jnp.where(feat_row == i, term, feat)
    feat = feat.astype(BF16)
    q2_sc[0:dh, :] = q_t
    q2_sc[dh:QFEAT, :] = feat
    qw_sc[0:dh, :] = feat
    qw_sc[dh:QFEAT, :] = q_t

    def reset_stats():
        m_sc[...] = jnp.full(m_sc.shape, NEG, F32)
        l_sc[...] = jnp.zeros(l_sc.shape, F32)
        acc_sc[...] = jnp.zeros(acc_sc.shape, F32)

    def per_head(a):
        return [a[:, hh * tq:(hh + 1) * tq] for hh in range(NSA_HPG)]

    kc = kc_ref[0, 0]
    ncp = kc.shape[0]
    n_col = lax.broadcasted_iota(jnp.int32, (ncp, 1), 0)
    ce = n_col * CMP_STRIDE + (CMP_BLOCK - 1)
    valid_c = (ce <= t_row) & (n_col < nc)
    s = _dot(kc, q_t) - slope_row * (t_row - ce).astype(F32)
    sm = jnp.where(valid_c, s, NEG)
    m = jnp.max(sm, axis=0, keepdims=True)
    p = jnp.where(valid_c, jnp.exp2(sm - m), 0.0)
    l = jnp.sum(p, axis=0, keepdims=True)
    p = p * (1.0 / jnp.where(l > 0.0, l, 1.0))
    br_sc[0] = _dot(vct_ref[0, 0], p.astype(BF16))
    ps4 = per_head(p)
    psum = (ps4[0] + ps4[1]) + (ps4[2] + ps4[3])
    wimp = wimp_ref[...]
    ph, pm, plo = _split3(psum)
    imp = _dot(wimp, ph) + _dot(wimp, pm) + _dot(wimp, plo)

    def shift_of(kv, off=None):
        d = slope_row * (t_f - (kv * tk).astype(F32))
        return d if off is None else d + jnp.where(off, -NEG, 0.0)

    reset_stats()
    n_back = WINDOW // tk
    qw = qw_sc[...]
    tiles = [jnp.maximum(qi - w, 0) for w in range(n_back + 1)]
    logits = [_dot(kw_sc[kv], qw) for kv in tiles]
    for w, (kv, s) in enumerate(zip(tiles, logits)):
        if w == 0:
            s = jnp.where((kv * tk + j_loc) <= t_row, s, NEG)
        elif w == n_back:
            s = jnp.where((t_row - (kv * tk + j_loc)) < WINDOW, s, NEG)
        _softmax_step(s, vt_sc[kv, dh:2 * dh, :], shift_of(kv, qi < w), m_sc, l_sc, acc_sc)
    br_sc[1] = acc_sc[...] * (1.0 / l_sc[...])

    sel = _select_blocks(imp, t_one, n_sel)
    penalty = ((sel - 1.0) * SEL_MASK).astype(BF16)
    q2_sc[QFEAT:KSEL_FEAT, :] = jnp.concatenate([penalty] * NSA_HPG, axis=1)

    per_tile = tk // SLC_BLOCK
    any_tok = jnp.max(sel, axis=1, keepdims=True)
    for kt in range(nt + 2):
        list_sc[kt] = qi
    n_act = jnp.int32(0)
    for kt in range(nt):
        list_sc[n_act] = kt
        hit = jnp.max(any_tok[kt * per_tile:(kt + 1) * per_tile]) > 0.5
        n_act = n_act + hit.astype(jnp.int32)
    list_sc[n_act] = qi
    n_plain = n_act - 1

    reset_stats()

    def sel_logits(idx, s_ref):
        s_ref[...] = _dot(ks_sc[list_sc[idx]], q2_sc[...])

    def sel_update(idx, s_ref):
        kv = list_sc[idx]
        _softmax_step(s_ref[...], vt_sc[kv, 0:dh, :], shift_of(kv, idx >= n_plain), m_sc, l_sc, acc_sc)

    sel_logits(0, sa_sc)

    def pair_body(i, carry):
        sel_logits(2 * i + 1, sb_sc)
        sel_update(2 * i, sa_sc)
        sel_logits(2 * i + 2, sa_sc)
        sel_update(2 * i + 1, sb_sc)
        return carry

    lax.fori_loop(0, lax.shift_right_logical(n_plain + 1, 1), pair_body, 0)
    causal = (qi * tk + j_loc) <= t_row
    _softmax_step(jnp.where(causal, sa_sc[...], NEG), vt_sc[qi, 0:dh, :], shift_of(qi), m_sc, l_sc, acc_sc)
    o_sel = acc_sc[...] * (1.0 / l_sc[...])

    sg = _sigmoid(gt_ref[0, 0])
    gates = [jnp.concatenate([sg[3 * hh + br:3 * hh + br + 1] for hh in range(NSA_HPG)], axis=1)
             for br in range(3)]
    o_all = gates[0] * br_sc[0] + gates[1] * o_sel + gates[2] * br_sc[1]
    o_ref[...] = jnp.concatenate(per_head(o_all), axis=0).T.astype(o_ref.dtype)


def _nsa(slopes, pm, kc, vct, gt, wimp, *, B, T, nc, n_sel):
    b, g, hpg, dh = B, NSA_KV_GROUPS, NSA_HPG, NSA_HEAD_DIM
    tq = NSA_TILE
    nt = T // tq
    hq = hpg * tq
    ncp = kc.shape[2]
    nbp = wimp.shape[0]
    qb = _NEW["nsa_q"] // (hpg * dh)
    kb, vb = _NEW["sel_win_k"] // LANE, _NEW["sel_win_v"] // LANE
    kern = functools.partial(_nsa_kernel, tq=tq, nt=nt, nc=nc, n_sel=n_sel)
    return pl.pallas_call(
        kern,
        grid=(b, g, nt),
        in_specs=[pl.BlockSpec(memory_space=pltpu.SMEM),
                  pl.BlockSpec((tq, hpg * dh), lambda i, j, k: (i * nt + k, qb + j)),
                  pl.BlockSpec((T, LANE), lambda i, j, k: (i, kb + j)),
                  pl.BlockSpec((T, LANE), lambda i, j, k: (i, vb + j)),
                  pl.BlockSpec((1, 1, ncp, dh), lambda i, j, k: (i, j, 0, 0)),
                  pl.BlockSpec((1, 1, dh, ncp), lambda i, j, k: (i, j, 0, 0)),
                  pl.BlockSpec((1, 1, 3 * hpg, tq), lambda i, j, k: (i, j, 0, k)),
                  pl.BlockSpec((nbp, ncp), lambda i, j, k: (0, 0))],
        out_specs=pl.BlockSpec((tq, hpg * dh), lambda i, j, k: (i * nt + k, j)),
        out_shape=jax.ShapeDtypeStruct((b * T, g * hpg * dh), BF16),
        scratch_shapes=[pltpu.VMEM((KSEL_FEAT, hq), BF16), pltpu.VMEM((QFEAT, hq), BF16),
                        pltpu.VMEM((nt, tq, KSEL_FEAT), BF16), pltpu.VMEM((nt, tq, QFEAT), BF16),
                        pltpu.VMEM((nt, 2 * dh, tq), BF16),
                        pltpu.VMEM((tq, hq), F32), pltpu.VMEM((tq, hq), F32),
                        pltpu.VMEM((1, hq), F32), pltpu.VMEM((1, hq), F32),
                        pltpu.VMEM((dh, hq), F32),
                        pltpu.VMEM((2, dh, hq), F32),
                        pltpu.SMEM((nt + 2,), jnp.int32)],
        compiler_params=pltpu.CompilerParams(dimension_semantics=("parallel", "parallel", "arbitrary"),
                                             vmem_limit_bytes=48 * 1024 * 1024),
        name="nsa",
    )(slopes, pm, pm, pm, kc, vct, gt, wimp)


def _gla_intra_scores(q, k, bcum):
    c, dk = q.shape
    sb = GLA_SUB
    row_c = lax.broadcasted_iota(jnp.int32, (c, dk), 0)
    row_s = lax.broadcasted_iota(jnp.int32, (sb, dk), 0)
    lane_s = lax.broadcasted_iota(jnp.int32, (sb, c), 1)
    blocks = []
    for i0 in range(0, c, sb):
        bi = bcum[i0:i0 + sb]
        qi = q[i0:i0 + sb]
        a_blk = jnp.zeros((sb, c), F32)
        for jl in range(sb):
            j = i0 + jl
            w = jnp.exp(jnp.where(row_s >= jl, bi - bcum[j:j + 1], NEG))
            a = jnp.sum(qi * k[j:j + 1] * w, axis=-1, keepdims=True)
            a_blk = jnp.where(lane_s == j, a, a_blk)
        if i0 > 0:
            r = bcum[i0 - 1:i0]
            qh = qi * jnp.exp(bi - r)
            kh = k * jnp.exp(jnp.where(row_c < i0, r - bcum, NEG))
            a_blk = a_blk + _dot_nt(qh.astype(BF16), kh.astype(BF16))
        blocks.append(a_blk)
    return jnp.concatenate(blocks, axis=0)


def _gla_kernel(q_ref, k_ref, v_ref, z_ref, ps_ref, wal_ref, bal_ref, gain_ref, tri_ref, o_ref,
                st_sc, *, ts):
    ti = pl.program_id(2)
    c = GLA_CHUNK

    @pl.when(ti == 0)
    def _():
        st_sc[...] = jnp.zeros(st_sc.shape, F32)

    ps = ps_ref[...]
    ph, pm, _ = _split3(ps)
    wh, wm, _ = _split3(wal_ref[...])
    zz = _dot(ph, wh) + _dot(ph, wm) + _dot(pm, wh) + bal_ref[...]
    log_a = (jnp.minimum(zz, 0.0) - jnp.log(1.0 + jnp.exp(-jnp.abs(zz)))) / GLA_TAU
    tri = tri_ref[...]
    qscale = GLA_KEY_DIM ** -0.5
    for ci in range(ts // c):
        sl = slice(ci * c, (ci + 1) * c)
        gh, gm, gl = _split3(log_a[sl])
        bcum = _dot(tri, gh) + _dot(tri, gm) + _dot(tri, gl)
        q = q_ref[sl, :].astype(F32) * qscale
        k = k_ref[sl, :].astype(F32)
        v = v_ref[sl, :]
        a_mat = _gla_intra_scores(q, k, bcum)
        st = st_sc[...]
        o = _dot(a_mat.astype(BF16), v) + _dot_nt((q * jnp.exp(bcum)).astype(BF16), st.astype(BF16))
        b_last = bcum[c - 1:c, :]
        kd = k * jnp.exp(b_last - bcum)
        st_sc[...] = st * jnp.exp(b_last) + _dot(v.astype(F32).T.astype(BF16), kd.astype(BF16))
        ms = jnp.mean(o * o, axis=-1, keepdims=True)
        y = o * lax.rsqrt(ms + EPS) * gain_ref[...]
        z = z_ref[sl, :].astype(F32)
        o_ref[sl, :] = (y * (z * _sigmoid(z))).astype(o_ref.dtype)


def _gla(pm, ps, wal, bal, gain, tri, *, B, T, ts=256):
    n = pm.shape[0]
    nts = T // ts
    dk, dv = GLA_KEY_DIM, GLA_VAL_DIM
    qb, kb = _NEW["gq"] // dk, _NEW["gk"] // dk
    vb, zb = _NEW["gv"] // dv, _NEW["gla_z"] // dv
    kern = functools.partial(_gla_kernel, ts=ts)
    return pl.pallas_call(
        kern,
        grid=(B, GLA_HEADS, nts),
        in_specs=[pl.BlockSpec((ts, dk), lambda b, h, i: (b * nts + i, qb + h)),
                  pl.BlockSpec((ts, dk), lambda b, h, i: (b * nts + i, kb + h)),
                  pl.BlockSpec((ts, dv), lambda b, h, i: (b * nts + i, vb + h)),
                  pl.BlockSpec((ts, dv), lambda b, h, i: (b * nts + i, zb + h)),
                  pl.BlockSpec((ts, SMALL_WIDTH), lambda b, h, i: (b * nts + i, 0)),
                  pl.BlockSpec((SMALL_WIDTH, dk), lambda b, h, i: (0, h)),
                  pl.BlockSpec((1, dk), lambda b, h, i: (0, h)),
                  pl.BlockSpec((1, dv), lambda b, h, i: (0, 0)),
                  pl.BlockSpec((GLA_CHUNK, GLA_CHUNK), lambda b, h, i: (0, 0))],
        out_specs=pl.BlockSpec((ts, dv), lambda b, h, i: (b * nts + i, h)),
        out_shape=jax.ShapeDtypeStruct((n, GLA_WIDTH), BF16),
        scratch_shapes=[pltpu.VMEM((dv, dk), F32)],
        compiler_params=pltpu.CompilerParams(dimension_semantics=("parallel", "parallel", "arbitrary")),
        name="gla",
    )(pm, pm, pm, pm, ps, wal, bal, gain, tri)


def _out_proj_kernel(x_ref, gate_ref, on_ref, nz_ref, mgn_ref, mgg_ref, yg_ref,
                     wn_ref, wg_ref, wo_ref, fg_ref, o_ref):
    nz = nz_ref[...].astype(F32)
    y_nsa = on_ref[...].astype(F32) * (nz * _sigmoid(nz))
    a = _dot(y_nsa.astype(BF16), wn_ref[...])
    bm = _dot(yg_ref[...], wg_ref[...])
    merged = _sigmoid(mgn_ref[...].astype(F32)) * a + _sigmoid(mgg_ref[...].astype(F32)) * bm
    xn = x_ref[...] + gate_ref[0] * _dot(merged.astype(BF16), wo_ref[...])
    ms = jnp.mean(xn * xn, axis=-1, keepdims=True)
    o_ref[...] = xn * lax.rsqrt(ms + EPS) * fg_ref[...]


def _out_proj(x2, gate, o_nsa, pm, yg, wn, wg, wo, fg, *, T, tm=256):
    n, d = x2.shape
    nt = T // tm
    nzb = _NEW["nsa_z"] // NSA_WIDTH
    mnb = _NEW["mg_nsa"] // D_MODEL
    mgb = _NEW["mg_gla"] // D_MODEL
    row = lambda i: (i, 0)
    const2 = lambda i: (0, 0)
    return pl.pallas_call(
        _out_proj_kernel,
        grid=(n // tm,),
        in_specs=[pl.BlockSpec((tm, d), row),
                  pl.BlockSpec((1, 1, d), lambda i: (i // nt, 0, 0)),
                  pl.BlockSpec((tm, NSA_WIDTH), row),
                  pl.BlockSpec((tm, NSA_WIDTH), lambda i: (i, nzb)),
                  pl.BlockSpec((tm, D_MODEL), lambda i: (i, mnb)),
                  pl.BlockSpec((tm, D_MODEL), lambda i: (i, mgb)),
                  pl.BlockSpec((tm, GLA_WIDTH), row),
                  pl.BlockSpec((NSA_WIDTH, d), const2),
                  pl.BlockSpec((GLA_WIDTH, d), const2),
                  pl.BlockSpec((d, d), const2),
                  pl.BlockSpec((1, d), const2)],
        out_specs=pl.BlockSpec((tm, d), row),
        out_shape=jax.ShapeDtypeStruct((n, d), F32),
        compiler_params=pltpu.CompilerParams(dimension_semantics=("parallel",),
                                             vmem_limit_bytes=60 * 1024 * 1024),
        name="out_proj",
    )(x2, gate, o_nsa, pm, pm, pm, yg, wn, wg, wo, fg)


def _importance_matrix(nbp, ncp):
    ratio = SLC_BLOCK // CMP_STRIDE
    n_sub = CMP_BLOCK // CMP_STRIDE
    w = np.zeros((nbp, ncp), np.float32)
    for blk in range(nbp):
        for m in range(ratio):
            for s in range(n_sub):
                n = ratio * blk + m - s
                if 0 <= n < ncp:
                    w[blk, n] += 1.0
    return w


def _main_cols(a):
    parts = [a[..., start:start + width] * (Q_SCALE if nm == "nsa_q" else 1.0)
             for nm, pieces in _MAIN_PIECES for start, width in pieces]
    return jnp.concatenate(parts, axis=-1)


def _small_cols(a):
    pad = SMALL_WIDTH - 3 * NSA_HEADS - GLA_GATE_RANK
    parts = [a[..., _OLD["nsa_g"][0]:_OLD["nsa_g"][0] + 3 * NSA_HEADS],
             a[..., _OLD["ga"][0]:_OLD["ga"][0] + GLA_GATE_RANK],
             jnp.zeros(a.shape[:-1] + (pad,), a.dtype)]
    return jnp.concatenate(parts, axis=-1)


def kernel(x, c, w_ada, b_ada, norm_gain, w_in, b_in, cmp_pos_k, cmp_pos_v, cmp_w1_k, cmp_w2_k, cmp_w1_v, cmp_w2_v,
           gla_w_alpha, gla_b_alpha, gla_norm_gain, w_br_nsa, w_br_gla, w_out, final_norm_gain):
    assert DEPTH == 1, "the final rmsnorm is fused into the single layer's output kernel"
    B, T, D = x.shape
    G, HPG, DH = NSA_KV_GROUPS, NSA_HPG, NSA_HEAD_DIM
    N = B * T
    nch = T // CMP_STRIDE
    nc = nch - CMP_BLOCK // CMP_STRIDE + 1
    nb = T // SLC_BLOCK
    nbp = KSEL_FEAT - QFEAT
    assert nb <= nbp and T % NSA_TILE == 0
    n_sel = min(N_SELECT, nb)
    tk = NSA_TILE
    l = 0

    slopes = 2.0 ** (-8.0 * jnp.arange(1, NSA_HEADS + 1, dtype=F32) / NSA_HEADS)
    wimp = jnp.asarray(_importance_matrix(nbp, nch), BF16)
    tri = jnp.asarray(np.tril(np.ones((GLA_CHUNK, GLA_CHUNK), np.float32)), BF16)
    bp = ((B + 7) // 8) * 8
    c8 = jnp.pad(c, ((0, bp - B), (0, 0)))
    x2 = x.reshape(N, D)

    mod = _ada(c8, w_ada[l], b_ada[l][None, :])
    shift = mod[:B, :D].reshape(B, 1, D)
    scale = mod[:B, D:2 * D].reshape(B, 1, D)
    gate = mod[:B, 2 * D:].reshape(B, 1, D)

    pm, ps = _in_proj(x2, norm_gain[l][None, :], shift, scale,
                      _main_cols(w_in[l]).astype(BF16), _main_cols(b_in[l])[None, :],
                      _small_cols(w_in[l]).astype(BF16), _small_cols(b_in[l])[None, :], T=T)

    def cols(name, width=NSA_KV_WIDTH):
        return pm[:, _NEW[name]:_NEW[name] + width]

    kvf = cols("ck", 2 * NSA_KV_WIDTH).reshape(B, nch, CMP_STRIDE, 2, G, DH)
    kvf = kvf.transpose(3, 0, 4, 1, 2, 5).reshape(2, B, G, nch, CMP_STRIDE * DH)
    posf = jnp.stack([cmp_pos_k[l], cmp_pos_v[l]]).reshape(2, 1, CMP_BLOCK * DH)
    posf = jnp.broadcast_to(posf, (2, 8, CMP_BLOCK * DH)).astype(BF16)
    w1 = jnp.stack([cmp_w1_k[l], cmp_w1_v[l]]).astype(BF16)
    w2 = jnp.stack([cmp_w2_k[l], cmp_w2_v[l]]).astype(BF16)
    kvc = _compress(kvf, posf, w1, w2)
    gt = ps[:, :3 * NSA_HEADS].reshape(B, T, G, 3 * HPG).transpose(0, 2, 3, 1)
    slopes2 = slopes * LOG2E
    o_nsa = _nsa(slopes2, pm, kvc[0], kvc[1].transpose(0, 1, 3, 2), gt, wimp, B=B, T=T, nc=nc, n_sel=n_sel)

    wal = jnp.pad(gla_w_alpha[l], ((GA_LANE, SMALL_WIDTH - GA_LANE - GLA_GATE_RANK), (0, 0)))
    y_gla = _gla(pm, ps, wal, gla_b_alpha[l][None, :], gla_norm_gain[l][None, :], tri, B=B, T=T)

    out = _out_proj(x2, gate, o_nsa, pm, y_gla,
                    w_br_nsa[l].astype(BF16), w_br_gla[l].astype(BF16), w_out[l].astype(BF16),
                    final_norm_gain[None, :], T=T)
    return out.reshape(B, T, D)
```

```python
import functools

import numpy as np
import jax
import jax.numpy as jnp
from jax import lax
from jax.experimental import pallas as pl
from jax.experimental.pallas import tpu as pltpu

D_MODEL = 2048
DEPTH = 1
NSA_HEADS = 16
NSA_KV_GROUPS = 4
NSA_HPG = NSA_HEADS // NSA_KV_GROUPS
NSA_HEAD_DIM = 64
CMP_BLOCK = 32
CMP_STRIDE = 16
CMP_HIDDEN = 256
SLC_BLOCK = 64
N_SELECT = 16
WINDOW = 512
NSA_WIDTH = NSA_HEADS * NSA_HEAD_DIM
NSA_KV_WIDTH = NSA_KV_GROUPS * NSA_HEAD_DIM
GLA_HEADS = 4
GLA_KEY_DIM = 128
GLA_VAL_DIM = 256
GLA_GATE_RANK = 16
GLA_TAU = 16.0
GLA_KEY_WIDTH = GLA_HEADS * GLA_KEY_DIM
GLA_WIDTH = GLA_HEADS * GLA_VAL_DIM
EPS = 1e-6
NEG = -1e30

F32 = jnp.float32
BF16 = jnp.bfloat16

_OLD = {}
_off = 0
for _name, _w in (("nsa_q", NSA_WIDTH), ("ck", NSA_KV_WIDTH), ("cv", NSA_KV_WIDTH), ("sk", NSA_KV_WIDTH),
                  ("sv", NSA_KV_WIDTH), ("wk", NSA_KV_WIDTH), ("wv", NSA_KV_WIDTH), ("nsa_g", 3 * NSA_HEADS),
                  ("nsa_z", NSA_WIDTH), ("gq", GLA_KEY_WIDTH), ("gk", GLA_KEY_WIDTH), ("gv", GLA_WIDTH),
                  ("ga", GLA_GATE_RANK), ("gla_z", GLA_WIDTH), ("mg_nsa", D_MODEL), ("mg_gla", D_MODEL)):
    _OLD[_name] = (_off, _w)
    _off += _w

def _pair_by_group(a, b):
    return [(_OLD[nm][0] + g * NSA_HEAD_DIM, NSA_HEAD_DIM) for g in range(NSA_KV_GROUPS) for nm in (a, b)]


_MAIN_PIECES = [(nm, [_OLD[nm]]) for nm in
                ("nsa_q", "nsa_z", "gla_z", "gv", "mg_nsa", "mg_gla", "gq", "gk", "ck", "cv")]
_MAIN_PIECES += [("sel_win_k", _pair_by_group("sk", "wk")), ("sel_win_v", _pair_by_group("sv", "wv"))]
_NEW = {}
_off = 0
for _name, _pieces in _MAIN_PIECES:
    _NEW[_name] = _off
    _off += sum(w for _, w in _pieces)
MAIN_WIDTH = _off
SMALL_WIDTH = 128
GA_LANE = 3 * NSA_HEADS

LANE = 128
GLA_CHUNK = 64
GLA_SUB = 16
NSA_TILE = 256
QFEAT = 2 * NSA_HEAD_DIM
KSEL_FEAT = QFEAT + LANE
N_SLOPE_TERMS = 3
SEL_MASK = 16384.0
BITS_PER_WORD = 16
LOG2E = 1.4426950408889634
Q_SCALE = LOG2E * NSA_HEAD_DIM ** -0.5


def _dot(a, b):
    return jnp.dot(a, b, preferred_element_type=F32)


def _dot_nt(a, b):
    return lax.dot_general(a, b, (((1,), (1,)), ((), ())), preferred_element_type=F32)


def _split3(x):
    hi = x.astype(BF16)
    r = x - hi.astype(F32)
    mid = r.astype(BF16)
    lo = (r - mid.astype(F32)).astype(BF16)
    return hi, mid, lo


def _sigmoid(x):
    return 1.0 / (1.0 + jnp.exp(-x))


def _ada_kernel(c_ref, w_ref, b_ref, o_ref):
    ch, cm, cl = _split3(c_ref[...])
    wh, wm, wl = _split3(w_ref[...])
    acc = _dot(ch, wh) + _dot(ch, wm) + _dot(cm, wh) + _dot(ch, wl) + _dot(cl, wh) + _dot(cm, wm)
    o_ref[...] = acc + b_ref[...]


def _ada(c8, w, b, *, tn=768):
    m, d = c8.shape
    n = w.shape[1]
    return pl.pallas_call(
        _ada_kernel,
        grid=(n // tn,),
        in_specs=[pl.BlockSpec((m, d), lambda j: (0, 0)),
                  pl.BlockSpec((d, tn), lambda j: (0, j)),
                  pl.BlockSpec((1, tn), lambda j: (0, j))],
        out_specs=pl.BlockSpec((m, tn), lambda j: (0, j)),
        out_shape=jax.ShapeDtypeStruct((m, n), F32),
        compiler_params=pltpu.CompilerParams(dimension_semantics=("arbitrary",),
                                             vmem_limit_bytes=40 * 1024 * 1024),
        name="ada",
    )(c8, w, b)


def _in_proj_kernel(x_ref, gain_ref, shift_ref, scale_ref, w_ref, b_ref, ws_ref, bs_ref,
                    o_ref, os_ref, h_ref):
    j = pl.program_id(1)

    @pl.when(j == 0)
    def _():
        x = x_ref[...]
        ms = jnp.mean(x * x, axis=-1, keepdims=True)
        y = x * lax.rsqrt(ms + EPS) * gain_ref[...]
        y = y * (1.0 + scale_ref[0]) + shift_ref[0]
        h = y.astype(BF16)
        h_ref[...] = h
        os_ref[...] = _dot(h, ws_ref[...]) + bs_ref[...]

    o_ref[...] = (_dot(h_ref[...], w_ref[...]) + b_ref[...]).astype(o_ref.dtype)


def _in_proj(x2, gain, shift, scale, w_main, b_main, w_small, b_small, *, T, tm=1024, tn=1536):
    n, d = x2.shape
    nm = w_main.shape[1]
    nt = T // tm
    return pl.pallas_call(
        _in_proj_kernel,
        grid=(n // tm, nm // tn),
        in_specs=[pl.BlockSpec((tm, d), lambda i, j: (i, 0)),
                  pl.BlockSpec((1, d), lambda i, j: (0, 0)),
                  pl.BlockSpec((1, 1, d), lambda i, j: (i // nt, 0, 0)),
                  pl.BlockSpec((1, 1, d), lambda i, j: (i // nt, 0, 0)),
                  pl.BlockSpec((d, tn), lambda i, j: (0, j)),
                  pl.BlockSpec((1, tn), lambda i, j: (0, j)),
                  pl.BlockSpec((d, SMALL_WIDTH), lambda i, j: (0, 0)),
                  pl.BlockSpec((1, SMALL_WIDTH), lambda i, j: (0, 0))],
        out_specs=[pl.BlockSpec((tm, tn), lambda i, j: (i, j)),
                   pl.BlockSpec((tm, SMALL_WIDTH), lambda i, j: (i, 0))],
        out_shape=[jax.ShapeDtypeStruct((n, nm), BF16),
                   jax.ShapeDtypeStruct((n, SMALL_WIDTH), F32)],
        scratch_shapes=[pltpu.VMEM((tm, d), BF16)],
        compiler_params=pltpu.CompilerParams(dimension_semantics=("parallel", "arbitrary"),
                                             vmem_limit_bytes=56 * 1024 * 1024),
        name="in_proj",
    )(x2, gain, shift, scale, w_main, b_main, w_small, b_small)


def _compress_kernel(kv_ref, pos_ref, w1_ref, w2_ref, o_ref):
    y = kv_ref[0, 0, 0]
    w1 = w1_ref[0]
    half = y.shape[1]
    nch = y.shape[0]
    z1 = _dot(y, w1[:half])
    z2 = _dot(y, w1[half:])
    posb = _dot(pos_ref[0], w1)[0:1]
    pre = z1 + pltpu.roll(z2, nch - 1, 0) + posb
    hid = pre * _sigmoid(pre)
    o_ref[0, 0, 0] = _dot(hid.astype(BF16), w2_ref[0]).astype(o_ref.dtype)


def _compress(kvf, posf, w1, w2):
    two, b, g, nch, half = kvf.shape
    hidden = w1.shape[2]
    dh = w2.shape[2]
    return pl.pallas_call(
        _compress_kernel,
        grid=(two, b, g),
        in_specs=[pl.BlockSpec((1, 1, 1, nch, half), lambda s, i, j: (s, i, j, 0, 0)),
                  pl.BlockSpec((1, 8, 2 * half), lambda s, i, j: (s, 0, 0)),
                  pl.BlockSpec((1, 2 * half, hidden), lambda s, i, j: (s, 0, 0)),
                  pl.BlockSpec((1, hidden, dh), lambda s, i, j: (s, 0, 0))],
        out_specs=pl.BlockSpec((1, 1, 1, nch, dh), lambda s, i, j: (s, i, j, 0, 0)),
        out_shape=jax.ShapeDtypeStruct((two, b, g, nch, dh), BF16),
        compiler_params=pltpu.CompilerParams(dimension_semantics=("parallel", "parallel", "parallel")),
        name="compress",
    )(kvf, posf, w1, w2)


def _select_blocks(imp, t_row, n_sel):
    nbp = imp.shape[0]
    blk = lax.broadcasted_iota(jnp.int32, (nbp, 1), 0)
    cur = lax.shift_right_logical(t_row, 6)
    forced = (blk == 0) | (blk == cur) | (blk == cur - 1)
    bvalid = blk * SLC_BLOCK <= t_row
    score = jnp.where(forced, 1e30, jnp.where(bvalid, imp, -1.0))
    blk_f = blk.astype(F32)
    sel = jnp.zeros(imp.shape, F32)
    for _ in range(n_sel):
        m = jnp.max(score, axis=0, keepdims=True)
        first = jnp.min(jnp.where(score == m, blk_f, float(nbp)), axis=0, keepdims=True)
        hit = blk_f == first
        sel = jnp.where(hit, 1.0, sel)
        score = jnp.where(hit, -1.0, score)
    return sel


def _softmax_step(s, vt, d, m_sc, l_sc, acc_sc, s_max=None):
    if s_max is None:
        s_max = jnp.max(s, axis=0, keepdims=True)
    m_prev = m_sc[...]
    m_new = jnp.maximum(m_prev, s_max - d)
    alpha = jnp.exp2(m_prev - m_new)
    p = jnp.exp2(s - (m_new + d))
    l_sc[...] = alpha * l_sc[...] + jnp.sum(p, axis=0, keepdims=True)
    acc_sc[...] = alpha * acc_sc[...] + _dot(vt, p.astype(BF16))
    m_sc[...] = m_new


def _stage_group(qslab_ref, kslab_ref, vslab_ref, qt_sc, ks_sc, kw_sc, vt_sc, nt, tk):
    dh = NSA_HEAD_DIM
    lane = lax.broadcasted_iota(jnp.int32, (tk, LANE), 1)
    row = lax.broadcasted_iota(jnp.int32, (tk, LANE), 0)
    pos = row.astype(F32)
    feat_sel = jnp.where((lane >= dh) & (lane < dh + N_SLOPE_TERMS), pos, 0.0)
    feat_win = jnp.where(lane < N_SLOPE_TERMS, pos, 0.0)
    blk_in_tile = lax.shift_right_logical(row, 6)

    def body(kt, carry):
        start = pl.multiple_of(kt * tk, tk)
        kk = kslab_ref[pl.ds(start, tk), :].astype(F32)
        onehot = jnp.where(kt * (tk // SLC_BLOCK) + blk_in_tile == lane, 1.0, 0.0)
        ks_sc[kt] = jnp.concatenate([jnp.where(lane < dh, kk, feat_sel), onehot], axis=1).astype(BF16)
        kw_sc[kt] = jnp.where(lane >= dh, kk, feat_win).astype(BF16)
        vt_sc[kt] = vslab_ref[pl.ds(start, tk), :].astype(F32).T.astype(BF16)
        x_t = qslab_ref[pl.ds(start, tk), :].astype(F32).T
        qt_sc[kt] = jnp.concatenate([x_t[hh * dh:(hh + 1) * dh] for hh in range(NSA_HPG)], axis=1).astype(BF16)
        return carry

    lax.fori_loop(0, nt, body, 0)


def _nsa_kernel(slopes_ref, q_ref, kslab_ref, vslab_ref, kc_ref, vct_ref, gt_ref, wimp_ref,
                o_ref, q2_sc, qw_sc, ks_sc, kw_sc, qt_sc, vt_sc, sa_sc, sb_sc, m_sc, l_sc, acc_sc, br_sc, list_sc,
                *, tq, nt, nc, n_sel):
    g = pl.program_id(1)
    qi = pl.program_id(2)
    tk = tq
    dh = NSA_HEAD_DIM
    hq = NSA_HPG * tq

    @pl.when(qi == 0)
    def _():
        _stage_group(q_ref, kslab_ref, vslab_ref, qt_sc, ks_sc, kw_sc, vt_sc, nt, tk)

    lane = lax.broadcasted_iota(jnp.int32, (1, hq), 1)
    slope_row = jnp.zeros((1, hq), F32)
    for hh in range(NSA_HPG):
        slope_row = jnp.where(lane >= hh * tq, slopes_ref[g * NSA_HPG + hh], slope_row)
    t_one = qi * tq + lax.broadcasted_iota(jnp.int32, (1, tq), 1)
    t_row = jnp.concatenate([t_one] * NSA_HPG, axis=1)
    t_f = t_row.astype(F32)
    j_loc = lax.broadcasted_iota(jnp.int32, (tk, 1), 0)

    q_t = qt_sc[qi]
    feat_row = lax.broadcasted_iota(jnp.int32, (dh, hq), 0)
    terms = [t.astype(F32) for t in _split3(slope_row)]
    feat = jnp.zeros((dh, hq), F32)
    for i, term in enumerate(terms):
        feat = jnp.where(feat_row == i, term, feat)
    feat = feat.astype(BF16)
    q2_sc[0:dh, :] = q_t
    q2_sc[dh:QFEAT, :] = feat
    qw_sc[0:dh, :] = feat
    qw_sc[dh:QFEAT, :] = q_t

    def reset_stats():
        m_sc[...] = jnp.full(m_sc.shape, NEG, F32)
        l_sc[...] = jnp.zeros(l_sc.shape, F32)
        acc_sc[...] = jnp.zeros(acc_sc.shape, F32)

    def per_head(a):
        return [a[:, hh * tq:(hh + 1) * tq] for hh in range(NSA_HPG)]

    kc = kc_ref[0, 0]
    ncp = kc.shape[0]
    n_col = lax.broadcasted_iota(jnp.int32, (ncp, 1), 0)
    ce = jnp.where(n_col < nc, n_col * CMP_STRIDE + (CMP_BLOCK - 1), 2 ** 30)
    ce_rel = (ce - qi * tq).astype(F32)
    s = jnp.where(ce <= t_row, _dot(kc, q_t) + slope_row * ce_rel, NEG)
    m = jnp.max(s, axis=0, keepdims=True)
    p = jnp.exp2(s - m)
    l = jnp.sum(p, axis=0, keepdims=True)
    p = p * jnp.where(t_row >= CMP_BLOCK - 1, 1.0 / l, 0.0)
    br_sc[0] = _dot(vct_ref[0, 0], p.astype(BF16))
    ps4 = per_head(p)
    psum = (ps4[0] + ps4[1]) + (ps4[2] + ps4[3])
    wimp = wimp_ref[...]
    ph, pm, plo = _split3(psum)
    imp = _dot(wimp, ph) + _dot(wimp, pm) + _dot(wimp, plo)

    def shift_of(kv, off=None):
        d = slope_row * (t_f - (kv * tk).astype(F32))
        return d if off is None else d + jnp.where(off, -NEG, 0.0)

    reset_stats()
    n_back = WINDOW // tk
    qw = qw_sc[...]
    tiles = [jnp.maximum(qi - w, 0) for w in range(n_back + 1)]
    logits = [_dot(kw_sc[kv], qw) for kv in tiles]
    for w, (kv, s) in enumerate(zip(tiles, logits)):
        if w == 0:
            s = jnp.where((kv * tk + j_loc) <= t_row, s, NEG)
        elif w == n_back:
            s = jnp.where((t_row - (kv * tk + j_loc)) < WINDOW, s, NEG)
        _softmax_step(s, vt_sc[kv, dh:2 * dh, :], shift_of(kv, qi < w), m_sc, l_sc, acc_sc)
    br_sc[1] = acc_sc[...] * (1.0 / l_sc[...])

    sa_sc[0:tk, :] = _dot(ks_sc[0, :, 0:QFEAT], q2_sc[0:QFEAT, :])

    sel = _select_blocks(imp, t_one, n_sel)
    penalty_f = (sel - 1.0) * SEL_MASK
    penalty = jnp.concatenate([penalty_f.astype(BF16)] * NSA_HPG, axis=1)
    q2_sc[QFEAT:KSEL_FEAT, :] = penalty

    per_tile = tk // SLC_BLOCK
    any_tok = jnp.max(sel, axis=1, keepdims=True)
    blk_col = lax.broadcasted_iota(jnp.int32, any_tok.shape, 0)
    bit = jnp.left_shift(1, jnp.bitwise_and(blk_col, BITS_PER_WORD - 1)).astype(F32)
    bits = jnp.where(any_tok > 0.5, bit, 0.0)
    n_words = -(-nt * per_tile // BITS_PER_WORD)
    words = [jnp.sum(bits[w * BITS_PER_WORD:(w + 1) * BITS_PER_WORD]).astype(jnp.int32) for w in range(n_words)]
    for kt in range(nt + 2):
        list_sc[kt] = qi
    n_act = jnp.int32(0)
    tile_mask = (1 << per_tile) - 1
    for kt in range(nt):
        list_sc[n_act] = kt
        w, sh = divmod(kt * per_tile, BITS_PER_WORD)
        hit = jnp.bitwise_and(words[w] >> sh, tile_mask) != 0
        n_act = n_act + hit.astype(jnp.int32)
    list_sc[n_act] = qi
    n_plain = n_act - 1

    reset_stats()

    def sel_logits(idx, s_ref):
        s = _dot(ks_sc[list_sc[idx]], q2_sc[...])
        s_ref[0:tk, :] = s
        s_ref[tk:tk + 1, :] = jnp.max(s, axis=0, keepdims=True)

    def sel_update(idx, s_ref):
        kv = list_sc[idx]
        _softmax_step(s_ref[0:tk, :], vt_sc[kv, 0:dh, :], shift_of(kv, idx >= n_plain), m_sc, l_sc, acc_sc,
                      s_max=s_ref[tk:tk + 1, :])

    pen0 = jnp.concatenate([penalty_f[0:per_tile]] * NSA_HPG, axis=1)
    pen0 = jnp.concatenate([jnp.broadcast_to(pen0[b:b + 1], (SLC_BLOCK, hq)) for b in range(per_tile)], axis=0)
    s0 = sa_sc[0:tk, :] + pen0
    sa_sc[0:tk, :] = s0
    sa_sc[tk:tk + 1, :] = jnp.max(s0, axis=0, keepdims=True)

    def pair_body(i, carry):
        sel_logits(2 * i + 1, sb_sc)
        sel_update(2 * i, sa_sc)
        sel_logits(2 * i + 2, sa_sc)
        sel_update(2 * i + 1, sb_sc)
        return carry

    lax.fori_loop(0, lax.shift_right_logical(n_plain + 1, 1), pair_body, 0)
    causal = (qi * tk + j_loc) <= t_row
    _softmax_step(jnp.where(causal, sa_sc[0:tk, :], NEG), vt_sc[qi, 0:dh, :], shift_of(qi), m_sc, l_sc, acc_sc)
    o_sel = acc_sc[...] * (1.0 / l_sc[...])

    sg = _sigmoid(gt_ref[0, 0])
    gates = [jnp.concatenate([sg[3 * hh + br:3 * hh + br + 1] for hh in range(NSA_HPG)], axis=1)
             for br in range(3)]
    o_all = gates[0] * br_sc[0] + gates[1] * o_sel + gates[2] * br_sc[1]
    o_ref[...] = jnp.concatenate(per_head(o_all), axis=0).T.astype(o_ref.dtype)


def _nsa(slopes, pm, kc, vct, gt, wimp, *, B, T, nc, n_sel):
    b, g, hpg, dh = B, NSA_KV_GROUPS, NSA_HPG, NSA_HEAD_DIM
    tq = NSA_TILE
    nt = T // tq
    hq = hpg * tq
    ncp = kc.shape[2]
    nbp = wimp.shape[0]
    qb = _NEW["nsa_q"] // (hpg * dh)
    kb, vb = _NEW["sel_win_k"] // LANE, _NEW["sel_win_v"] // LANE
    kern = functools.partial(_nsa_kernel, tq=tq, nt=nt, nc=nc, n_sel=n_sel)
    return pl.pallas_call(
        kern,
        grid=(b, g, nt),
        in_specs=[pl.BlockSpec(memory_space=pltpu.SMEM),
                  pl.BlockSpec((T, hpg * dh), lambda i, j, k: (i, qb + j)),
                  pl.BlockSpec((T, LANE), lambda i, j, k: (i, kb + j)),
                  pl.BlockSpec((T, LANE), lambda i, j, k: (i, vb + j)),
                  pl.BlockSpec((1, 1, ncp, dh), lambda i, j, k: (i, j, 0, 0)),
                  pl.BlockSpec((1, 1, dh, ncp), lambda i, j, k: (i, j, 0, 0)),
                  pl.BlockSpec((1, 1, 3 * hpg, tq), lambda i, j, k: (i, j, 0, k)),
                  pl.BlockSpec((nbp, ncp), lambda i, j, k: (0, 0))],
        out_specs=pl.BlockSpec((tq, hpg * dh), lambda i, j, k: (i * nt + k, j)),
        out_shape=jax.ShapeDtypeStruct((b * T, g * hpg * dh), BF16),
        scratch_shapes=[pltpu.VMEM((KSEL_FEAT, hq), BF16), pltpu.VMEM((QFEAT, hq), BF16),
                        pltpu.VMEM((nt, tq, KSEL_FEAT), BF16), pltpu.VMEM((nt, tq, QFEAT), BF16),
                        pltpu.VMEM((nt, dh, hq), BF16), pltpu.VMEM((nt, 2 * dh, tq), BF16),
                        pltpu.VMEM((tq + 8, hq), F32), pltpu.VMEM((tq + 8, hq), F32),
                        pltpu.VMEM((1, hq), F32), pltpu.VMEM((1, hq), F32),
                        pltpu.VMEM((dh, hq), F32),
                        pltpu.VMEM((2, dh, hq), F32),
                        pltpu.SMEM((nt + 2,), jnp.int32)],
        compiler_params=pltpu.CompilerParams(dimension_semantics=("parallel", "parallel", "arbitrary"),
                                             vmem_limit_bytes=48 * 1024 * 1024),
        name="nsa",
    )(slopes, pm, pm, pm, kc, vct, gt, wimp)


def _gla_intra_scores(q, k, bcum):
    c, dk = q.shape
    sb = GLA_SUB
    row_c = lax.broadcasted_iota(jnp.int32, (c, dk), 0)
    row_s = lax.broadcasted_iota(jnp.int32, (sb, 1), 0)
    lane_s = lax.broadcasted_iota(jnp.int32, (sb, c), 1)
    blocks = []
    for i0 in range(0, c, sb):
        bi = bcum[i0:i0 + sb]
        qi = q[i0:i0 + sb]
        a_blk = jnp.zeros((sb, c), F32)
        for jl in range(sb):
            j = i0 + jl
            w = jnp.exp2(bi - bcum[j:j + 1])
            a = jnp.sum(qi * k[j:j + 1] * w, axis=-1, keepdims=True)
            a_blk = jnp.where(lane_s == j, jnp.where(row_s >= jl, a, 0.0), a_blk)
        if i0 > 0:
            r = bcum[i0 - 1:i0]
            qh = qi * jnp.exp2(bi - r)
            kh = k * jnp.exp2(jnp.where(row_c < i0, r - bcum, NEG))
            a_blk = a_blk + _dot_nt(qh.astype(BF16), kh.astype(BF16))
        blocks.append(a_blk)
    return jnp.concatenate(blocks, axis=0)


def _gla_kernel(q_ref, k_ref, v_ref, z_ref, ps_ref, wal_ref, bal_ref, gain_ref, tri_ref, o_ref,
                st_sc, *, ts):
    ti = pl.program_id(2)
    c = GLA_CHUNK

    @pl.when(ti == 0)
    def _():
        st_sc[...] = jnp.zeros(st_sc.shape, F32)

    ps = ps_ref[...]
    ph, pm, _ = _split3(ps)
    wh, wm, _ = _split3(wal_ref[...])
    zz = _dot(ph, wh) + _dot(ph, wm) + _dot(pm, wh) + bal_ref[...]
    log_a = (jnp.minimum(zz, 0.0) - jnp.log(1.0 + jnp.exp(-jnp.abs(zz)))) * (LOG2E / GLA_TAU)
    tri = tri_ref[...]
    qscale = GLA_KEY_DIM ** -0.5
    for ci in range(ts // c):
        sl = slice(ci * c, (ci + 1) * c)
        gh, gm, gl = _split3(log_a[sl])
        bcum = _dot(tri, gh) + _dot(tri, gm) + _dot(tri, gl)
        q = q_ref[sl, :].astype(F32) * qscale
        k = k_ref[sl, :].astype(F32)
        v = v_ref[sl, :]
        a_mat = _gla_intra_scores(q, k, bcum)
        st = st_sc[...]
        o = _dot(a_mat.astype(BF16), v) + _dot_nt((q * jnp.exp2(bcum)).astype(BF16), st.astype(BF16))
        b_last = bcum[c - 1:c, :]
        kd = k * jnp.exp2(b_last - bcum)
        st_sc[...] = st * jnp.exp2(b_last) + _dot(v.astype(F32).T.astype(BF16), kd.astype(BF16))
        ms = jnp.mean(o * o, axis=-1, keepdims=True)
        y = o * lax.rsqrt(ms + EPS) * gain_ref[...]
        z = z_ref[sl, :].astype(F32)
        o_ref[sl, :] = (y * (z * _sigmoid(z))).astype(o_ref.dtype)


def _gla(pm, ps, wal, bal, gain, tri, *, B, T, ts=256):
    n = pm.shape[0]
    nts = T // ts
    dk, dv = GLA_KEY_DIM, GLA_VAL_DIM
    qb, kb = _NEW["gq"] // dk, _NEW["gk"] // dk
    vb, zb = _NEW["gv"] // dv, _NEW["gla_z"] // dv
    kern = functools.partial(_gla_kernel, ts=ts)
    return pl.pallas_call(
        kern,
        grid=(B, GLA_HEADS, nts),
        in_specs=[pl.BlockSpec((ts, dk), lambda b, h, i: (b * nts + i, qb + h)),
                  pl.BlockSpec((ts, dk), lambda b, h, i: (b * nts + i, kb + h)),
                  pl.BlockSpec((ts, dv), lambda b, h, i: (b * nts + i, vb + h)),
                  pl.BlockSpec((ts, dv), lambda b, h, i: (b * nts + i, zb + h)),
                  pl.BlockSpec((ts, SMALL_WIDTH), lambda b, h, i: (b * nts + i, 0)),
                  pl.BlockSpec((SMALL_WIDTH, dk), lambda b, h, i: (0, h)),
                  pl.BlockSpec((1, dk), lambda b, h, i: (0, h)),
                  pl.BlockSpec((1, dv), lambda b, h, i: (0, 0)),
                  pl.BlockSpec((GLA_CHUNK, GLA_CHUNK), lambda b, h, i: (0, 0))],
        out_specs=pl.BlockSpec((ts, dv), lambda b, h, i: (b * nts + i, h)),
        out_shape=jax.ShapeDtypeStruct((n, GLA_WIDTH), BF16),
        scratch_shapes=[pltpu.VMEM((dv, dk), F32)],
        compiler_params=pltpu.CompilerParams(dimension_semantics=("parallel", "parallel", "arbitrary")),
        name="gla",
    )(pm, pm, pm, pm, ps, wal, bal, gain, tri)


def _out_proj_kernel(x_ref, gate_ref, on_ref, nz_ref, mgn_ref, mgg_ref, yg_ref,
                     wn_ref, wg_ref, wo_ref, fg_ref, o_ref):
    nz = nz_ref[...].astype(F32)
    y_nsa = on_ref[...].astype(F32) * (nz * _sigmoid(nz))
    a = _dot(y_nsa.astype(BF16), wn_ref[...])
    bm = _dot(yg_ref[...], wg_ref[...])
    merged = _sigmoid(mgn_ref[...].astype(F32)) * a + _sigmoid(mgg_ref[...].astype(F32)) * bm
    xn = x_ref[...] + gate_ref[0] * _dot(merged.astype(BF16), wo_ref[...])
    ms = jnp.mean(xn * xn, axis=-1, keepdims=True)
    o_ref[...] = xn * lax.rsqrt(ms + EPS) * fg_ref[...]


def _out_proj(x2, gate, o_nsa, pm, yg, wn, wg, wo, fg, *, T, tm=256):
    n, d = x2.shape
    nt = T // tm
    nzb = _NEW["nsa_z"] // NSA_WIDTH
    mnb = _NEW["mg_nsa"] // D_MODEL
    mgb = _NEW["mg_gla"] // D_MODEL
    row = lambda i: (i, 0)
    const2 = lambda i: (0, 0)
    return pl.pallas_call(
        _out_proj_kernel,
        grid=(n // tm,),
        in_specs=[pl.BlockSpec((tm, d), row),
                  pl.BlockSpec((1, 1, d), lambda i: (i // nt, 0, 0)),
                  pl.BlockSpec((tm, NSA_WIDTH), row),
                  pl.BlockSpec((tm, NSA_WIDTH), lambda i: (i, nzb)),
                  pl.BlockSpec((tm, D_MODEL), lambda i: (i, mnb)),
                  pl.BlockSpec((tm, D_MODEL), lambda i: (i, mgb)),
                  pl.BlockSpec((tm, GLA_WIDTH), row),
                  pl.BlockSpec((NSA_WIDTH, d), const2),
                  pl.BlockSpec((GLA_WIDTH, d), const2),
                  pl.BlockSpec((d, d), const2),
                  pl.BlockSpec((1, d), const2)],
        out_specs=pl.BlockSpec((tm, d), row),
        out_shape=jax.ShapeDtypeStruct((n, d), F32),
        compiler_params=pltpu.CompilerParams(dimension_semantics=("parallel",),
                                             vmem_limit_bytes=60 * 1024 * 1024),
        name="out_proj",
    )(x2, gate, o_nsa, pm, pm, pm, yg, wn, wg, wo, fg)


def _importance_matrix(nbp, ncp):
    ratio = SLC_BLOCK // CMP_STRIDE
    n_sub = CMP_BLOCK // CMP_STRIDE
    w = np.zeros((nbp, ncp), np.float32)
    for blk in range(nbp):
        for m in range(ratio):
            for s in range(n_sub):
                n = ratio * blk + m - s
                if 0 <= n < ncp:
                    w[blk, n] += 1.0
    return w


def _main_cols(a):
    parts = [a[..., start:start + width] * (Q_SCALE if nm == "nsa_q" else 1.0)
             for nm, pieces in _MAIN_PIECES for start, width in pieces]
    return jnp.concatenate(parts, axis=-1)


def _small_cols(a):
    pad = SMALL_WIDTH - 3 * NSA_HEADS - GLA_GATE_RANK
    parts = [a[..., _OLD["nsa_g"][0]:_OLD["nsa_g"][0] + 3 * NSA_HEADS],
             a[..., _OLD["ga"][0]:_OLD["ga"][0] + GLA_GATE_RANK],
             jnp.zeros(a.shape[:-1] + (pad,), a.dtype)]
    return jnp.concatenate(parts, axis=-1)


def kernel(x, c, w_ada, b_ada, norm_gain, w_in, b_in, cmp_pos_k, cmp_pos_v, cmp_w1_k, cmp_w2_k, cmp_w1_v, cmp_w2_v,
           gla_w_alpha, gla_b_alpha, gla_norm_gain, w_br_nsa, w_br_gla, w_out, final_norm_gain):
    assert DEPTH == 1, "the final rmsnorm is fused into the single layer's output kernel"
    B, T, D = x.shape
    G, HPG, DH = NSA_KV_GROUPS, NSA_HPG, NSA_HEAD_DIM
    N = B * T
    nch = T // CMP_STRIDE
    nc = nch - CMP_BLOCK // CMP_STRIDE + 1
    nb = T // SLC_BLOCK
    nbp = KSEL_FEAT - QFEAT
    assert nb <= nbp and T % NSA_TILE == 0
    n_sel = min(N_SELECT, nb)
    tk = NSA_TILE
    l = 0

    slopes = 2.0 ** (-8.0 * jnp.arange(1, NSA_HEADS + 1, dtype=F32) / NSA_HEADS)
    wimp = jnp.asarray(_importance_matrix(nbp, nch), BF16)
    tri = jnp.asarray(np.tril(np.ones((GLA_CHUNK, GLA_CHUNK), np.float32)), BF16)
    bp = ((B + 7) // 8) * 8
    c8 = jnp.pad(c, ((0, bp - B), (0, 0)))
    x2 = x.reshape(N, D)

    mod = _ada(c8, w_ada[l], b_ada[l][None, :])
    shift = mod[:B, :D].reshape(B, 1, D)
    scale = mod[:B, D:2 * D].reshape(B, 1, D)
    gate = mod[:B, 2 * D:].reshape(B, 1, D)

    pm, ps = _in_proj(x2, norm_gain[l][None, :], shift, scale,
                      _main_cols(w_in[l]).astype(BF16), _main_cols(b_in[l])[None, :],
                      _small_cols(w_in[l]).astype(BF16), _small_cols(b_in[l])[None, :], T=T)

    def cols(name, width=NSA_KV_WIDTH):
        return pm[:, _NEW[name]:_NEW[name] + width]

    kvf = cols("ck", 2 * NSA_KV_WIDTH).reshape(B, nch, CMP_STRIDE, 2, G, DH)
    kvf = kvf.transpose(3, 0, 4, 1, 2, 5).reshape(2, B, G, nch, CMP_STRIDE * DH)
    posf = jnp.stack([cmp_pos_k[l], cmp_pos_v[l]]).reshape(2, 1, CMP_BLOCK * DH)
    posf = jnp.broadcast_to(posf, (2, 8, CMP_BLOCK * DH)).astype(BF16)
    w1 = jnp.stack([cmp_w1_k[l], cmp_w1_v[l]]).astype(BF16)
    w2 = jnp.stack([cmp_w2_k[l], cmp_w2_v[l]]).astype(BF16)
    kvc = _compress(kvf, posf, w1, w2)
    gt = ps[:, :3 * NSA_HEADS].reshape(B, T, G, 3 * HPG).transpose(0, 2, 3, 1)
    slopes2 = slopes * LOG2E
    o_nsa = _nsa(slopes2, pm, kvc[0], kvc[1].transpose(0, 1, 3, 2), gt, wimp, B=B, T=T, nc=nc, n_sel=n_sel)

    wal = jnp.pad(gla_w_alpha[l], ((GA_LANE, SMALL_WIDTH - GA_LANE - GLA_GATE_RANK), (0, 0)))
    y_gla = _gla(pm, ps, wal, gla_b_alpha[l][None, :], gla_norm_gain[l][None, :], tri, B=B, T=T)

    out = _out_proj(x2, gate, o_nsa, pm, y_gla,
                    w_br_nsa[l].astype(BF16), w_br_gla[l].astype(BF16), w_out[l].astype(BF16),
                    final_norm_gain[None, :], T=T)
    return out.reshape(B, T, D)
```

```python
import functools

import numpy as np
import jax
import jax.numpy as jnp
from jax import lax
from jax.experimental import pallas as pl
from jax.experimental.pallas import tpu as pltpu

D_MODEL = 2048
DEPTH = 1
NSA_HEADS = 16
NSA_KV_GROUPS = 4
NSA_HPG = NSA_HEADS // NSA_KV_GROUPS
NSA_HEAD_DIM = 64
CMP_BLOCK = 32
CMP_STRIDE = 16
CMP_HIDDEN = 256
SLC_BLOCK = 64
N_SELECT = 16
WINDOW = 512
NSA_WIDTH = NSA_HEADS * NSA_HEAD_DIM
NSA_KV_WIDTH = NSA_KV_GROUPS * NSA_HEAD_DIM
GLA_HEADS = 4
GLA_KEY_DIM = 128
GLA_VAL_DIM = 256
GLA_GATE_RANK = 16
GLA_TAU = 16.0
GLA_KEY_WIDTH = GLA_HEADS * GLA_KEY_DIM
GLA_WIDTH = GLA_HEADS * GLA_VAL_DIM
EPS = 1e-6
NEG = -1e30

F32 = jnp.float32
BF16 = jnp.bfloat16

_OLD = {}
_off = 0
for _name, _w in (("nsa_q", NSA_WIDTH), ("ck", NSA_KV_WIDTH), ("cv", NSA_KV_WIDTH), ("sk", NSA_KV_WIDTH),
                  ("sv", NSA_KV_WIDTH), ("wk", NSA_KV_WIDTH), ("wv", NSA_KV_WIDTH), ("nsa_g", 3 * NSA_HEADS),
                  ("nsa_z", NSA_WIDTH), ("gq", GLA_KEY_WIDTH), ("gk", GLA_KEY_WIDTH), ("gv", GLA_WIDTH),
                  ("ga", GLA_GATE_RANK), ("gla_z", GLA_WIDTH), ("mg_nsa", D_MODEL), ("mg_gla", D_MODEL)):
    _OLD[_name] = (_off, _w)
    _off += _w

def _pair_by_group(a, b):
    return [(_OLD[nm][0] + g * NSA_HEAD_DIM, NSA_HEAD_DIM) for g in range(NSA_KV_GROUPS) for nm in (a, b)]


_MAIN_PIECES = [(nm, [_OLD[nm]]) for nm in
                ("nsa_q", "nsa_z", "gla_z", "gv", "mg_nsa", "mg_gla", "gq", "gk", "ck", "cv")]
_MAIN_PIECES += [("sel_win_k", _pair_by_group("sk", "wk")), ("sel_win_v", _pair_by_group("sv", "wv"))]
_NEW = {}
_off = 0
for _name, _pieces in _MAIN_PIECES:
    _NEW[_name] = _off
    _off += sum(w for _, w in _pieces)
MAIN_WIDTH = _off
SMALL_WIDTH = 128
GA_LANE = 3 * NSA_HEADS

LANE = 128
GLA_CHUNK = 64
GLA_SUB = 16
NSA_TILE = 256
QFEAT = 2 * NSA_HEAD_DIM
KSEL_FEAT = QFEAT + LANE
N_SLOPE_TERMS = 3
SEL_MASK = 16384.0
BITS_PER_WORD = 16
LOG2E = 1.4426950408889634
Q_SCALE = LOG2E * NSA_HEAD_DIM ** -0.5


def _dot(a, b):
    return jnp.dot(a, b, preferred_element_type=F32)


def _dot_nt(a, b):
    return lax.dot_general(a, b, (((1,), (1,)), ((), ())), preferred_element_type=F32)


def _split3(x):
    hi = x.astype(BF16)
    r = x - hi.astype(F32)
    mid = r.astype(BF16)
    lo = (r - mid.astype(F32)).astype(BF16)
    return hi, mid, lo


def _sigmoid(x):
    return 1.0 / (1.0 + jnp.exp(-x))


def _ada_kernel(c_ref, w_ref, b_ref, o_ref):
    ch, cm, cl = _split3(c_ref[...])
    wh, wm, wl = _split3(w_ref[...])
    acc = _dot(ch, wh) + _dot(ch, wm) + _dot(cm, wh) + _dot(ch, wl) + _dot(cl, wh) + _dot(cm, wm)
    o_ref[...] = acc + b_ref[...]


def _ada(c8, w, b, *, tn=768):
    m, d = c8.shape
    n = w.shape[1]
    return pl.pallas_call(
        _ada_kernel,
        grid=(n // tn,),
        in_specs=[pl.BlockSpec((m, d), lambda j: (0, 0)),
                  pl.BlockSpec((d, tn), lambda j: (0, j)),
                  pl.BlockSpec((1, tn), lambda j: (0, j))],
        out_specs=pl.BlockSpec((m, tn), lambda j: (0, j)),
        out_shape=jax.ShapeDtypeStruct((m, n), F32),
        compiler_params=pltpu.CompilerParams(dimension_semantics=("arbitrary",),
                                             vmem_limit_bytes=40 * 1024 * 1024),
        name="ada",
    )(c8, w, b)


WPREP_TILE = 512
WPREP_WINDOW = 6


def _wprep_tiles():
    tiles = []
    for name, pieces in _MAIN_PIECES:
        scale = Q_SCALE if name == "nsa_q" else 1.0
        if len(pieces) == 1:
            start, width = pieces[0]
            assert width % WPREP_TILE == 0 or (name in ("ck", "cv") and width * 2 == WPREP_TILE)
            if name == "cv":
                assert start == _OLD["ck"][0] + _OLD["ck"][1]
                continue
            for off in range(0, max(width, WPREP_TILE), WPREP_TILE):
                tiles.append((start + off, (start + off) % LANE, scale))
        else:
            first, second = pieces[0][0], pieces[1][0]
            assert second == first + WPREP_TILE and first % LANE == 0 and len(pieces) * NSA_HEAD_DIM == WPREP_TILE
            tiles.append((first, -1, scale))
    return tiles


def _wprep_kernel(start_ref, shift_ref, scale_ref, b0, b1, b2, b3, b4, b5, o_ref):
    j = pl.program_id(0)
    blocks = (b0, b1, b2, b3, b4, b5)
    lane = lax.broadcasted_iota(jnp.int32, (1, LANE), 1)
    shift = shift_ref[j]
    scale = scale_ref[j]

    @pl.when(shift >= 0)
    def _():
        amt = jnp.bitwise_and(LANE - shift, LANE - 1)
        rolled = [pltpu.roll(b[...], amt, 1) for b in blocks[:WPREP_TILE // LANE + 1]]
        keep = lane < LANE - shift
        for cb in range(WPREP_TILE // LANE):
            o_ref[:, cb * LANE:(cb + 1) * LANE] = (jnp.where(keep, rolled[cb], rolled[cb + 1]) * scale).astype(o_ref.dtype)

    @pl.when(shift < 0)
    def _():
        half = NSA_HEAD_DIM
        for g in range(NSA_KV_GROUPS):
            a = blocks[g // 2][...]
            b = blocks[WPREP_TILE // LANE + g // 2][...]
            if g % 2 == 1:
                a = pltpu.roll(a, half, 1)
            else:
                b = pltpu.roll(b, half, 1)
            o_ref[:, g * LANE:(g + 1) * LANE] = (jnp.where(lane < half, a, b) * scale).astype(o_ref.dtype)


def _wprep(w):
    d, width = w.shape
    tiles = _wprep_tiles()
    assert len(tiles) * WPREP_TILE == MAIN_WIDTH
    last = (width - 1) // LANE
    start = jnp.asarray([t[0] // LANE for t in tiles], jnp.int32)
    shift = jnp.asarray([t[1] for t in tiles], jnp.int32)
    scale = jnp.asarray([t[2] for t in tiles], F32)
    src = [pl.BlockSpec((d, LANE), lambda j, st, sh, k=k: (0, jnp.minimum(st[j] + k, last)))
           for k in range(WPREP_WINDOW)]
    return pl.pallas_call(
        _wprep_kernel,
        grid_spec=pltpu.PrefetchScalarGridSpec(
            num_scalar_prefetch=2,
            grid=(len(tiles),),
            in_specs=[pl.BlockSpec(memory_space=pltpu.SMEM)] + src,
            out_specs=pl.BlockSpec((d, WPREP_TILE), lambda j, st, sh: (0, j))),
        out_shape=jax.ShapeDtypeStruct((d, MAIN_WIDTH), BF16),
        compiler_params=pltpu.CompilerParams(dimension_semantics=("arbitrary",),
                                             vmem_limit_bytes=40 * 1024 * 1024),
        name="wprep",
    )(start, shift, scale, *([w] * WPREP_WINDOW))


def _in_proj_kernel(x_ref, gain_ref, shift_ref, scale_ref, w_ref, b_ref, ws_ref, bs_ref,
                    o_ref, os_ref, h_ref):
    j = pl.program_id(1)

    @pl.when(j == 0)
    def _():
        x = x_ref[...]
        ms = jnp.mean(x * x, axis=-1, keepdims=True)
        y = x * lax.rsqrt(ms + EPS) * gain_ref[...]
        y = y * (1.0 + scale_ref[0]) + shift_ref[0]
        h = y.astype(BF16)
        h_ref[...] = h
        os_ref[...] = _dot(h, ws_ref[...]) + bs_ref[...]

    o_ref[...] = (_dot(h_ref[...], w_ref[...]) + b_ref[...]).astype(o_ref.dtype)


def _in_proj(x2, gain, shift, scale, w_main, b_main, w_small, b_small, *, T, tm=1024, tn=1536):
    n, d = x2.shape
    nm = w_main.shape[1]
    nt = T // tm
    return pl.pallas_call(
        _in_proj_kernel,
        grid=(n // tm, nm // tn),
        in_specs=[pl.BlockSpec((tm, d), lambda i, j: (i, 0)),
                  pl.BlockSpec((1, d), lambda i, j: (0, 0)),
                  pl.BlockSpec((1, 1, d), lambda i, j: (i // nt, 0, 0)),
                  pl.BlockSpec((1, 1, d), lambda i, j: (i // nt, 0, 0)),
                  pl.BlockSpec((d, tn), lambda i, j: (0, j)),
                  pl.BlockSpec((1, tn), lambda i, j: (0, j)),
                  pl.BlockSpec((d, SMALL_WIDTH), lambda i, j: (0, 0)),
                  pl.BlockSpec((1, SMALL_WIDTH), lambda i, j: (0, 0))],
        out_specs=[pl.BlockSpec((tm, tn), lambda i, j: (i, j)),
                   pl.BlockSpec((tm, SMALL_WIDTH), lambda i, j: (i, 0))],
        out_shape=[jax.ShapeDtypeStruct((n, nm), BF16),
                   jax.ShapeDtypeStruct((n, SMALL_WIDTH), F32)],
        scratch_shapes=[pltpu.VMEM((tm, d), BF16)],
        compiler_params=pltpu.CompilerParams(dimension_semantics=("parallel", "arbitrary"),
                                             vmem_limit_bytes=56 * 1024 * 1024),
        name="in_proj",
    )(x2, gain, shift, scale, w_main, b_main, w_small, b_small)


def _compress_kernel(kv_ref, pos_ref, w1_ref, w2_ref, o_ref):
    y = kv_ref[0, 0, 0]
    w1 = w1_ref[0]
    half = y.shape[1]
    nch = y.shape[0]
    z1 = _dot(y, w1[:half])
    z2 = _dot(y, w1[half:])
    posb = _dot(pos_ref[0], w1)[0:1]
    pre = z1 + pltpu.roll(z2, nch - 1, 0) + posb
    hid = pre * _sigmoid(pre)
    o_ref[0, 0, 0] = _dot(hid.astype(BF16), w2_ref[0]).astype(o_ref.dtype)


def _compress(kvf, posf, w1, w2):
    two, b, g, nch, half = kvf.shape
    hidden = w1.shape[2]
    dh = w2.shape[2]
    return pl.pallas_call(
        _compress_kernel,
        grid=(two, b, g),
        in_specs=[pl.BlockSpec((1, 1, 1, nch, half), lambda s, i, j: (s, i, j, 0, 0)),
                  pl.BlockSpec((1, 8, 2 * half), lambda s, i, j: (s, 0, 0)),
                  pl.BlockSpec((1, 2 * half, hidden), lambda s, i, j: (s, 0, 0)),
                  pl.BlockSpec((1, hidden, dh), lambda s, i, j: (s, 0, 0))],
        out_specs=pl.BlockSpec((1, 1, 1, nch, dh), lambda s, i, j: (s, i, j, 0, 0)),
        out_shape=jax.ShapeDtypeStruct((two, b, g, nch, dh), BF16),
        compiler_params=pltpu.CompilerParams(dimension_semantics=("parallel", "parallel", "parallel")),
        name="compress",
    )(kvf, posf, w1, w2)


def _select_blocks(imp, t_row, n_sel):
    nbp = imp.shape[0]
    blk = lax.broadcasted_iota(jnp.int32, (nbp, 1), 0)
    cur = lax.shift_right_logical(t_row, 6)
    forced = (blk == 0) | (blk == cur) | (blk == cur - 1)
    bvalid = blk * SLC_BLOCK <= t_row
    score = jnp.where(forced, 1e30, jnp.where(bvalid, imp, -1.0))
    blk_f = blk.astype(F32)
    sel = jnp.zeros(imp.shape, F32)
    for _ in range(n_sel):
        m = jnp.max(score, axis=0, keepdims=True)
        first = jnp.min(jnp.where(score == m, blk_f, float(nbp)), axis=0, keepdims=True)
        hit = blk_f == first
        sel = jnp.where(hit, 1.0, sel)
        score = jnp.where(hit, -1.0, score)
    return sel


def _softmax_step(s, vt, d, m_sc, l_sc, acc_sc, s_max=None):
    if s_max is None:
        s_max = jnp.max(s, axis=0, keepdims=True)
    m_prev = m_sc[...]
    m_new = jnp.maximum(m_prev, s_max - d)
    alpha = jnp.exp2(m_prev - m_new)
    p = jnp.exp2(s - (m_new + d))
    l_sc[...] = alpha * l_sc[...] + jnp.sum(p, axis=0, keepdims=True)
    acc_sc[...] = alpha * acc_sc[...] + _dot(vt, p.astype(BF16))
    m_sc[...] = m_new


def _stage_group(qslab_ref, kslab_ref, vslab_ref, qt_sc, ks_sc, kw_sc, vt_sc, nt, tk):
    dh = NSA_HEAD_DIM
    lane = lax.broadcasted_iota(jnp.int32, (tk, LANE), 1)
    row = lax.broadcasted_iota(jnp.int32, (tk, LANE), 0)
    pos = row.astype(F32)
    feat_sel = jnp.where((lane >= dh) & (lane < dh + N_SLOPE_TERMS), pos, 0.0)
    feat_win = jnp.where(lane < N_SLOPE_TERMS, pos, 0.0)
    blk_in_tile = lax.shift_right_logical(row, 6)

    def body(kt, carry):
        start = pl.multiple_of(kt * tk, tk)
        kk = kslab_ref[pl.ds(start, tk), :].astype(F32)
        onehot = jnp.where(kt * (tk // SLC_BLOCK) + blk_in_tile == lane, 1.0, 0.0)
        ks_sc[kt] = jnp.concatenate([jnp.where(lane < dh, kk, feat_sel), onehot], axis=1).astype(BF16)
        kw_sc[kt] = jnp.where(lane >= dh, kk, feat_win).astype(BF16)
        vt_sc[kt] = vslab_ref[pl.ds(start, tk), :].astype(F32).T.astype(BF16)
        x_t = qslab_ref[pl.ds(start, tk), :].astype(F32).T
        qt_sc[kt] = jnp.concatenate([x_t[hh * dh:(hh + 1) * dh] for hh in range(NSA_HPG)], axis=1).astype(BF16)
        return carry

    lax.fori_loop(0, nt, body, 0)


def _nsa_kernel(slopes_ref, q_ref, kslab_ref, vslab_ref, kc_ref, vct_ref, gt_ref, wimp_ref,
                o_ref, q2_sc, qw_sc, ks_sc, kw_sc, qt_sc, vt_sc, sa_sc, sb_sc, m_sc, l_sc, acc_sc, br_sc, list_sc,
                *, tq, nt, nc, n_sel):
    g = pl.program_id(1)
    qi = pl.program_id(2)
    tk = tq
    dh = NSA_HEAD_DIM
    hq = NSA_HPG * tq

    @pl.when(qi == 0)
    def _():
        _stage_group(q_ref, kslab_ref, vslab_ref, qt_sc, ks_sc, kw_sc, vt_sc, nt, tk)

    lane = lax.broadcasted_iota(jnp.int32, (1, hq), 1)
    slope_row = jnp.zeros((1, hq), F32)
    for hh in range(NSA_HPG):
        slope_row = jnp.where(lane >= hh * tq, slopes_ref[g * NSA_HPG + hh], slope_row)
    t_one = qi * tq + lax.broadcasted_iota(jnp.int32, (1, tq), 1)
    t_row = jnp.concatenate([t_one] * NSA_HPG, axis=1)
    t_f = t_row.astype(F32)
    j_loc = lax.broadcasted_iota(jnp.int32, (tk, 1), 0)

    q_t = qt_sc[qi]
    feat_row = lax.broadcasted_iota(jnp.int32, (dh, hq), 0)
    terms = [t.astype(F32) for t in _split3(slope_row)]
    feat = jnp.zeros((dh, hq), F32)
    for i, term in enumerate(terms):
        feat = jnp.where(feat_row == i, term, feat)
    feat = feat.astype(BF16)
    q2_sc[0:dh, :] = q_t
    q2_sc[dh:QFEAT, :] = feat
    qw_sc[0:dh, :] = feat
    qw_sc[dh:QFEAT, :] = q_t

    def reset_stats():
        m_sc[...] = jnp.full(m_sc.shape, NEG, F32)
        l_sc[...] = jnp.zeros(l_sc.shape, F32)
        acc_sc[...] = jnp.zeros(acc_sc.shape, F32)

    def per_head(a):
        return [a[:, hh * tq:(hh + 1) * tq] for hh in range(NSA_HPG)]

    kc = kc_ref[0, 0]
    ncp = kc.shape[0]
    n_col = lax.broadcasted_iota(jnp.int32, (ncp, 1), 0)
    ce = jnp.where(n_col < nc, n_col * CMP_STRIDE + (CMP_BLOCK - 1), 2 ** 30)
    ce_rel = (ce - qi * tq).astype(F32)
    s = jnp.where(ce <= t_row, _dot(kc, q_t) + slope_row * ce_rel, NEG)
    m = jnp.max(s, axis=0, keepdims=True)
    p = jnp.exp2(s - m)
    l = jnp.sum(p, axis=0, keepdims=True)
    p = p * jnp.where(t_row >= CMP_BLOCK - 1, 1.0 / l, 0.0)
    br_sc[0] = _dot(vct_ref[0, 0], p.astype(BF16))
    ps4 = per_head(p)
    psum = (ps4[0] + ps4[1]) + (ps4[2] + ps4[3])
    wimp = wimp_ref[...]
    ph, pm, plo = _split3(psum)
    imp = _dot(wimp, ph) + _dot(wimp, pm) + _dot(wimp, plo)

    def shift_of(kv, off=None):
        d = slope_row * (t_f - (kv * tk).astype(F32))
        return d if off is None else d + jnp.where(off, -NEG, 0.0)

    reset_stats()
    n_back = WINDOW // tk
    qw = qw_sc[...]
    tiles = [jnp.maximum(qi - w, 0) for w in range(n_back + 1)]
    logits = [_dot(kw_sc[kv], qw) for kv in tiles]
    for w, (kv, s) in enumerate(zip(tiles, logits)):
        if w == 0:
            s = jnp.where((kv * tk + j_loc) <= t_row, s, NEG)
        elif w == n_back:
            s = jnp.where((t_row - (kv * tk + j_loc)) < WINDOW, s, NEG)
        _softmax_step(s, vt_sc[kv, dh:2 * dh, :], shift_of(kv, qi < w), m_sc, l_sc, acc_sc)
    br_sc[1] = acc_sc[...] * (1.0 / l_sc[...])

    sa_sc[0:tk, :] = _dot(ks_sc[0, :, 0:QFEAT], q2_sc[0:QFEAT, :])

    sel = _select_blocks(imp, t_one, n_sel)
    penalty_f = (sel - 1.0) * SEL_MASK
    penalty = jnp.concatenate([penalty_f.astype(BF16)] * NSA_HPG, axis=1)
    q2_sc[QFEAT:KSEL_FEAT, :] = penalty

    per_tile = tk // SLC_BLOCK
    any_tok = jnp.max(sel, axis=1, keepdims=True)
    blk_col = lax.broadcasted_iota(jnp.int32, any_tok.shape, 0)
    bit = jnp.left_shift(1, jnp.bitwise_and(blk_col, BITS_PER_WORD - 1)).astype(F32)
    bits = jnp.where(any_tok > 0.5, bit, 0.0)
    n_words = -(-nt * per_tile // BITS_PER_WORD)
    words = [jnp.sum(bits[w * BITS_PER_WORD:(w + 1) * BITS_PER_WORD]).astype(jnp.int32) for w in range(n_words)]
    for kt in range(nt + 2):
        list_sc[kt] = qi
    n_act = jnp.int32(0)
    tile_mask = (1 << per_tile) - 1
    for kt in range(nt):
        list_sc[n_act] = kt
        w, sh = divmod(kt * per_tile, BITS_PER_WORD)
        hit = jnp.bitwise_and(words[w] >> sh, tile_mask) != 0
        n_act = n_act + hit.astype(jnp.int32)
    list_sc[n_act] = qi
    n_plain = n_act - 1

    reset_stats()

    def sel_logits(idx, s_ref):
        s = _dot(ks_sc[list_sc[idx]], q2_sc[...])
        s_ref[0:tk, :] = s
        s_ref[tk:tk + 1, :] = jnp.max(s, axis=0, keepdims=True)

    def sel_update(idx, s_ref):
        kv = list_sc[idx]
        _softmax_step(s_ref[0:tk, :], vt_sc[kv, 0:dh, :], shift_of(kv, idx >= n_plain), m_sc, l_sc, acc_sc,
                      s_max=s_ref[tk:tk + 1, :])

    pen0 = jnp.concatenate([penalty_f[0:per_tile]] * NSA_HPG, axis=1)
    pen0 = jnp.concatenate([jnp.broadcast_to(pen0[b:b + 1], (SLC_BLOCK, hq)) for b in range(per_tile)], axis=0)
    s0 = sa_sc[0:tk, :] + pen0
    sa_sc[0:tk, :] = s0
    sa_sc[tk:tk + 1, :] = jnp.max(s0, axis=0, keepdims=True)

    def pair_body(i, carry):
        sel_logits(2 * i + 1, sb_sc)
        sel_update(2 * i, sa_sc)
        sel_logits(2 * i + 2, sa_sc)
        sel_update(2 * i + 1, sb_sc)
        return carry

    lax.fori_loop(0, lax.shift_right_logical(n_plain + 1, 1), pair_body, 0)
    causal = (qi * tk + j_loc) <= t_row
    _softmax_step(jnp.where(causal, sa_sc[0:tk, :], NEG), vt_sc[qi, 0:dh, :], shift_of(qi), m_sc, l_sc, acc_sc)
    o_sel = acc_sc[...] * (1.0 / l_sc[...])

    sg = _sigmoid(gt_ref[0, 0])
    gates = [jnp.concatenate([sg[3 * hh + br:3 * hh + br + 1] for hh in range(NSA_HPG)], axis=1)
             for br in range(3)]
    o_all = gates[0] * br_sc[0] + gates[1] * o_sel + gates[2] * br_sc[1]
    o_ref[...] = jnp.concatenate(per_head(o_all), axis=0).T.astype(o_ref.dtype)


def _nsa(slopes, pm, kc, vct, gt, wimp, *, B, T, nc, n_sel):
    b, g, hpg, dh = B, NSA_KV_GROUPS, NSA_HPG, NSA_HEAD_DIM
    tq = NSA_TILE
    nt = T // tq
    hq = hpg * tq
    ncp = kc.shape[2]
    nbp = wimp.shape[0]
    qb = _NEW["nsa_q"] // (hpg * dh)
    kb, vb = _NEW["sel_win_k"] // LANE, _NEW["sel_win_v"] // LANE
    kern = functools.partial(_nsa_kernel, tq=tq, nt=nt, nc=nc, n_sel=n_sel)
    return pl.pallas_call(
        kern,
        grid=(b, g, nt),
        in_specs=[pl.BlockSpec(memory_space=pltpu.SMEM),
                  pl.BlockSpec((T, hpg * dh), lambda i, j, k: (i, qb + j)),
                  pl.BlockSpec((T, LANE), lambda i, j, k: (i, kb + j)),
                  pl.BlockSpec((T, LANE), lambda i, j, k: (i, vb + j)),
                  pl.BlockSpec((1, 1, ncp, dh), lambda i, j, k: (i, j, 0, 0)),
                  pl.BlockSpec((1, 1, dh, ncp), lambda i, j, k: (i, j, 0, 0)),
                  pl.BlockSpec((1, 1, 3 * hpg, tq), lambda i, j, k: (i, j, 0, k)),
                  pl.BlockSpec((nbp, ncp), lambda i, j, k: (0, 0))],
        out_specs=pl.BlockSpec((tq, hpg * dh), lambda i, j, k: (i * nt + k, j)),
        out_shape=jax.ShapeDtypeStruct((b * T, g * hpg * dh), BF16),
        scratch_shapes=[pltpu.VMEM((KSEL_FEAT, hq), BF16), pltpu.VMEM((QFEAT, hq), BF16),
                        pltpu.VMEM((nt, tq, KSEL_FEAT), BF16), pltpu.VMEM((nt, tq, QFEAT), BF16),
                        pltpu.VMEM((nt, dh, hq), BF16), pltpu.VMEM((nt, 2 * dh, tq), BF16),
                        pltpu.VMEM((tq + 8, hq), F32), pltpu.VMEM((tq + 8, hq), F32),
                        pltpu.VMEM((1, hq), F32), pltpu.VMEM((1, hq), F32),
                        pltpu.VMEM((dh, hq), F32),
                        pltpu.VMEM((2, dh, hq), F32),
                        pltpu.SMEM((nt + 2,), jnp.int32)],
        compiler_params=pltpu.CompilerParams(dimension_semantics=("parallel", "parallel", "arbitrary"),
                                             vmem_limit_bytes=48 * 1024 * 1024),
        name="nsa",
    )(slopes, pm, pm, pm, kc, vct, gt, wimp)


def _gla_intra_scores(q, k, bcum):
    c, dk = q.shape
    sb = GLA_SUB
    row_c = lax.broadcasted_iota(jnp.int32, (c, dk), 0)
    row_s = lax.broadcasted_iota(jnp.int32, (sb, 1), 0)
    lane_s = lax.broadcasted_iota(jnp.int32, (sb, c), 1)
    blocks = []
    for i0 in range(0, c, sb):
        bi = bcum[i0:i0 + sb]
        qi = q[i0:i0 + sb]
        a_blk = jnp.zeros((sb, c), F32)
        for jl in range(sb):
            j = i0 + jl
            w = jnp.exp2(bi - bcum[j:j + 1])
            a = jnp.sum(qi * k[j:j + 1] * w, axis=-1, keepdims=True)
            a_blk = jnp.where(lane_s == j, jnp.where(row_s >= jl, a, 0.0), a_blk)
        if i0 > 0:
            r = bcum[i0 - 1:i0]
            qh = qi * jnp.exp2(bi - r)
            kh = k * jnp.exp2(jnp.where(row_c < i0, r - bcum, NEG))
            a_blk = a_blk + _dot_nt(qh.astype(BF16), kh.astype(BF16))
        blocks.append(a_blk)
    return jnp.concatenate(blocks, axis=0)


def _gla_kernel(q_ref, k_ref, v_ref, z_ref, ps_ref, wal_ref, bal_ref, gain_ref, tri_ref, o_ref,
                st_sc, *, ts):
    ti = pl.program_id(2)
    c = GLA_CHUNK

    @pl.when(ti == 0)
    def _():
        st_sc[...] = jnp.zeros(st_sc.shape, F32)

    ps = ps_ref[...]
    ph, pm, _ = _split3(ps)
    wh, wm, _ = _split3(wal_ref[...])
    zz = _dot(ph, wh) + _dot(ph, wm) + _dot(pm, wh) + bal_ref[...]
    log_a = (jnp.minimum(zz, 0.0) - jnp.log(1.0 + jnp.exp(-jnp.abs(zz)))) * (LOG2E / GLA_TAU)
    tri = tri_ref[...]
    qscale = GLA_KEY_DIM ** -0.5
    for ci in range(ts // c):
        sl = slice(ci * c, (ci + 1) * c)
        gh, gm, gl = _split3(log_a[sl])
        bcum = _dot(tri, gh) + _dot(tri, gm) + _dot(tri, gl)
        q = q_ref[sl, :].astype(F32) * qscale
        k = k_ref[sl, :].astype(F32)
        v = v_ref[sl, :]
        a_mat = _gla_intra_scores(q, k, bcum)
        st = st_sc[...]
        o = _dot(a_mat.astype(BF16), v) + _dot_nt((q * jnp.exp2(bcum)).astype(BF16), st.astype(BF16))
        b_last = bcum[c - 1:c, :]
        kd = k * jnp.exp2(b_last - bcum)
        st_sc[...] = st * jnp.exp2(b_last) + _dot(v.astype(F32).T.astype(BF16), kd.astype(BF16))
        ms = jnp.mean(o * o, axis=-1, keepdims=True)
        y = o * lax.rsqrt(ms + EPS) * gain_ref[...]
        z = z_ref[sl, :].astype(F32)
        o_ref[sl, :] = (y * (z * _sigmoid(z))).astype(o_ref.dtype)


def _gla(pm, ps, wal, bal, gain, tri, *, B, T, ts=1024):
    n = pm.shape[0]
    nts = T // ts
    dk, dv = GLA_KEY_DIM, GLA_VAL_DIM
    qb, kb = _NEW["gq"] // dk, _NEW["gk"] // dk
    vb, zb = _NEW["gv"] // dv, _NEW["gla_z"] // dv
    kern = functools.partial(_gla_kernel, ts=ts)
    return pl.pallas_call(
        kern,
        grid=(B, GLA_HEADS, nts),
        in_specs=[pl.BlockSpec((ts, dk), lambda b, h, i: (b * nts + i, qb + h)),
                  pl.BlockSpec((ts, dk), lambda b, h, i: (b * nts + i, kb + h)),
                  pl.BlockSpec((ts, dv), lambda b, h, i: (b * nts + i, vb + h)),
                  pl.BlockSpec((ts, dv), lambda b, h, i: (b * nts + i, zb + h)),
                  pl.BlockSpec((ts, SMALL_WIDTH), lambda b, h, i: (b * nts + i, 0)),
                  pl.BlockSpec((SMALL_WIDTH, dk), lambda b, h, i: (0, h)),
                  pl.BlockSpec((1, dk), lambda b, h, i: (0, h)),
                  pl.BlockSpec((1, dv), lambda b, h, i: (0, 0)),
                  pl.BlockSpec((GLA_CHUNK, GLA_CHUNK), lambda b, h, i: (0, 0))],
        out_specs=pl.BlockSpec((ts, dv), lambda b, h, i: (b * nts + i, h)),
        out_shape=jax.ShapeDtypeStruct((n, GLA_WIDTH), BF16),
        scratch_shapes=[pltpu.VMEM((dv, dk), F32)],
        compiler_params=pltpu.CompilerParams(dimension_semantics=("parallel", "parallel", "arbitrary")),
        name="gla",
    )(pm, pm, pm, pm, ps, wal, bal, gain, tri)


def _out_proj_kernel(x_ref, gate_ref, on_ref, nz_ref, mgn_ref, mgg_ref, yg_ref,
                     wn_ref, wg_ref, wo_ref, fg_ref, o_ref):
    nz = nz_ref[...].astype(F32)
    y_nsa = on_ref[...].astype(F32) * (nz * _sigmoid(nz))
    a = _dot(y_nsa.astype(BF16), wn_ref[...])
    bm = _dot(yg_ref[...], wg_ref[...])
    merged = _sigmoid(mgn_ref[...].astype(F32)) * a + _sigmoid(mgg_ref[...].astype(F32)) * bm
    xn = x_ref[...] + gate_ref[0] * _dot(merged.astype(BF16), wo_ref[...])
    ms = jnp.mean(xn * xn, axis=-1, keepdims=True)
    o_ref[...] = xn * lax.rsqrt(ms + EPS) * fg_ref[...]


def _out_proj(x2, gate, o_nsa, pm, yg, wn, wg, wo, fg, *, T, tm=256):
    n, d = x2.shape
    nt = T // tm
    nzb = _NEW["nsa_z"] // NSA_WIDTH
    mnb = _NEW["mg_nsa"] // D_MODEL
    mgb = _NEW["mg_gla"] // D_MODEL
    row = lambda i: (i, 0)
    const2 = lambda i: (0, 0)
    return pl.pallas_call(
        _out_proj_kernel,
        grid=(n // tm,),
        in_specs=[pl.BlockSpec((tm, d), row),
                  pl.BlockSpec((1, 1, d), lambda i: (i // nt, 0, 0)),
                  pl.BlockSpec((tm, NSA_WIDTH), row),
                  pl.BlockSpec((tm, NSA_WIDTH), lambda i: (i, nzb)),
                  pl.BlockSpec((tm, D_MODEL), lambda i: (i, mnb)),
                  pl.BlockSpec((tm, D_MODEL), lambda i: (i, mgb)),
                  pl.BlockSpec((tm, GLA_WIDTH), row),
                  pl.BlockSpec((NSA_WIDTH, d), const2),
                  pl.BlockSpec((GLA_WIDTH, d), const2),
                  pl.BlockSpec((d, d), const2),
                  pl.BlockSpec((1, d), const2)],
        out_specs=pl.BlockSpec((tm, d), row),
        out_shape=jax.ShapeDtypeStruct((n, d), F32),
        compiler_params=pltpu.CompilerParams(dimension_semantics=("parallel",),
                                             vmem_limit_bytes=60 * 1024 * 1024),
        name="out_proj",
    )(x2, gate, o_nsa, pm, pm, pm, yg, wn, wg, wo, fg)


def _importance_matrix(nbp, ncp):
    ratio = SLC_BLOCK // CMP_STRIDE
    n_sub = CMP_BLOCK // CMP_STRIDE
    w = np.zeros((nbp, ncp), np.float32)
    for blk in range(nbp):
        for m in range(ratio):
            for s in range(n_sub):
                n = ratio * blk + m - s
                if 0 <= n < ncp:
                    w[blk, n] += 1.0
    return w


def _main_cols(a):
    parts = [a[..., start:start + width] * (Q_SCALE if nm == "nsa_q" else 1.0)
             for nm, pieces in _MAIN_PIECES for start, width in pieces]
    return jnp.concatenate(parts, axis=-1)


def _small_cols(a):
    pad = SMALL_WIDTH - 3 * NSA_HEADS - GLA_GATE_RANK
    parts = [a[..., _OLD["nsa_g"][0]:_OLD["nsa_g"][0] + 3 * NSA_HEADS],
             a[..., _OLD["ga"][0]:_OLD["ga"][0] + GLA_GATE_RANK],
             jnp.zeros(a.shape[:-1] + (pad,), a.dtype)]
    return jnp.concatenate(parts, axis=-1)


def kernel(x, c, w_ada, b_ada, norm_gain, w_in, b_in, cmp_pos_k, cmp_pos_v, cmp_w1_k, cmp_w2_k, cmp_w1_v, cmp_w2_v,
           gla_w_alpha, gla_b_alpha, gla_norm_gain, w_br_nsa, w_br_gla, w_out, final_norm_gain):
    assert DEPTH == 1, "the final rmsnorm is fused into the single layer's output kernel"
    B, T, D = x.shape
    G, HPG, DH = NSA_KV_GROUPS, NSA_HPG, NSA_HEAD_DIM
    N = B * T
    nch = T // CMP_STRIDE
    nc = nch - CMP_BLOCK // CMP_STRIDE + 1
    nb = T // SLC_BLOCK
    nbp = KSEL_FEAT - QFEAT
    assert nb <= nbp and T % NSA_TILE == 0
    n_sel = min(N_SELECT, nb)
    tk = NSA_TILE
    l = 0

    slopes = 2.0 ** (-8.0 * jnp.arange(1, NSA_HEADS + 1, dtype=F32) / NSA_HEADS)
    wimp = jnp.asarray(_importance_matrix(nbp, nch), BF16)
    tri = jnp.asarray(np.tril(np.ones((GLA_CHUNK, GLA_CHUNK), np.float32)), BF16)
    bp = ((B + 7) // 8) * 8
    c8 = jnp.pad(c, ((0, bp - B), (0, 0)))
    x2 = x.reshape(N, D)

    mod = _ada(c8, w_ada[l], b_ada[l][None, :])
    shift = mod[:B, :D].reshape(B, 1, D)
    scale = mod[:B, D:2 * D].reshape(B, 1, D)
    gate = mod[:B, 2 * D:].reshape(B, 1, D)

    pm, ps = _in_proj(x2, norm_gain[l][None, :], shift, scale,
                      _wprep(w_in[l]), _main_cols(b_in[l])[None, :],
                      _small_cols(w_in[l]).astype(BF16), _small_cols(b_in[l])[None, :], T=T)

    def cols(name, width=NSA_KV_WIDTH):
        return pm[:, _NEW[name]:_NEW[name] + width]

    kvf = cols("ck", 2 * NSA_KV_WIDTH).reshape(B, nch, CMP_STRIDE, 2, G, DH)
    kvf = kvf.transpose(3, 0, 4, 1, 2, 5).reshape(2, B, G, nch, CMP_STRIDE * DH)
    posf = jnp.stack([cmp_pos_k[l], cmp_pos_v[l]]).reshape(2, 1, CMP_BLOCK * DH)
    posf = jnp.broadcast_to(posf, (2, 8, CMP_BLOCK * DH)).astype(BF16)
    w1 = jnp.stack([cmp_w1_k[l], cmp_w1_v[l]]).astype(BF16)
    w2 = jnp.stack([cmp_w2_k[l], cmp_w2_v[l]]).astype(BF16)
    kvc = _compress(kvf, posf, w1, w2)
    gt = ps[:, :3 * NSA_HEADS].reshape(B, T, G, 3 * HPG).transpose(0, 2, 3, 1)
    slopes2 = slopes * LOG2E
    o_nsa = _nsa(slopes2, pm, kvc[0], kvc[1].transpose(0, 1, 3, 2), gt, wimp, B=B, T=T, nc=nc, n_sel=n_sel)

    wal = jnp.pad(gla_w_alpha[l], ((GA_LANE, SMALL_WIDTH - GA_LANE - GLA_GATE_RANK), (0, 0)))
    y_gla = _gla(pm, ps, wal, gla_b_alpha[l][None, :], gla_norm_gain[l][None, :], tri, B=B, T=T)

    out = _out_proj(x2, gate, o_nsa, pm, y_gla,
                    w_br_nsa[l].astype(BF16), w_br_gla[l].astype(BF16), w_out[l].astype(BF16),
                    final_norm_gain[None, :], T=T)
    return out.reshape(B, T, D)
```

```python
import functools

import numpy as np
import jax
import jax.numpy as jnp
from jax import lax
from jax.experimental import pallas as pl
from jax.experimental.pallas import tpu as pltpu

D_MODEL = 2048
DEPTH = 1
NSA_HEADS = 16
NSA_KV_GROUPS = 4
NSA_HPG = NSA_HEADS // NSA_KV_GROUPS
NSA_HEAD_DIM = 64
CMP_BLOCK = 32
CMP_STRIDE = 16
CMP_HIDDEN = 256
SLC_BLOCK = 64
N_SELECT = 16
WINDOW = 512
NSA_WIDTH = NSA_HEADS * NSA_HEAD_DIM
NSA_KV_WIDTH = NSA_KV_GROUPS * NSA_HEAD_DIM
GLA_HEADS = 4
GLA_KEY_DIM = 128
GLA_VAL_DIM = 256
GLA_GATE_RANK = 16
GLA_TAU = 16.0
GLA_KEY_WIDTH = GLA_HEADS * GLA_KEY_DIM
GLA_WIDTH = GLA_HEADS * GLA_VAL_DIM
EPS = 1e-6
NEG = -1e30

F32 = jnp.float32
BF16 = jnp.bfloat16

_OLD = {}
_off = 0
for _name, _w in (("nsa_q", NSA_WIDTH), ("ck", NSA_KV_WIDTH), ("cv", NSA_KV_WIDTH), ("sk", NSA_KV_WIDTH),
                  ("sv", NSA_KV_WIDTH), ("wk", NSA_KV_WIDTH), ("wv", NSA_KV_WIDTH), ("nsa_g", 3 * NSA_HEADS),
                  ("nsa_z", NSA_WIDTH), ("gq", GLA_KEY_WIDTH), ("gk", GLA_KEY_WIDTH), ("gv", GLA_WIDTH),
                  ("ga", GLA_GATE_RANK), ("gla_z", GLA_WIDTH), ("mg_nsa", D_MODEL), ("mg_gla", D_MODEL)):
    _OLD[_name] = (_off, _w)
    _off += _w

def _pair_by_group(a, b):
    return [(_OLD[nm][0] + g * NSA_HEAD_DIM, NSA_HEAD_DIM) for g in range(NSA_KV_GROUPS) for nm in (a, b)]


_MAIN_PIECES = [(nm, [_OLD[nm]]) for nm in
                ("nsa_q", "nsa_z", "gla_z", "gv", "mg_nsa", "mg_gla", "gq", "gk", "ck", "cv")]
_MAIN_PIECES += [("sel_win_k", _pair_by_group("sk", "wk")), ("sel_win_v", _pair_by_group("sv", "wv"))]
_NEW = {}
_off = 0
for _name, _pieces in _MAIN_PIECES:
    _NEW[_name] = _off
    _off += sum(w for _, w in _pieces)
MAIN_WIDTH = _off
SMALL_WIDTH = 128
GA_LANE = 3 * NSA_HEADS

LANE = 128
GLA_CHUNK = 64
GLA_SUB = 16
NSA_TILE = 256
QFEAT = 2 * NSA_HEAD_DIM
KSEL_FEAT = QFEAT + LANE
N_SLOPE_TERMS = 3
SEL_MASK = 16384.0
BITS_PER_WORD = 16
LOG2E = 1.4426950408889634
Q_SCALE = LOG2E * NSA_HEAD_DIM ** -0.5


def _dot(a, b):
    return jnp.dot(a, b, preferred_element_type=F32)


def _dot_nt(a, b):
    return lax.dot_general(a, b, (((1,), (1,)), ((), ())), preferred_element_type=F32)


def _split3(x):
    hi = x.astype(BF16)
    r = x - hi.astype(F32)
    mid = r.astype(BF16)
    lo = (r - mid.astype(F32)).astype(BF16)
    return hi, mid, lo


def _sigmoid(x):
    return 1.0 / (1.0 + jnp.exp(-x))


def _ada_kernel(c_ref, w_ref, b_ref, o_ref):
    ch, cm, cl = _split3(c_ref[...])
    wh, wm, wl = _split3(w_ref[...])
    acc = _dot(ch, wh) + _dot(ch, wm) + _dot(cm, wh) + _dot(ch, wl) + _dot(cl, wh) + _dot(cm, wm)
    o_ref[...] = acc + b_ref[...]


def _ada(c8, w, b, *, tn=768):
    m, d = c8.shape
    n = w.shape[1]
    return pl.pallas_call(
        _ada_kernel,
        grid=(n // tn,),
        in_specs=[pl.BlockSpec((m, d), lambda j: (0, 0)),
                  pl.BlockSpec((d, tn), lambda j: (0, j)),
                  pl.BlockSpec((1, tn), lambda j: (0, j))],
        out_specs=pl.BlockSpec((m, tn), lambda j: (0, j)),
        out_shape=jax.ShapeDtypeStruct((m, n), F32),
        compiler_params=pltpu.CompilerParams(dimension_semantics=("arbitrary",),
                                             vmem_limit_bytes=40 * 1024 * 1024),
        name="ada",
    )(c8, w, b)


WPREP_TILE = 512
WPREP_WINDOW = 6


def _wprep_tiles():
    tiles = []
    for name, pieces in _MAIN_PIECES:
        scale = Q_SCALE if name == "nsa_q" else 1.0
        if len(pieces) == 1:
            start, width = pieces[0]
            assert width % WPREP_TILE == 0 or (name in ("ck", "cv") and width * 2 == WPREP_TILE)
            if name == "cv":
                assert start == _OLD["ck"][0] + _OLD["ck"][1]
                continue
            for off in range(0, max(width, WPREP_TILE), WPREP_TILE):
                tiles.append((start + off, (start + off) % LANE, scale))
        else:
            first, second = pieces[0][0], pieces[1][0]
            assert second == first + WPREP_TILE and first % LANE == 0 and len(pieces) * NSA_HEAD_DIM == WPREP_TILE
            tiles.append((first, -1, scale))
    return tiles


def _wprep_kernel(start_ref, shift_ref, scale_ref, b0, b1, b2, b3, b4, b5, o_ref):
    j = pl.program_id(0)
    blocks = (b0, b1, b2, b3, b4, b5)
    lane = lax.broadcasted_iota(jnp.int32, (1, LANE), 1)
    shift = shift_ref[j]
    scale = scale_ref[j]

    @pl.when(shift >= 0)
    def _():
        amt = jnp.bitwise_and(LANE - shift, LANE - 1)
        rolled = [pltpu.roll(b[...], amt, 1) for b in blocks[:WPREP_TILE // LANE + 1]]
        keep = lane < LANE - shift
        for cb in range(WPREP_TILE // LANE):
            o_ref[:, cb * LANE:(cb + 1) * LANE] = (jnp.where(keep, rolled[cb], rolled[cb + 1]) * scale).astype(o_ref.dtype)

    @pl.when(shift < 0)
    def _():
        half = NSA_HEAD_DIM
        for g in range(NSA_KV_GROUPS):
            a = blocks[g // 2][...]
            b = blocks[WPREP_TILE // LANE + g // 2][...]
            if g % 2 == 1:
                a = pltpu.roll(a, half, 1)
            else:
                b = pltpu.roll(b, half, 1)
            o_ref[:, g * LANE:(g + 1) * LANE] = (jnp.where(lane < half, a, b) * scale).astype(o_ref.dtype)


def _wprep(w):
    d, width = w.shape
    tiles = _wprep_tiles()
    assert len(tiles) * WPREP_TILE == MAIN_WIDTH
    last = (width - 1) // LANE
    start = jnp.asarray([t[0] // LANE for t in tiles], jnp.int32)
    shift = jnp.asarray([t[1] for t in tiles], jnp.int32)
    scale = jnp.asarray([t[2] for t in tiles], F32)
    src = [pl.BlockSpec((d, LANE), lambda j, st, sh, k=k: (0, jnp.minimum(st[j] + k, last)))
           for k in range(WPREP_WINDOW)]
    return pl.pallas_call(
        _wprep_kernel,
        grid_spec=pltpu.PrefetchScalarGridSpec(
            num_scalar_prefetch=2,
            grid=(len(tiles),),
            in_specs=[pl.BlockSpec(memory_space=pltpu.SMEM)] + src,
            out_specs=pl.BlockSpec((d, WPREP_TILE), lambda j, st, sh: (0, j))),
        out_shape=jax.ShapeDtypeStruct((d, MAIN_WIDTH), BF16),
        compiler_params=pltpu.CompilerParams(dimension_semantics=("arbitrary",),
                                             vmem_limit_bytes=40 * 1024 * 1024),
        name="wprep",
    )(start, shift, scale, *([w] * WPREP_WINDOW))


def _wprep_small_kernel(g_ref, a_ref, o_ref):
    lane = lax.broadcasted_iota(jnp.int32, (1, LANE), 1)
    n_gate = 3 * NSA_HEADS
    o_ref[...] = jnp.where(lane < n_gate, g_ref[...],
                           jnp.where(lane < n_gate + GLA_GATE_RANK, a_ref[...], 0.0)).astype(o_ref.dtype)


def _wprep_small(w):
    d = w.shape[0]
    g0, a0 = _OLD["nsa_g"][0], _OLD["ga"][0]
    assert g0 % LANE == 0 and a0 % LANE == GA_LANE
    return pl.pallas_call(
        _wprep_small_kernel,
        grid=(1,),
        in_specs=[pl.BlockSpec((d, LANE), lambda i: (0, g0 // LANE)),
                  pl.BlockSpec((d, LANE), lambda i: (0, a0 // LANE))],
        out_specs=pl.BlockSpec((d, LANE), lambda i: (0, 0)),
        out_shape=jax.ShapeDtypeStruct((d, SMALL_WIDTH), BF16),
        name="wprep_small",
    )(w, w)


def _in_proj_kernel(x_ref, gain_ref, shift_ref, scale_ref, w_ref, b_ref, ws_ref, bs_ref,
                    o_ref, os_ref, h_ref):
    j = pl.program_id(1)

    @pl.when(j == 0)
    def _():
        x = x_ref[...]
        ms = jnp.mean(x * x, axis=-1, keepdims=True)
        y = x * lax.rsqrt(ms + EPS) * gain_ref[...]
        y = y * (1.0 + scale_ref[0]) + shift_ref[0]
        h = y.astype(BF16)
        h_ref[...] = h
        os_ref[...] = _dot(h, ws_ref[...]) + bs_ref[...]

    o_ref[...] = (_dot(h_ref[...], w_ref[...]) + b_ref[...]).astype(o_ref.dtype)


def _in_proj(x2, gain, shift, scale, w_main, b_main, w_small, b_small, *, T, tm=1024, tn=1536):
    n, d = x2.shape
    nm = w_main.shape[1]
    nt = T // tm
    return pl.pallas_call(
        _in_proj_kernel,
        grid=(n // tm, nm // tn),
        in_specs=[pl.BlockSpec((tm, d), lambda i, j: (i, 0)),
                  pl.BlockSpec((1, d), lambda i, j: (0, 0)),
                  pl.BlockSpec((1, 1, d), lambda i, j: (i // nt, 0, 0)),
                  pl.BlockSpec((1, 1, d), lambda i, j: (i // nt, 0, 0)),
                  pl.BlockSpec((d, tn), lambda i, j: (0, j)),
                  pl.BlockSpec((1, tn), lambda i, j: (0, j)),
                  pl.BlockSpec((d, SMALL_WIDTH), lambda i, j: (0, 0)),
                  pl.BlockSpec((1, SMALL_WIDTH), lambda i, j: (0, 0))],
        out_specs=[pl.BlockSpec((tm, tn), lambda i, j: (i, j)),
                   pl.BlockSpec((tm, SMALL_WIDTH), lambda i, j: (i, 0))],
        out_shape=[jax.ShapeDtypeStruct((n, nm), BF16),
                   jax.ShapeDtypeStruct((n, SMALL_WIDTH), F32)],
        scratch_shapes=[pltpu.VMEM((tm, d), BF16)],
        compiler_params=pltpu.CompilerParams(dimension_semantics=("parallel", "arbitrary"),
                                             vmem_limit_bytes=56 * 1024 * 1024),
        name="in_proj",
    )(x2, gain, shift, scale, w_main, b_main, w_small, b_small)


def _compress_kernel(kv_ref, pos_ref, w1_ref, w2_ref, o_ref):
    y = kv_ref[0, 0, 0]
    w1 = w1_ref[0]
    half = y.shape[1]
    nch = y.shape[0]
    z1 = _dot(y, w1[:half])
    z2 = _dot(y, w1[half:])
    posb = _dot(pos_ref[0], w1)[0:1]
    pre = z1 + pltpu.roll(z2, nch - 1, 0) + posb
    hid = pre * _sigmoid(pre)
    o_ref[0, 0, 0] = _dot(hid.astype(BF16), w2_ref[0]).astype(o_ref.dtype)


def _compress(kvf, posf, w1, w2):
    two, b, g, nch, half = kvf.shape
    hidden = w1.shape[2]
    dh = w2.shape[2]
    return pl.pallas_call(
        _compress_kernel,
        grid=(two, b, g),
        in_specs=[pl.BlockSpec((1, 1, 1, nch, half), lambda s, i, j: (s, i, j, 0, 0)),
                  pl.BlockSpec((1, 8, 2 * half), lambda s, i, j: (s, 0, 0)),
                  pl.BlockSpec((1, 2 * half, hidden), lambda s, i, j: (s, 0, 0)),
                  pl.BlockSpec((1, hidden, dh), lambda s, i, j: (s, 0, 0))],
        out_specs=pl.BlockSpec((1, 1, 1, nch, dh), lambda s, i, j: (s, i, j, 0, 0)),
        out_shape=jax.ShapeDtypeStruct((two, b, g, nch, dh), BF16),
        compiler_params=pltpu.CompilerParams(dimension_semantics=("parallel", "parallel", "parallel")),
        name="compress",
    )(kvf, posf, w1, w2)


def _select_blocks(imp, t_row, n_sel):
    nbp = imp.shape[0]
    blk = lax.broadcasted_iota(jnp.int32, (nbp, 1), 0)
    cur = lax.shift_right_logical(t_row, 6)
    forced = (blk == 0) | (blk == cur) | (blk == cur - 1)
    bvalid = blk * SLC_BLOCK <= t_row
    score = jnp.where(forced, 1e30, jnp.where(bvalid, imp, -1.0))
    blk_f = blk.astype(F32)
    sel = jnp.zeros(imp.shape, F32)
    for _ in range(n_sel):
        m = jnp.max(score, axis=0, keepdims=True)
        first = jnp.min(jnp.where(score == m, blk_f, float(nbp)), axis=0, keepdims=True)
        hit = blk_f == first
        sel = jnp.where(hit, 1.0, sel)
        score = jnp.where(hit, -1.0, score)
    return sel


def _softmax_step(s, vt, d, m_sc, l_sc, acc_sc, s_max=None):
    if s_max is None:
        s_max = jnp.max(s, axis=0, keepdims=True)
    m_prev = m_sc[...]
    m_new = jnp.maximum(m_prev, s_max - d)
    alpha = jnp.exp2(m_prev - m_new)
    p = jnp.exp2(s - (m_new + d))
    l_sc[...] = alpha * l_sc[...] + jnp.sum(p, axis=0, keepdims=True)
    acc_sc[...] = alpha * acc_sc[...] + _dot(vt, p.astype(BF16))
    m_sc[...] = m_new


def _stage_group(qslab_ref, kslab_ref, vslab_ref, qt_sc, ks_sc, kw_sc, vt_sc, nt, tk):
    dh = NSA_HEAD_DIM
    lane = lax.broadcasted_iota(jnp.int32, (tk, LANE), 1)
    row = lax.broadcasted_iota(jnp.int32, (tk, LANE), 0)
    pos = row.astype(F32)
    feat_sel = jnp.where((lane >= dh) & (lane < dh + N_SLOPE_TERMS), pos, 0.0)
    feat_win = jnp.where(lane < N_SLOPE_TERMS, pos, 0.0)
    blk_in_tile = lax.shift_right_logical(row, 6)

    def body(kt, carry):
        start = pl.multiple_of(kt * tk, tk)
        kk = kslab_ref[pl.ds(start, tk), :].astype(F32)
        onehot = jnp.where(kt * (tk // SLC_BLOCK) + blk_in_tile == lane, 1.0, 0.0)
        ks_sc[kt] = jnp.concatenate([jnp.where(lane < dh, kk, feat_sel), onehot], axis=1).astype(BF16)
        kw_sc[kt] = jnp.where(lane >= dh, kk, feat_win).astype(BF16)
        vt_sc[kt] = vslab_ref[pl.ds(start, tk), :].astype(F32).T.astype(BF16)
        x_t = qslab_ref[pl.ds(start, tk), :].astype(F32).T
        qt_sc[kt] = jnp.concatenate([x_t[hh * dh:(hh + 1) * dh] for hh in range(NSA_HPG)], axis=1).astype(BF16)
        return carry

    lax.fori_loop(0, nt, body, 0)


def _nsa_kernel(slopes_ref, q_ref, kslab_ref, vslab_ref, kc_ref, vct_ref, gt_ref, wimp_ref,
                o_ref, q2_sc, qw_sc, ks_sc, kw_sc, qt_sc, vt_sc, sa_sc, sb_sc, m_sc, l_sc, acc_sc, br_sc, list_sc,
                words_sc, *, tq, nt, nc, n_sel):
    g = pl.program_id(1)
    qi = pl.program_id(2)
    tk = tq
    dh = NSA_HEAD_DIM
    hq = NSA_HPG * tq

    @pl.when(qi == 0)
    def _():
        _stage_group(q_ref, kslab_ref, vslab_ref, qt_sc, ks_sc, kw_sc, vt_sc, nt, tk)

    lane = lax.broadcasted_iota(jnp.int32, (1, hq), 1)
    slope_row = jnp.zeros((1, hq), F32)
    for hh in range(NSA_HPG):
        slope_row = jnp.where(lane >= hh * tq, slopes_ref[g * NSA_HPG + hh], slope_row)
    t_one = qi * tq + lax.broadcasted_iota(jnp.int32, (1, tq), 1)
    t_row = jnp.concatenate([t_one] * NSA_HPG, axis=1)
    t_f = t_row.astype(F32)
    j_loc = lax.broadcasted_iota(jnp.int32, (tk, 1), 0)

    q_t = qt_sc[qi]
    feat_row = lax.broadcasted_iota(jnp.int32, (dh, hq), 0)
    terms = [t.astype(F32) for t in _split3(slope_row)]
    feat = jnp.zeros((dh, hq), F32)
    for i, term in enumerate(terms):
        feat = jnp.where(feat_row == i, term, feat)
    feat = feat.astype(BF16)
    q2_sc[0:dh, :] = q_t
    q2_sc[dh:QFEAT, :] = feat
    qw_sc[0:dh, :] = feat
    qw_sc[dh:QFEAT, :] = q_t

    def reset_stats():
        m_sc[...] = jnp.full(m_sc.shape, NEG, F32)
        l_sc[...] = jnp.zeros(l_sc.shape, F32)
        acc_sc[...] = jnp.zeros(acc_sc.shape, F32)

    def per_head(a):
        return [a[:, hh * tq:(hh + 1) * tq] for hh in range(NSA_HPG)]

    def shift_of(kv, off=None):
        d = slope_row * (t_f - (kv * tk).astype(F32))
        return d if off is None else d + jnp.where(off, -NEG, 0.0)

    per_tile = tk // SLC_BLOCK
    nbp = KSEL_FEAT - QFEAT
    n_words = -(-nt * per_tile // BITS_PER_WORD)

    def front(nk, nblk):
        kc = kc_ref[0, 0, 0:nk, :]
        n_col = lax.broadcasted_iota(jnp.int32, (nk, 1), 0)
        ce = jnp.where(n_col < nc, n_col * CMP_STRIDE + (CMP_BLOCK - 1), 2 ** 30)
        ce_rel = (ce - qi * tq).astype(F32)
        s = jnp.where(ce <= t_row, _dot(kc, q_t) + slope_row * ce_rel, NEG)
        m = jnp.max(s, axis=0, keepdims=True)
        p = jnp.exp2(s - m)
        l = jnp.sum(p, axis=0, keepdims=True)
        p = p * jnp.where(t_row >= CMP_BLOCK - 1, 1.0 / l, 0.0)
        br_sc[0] = _dot(vct_ref[0, 0, :, 0:nk], p.astype(BF16))
        ps4 = per_head(p)
        psum = (ps4[0] + ps4[1]) + (ps4[2] + ps4[3])
        wimp = wimp_ref[0:nblk, 0:nk]
        ph, pm, plo = _split3(psum)
        imp = _dot(wimp, ph) + _dot(wimp, pm) + _dot(wimp, plo)

        reset_stats()
        n_back = WINDOW // tk
        qw = qw_sc[...]
        tiles = [jnp.maximum(qi - w, 0) for w in range(n_back + 1)]
        logits = [_dot(kw_sc[kv], qw) for kv in tiles]
        for w, (kv, s) in enumerate(zip(tiles, logits)):
            if w == 0:
                s = jnp.where((kv * tk + j_loc) <= t_row, s, NEG)
            elif w == n_back:
                s = jnp.where((t_row - (kv * tk + j_loc)) < WINDOW, s, NEG)
            _softmax_step(s, vt_sc[kv, dh:2 * dh, :], shift_of(kv, qi < w), m_sc, l_sc, acc_sc)
        br_sc[1] = acc_sc[...] * (1.0 / l_sc[...])

        s0 = _dot(ks_sc[0, :, 0:QFEAT], q2_sc[0:QFEAT, :])

        sel = _select_blocks(imp, t_one, n_sel)
        penalty_f = (sel - 1.0) * SEL_MASK
        q2_sc[QFEAT:QFEAT + nblk, :] = jnp.concatenate([penalty_f.astype(BF16)] * NSA_HPG, axis=1)
        if nblk < nbp:
            q2_sc[QFEAT + nblk:KSEL_FEAT, :] = jnp.full((nbp - nblk, hq), -SEL_MASK, BF16)
        pen0 = jnp.concatenate([penalty_f[0:per_tile]] * NSA_HPG, axis=1)
        pen0 = jnp.concatenate([jnp.broadcast_to(pen0[b:b + 1], (SLC_BLOCK, hq)) for b in range(per_tile)], axis=0)
        s0 = s0 + pen0
        sa_sc[0:tk, :] = s0
        sa_sc[tk:tk + 1, :] = jnp.max(s0, axis=0, keepdims=True)

        any_tok = jnp.max(sel, axis=1, keepdims=True)
        blk_col = lax.broadcasted_iota(jnp.int32, any_tok.shape, 0)
        bit = jnp.left_shift(1, jnp.bitwise_and(blk_col, BITS_PER_WORD - 1)).astype(F32)
        bits = jnp.where(any_tok > 0.5, bit, 0.0)
        for w in range(n_words):
            lo, hi = w * BITS_PER_WORD, min((w + 1) * BITS_PER_WORD, nblk)
            words_sc[w] = jnp.sum(bits[lo:hi]).astype(jnp.int32) if lo < nblk else jnp.int32(0)

    n_var = 4
    span = nt // n_var
    ncp = kc_ref.shape[2]
    for v in range(1, n_var + 1):
        @pl.when((qi >= (v - 1) * span) & (qi < v * span))
        def _(v=v):
            front(ncp * v // n_var, (nt * per_tile) * v // n_var)

    words = [words_sc[w] for w in range(n_words)]
    for kt in range(nt + 2):
        list_sc[kt] = qi
    n_act = jnp.int32(0)
    tile_mask = (1 << per_tile) - 1
    for kt in range(nt):
        list_sc[n_act] = kt
        w, sh = divmod(kt * per_tile, BITS_PER_WORD)
        hit = jnp.bitwise_and(words[w] >> sh, tile_mask) != 0
        n_act = n_act + hit.astype(jnp.int32)
    list_sc[n_act] = qi
    n_plain = n_act - 1

    reset_stats()

    def sel_logits(idx, s_ref):
        s = _dot(ks_sc[list_sc[idx]], q2_sc[...])
        s_ref[0:tk, :] = s
        s_ref[tk:tk + 1, :] = jnp.max(s, axis=0, keepdims=True)

    def sel_update(idx, s_ref):
        kv = list_sc[idx]
        _softmax_step(s_ref[0:tk, :], vt_sc[kv, 0:dh, :], shift_of(kv, idx >= n_plain), m_sc, l_sc, acc_sc,
                      s_max=s_ref[tk:tk + 1, :])

    def pair_body(i, carry):
        sel_logits(2 * i + 1, sb_sc)
        sel_update(2 * i, sa_sc)
        sel_logits(2 * i + 2, sa_sc)
        sel_update(2 * i + 1, sb_sc)
        return carry

    lax.fori_loop(0, lax.shift_right_logical(n_plain + 1, 1), pair_body, 0)
    causal = (qi * tk + j_loc) <= t_row
    _softmax_step(jnp.where(causal, sa_sc[0:tk, :], NEG), vt_sc[qi, 0:dh, :], shift_of(qi), m_sc, l_sc, acc_sc)
    o_sel = acc_sc[...] * (1.0 / l_sc[...])

    sg = _sigmoid(gt_ref[0, 0])
    gates = [jnp.concatenate([sg[3 * hh + br:3 * hh + br + 1] for hh in range(NSA_HPG)], axis=1)
             for br in range(3)]
    o_all = gates[0] * br_sc[0] + gates[1] * o_sel + gates[2] * br_sc[1]
    o_ref[...] = jnp.concatenate(per_head(o_all), axis=0).T.astype(o_ref.dtype)


def _nsa(slopes, pm, kc, vct, gt, wimp, *, B, T, nc, n_sel):
    b, g, hpg, dh = B, NSA_KV_GROUPS, NSA_HPG, NSA_HEAD_DIM
    tq = NSA_TILE
    nt = T // tq
    hq = hpg * tq
    ncp = kc.shape[2]
    nbp = wimp.shape[0]
    qb = _NEW["nsa_q"] // (hpg * dh)
    kb, vb = _NEW["sel_win_k"] // LANE, _NEW["sel_win_v"] // LANE
    kern = functools.partial(_nsa_kernel, tq=tq, nt=nt, nc=nc, n_sel=n_sel)
    return pl.pallas_call(
        kern,
        grid=(b, g, nt),
        in_specs=[pl.BlockSpec(memory_space=pltpu.SMEM),
                  pl.BlockSpec((T, hpg * dh), lambda i, j, k: (i, qb + j)),
                  pl.BlockSpec((T, LANE), lambda i, j, k: (i, kb + j)),
                  pl.BlockSpec((T, LANE), lambda i, j, k: (i, vb + j)),
                  pl.BlockSpec((1, 1, ncp, dh), lambda i, j, k: (i, j, 0, 0)),
                  pl.BlockSpec((1, 1, dh, ncp), lambda i, j, k: (i, j, 0, 0)),
                  pl.BlockSpec((1, 1, 3 * hpg, tq), lambda i, j, k: (i, j, 0, k)),
                  pl.BlockSpec((nbp, ncp), lambda i, j, k: (0, 0))],
        out_specs=pl.BlockSpec((tq, hpg * dh), lambda i, j, k: (i * nt + k, j)),
        out_shape=jax.ShapeDtypeStruct((b * T, g * hpg * dh), BF16),
        scratch_shapes=[pltpu.VMEM((KSEL_FEAT, hq), BF16), pltpu.VMEM((QFEAT, hq), BF16),
                        pltpu.VMEM((nt, tq, KSEL_FEAT), BF16), pltpu.VMEM((nt, tq, QFEAT), BF16),
                        pltpu.VMEM((nt, dh, hq), BF16), pltpu.VMEM((nt, 2 * dh, tq), BF16),
                        pltpu.VMEM((tq + 8, hq), F32), pltpu.VMEM((tq + 8, hq), F32),
                        pltpu.VMEM((1, hq), F32), pltpu.VMEM((1, hq), F32),
                        pltpu.VMEM((dh, hq), F32),
                        pltpu.VMEM((2, dh, hq), F32),
                        pltpu.SMEM((nt + 2,), jnp.int32),
                        pltpu.SMEM((-(-nt * (tq // SLC_BLOCK) // BITS_PER_WORD),), jnp.int32)],
        compiler_params=pltpu.CompilerParams(dimension_semantics=("parallel", "parallel", "arbitrary"),
                                             vmem_limit_bytes=48 * 1024 * 1024),
        name="nsa",
    )(slopes, pm, pm, pm, kc, vct, gt, wimp)


def _gla_intra_scores(q, k, bcum):
    c, dk = q.shape
    sb = GLA_SUB
    row_c = lax.broadcasted_iota(jnp.int32, (c, dk), 0)
    row_s = lax.broadcasted_iota(jnp.int32, (sb, 1), 0)
    lane_s = lax.broadcasted_iota(jnp.int32, (sb, c), 1)
    blocks = []
    for i0 in range(0, c, sb):
        bi = bcum[i0:i0 + sb]
        qi = q[i0:i0 + sb]
        a_blk = jnp.zeros((sb, c), F32)
        for jl in range(sb):
            j = i0 + jl
            w = jnp.exp2(bi - bcum[j:j + 1])
            a = jnp.sum(qi * k[j:j + 1] * w, axis=-1, keepdims=True)
            a_blk = jnp.where(lane_s == j, jnp.where(row_s >= jl, a, 0.0), a_blk)
        if i0 > 0:
            r = bcum[i0 - 1:i0]
            qh = qi * jnp.exp2(bi - r)
            kh = k * jnp.exp2(jnp.where(row_c < i0, r - bcum, NEG))
            a_blk = a_blk + _dot_nt(qh.astype(BF16), kh.astype(BF16))
        blocks.append(a_blk)
    return jnp.concatenate(blocks, axis=0)


def _gla_kernel(q_ref, k_ref, v_ref, z_ref, ps_ref, wal_ref, bal_ref, gain_ref, tri_ref, o_ref,
                st_sc, *, ts):
    ti = pl.program_id(2)
    c = GLA_CHUNK

    @pl.when(ti == 0)
    def _():
        st_sc[...] = jnp.zeros(st_sc.shape, F32)

    ps = ps_ref[...]
    ph, pm, _ = _split3(ps)
    wh, wm, _ = _split3(wal_ref[...])
    zz = _dot(ph, wh) + _dot(ph, wm) + _dot(pm, wh) + bal_ref[...]
    log_a = (jnp.minimum(zz, 0.0) - jnp.log(1.0 + jnp.exp(-jnp.abs(zz)))) * (LOG2E / GLA_TAU)
    tri = tri_ref[...]
    qscale = GLA_KEY_DIM ** -0.5
    for ci in range(ts // c):
        sl = slice(ci * c, (ci + 1) * c)
        gh, gm, gl = _split3(log_a[sl])
        bcum = _dot(tri, gh) + _dot(tri, gm) + _dot(tri, gl)
        q = q_ref[sl, :].astype(F32) * qscale
        k = k_ref[sl, :].astype(F32)
        v = v_ref[sl, :]
        a_mat = _gla_intra_scores(q, k, bcum)
        st = st_sc[...]
        o = _dot(a_mat.astype(BF16), v) + _dot_nt((q * jnp.exp2(bcum)).astype(BF16), st.astype(BF16))
        b_last = bcum[c - 1:c, :]
        kd = k * jnp.exp2(b_last - bcum)
        st_sc[...] = st * jnp.exp2(b_last) + _dot(v.astype(F32).T.astype(BF16), kd.astype(BF16))
        ms = jnp.mean(o * o, axis=-1, keepdims=True)
        y = o * lax.rsqrt(ms + EPS) * gain_ref[...]
        z = z_ref[sl, :].astype(F32)
        o_ref[sl, :] = (y * (z * _sigmoid(z))).astype(o_ref.dtype)


def _gla(pm, ps, wal, bal, gain, tri, *, B, T, ts=1024):
    n = pm.shape[0]
    nts = T // ts
    dk, dv = GLA_KEY_DIM, GLA_VAL_DIM
    qb, kb = _NEW["gq"] // dk, _NEW["gk"] // dk
    vb, zb = _NEW["gv"] // dv, _NEW["gla_z"] // dv
    kern = functools.partial(_gla_kernel, ts=ts)
    return pl.pallas_call(
        kern,
        grid=(B, GLA_HEADS, nts),
        in_specs=[pl.BlockSpec((ts, dk), lambda b, h, i: (b * nts + i, qb + h)),
                  pl.BlockSpec((ts, dk), lambda b, h, i: (b * nts + i, kb + h)),
                  pl.BlockSpec((ts, dv), lambda b, h, i: (b * nts + i, vb + h)),
                  pl.BlockSpec((ts, dv), lambda b, h, i: (b * nts + i, zb + h)),
                  pl.BlockSpec((ts, SMALL_WIDTH), lambda b, h, i: (b * nts + i, 0)),
                  pl.BlockSpec((SMALL_WIDTH, dk), lambda b, h, i: (0, h)),
                  pl.BlockSpec((1, dk), lambda b, h, i: (0, h)),
                  pl.BlockSpec((1, dv), lambda b, h, i: (0, 0)),
                  pl.BlockSpec((GLA_CHUNK, GLA_CHUNK), lambda b, h, i: (0, 0))],
        out_specs=pl.BlockSpec((ts, dv), lambda b, h, i: (b * nts + i, h)),
        out_shape=jax.ShapeDtypeStruct((n, GLA_WIDTH), BF16),
        scratch_shapes=[pltpu.VMEM((dv, dk), F32)],
        compiler_params=pltpu.CompilerParams(dimension_semantics=("parallel", "parallel", "arbitrary")),
        name="gla",
    )(pm, pm, pm, pm, ps, wal, bal, gain, tri)


def _out_proj_kernel(x_ref, gate_ref, on_ref, nz_ref, mgn_ref, mgg_ref, yg_ref,
                     wn_ref, wg_ref, wo_ref, fg_ref, o_ref):
    nz = nz_ref[...].astype(F32)
    y_nsa = on_ref[...].astype(F32) * (nz * _sigmoid(nz))
    a = _dot(y_nsa.astype(BF16), wn_ref[...])
    bm = _dot(yg_ref[...], wg_ref[...])
    merged = _sigmoid(mgn_ref[...].astype(F32)) * a + _sigmoid(mgg_ref[...].astype(F32)) * bm
    xn = x_ref[...] + gate_ref[0] * _dot(merged.astype(BF16), wo_ref[...])
    ms = jnp.mean(xn * xn, axis=-1, keepdims=True)
    o_ref[...] = xn * lax.rsqrt(ms + EPS) * fg_ref[...]


def _out_proj(x2, gate, o_nsa, pm, yg, wn, wg, wo, fg, *, T, tm=256):
    n, d = x2.shape
    nt = T // tm
    nzb = _NEW["nsa_z"] // NSA_WIDTH
    mnb = _NEW["mg_nsa"] // D_MODEL
    mgb = _NEW["mg_gla"] // D_MODEL
    row = lambda i: (i, 0)
    const2 = lambda i: (0, 0)
    return pl.pallas_call(
        _out_proj_kernel,
        grid=(n // tm,),
        in_specs=[pl.BlockSpec((tm, d), row),
                  pl.BlockSpec((1, 1, d), lambda i: (i // nt, 0, 0)),
                  pl.BlockSpec((tm, NSA_WIDTH), row),
                  pl.BlockSpec((tm, NSA_WIDTH), lambda i: (i, nzb)),
                  pl.BlockSpec((tm, D_MODEL), lambda i: (i, mnb)),
                  pl.BlockSpec((tm, D_MODEL), lambda i: (i, mgb)),
                  pl.BlockSpec((tm, GLA_WIDTH), row),
                  pl.BlockSpec((NSA_WIDTH, d), const2),
                  pl.BlockSpec((GLA_WIDTH, d), const2),
                  pl.BlockSpec((d, d), const2),
                  pl.BlockSpec((1, d), const2)],
        out_specs=pl.BlockSpec((tm, d), row),
        out_shape=jax.ShapeDtypeStruct((n, d), F32),
        compiler_params=pltpu.CompilerParams(dimension_semantics=("parallel",),
                                             vmem_limit_bytes=60 * 1024 * 1024),
        name="out_proj",
    )(x2, gate, o_nsa, pm, pm, pm, yg, wn, wg, wo, fg)


def _importance_matrix(nbp, ncp):
    ratio = SLC_BLOCK // CMP_STRIDE
    n_sub = CMP_BLOCK // CMP_STRIDE
    w = np.zeros((nbp, ncp), np.float32)
    for blk in range(nbp):
        for m in range(ratio):
            for s in range(n_sub):
                n = ratio * blk + m - s
                if 0 <= n < ncp:
                    w[blk, n] += 1.0
    return w


def _main_cols(a):
    parts = [a[..., start:start + width] * (Q_SCALE if nm == "nsa_q" else 1.0)
             for nm, pieces in _MAIN_PIECES for start, width in pieces]
    return jnp.concatenate(parts, axis=-1)


def _small_cols(a):
    pad = SMALL_WIDTH - 3 * NSA_HEADS - GLA_GATE_RANK
    parts = [a[..., _OLD["nsa_g"][0]:_OLD["nsa_g"][0] + 3 * NSA_HEADS],
             a[..., _OLD["ga"][0]:_OLD["ga"][0] + GLA_GATE_RANK],
             jnp.zeros(a.shape[:-1] + (pad,), a.dtype)]
    return jnp.concatenate(parts, axis=-1)


def kernel(x, c, w_ada, b_ada, norm_gain, w_in, b_in, cmp_pos_k, cmp_pos_v, cmp_w1_k, cmp_w2_k, cmp_w1_v, cmp_w2_v,
           gla_w_alpha, gla_b_alpha, gla_norm_gain, w_br_nsa, w_br_gla, w_out, final_norm_gain):
    assert DEPTH == 1, "the final rmsnorm is fused into the single layer's output kernel"
    B, T, D = x.shape
    G, HPG, DH = NSA_KV_GROUPS, NSA_HPG, NSA_HEAD_DIM
    N = B * T
    nch = T // CMP_STRIDE
    nc = nch - CMP_BLOCK // CMP_STRIDE + 1
    nb = T // SLC_BLOCK
    nbp = KSEL_FEAT - QFEAT
    assert nb <= nbp and T % NSA_TILE == 0
    n_sel = min(N_SELECT, nb)
    tk = NSA_TILE
    l = 0

    slopes = 2.0 ** (-8.0 * jnp.arange(1, NSA_HEADS + 1, dtype=F32) / NSA_HEADS)
    wimp = jnp.asarray(_importance_matrix(nbp, nch), BF16)
    tri = jnp.asarray(np.tril(np.ones((GLA_CHUNK, GLA_CHUNK), np.float32)), BF16)
    bp = ((B + 7) // 8) * 8
    c8 = jnp.pad(c, ((0, bp - B), (0, 0)))
    x2 = x.reshape(N, D)

    mod = _ada(c8, w_ada[l], b_ada[l][None, :])
    shift = mod[:B, :D].reshape(B, 1, D)
    scale = mod[:B, D:2 * D].reshape(B, 1, D)
    gate = mod[:B, 2 * D:].reshape(B, 1, D)

    pm, ps = _in_proj(x2, norm_gain[l][None, :], shift, scale,
                      _wprep(w_in[l]), _main_cols(b_in[l])[None, :],
                      _wprep_small(w_in[l]), _small_cols(b_in[l])[None, :], T=T)

    def cols(name, width=NSA_KV_WIDTH):
        return pm[:, _NEW[name]:_NEW[name] + width]

    kvf = cols("ck", 2 * NSA_KV_WIDTH).reshape(B, nch, CMP_STRIDE, 2, G, DH)
    kvf = kvf.transpose(3, 0, 4, 1, 2, 5).reshape(2, B, G, nch, CMP_STRIDE * DH)
    posf = jnp.stack([cmp_pos_k[l], cmp_pos_v[l]]).reshape(2, 1, CMP_BLOCK * DH)
    posf = jnp.broadcast_to(posf, (2, 8, CMP_BLOCK * DH)).astype(BF16)
    w1 = jnp.stack([cmp_w1_k[l], cmp_w1_v[l]]).astype(BF16)
    w2 = jnp.stack([cmp_w2_k[l], cmp_w2_v[l]]).astype(BF16)
    kvc = _compress(kvf, posf, w1, w2)
    gt = ps[:, :3 * NSA_HEADS].reshape(B, T, G, 3 * HPG).transpose(0, 2, 3, 1)
    slopes2 = slopes * LOG2E
    o_nsa = _nsa(slopes2, pm, kvc[0], kvc[1].transpose(0, 1, 3, 2), gt, wimp, B=B, T=T, nc=nc, n_sel=n_sel)

    wal = jnp.pad(gla_w_alpha[l], ((GA_LANE, SMALL_WIDTH - GA_LANE - GLA_GATE_RANK), (0, 0)))
    y_gla = _gla(pm, ps, wal, gla_b_alpha[l][None, :], gla_norm_gain[l][None, :], tri, B=B, T=T)

    out = _out_proj(x2, gate, o_nsa, pm, y_gla,
                    w_br_nsa[l].astype(BF16), w_br_gla[l].astype(BF16), w_out[l].astype(BF16),
                    final_norm_gain[None, :], T=T)
    return out.reshape(B, T, D)
```

```python
import functools

import numpy as np
import jax
import jax.numpy as jnp
from jax import lax
from jax.experimental import pallas as pl
from jax.experimental.pallas import tpu as pltpu

D_MODEL = 2048
DEPTH = 1
NSA_HEADS = 16
NSA_KV_GROUPS = 4
NSA_HPG = NSA_HEADS // NSA_KV_GROUPS
NSA_HEAD_DIM = 64
CMP_BLOCK = 32
CMP_STRIDE = 16
CMP_HIDDEN = 256
SLC_BLOCK = 64
N_SELECT = 16
WINDOW = 512
NSA_WIDTH = NSA_HEADS * NSA_HEAD_DIM
NSA_KV_WIDTH = NSA_KV_GROUPS * NSA_HEAD_DIM
GLA_HEADS = 4
GLA_KEY_DIM = 128
GLA_VAL_DIM = 256
GLA_GATE_RANK = 16
GLA_TAU = 16.0
GLA_KEY_WIDTH = GLA_HEADS * GLA_KEY_DIM
GLA_WIDTH = GLA_HEADS * GLA_VAL_DIM
EPS = 1e-6
NEG = -1e30

F32 = jnp.float32
BF16 = jnp.bfloat16

_OLD = {}
_off = 0
for _name, _w in (("nsa_q", NSA_WIDTH), ("ck", NSA_KV_WIDTH), ("cv", NSA_KV_WIDTH), ("sk", NSA_KV_WIDTH),
                  ("sv", NSA_KV_WIDTH), ("wk", NSA_KV_WIDTH), ("wv", NSA_KV_WIDTH), ("nsa_g", 3 * NSA_HEADS),
                  ("nsa_z", NSA_WIDTH), ("gq", GLA_KEY_WIDTH), ("gk", GLA_KEY_WIDTH), ("gv", GLA_WIDTH),
                  ("ga", GLA_GATE_RANK), ("gla_z", GLA_WIDTH), ("mg_nsa", D_MODEL), ("mg_gla", D_MODEL)):
    _OLD[_name] = (_off, _w)
    _off += _w

def _pair_by_group(a, b):
    return [(_OLD[nm][0] + g * NSA_HEAD_DIM, NSA_HEAD_DIM) for g in range(NSA_KV_GROUPS) for nm in (a, b)]


_MAIN_PIECES = [(nm, [_OLD[nm]]) for nm in
                ("nsa_q", "nsa_z", "gla_z", "gv", "mg_nsa", "mg_gla", "gq", "gk", "ck", "cv")]
_MAIN_PIECES += [("sel_win_k", _pair_by_group("sk", "wk")), ("sel_win_v", _pair_by_group("sv", "wv"))]
_NEW = {}
_off = 0
for _name, _pieces in _MAIN_PIECES:
    _NEW[_name] = _off
    _off += sum(w for _, w in _pieces)
MAIN_WIDTH = _off
SMALL_WIDTH = 128
GA_LANE = 3 * NSA_HEADS

LANE = 128
GLA_CHUNK = 64
GLA_SUB = 16
NSA_TILE = 256
QFEAT = 2 * NSA_HEAD_DIM
KSEL_FEAT = QFEAT + LANE
N_SLOPE_TERMS = 3
SEL_MASK = 16384.0
BITS_PER_WORD = 16
LOG2E = 1.4426950408889634
Q_SCALE = LOG2E * NSA_HEAD_DIM ** -0.5


def _dot(a, b):
    return jnp.dot(a, b, preferred_element_type=F32)


def _dot_nt(a, b):
    return lax.dot_general(a, b, (((1,), (1,)), ((), ())), preferred_element_type=F32)


def _split3(x):
    hi = x.astype(BF16)
    r = x - hi.astype(F32)
    mid = r.astype(BF16)
    lo = (r - mid.astype(F32)).astype(BF16)
    return hi, mid, lo


def _sigmoid(x):
    return 1.0 / (1.0 + jnp.exp(-x))


def _ada_kernel(c_ref, w_ref, b_ref, o_ref):
    ch, cm, cl = _split3(c_ref[...])
    wh, wm, wl = _split3(w_ref[...])
    acc = _dot(ch, wh) + _dot(ch, wm) + _dot(cm, wh) + _dot(ch, wl) + _dot(cl, wh) + _dot(cm, wm)
    o_ref[...] = acc + b_ref[...]


def _ada(c8, w, b, *, tn=768):
    m, d = c8.shape
    n = w.shape[1]
    return pl.pallas_call(
        _ada_kernel,
        grid=(n // tn,),
        in_specs=[pl.BlockSpec((m, d), lambda j: (0, 0)),
                  pl.BlockSpec((d, tn), lambda j: (0, j)),
                  pl.BlockSpec((1, tn), lambda j: (0, j))],
        out_specs=pl.BlockSpec((m, tn), lambda j: (0, j)),
        out_shape=jax.ShapeDtypeStruct((m, n), F32),
        compiler_params=pltpu.CompilerParams(dimension_semantics=("arbitrary",),
                                             vmem_limit_bytes=40 * 1024 * 1024),
        name="ada",
    )(c8, w, b)


WPREP_TILE = 512


def _wprep_tiles():
    tiles = []
    for name, pieces in _MAIN_PIECES:
        scale = Q_SCALE if name == "nsa_q" else 1.0
        if len(pieces) == 1:
            start, width = pieces[0]
            assert width % WPREP_TILE == 0 or (name in ("ck", "cv") and width * 2 == WPREP_TILE)
            if name == "cv":
                assert start == _OLD["ck"][0] + _OLD["ck"][1]
                continue
            for off in range(0, max(width, WPREP_TILE), WPREP_TILE):
                tiles.append((start + off, (start + off) % LANE, scale))
        else:
            first, second = pieces[0][0], pieces[1][0]
            assert second == first + WPREP_TILE and first % LANE == 0 and len(pieces) * NSA_HEAD_DIM == WPREP_TILE
            tiles.append((first, -1, scale))
    return tiles


def _wprep_kernel(start_ref, kind_ref, scale_ref, a_ref, b_ref, o_ref):
    j = pl.program_id(0)
    scale = scale_ref[j]

    @pl.when(kind_ref[j] >= 0)
    def _():
        o_ref[...] = (a_ref[...] * scale).astype(o_ref.dtype)

    @pl.when(kind_ref[j] < 0)
    def _():
        half = NSA_HEAD_DIM
        for g in range(NSA_KV_GROUPS):
            o_ref[2 * g * half:(2 * g + 1) * half, :] = (a_ref[g * half:(g + 1) * half, :] * scale).astype(o_ref.dtype)
            o_ref[(2 * g + 1) * half:(2 * g + 2) * half, :] = (b_ref[g * half:(g + 1) * half, :] * scale).astype(o_ref.dtype)


def _wprep(wt):
    width, d = wt.shape
    tiles = _wprep_tiles()
    half_tile = WPREP_TILE // 2
    assert len(tiles) * WPREP_TILE == MAIN_WIDTH and all(t[0] % 8 == 0 for t in tiles)
    start = jnp.asarray([t[0] for t in tiles], jnp.int32)
    kind = jnp.asarray([t[1] for t in tiles], jnp.int32)
    scale = jnp.asarray([t[2] for t in tiles], F32)
    return pl.pallas_call(
        _wprep_kernel,
        grid_spec=pltpu.PrefetchScalarGridSpec(
            num_scalar_prefetch=2,
            grid=(len(tiles),),
            in_specs=[pl.BlockSpec(memory_space=pltpu.SMEM),
                      pl.BlockSpec((pl.Element(WPREP_TILE), pl.Element(d)),
                                   lambda j, st, kd: (pl.multiple_of(st[j], 8), 0)),
                      pl.BlockSpec((pl.Element(half_tile), pl.Element(d)),
                                   lambda j, st, kd: (pl.multiple_of(
                                       jnp.minimum(st[j] + WPREP_TILE, width - half_tile), 8), 0))],
            out_specs=pl.BlockSpec((WPREP_TILE, d), lambda j, st, kd: (j, 0))),
        out_shape=jax.ShapeDtypeStruct((MAIN_WIDTH, d), BF16),
        compiler_params=pltpu.CompilerParams(dimension_semantics=("arbitrary",),
                                             vmem_limit_bytes=40 * 1024 * 1024),
        name="wprep",
    )(start, kind, scale, wt, wt)


def _wprep_small_kernel(g_ref, a_ref, o_ref):
    n_gate = 3 * NSA_HEADS
    rows = jnp.concatenate([g_ref[0:n_gate, :], a_ref[...],
                            jnp.zeros((SMALL_WIDTH - n_gate - GLA_GATE_RANK, g_ref.shape[1]), F32)], axis=0)
    o_ref[...] = rows.astype(o_ref.dtype)


def _wprep_small(wt):
    d = wt.shape[1]
    g0, a0 = _OLD["nsa_g"][0], _OLD["ga"][0]
    assert g0 % 8 == 0 and a0 % 8 == 0
    return pl.pallas_call(
        _wprep_small_kernel,
        grid=(1,),
        in_specs=[pl.BlockSpec((pl.Element(NSA_HEAD_DIM), pl.Element(d)), lambda i: (g0, 0)),
                  pl.BlockSpec((pl.Element(GLA_GATE_RANK), pl.Element(d)), lambda i: (a0, 0))],
        out_specs=pl.BlockSpec((SMALL_WIDTH, d), lambda i: (0, 0)),
        out_shape=jax.ShapeDtypeStruct((SMALL_WIDTH, d), BF16),
        name="wprep_small",
    )(wt, wt)


def _in_proj_kernel(x_ref, gain_ref, shift_ref, scale_ref, w_ref, b_ref, ws_ref, bs_ref,
                    o_ref, os_ref, h_ref):
    j = pl.program_id(1)

    @pl.when(j == 0)
    def _():
        x = x_ref[...]
        ms = jnp.mean(x * x, axis=-1, keepdims=True)
        y = x * lax.rsqrt(ms + EPS) * gain_ref[...]
        y = y * (1.0 + scale_ref[0]) + shift_ref[0]
        h = y.astype(BF16)
        h_ref[...] = h
        os_ref[...] = _dot_nt(h, ws_ref[...]) + bs_ref[...]

    o_ref[...] = (_dot_nt(h_ref[...], w_ref[...]) + b_ref[...]).astype(o_ref.dtype)


def _in_proj(x2, gain, shift, scale, w_main, b_main, w_small, b_small, *, T, tm=1024, tn=1536):
    n, d = x2.shape
    nm = w_main.shape[0]
    nt = T // tm
    return pl.pallas_call(
        _in_proj_kernel,
        grid=(n // tm, nm // tn),
        in_specs=[pl.BlockSpec((tm, d), lambda i, j: (i, 0)),
                  pl.BlockSpec((1, d), lambda i, j: (0, 0)),
                  pl.BlockSpec((1, 1, d), lambda i, j: (i // nt, 0, 0)),
                  pl.BlockSpec((1, 1, d), lambda i, j: (i // nt, 0, 0)),
                  pl.BlockSpec((tn, d), lambda i, j: (j, 0)),
                  pl.BlockSpec((1, tn), lambda i, j: (0, j)),
                  pl.BlockSpec((SMALL_WIDTH, d), lambda i, j: (0, 0)),
                  pl.BlockSpec((1, SMALL_WIDTH), lambda i, j: (0, 0))],
        out_specs=[pl.BlockSpec((tm, tn), lambda i, j: (i, j)),
                   pl.BlockSpec((tm, SMALL_WIDTH), lambda i, j: (i, 0))],
        out_shape=[jax.ShapeDtypeStruct((n, nm), BF16),
                   jax.ShapeDtypeStruct((n, SMALL_WIDTH), F32)],
        scratch_shapes=[pltpu.VMEM((tm, d), BF16)],
        compiler_params=pltpu.CompilerParams(dimension_semantics=("parallel", "arbitrary"),
                                             vmem_limit_bytes=56 * 1024 * 1024),
        name="in_proj",
    )(x2, gain, shift, scale, w_main, b_main, w_small, b_small)


def _compress_kernel(kv_ref, pos_ref, w1_ref, w2_ref, o_ref):
    y = kv_ref[0, 0, 0]
    w1 = w1_ref[0]
    half = y.shape[1]
    nch = y.shape[0]
    z1 = _dot(y, w1[:half])
    z2 = _dot(y, w1[half:])
    posb = _dot(pos_ref[0], w1)[0:1]
    pre = z1 + pltpu.roll(z2, nch - 1, 0) + posb
    hid = pre * _sigmoid(pre)
    o_ref[0, 0, 0] = _dot(hid.astype(BF16), w2_ref[0]).astype(o_ref.dtype)


def _compress(kvf, posf, w1, w2):
    two, b, g, nch, half = kvf.shape
    hidden = w1.shape[2]
    dh = w2.shape[2]
    return pl.pallas_call(
        _compress_kernel,
        grid=(two, b, g),
        in_specs=[pl.BlockSpec((1, 1, 1, nch, half), lambda s, i, j: (s, i, j, 0, 0)),
                  pl.BlockSpec((1, 8, 2 * half), lambda s, i, j: (s, 0, 0)),
                  pl.BlockSpec((1, 2 * half, hidden), lambda s, i, j: (s, 0, 0)),
                  pl.BlockSpec((1, hidden, dh), lambda s, i, j: (s, 0, 0))],
        out_specs=pl.BlockSpec((1, 1, 1, nch, dh), lambda s, i, j: (s, i, j, 0, 0)),
        out_shape=jax.ShapeDtypeStruct((two, b, g, nch, dh), BF16),
        compiler_params=pltpu.CompilerParams(dimension_semantics=("parallel", "parallel", "parallel")),
        name="compress",
    )(kvf, posf, w1, w2)


def _select_blocks(imp, t_row, n_sel):
    nbp = imp.shape[0]
    blk = lax.broadcasted_iota(jnp.int32, (nbp, 1), 0)
    cur = lax.shift_right_logical(t_row, 6)
    forced = (blk == 0) | (blk == cur) | (blk == cur - 1)
    bvalid = blk * SLC_BLOCK <= t_row
    score = jnp.where(forced, 1e30, jnp.where(bvalid, imp, -1.0))
    blk_f = blk.astype(F32)
    sel = jnp.zeros(imp.shape, F32)
    for _ in range(n_sel):
        m = jnp.max(score, axis=0, keepdims=True)
        first = jnp.min(jnp.where(score == m, blk_f, float(nbp)), axis=0, keepdims=True)
        hit = blk_f == first
        sel = jnp.where(hit, 1.0, sel)
        score = jnp.where(hit, -1.0, score)
    return sel


def _softmax_step(s, vt, d, m_sc, l_sc, acc_sc, s_max=None):
    if s_max is None:
        s_max = jnp.max(s, axis=0, keepdims=True)
    m_prev = m_sc[...]
    m_new = jnp.maximum(m_prev, s_max - d)
    alpha = jnp.exp2(m_prev - m_new)
    p = jnp.exp2(s - (m_new + d))
    l_sc[...] = alpha * l_sc[...] + jnp.sum(p, axis=0, keepdims=True)
    acc_sc[...] = alpha * acc_sc[...] + _dot(vt, p.astype(BF16))
    m_sc[...] = m_new


def _stage_group(qslab_ref, kslab_ref, vslab_ref, qt_sc, ks_sc, kw_sc, vt_sc, nt, tk):
    dh = NSA_HEAD_DIM
    lane = lax.broadcasted_iota(jnp.int32, (tk, LANE), 1)
    row = lax.broadcasted_iota(jnp.int32, (tk, LANE), 0)
    pos = row.astype(F32)
    feat_sel = jnp.where((lane >= dh) & (lane < dh + N_SLOPE_TERMS), pos, 0.0)
    feat_win = jnp.where(lane < N_SLOPE_TERMS, pos, 0.0)
    blk_in_tile = lax.shift_right_logical(row, 6)

    def body(kt, carry):
        start = pl.multiple_of(kt * tk, tk)
        kk = kslab_ref[pl.ds(start, tk), :].astype(F32)
        onehot = jnp.where(kt * (tk // SLC_BLOCK) + blk_in_tile == lane, 1.0, 0.0)
        ks_sc[kt] = jnp.concatenate([jnp.where(lane < dh, kk, feat_sel), onehot], axis=1).astype(BF16)
        kw_sc[kt] = jnp.where(lane >= dh, kk, feat_win).astype(BF16)
        vt_sc[kt] = vslab_ref[pl.ds(start, tk), :].astype(F32).T.astype(BF16)
        x_t = qslab_ref[pl.ds(start, tk), :].astype(F32).T
        qt_sc[kt] = jnp.concatenate([x_t[hh * dh:(hh + 1) * dh] for hh in range(NSA_HPG)], axis=1).astype(BF16)
        return carry

    lax.fori_loop(0, nt, body, 0)


def _nsa_kernel(slopes_ref, q_ref, kslab_ref, vslab_ref, kc_ref, vct_ref, gt_ref, wimp_ref,
                o_ref, q2_sc, qw_sc, ks_sc, kw_sc, qt_sc, vt_sc, sa_sc, sb_sc, m_sc, l_sc, acc_sc, br_sc, list_sc,
                words_sc, *, tq, nt, nc, n_sel):
    g = pl.program_id(1)
    qi = pl.program_id(2)
    tk = tq
    dh = NSA_HEAD_DIM
    hq = NSA_HPG * tq

    @pl.when(qi == 0)
    def _():
        _stage_group(q_ref, kslab_ref, vslab_ref, qt_sc, ks_sc, kw_sc, vt_sc, nt, tk)

    lane = lax.broadcasted_iota(jnp.int32, (1, hq), 1)
    slope_row = jnp.zeros((1, hq), F32)
    for hh in range(NSA_HPG):
        slope_row = jnp.where(lane >= hh * tq, slopes_ref[g * NSA_HPG + hh], slope_row)
    t_one = qi * tq + lax.broadcasted_iota(jnp.int32, (1, tq), 1)
    t_row = jnp.concatenate([t_one] * NSA_HPG, axis=1)
    t_f = t_row.astype(F32)
    j_loc = lax.broadcasted_iota(jnp.int32, (tk, 1), 0)

    q_t = qt_sc[qi]
    feat_row = lax.broadcasted_iota(jnp.int32, (dh, hq), 0)
    terms = [t.astype(F32) for t in _split3(slope_row)]
    feat = jnp.zeros((dh, hq), F32)
    for i, term in enumerate(terms):
        feat = jnp.where(feat_row == i, term, feat)
    feat = feat.astype(BF16)
    q2_sc[0:dh, :] = q_t
    q2_sc[dh:QFEAT, :] = feat
    qw_sc[0:dh, :] = feat
    qw_sc[dh:QFEAT, :] = q_t

    def reset_stats():
        m_sc[...] = jnp.full(m_sc.shape, NEG, F32)
        l_sc[...] = jnp.zeros(l_sc.shape, F32)
        acc_sc[...] = jnp.zeros(acc_sc.shape, F32)

    def per_head(a):
        return [a[:, hh * tq:(hh + 1) * tq] for hh in range(NSA_HPG)]

    def shift_of(kv, off=None):
        d = slope_row * (t_f - (kv * tk).astype(F32))
        return d if off is None else d + jnp.where(off, -NEG, 0.0)

    per_tile = tk // SLC_BLOCK
    nbp = KSEL_FEAT - QFEAT
    n_words = -(-nt * per_tile // BITS_PER_WORD)

    def front(nk, nblk):
        kc = kc_ref[0, 0, 0:nk, :]
        n_col = lax.broadcasted_iota(jnp.int32, (nk, 1), 0)
        ce = jnp.where(n_col < nc, n_col * CMP_STRIDE + (CMP_BLOCK - 1), 2 ** 30)
        ce_rel = (ce - qi * tq).astype(F32)
        s = jnp.where(ce <= t_row, _dot(kc, q_t) + slope_row * ce_rel, NEG)
        m = jnp.max(s, axis=0, keepdims=True)
        p = jnp.exp2(s - m)
        l = jnp.sum(p, axis=0, keepdims=True)
        p = p * jnp.where(t_row >= CMP_BLOCK - 1, 1.0 / l, 0.0)
        br_sc[0] = _dot(vct_ref[0, 0, :, 0:nk], p.astype(BF16))
        ps4 = per_head(p)
        psum = (ps4[0] + ps4[1]) + (ps4[2] + ps4[3])
        wimp = wimp_ref[0:nblk, 0:nk]
        ph, pm, plo = _split3(psum)
        imp = _dot(wimp, ph) + _dot(wimp, pm) + _dot(wimp, plo)

        reset_stats()
        n_back = WINDOW // tk
        qw = qw_sc[...]
        tiles = [jnp.maximum(qi - w, 0) for w in range(n_back + 1)]
        logits = [_dot(kw_sc[kv], qw) for kv in tiles]
        for w, (kv, s) in enumerate(zip(tiles, logits)):
            if w == 0:
                s = jnp.where((kv * tk + j_loc) <= t_row, s, NEG)
            elif w == n_back:
                s = jnp.where((t_row - (kv * tk + j_loc)) < WINDOW, s, NEG)
            _softmax_step(s, vt_sc[kv, dh:2 * dh, :], shift_of(kv, qi < w), m_sc, l_sc, acc_sc)
        br_sc[1] = acc_sc[...] * (1.0 / l_sc[...])

        s0 = _dot(ks_sc[0, :, 0:QFEAT], q2_sc[0:QFEAT, :])

        sel = _select_blocks(imp, t_one, n_sel)
        penalty_f = (sel - 1.0) * SEL_MASK
        q2_sc[QFEAT:QFEAT + nblk, :] = jnp.concatenate([penalty_f.astype(BF16)] * NSA_HPG, axis=1)
        if nblk < nbp:
            q2_sc[QFEAT + nblk:KSEL_FEAT, :] = jnp.full((nbp - nblk, hq), -SEL_MASK, BF16)
        pen0 = jnp.concatenate([penalty_f[0:per_tile]] * NSA_HPG, axis=1)
        pen0 = jnp.concatenate([jnp.broadcast_to(pen0[b:b + 1], (SLC_BLOCK, hq)) for b in range(per_tile)], axis=0)
        s0 = s0 + pen0
        sa_sc[0:tk, :] = s0
        sa_sc[tk:tk + 1, :] = jnp.max(s0, axis=0, keepdims=True)

        any_tok = jnp.max(sel, axis=1, keepdims=True)
        blk_col = lax.broadcasted_iota(jnp.int32, any_tok.shape, 0)
        bit = jnp.left_shift(1, jnp.bitwise_and(blk_col, BITS_PER_WORD - 1)).astype(F32)
        bits = jnp.where(any_tok > 0.5, bit, 0.0)
        for w in range(n_words):
            lo, hi = w * BITS_PER_WORD, min((w + 1) * BITS_PER_WORD, nblk)
            words_sc[w] = jnp.sum(bits[lo:hi]).astype(jnp.int32) if lo < nblk else jnp.int32(0)

    n_var = 4
    span = nt // n_var
    ncp = kc_ref.shape[2]
    for v in range(1, n_var + 1):
        @pl.when((qi >= (v - 1) * span) & (qi < v * span))
        def _(v=v):
            front(ncp * v // n_var, (nt * per_tile) * v // n_var)

    words = [words_sc[w] for w in range(n_words)]
    for kt in range(nt + 2):
        list_sc[kt] = qi
    n_act = jnp.int32(0)
    tile_mask = (1 << per_tile) - 1
    for kt in range(nt):
        list_sc[n_act] = kt
        w, sh = divmod(kt * per_tile, BITS_PER_WORD)
        hit = jnp.bitwise_and(words[w] >> sh, tile_mask) != 0
        n_act = n_act + hit.astype(jnp.int32)
    list_sc[n_act] = qi
    n_plain = n_act - 1

    reset_stats()

    def sel_logits(idx, s_ref):
        s = _dot(ks_sc[list_sc[idx]], q2_sc[...])
        s_ref[0:tk, :] = s
        s_ref[tk:tk + 1, :] = jnp.max(s, axis=0, keepdims=True)

    def sel_update(idx, s_ref):
        kv = list_sc[idx]
        _softmax_step(s_ref[0:tk, :], vt_sc[kv, 0:dh, :], shift_of(kv, idx >= n_plain), m_sc, l_sc, acc_sc,
                      s_max=s_ref[tk:tk + 1, :])

    def pair_body(i, carry):
        sel_logits(2 * i + 1, sb_sc)
        sel_update(2 * i, sa_sc)
        sel_logits(2 * i + 2, sa_sc)
        sel_update(2 * i + 1, sb_sc)
        return carry

    lax.fori_loop(0, lax.shift_right_logical(n_plain + 1, 1), pair_body, 0)
    causal = (qi * tk + j_loc) <= t_row
    _softmax_step(jnp.where(causal, sa_sc[0:tk, :], NEG), vt_sc[qi, 0:dh, :], shift_of(qi), m_sc, l_sc, acc_sc)
    o_sel = acc_sc[...] * (1.0 / l_sc[...])

    sg = _sigmoid(gt_ref[0, 0])
    gates = [jnp.concatenate([sg[3 * hh + br:3 * hh + br + 1] for hh in range(NSA_HPG)], axis=1)
             for br in range(3)]
    o_all = gates[0] * br_sc[0] + gates[1] * o_sel + gates[2] * br_sc[1]
    o_ref[...] = jnp.concatenate(per_head(o_all), axis=0).T.astype(o_ref.dtype)


def _nsa(slopes, pm, kc, vct, gt, wimp, *, B, T, nc, n_sel):
    b, g, hpg, dh = B, NSA_KV_GROUPS, NSA_HPG, NSA_HEAD_DIM
    tq = NSA_TILE
    nt = T // tq
    hq = hpg * tq
    ncp = kc.shape[2]
    nbp = wimp.shape[0]
    qb = _NEW["nsa_q"] // (hpg * dh)
    kb, vb = _NEW["sel_win_k"] // LANE, _NEW["sel_win_v"] // LANE
    kern = functools.partial(_nsa_kernel, tq=tq, nt=nt, nc=nc, n_sel=n_sel)
    return pl.pallas_call(
        kern,
        grid=(b, g, nt),
        in_specs=[pl.BlockSpec(memory_space=pltpu.SMEM),
                  pl.BlockSpec((T, hpg * dh), lambda i, j, k: (i, qb + j)),
                  pl.BlockSpec((T, LANE), lambda i, j, k: (i, kb + j)),
                  pl.BlockSpec((T, LANE), lambda i, j, k: (i, vb + j)),
                  pl.BlockSpec((1, 1, ncp, dh), lambda i, j, k: (i, j, 0, 0)),
                  pl.BlockSpec((1, 1, dh, ncp), lambda i, j, k: (i, j, 0, 0)),
                  pl.BlockSpec((1, 1, 3 * hpg, tq), lambda i, j, k: (i, j, 0, k)),
                  pl.BlockSpec((nbp, ncp), lambda i, j, k: (0, 0))],
        out_specs=pl.BlockSpec((tq, hpg * dh), lambda i, j, k: (i * nt + k, j)),
        out_shape=jax.ShapeDtypeStruct((b * T, g * hpg * dh), BF16),
        scratch_shapes=[pltpu.VMEM((KSEL_FEAT, hq), BF16), pltpu.VMEM((QFEAT, hq), BF16),
                        pltpu.VMEM((nt, tq, KSEL_FEAT), BF16), pltpu.VMEM((nt, tq, QFEAT), BF16),
                        pltpu.VMEM((nt, dh, hq), BF16), pltpu.VMEM((nt, 2 * dh, tq), BF16),
                        pltpu.VMEM((tq + 8, hq), F32), pltpu.VMEM((tq + 8, hq), F32),
                        pltpu.VMEM((1, hq), F32), pltpu.VMEM((1, hq), F32),
                        pltpu.VMEM((dh, hq), F32),
                        pltpu.VMEM((2, dh, hq), F32),
                        pltpu.SMEM((nt + 2,), jnp.int32),
                        pltpu.SMEM((-(-nt * (tq // SLC_BLOCK) // BITS_PER_WORD),), jnp.int32)],
        compiler_params=pltpu.CompilerParams(dimension_semantics=("parallel", "parallel", "arbitrary"),
                                             vmem_limit_bytes=48 * 1024 * 1024),
        name="nsa",
    )(slopes, pm, pm, pm, kc, vct, gt, wimp)


def _gla_intra_scores(q, k, bcum):
    c, dk = q.shape
    sb = GLA_SUB
    row_c = lax.broadcasted_iota(jnp.int32, (c, dk), 0)
    row_s = lax.broadcasted_iota(jnp.int32, (sb, 1), 0)
    lane_s = lax.broadcasted_iota(jnp.int32, (sb, c), 1)
    blocks = []
    for i0 in range(0, c, sb):
        bi = bcum[i0:i0 + sb]
        qi = q[i0:i0 + sb]
        a_blk = jnp.zeros((sb, c), F32)
        for jl in range(sb):
            j = i0 + jl
            w = jnp.exp2(bi - bcum[j:j + 1])
            a = jnp.sum(qi * k[j:j + 1] * w, axis=-1, keepdims=True)
            a_blk = jnp.where(lane_s == j, jnp.where(row_s >= jl, a, 0.0), a_blk)
        if i0 > 0:
            r = bcum[i0 - 1:i0]
            qh = qi * jnp.exp2(bi - r)
            kh = k * jnp.exp2(jnp.where(row_c < i0, r - bcum, NEG))
            a_blk = a_blk + _dot_nt(qh.astype(BF16), kh.astype(BF16))
        blocks.append(a_blk)
    return jnp.concatenate(blocks, axis=0)


def _gla_kernel(q_ref, k_ref, v_ref, z_ref, ps_ref, wal_ref, bal_ref, gain_ref, tri_ref, o_ref,
                st_sc, *, ts):
    ti = pl.program_id(2)
    c = GLA_CHUNK

    @pl.when(ti == 0)
    def _():
        st_sc[...] = jnp.zeros(st_sc.shape, F32)

    ps = ps_ref[...]
    ph, pm, _ = _split3(ps)
    wh, wm, _ = _split3(wal_ref[...])
    zz = _dot(ph, wh) + _dot(ph, wm) + _dot(pm, wh) + bal_ref[...]
    log_a = (jnp.minimum(zz, 0.0) - jnp.log(1.0 + jnp.exp(-jnp.abs(zz)))) * (LOG2E / GLA_TAU)
    tri = tri_ref[...]
    qscale = GLA_KEY_DIM ** -0.5
    for ci in range(ts // c):
        sl = slice(ci * c, (ci + 1) * c)
        gh, gm, gl = _split3(log_a[sl])
        bcum = _dot(tri, gh) + _dot(tri, gm) + _dot(tri, gl)
        q = q_ref[sl, :].astype(F32) * qscale
        k = k_ref[sl, :].astype(F32)
        v = v_ref[sl, :]
        a_mat = _gla_intra_scores(q, k, bcum)
        st = st_sc[...]
        o = _dot(a_mat.astype(BF16), v) + _dot_nt((q * jnp.exp2(bcum)).astype(BF16), st.astype(BF16))
        b_last = bcum[c - 1:c, :]
        kd = k * jnp.exp2(b_last - bcum)
        st_sc[...] = st * jnp.exp2(b_last) + _dot(v.astype(F32).T.astype(BF16), kd.astype(BF16))
        ms = jnp.mean(o * o, axis=-1, keepdims=True)
        y = o * lax.rsqrt(ms + EPS) * gain_ref[...]
        z = z_ref[sl, :].astype(F32)
        o_ref[sl, :] = (y * (z * _sigmoid(z))).astype(o_ref.dtype)


def _gla(pm, ps, wal, bal, gain, tri, *, B, T, ts=1024):
    n = pm.shape[0]
    nts = T // ts
    dk, dv = GLA_KEY_DIM, GLA_VAL_DIM
    qb, kb = _NEW["gq"] // dk, _NEW["gk"] // dk
    vb, zb = _NEW["gv"] // dv, _NEW["gla_z"] // dv
    kern = functools.partial(_gla_kernel, ts=ts)
    return pl.pallas_call(
        kern,
        grid=(B, GLA_HEADS, nts),
        in_specs=[pl.BlockSpec((ts, dk), lambda b, h, i: (b * nts + i, qb + h)),
                  pl.BlockSpec((ts, dk), lambda b, h, i: (b * nts + i, kb + h)),
                  pl.BlockSpec((ts, dv), lambda b, h, i: (b * nts + i, vb + h)),
                  pl.BlockSpec((ts, dv), lambda b, h, i: (b * nts + i, zb + h)),
                  pl.BlockSpec((ts, SMALL_WIDTH), lambda b, h, i: (b * nts + i, 0)),
                  pl.BlockSpec((SMALL_WIDTH, dk), lambda b, h, i: (0, h)),
                  pl.BlockSpec((1, dk), lambda b, h, i: (0, h)),
                  pl.BlockSpec((1, dv), lambda b, h, i: (0, 0)),
                  pl.BlockSpec((GLA_CHUNK, GLA_CHUNK), lambda b, h, i: (0, 0))],
        out_specs=pl.BlockSpec((ts, dv), lambda b, h, i: (b * nts + i, h)),
        out_shape=jax.ShapeDtypeStruct((n, GLA_WIDTH), BF16),
        scratch_shapes=[pltpu.VMEM((dv, dk), F32)],
        compiler_params=pltpu.CompilerParams(dimension_semantics=("parallel", "parallel", "arbitrary")),
        name="gla",
    )(pm, pm, pm, pm, ps, wal, bal, gain, tri)


def _out_proj_kernel(x_ref, gate_ref, on_ref, nz_ref, mgn_ref, mgg_ref, yg_ref,
                     wn_ref, wg_ref, wo_ref, fg_ref, o_ref):
    nz = nz_ref[...].astype(F32)
    y_nsa = on_ref[...].astype(F32) * (nz * _sigmoid(nz))
    a = _dot(y_nsa.astype(BF16), wn_ref[...])
    bm = _dot(yg_ref[...], wg_ref[...])
    merged = _sigmoid(mgn_ref[...].astype(F32)) * a + _sigmoid(mgg_ref[...].astype(F32)) * bm
    xn = x_ref[...] + gate_ref[0] * _dot(merged.astype(BF16), wo_ref[...])
    ms = jnp.mean(xn * xn, axis=-1, keepdims=True)
    o_ref[...] = xn * lax.rsqrt(ms + EPS) * fg_ref[...]


def _out_proj(x2, gate, o_nsa, pm, yg, wn, wg, wo, fg, *, T, tm=256):
    n, d = x2.shape
    nt = T // tm
    nzb = _NEW["nsa_z"] // NSA_WIDTH
    mnb = _NEW["mg_nsa"] // D_MODEL
    mgb = _NEW["mg_gla"] // D_MODEL
    row = lambda i: (i, 0)
    const2 = lambda i: (0, 0)
    return pl.pallas_call(
        _out_proj_kernel,
        grid=(n // tm,),
        in_specs=[pl.BlockSpec((tm, d), row),
                  pl.BlockSpec((1, 1, d), lambda i: (i // nt, 0, 0)),
                  pl.BlockSpec((tm, NSA_WIDTH), row),
                  pl.BlockSpec((tm, NSA_WIDTH), lambda i: (i, nzb)),
                  pl.BlockSpec((tm, D_MODEL), lambda i: (i, mnb)),
                  pl.BlockSpec((tm, D_MODEL), lambda i: (i, mgb)),
                  pl.BlockSpec((tm, GLA_WIDTH), row),
                  pl.BlockSpec((NSA_WIDTH, d), const2),
                  pl.BlockSpec((GLA_WIDTH, d), const2),
                  pl.BlockSpec((d, d), const2),
                  pl.BlockSpec((1, d), const2)],
        out_specs=pl.BlockSpec((tm, d), row),
        out_shape=jax.ShapeDtypeStruct((n, d), F32),
        compiler_params=pltpu.CompilerParams(dimension_semantics=("parallel",),
                                             vmem_limit_bytes=60 * 1024 * 1024),
        name="out_proj",
    )(x2, gate, o_nsa, pm, pm, pm, yg, wn, wg, wo, fg)


def _importance_matrix(nbp, ncp):
    ratio = SLC_BLOCK // CMP_STRIDE
    n_sub = CMP_BLOCK // CMP_STRIDE
    w = np.zeros((nbp, ncp), np.float32)
    for blk in range(nbp):
        for m in range(ratio):
            for s in range(n_sub):
                n = ratio * blk + m - s
                if 0 <= n < ncp:
                    w[blk, n] += 1.0
    return w


def _main_cols(a):
    parts = [a[..., start:start + width] * (Q_SCALE if nm == "nsa_q" else 1.0)
             for nm, pieces in _MAIN_PIECES for start, width in pieces]
    return jnp.concatenate(parts, axis=-1)


def _small_cols(a):
    pad = SMALL_WIDTH - 3 * NSA_HEADS - GLA_GATE_RANK
    parts = [a[..., _OLD["nsa_g"][0]:_OLD["nsa_g"][0] + 3 * NSA_HEADS],
             a[..., _OLD["ga"][0]:_OLD["ga"][0] + GLA_GATE_RANK],
             jnp.zeros(a.shape[:-1] + (pad,), a.dtype)]
    return jnp.concatenate(parts, axis=-1)


def kernel(x, c, w_ada, b_ada, norm_gain, w_in, b_in, cmp_pos_k, cmp_pos_v, cmp_w1_k, cmp_w2_k, cmp_w1_v, cmp_w2_v,
           gla_w_alpha, gla_b_alpha, gla_norm_gain, w_br_nsa, w_br_gla, w_out, final_norm_gain):
    assert DEPTH == 1, "the final rmsnorm is fused into the single layer's output kernel"
    B, T, D = x.shape
    G, HPG, DH = NSA_KV_GROUPS, NSA_HPG, NSA_HEAD_DIM
    N = B * T
    nch = T // CMP_STRIDE
    nc = nch - CMP_BLOCK // CMP_STRIDE + 1
    nb = T // SLC_BLOCK
    nbp = KSEL_FEAT - QFEAT
    assert nb <= nbp and T % NSA_TILE == 0
    n_sel = min(N_SELECT, nb)
    tk = NSA_TILE
    l = 0

    slopes = 2.0 ** (-8.0 * jnp.arange(1, NSA_HEADS + 1, dtype=F32) / NSA_HEADS)
    wimp = jnp.asarray(_importance_matrix(nbp, nch), BF16)
    tri = jnp.asarray(np.tril(np.ones((GLA_CHUNK, GLA_CHUNK), np.float32)), BF16)
    bp = ((B + 7) // 8) * 8
    c8 = jnp.pad(c, ((0, bp - B), (0, 0)))
    x2 = x.reshape(N, D)

    mod = _ada(c8, w_ada[l], b_ada[l][None, :])
    shift = mod[:B, :D].reshape(B, 1, D)
    scale = mod[:B, D:2 * D].reshape(B, 1, D)
    gate = mod[:B, 2 * D:].reshape(B, 1, D)

    w_in_t = jnp.transpose(w_in[l])
    pm, ps = _in_proj(x2, norm_gain[l][None, :], shift, scale,
                      _wprep(w_in_t), _main_cols(b_in[l])[None, :],
                      _wprep_small(w_in_t), _small_cols(b_in[l])[None, :], T=T)

    def cols(name, width=NSA_KV_WIDTH):
        return pm[:, _NEW[name]:_NEW[name] + width]

    kvf = cols("ck", 2 * NSA_KV_WIDTH).reshape(B, nch, CMP_STRIDE, 2, G, DH)
    kvf = kvf.transpose(3, 0, 4, 1, 2, 5).reshape(2, B, G, nch, CMP_STRIDE * DH)
    posf = jnp.stack([cmp_pos_k[l], cmp_pos_v[l]]).reshape(2, 1, CMP_BLOCK * DH)
    posf = jnp.broadcast_to(posf, (2, 8, CMP_BLOCK * DH)).astype(BF16)
    w1 = jnp.stack([cmp_w1_k[l], cmp_w1_v[l]]).astype(BF16)
    w2 = jnp.stack([cmp_w2_k[l], cmp_w2_v[l]]).astype(BF16)
    kvc = _compress(kvf, posf, w1, w2)
    gt = ps[:, :3 * NSA_HEADS].reshape(B, T, G, 3 * HPG).transpose(0, 2, 3, 1)
    slopes2 = slopes * LOG2E
    o_nsa = _nsa(slopes2, pm, kvc[0], kvc[1].transpose(0, 1, 3, 2), gt, wimp, B=B, T=T, nc=nc, n_sel=n_sel)

    wal = jnp.pad(gla_w_alpha[l], ((GA_LANE, SMALL_WIDTH - GA_LANE - GLA_GATE_RANK), (0, 0)))
    y_gla = _gla(pm, ps, wal, gla_b_alpha[l][None, :], gla_norm_gain[l][None, :], tri, B=B, T=T)

    out = _out_proj(x2, gate, o_nsa, pm, y_gla,
                    w_br_nsa[l].astype(BF16), w_br_gla[l].astype(BF16), w_out[l].astype(BF16),
                    final_norm_gain[None, :], T=T)
    return out.reshape(B, T, D)
```

```python
import functools

import numpy as np
import jax
import jax.numpy as jnp
from jax import lax
from jax.experimental import pallas as pl
from jax.experimental.pallas import tpu as pltpu

D_MODEL = 2048
DEPTH = 1
NSA_HEADS = 16
NSA_KV_GROUPS = 4
NSA_HPG = NSA_HEADS // NSA_KV_GROUPS
NSA_HEAD_DIM = 64
CMP_BLOCK = 32
CMP_STRIDE = 16
CMP_HIDDEN = 256
SLC_BLOCK = 64
N_SELECT = 16
WINDOW = 512
NSA_WIDTH = NSA_HEADS * NSA_HEAD_DIM
NSA_KV_WIDTH = NSA_KV_GROUPS * NSA_HEAD_DIM
GLA_HEADS = 4
GLA_KEY_DIM = 128
GLA_VAL_DIM = 256
GLA_GATE_RANK = 16
GLA_TAU = 16.0
GLA_KEY_WIDTH = GLA_HEADS * GLA_KEY_DIM
GLA_WIDTH = GLA_HEADS * GLA_VAL_DIM
EPS = 1e-6
NEG = -1e30

F32 = jnp.float32
BF16 = jnp.bfloat16

_OLD = {}
_off = 0
for _name, _w in (("nsa_q", NSA_WIDTH), ("ck", NSA_KV_WIDTH), ("cv", NSA_KV_WIDTH), ("sk", NSA_KV_WIDTH),
                  ("sv", NSA_KV_WIDTH), ("wk", NSA_KV_WIDTH), ("wv", NSA_KV_WIDTH), ("nsa_g", 3 * NSA_HEADS),
                  ("nsa_z", NSA_WIDTH), ("gq", GLA_KEY_WIDTH), ("gk", GLA_KEY_WIDTH), ("gv", GLA_WIDTH),
                  ("ga", GLA_GATE_RANK), ("gla_z", GLA_WIDTH), ("mg_nsa", D_MODEL), ("mg_gla", D_MODEL)):
    _OLD[_name] = (_off, _w)
    _off += _w

def _pair_by_group(a, b):
    return [(_OLD[nm][0] + g * NSA_HEAD_DIM, NSA_HEAD_DIM) for g in range(NSA_KV_GROUPS) for nm in (a, b)]


_MAIN_PIECES = [(nm, [_OLD[nm]]) for nm in
                ("nsa_q", "nsa_z", "gla_z", "gv", "mg_nsa", "mg_gla", "gq", "gk", "ck", "cv")]
_MAIN_PIECES += [("sel_win_k", _pair_by_group("sk", "wk")), ("sel_win_v", _pair_by_group("sv", "wv"))]
_NEW = {}
_off = 0
for _name, _pieces in _MAIN_PIECES:
    _NEW[_name] = _off
    _off += sum(w for _, w in _pieces)
MAIN_WIDTH = _off
SMALL_WIDTH = 128
GA_LANE = 3 * NSA_HEADS

LANE = 128
GLA_CHUNK = 64
GLA_SUB = 16
GLA_PAIR = 2
NSA_TILE = 256
QFEAT = 2 * NSA_HEAD_DIM
KSEL_FEAT = QFEAT + LANE
N_SLOPE_TERMS = 3
SEL_MASK = 16384.0
BITS_PER_WORD = 16
LOG2E = 1.4426950408889634
Q_SCALE = LOG2E * NSA_HEAD_DIM ** -0.5


def _dot(a, b):
    return jnp.dot(a, b, preferred_element_type=F32)


def _dot_nt(a, b):
    return lax.dot_general(a, b, (((1,), (1,)), ((), ())), preferred_element_type=F32)


def _split3(x):
    hi = x.astype(BF16)
    r = x - hi.astype(F32)
    mid = r.astype(BF16)
    lo = (r - mid.astype(F32)).astype(BF16)
    return hi, mid, lo


def _sigmoid(x):
    return 1.0 / (1.0 + jnp.exp(-x))


def _ada_kernel(c_ref, w_ref, b_ref, o_ref):
    ch, cm, cl = _split3(c_ref[...])
    wh, wm, wl = _split3(w_ref[...])
    acc = _dot(ch, wh) + _dot(ch, wm) + _dot(cm, wh) + _dot(ch, wl) + _dot(cl, wh) + _dot(cm, wm)
    o_ref[...] = acc + b_ref[...]


def _ada(c8, w, b, *, tn=768):
    m, d = c8.shape
    n = w.shape[1]
    return pl.pallas_call(
        _ada_kernel,
        grid=(n // tn,),
        in_specs=[pl.BlockSpec((m, d), lambda j: (0, 0)),
                  pl.BlockSpec((d, tn), lambda j: (0, j)),
                  pl.BlockSpec((1, tn), lambda j: (0, j))],
        out_specs=pl.BlockSpec((m, tn), lambda j: (0, j)),
        out_shape=jax.ShapeDtypeStruct((m, n), F32),
        compiler_params=pltpu.CompilerParams(dimension_semantics=("arbitrary",),
                                             vmem_limit_bytes=40 * 1024 * 1024),
        name="ada",
    )(c8, w, b)


WPREP_TILE = 512


def _wprep_tiles():
    tiles = []
    for name, pieces in _MAIN_PIECES:
        scale = Q_SCALE if name == "nsa_q" else 1.0
        if len(pieces) == 1:
            start, width = pieces[0]
            assert width % WPREP_TILE == 0 or (name in ("ck", "cv") and width * 2 == WPREP_TILE)
            if name == "cv":
                assert start == _OLD["ck"][0] + _OLD["ck"][1]
                continue
            for off in range(0, max(width, WPREP_TILE), WPREP_TILE):
                tiles.append((start + off, (start + off) % LANE, scale))
        else:
            first, second = pieces[0][0], pieces[1][0]
            assert second == first + WPREP_TILE and first % LANE == 0 and len(pieces) * NSA_HEAD_DIM == WPREP_TILE
            tiles.append((first, -1, scale))
    return tiles


def _wprep_kernel(start_ref, kind_ref, scale_ref, a_ref, b_ref, o_ref):
    j = pl.program_id(0)
    scale = scale_ref[j]

    @pl.when(kind_ref[j] >= 0)
    def _():
        o_ref[...] = (a_ref[...] * scale).astype(o_ref.dtype)

    @pl.when(kind_ref[j] < 0)
    def _():
        half = NSA_HEAD_DIM
        for g in range(NSA_KV_GROUPS):
            o_ref[2 * g * half:(2 * g + 1) * half, :] = (a_ref[g * half:(g + 1) * half, :] * scale).astype(o_ref.dtype)
            o_ref[(2 * g + 1) * half:(2 * g + 2) * half, :] = (b_ref[g * half:(g + 1) * half, :] * scale).astype(o_ref.dtype)


def _wprep(wt):
    width, d = wt.shape
    tiles = _wprep_tiles()
    half_tile = WPREP_TILE // 2
    assert len(tiles) * WPREP_TILE == MAIN_WIDTH and all(t[0] % 8 == 0 for t in tiles)
    start = jnp.asarray([t[0] for t in tiles], jnp.int32)
    kind = jnp.asarray([t[1] for t in tiles], jnp.int32)
    scale = jnp.asarray([t[2] for t in tiles], F32)
    return pl.pallas_call(
        _wprep_kernel,
        grid_spec=pltpu.PrefetchScalarGridSpec(
            num_scalar_prefetch=2,
            grid=(len(tiles),),
            in_specs=[pl.BlockSpec(memory_space=pltpu.SMEM),
                      pl.BlockSpec((pl.Element(WPREP_TILE), pl.Element(d)),
                                   lambda j, st, kd: (pl.multiple_of(st[j], 8), 0)),
                      pl.BlockSpec((pl.Element(half_tile), pl.Element(d)),
                                   lambda j, st, kd: (pl.multiple_of(
                                       jnp.minimum(st[j] + WPREP_TILE, width - half_tile), 8), 0))],
            out_specs=pl.BlockSpec((WPREP_TILE, d), lambda j, st, kd: (j, 0))),
        out_shape=jax.ShapeDtypeStruct((MAIN_WIDTH, d), BF16),
        compiler_params=pltpu.CompilerParams(dimension_semantics=("arbitrary",),
                                             vmem_limit_bytes=40 * 1024 * 1024),
        name="wprep",
    )(start, kind, scale, wt, wt)


def _wprep_small_kernel(g_ref, a_ref, o_ref):
    n_gate = 3 * NSA_HEADS
    rows = jnp.concatenate([g_ref[0:n_gate, :], a_ref[...],
                            jnp.zeros((SMALL_WIDTH - n_gate - GLA_GATE_RANK, g_ref.shape[1]), F32)], axis=0)
    o_ref[...] = rows.astype(o_ref.dtype)


def _wprep_small(wt):
    d = wt.shape[1]
    g0, a0 = _OLD["nsa_g"][0], _OLD["ga"][0]
    assert g0 % 8 == 0 and a0 % 8 == 0
    return pl.pallas_call(
        _wprep_small_kernel,
        grid=(1,),
        in_specs=[pl.BlockSpec((pl.Element(NSA_HEAD_DIM), pl.Element(d)), lambda i: (g0, 0)),
                  pl.BlockSpec((pl.Element(GLA_GATE_RANK), pl.Element(d)), lambda i: (a0, 0))],
        out_specs=pl.BlockSpec((SMALL_WIDTH, d), lambda i: (0, 0)),
        out_shape=jax.ShapeDtypeStruct((SMALL_WIDTH, d), BF16),
        name="wprep_small",
    )(wt, wt)


def _in_proj_kernel(x_ref, gain_ref, shift_ref, scale_ref, w_ref, b_ref, ws_ref, bs_ref,
                    o_ref, os_ref, h_ref):
    j = pl.program_id(1)

    @pl.when(j == 0)
    def _():
        x = x_ref[...]
        ms = jnp.mean(x * x, axis=-1, keepdims=True)
        y = x * lax.rsqrt(ms + EPS) * gain_ref[...]
        y = y * (1.0 + scale_ref[0]) + shift_ref[0]
        h = y.astype(BF16)
        h_ref[...] = h
        os_ref[...] = _dot_nt(h, ws_ref[...]) + bs_ref[...]

    o_ref[...] = (_dot_nt(h_ref[...], w_ref[...]) + b_ref[...]).astype(o_ref.dtype)


def _in_proj(x2, gain, shift, scale, w_main, b_main, w_small, b_small, *, T, tm=1024, tn=1536):
    n, d = x2.shape
    nm = w_main.shape[0]
    nt = T // tm
    return pl.pallas_call(
        _in_proj_kernel,
        grid=(n // tm, nm // tn),
        in_specs=[pl.BlockSpec((tm, d), lambda i, j: (i, 0)),
                  pl.BlockSpec((1, d), lambda i, j: (0, 0)),
                  pl.BlockSpec((1, 1, d), lambda i, j: (i // nt, 0, 0)),
                  pl.BlockSpec((1, 1, d), lambda i, j: (i // nt, 0, 0)),
                  pl.BlockSpec((tn, d), lambda i, j: (j, 0)),
                  pl.BlockSpec((1, tn), lambda i, j: (0, j)),
                  pl.BlockSpec((SMALL_WIDTH, d), lambda i, j: (0, 0)),
                  pl.BlockSpec((1, SMALL_WIDTH), lambda i, j: (0, 0))],
        out_specs=[pl.BlockSpec((tm, tn), lambda i, j: (i, j)),
                   pl.BlockSpec((tm, SMALL_WIDTH), lambda i, j: (i, 0))],
        out_shape=[jax.ShapeDtypeStruct((n, nm), BF16),
                   jax.ShapeDtypeStruct((n, SMALL_WIDTH), F32)],
        scratch_shapes=[pltpu.VMEM((tm, d), BF16)],
        compiler_params=pltpu.CompilerParams(dimension_semantics=("parallel", "arbitrary"),
                                             vmem_limit_bytes=56 * 1024 * 1024),
        name="in_proj",
    )(x2, gain, shift, scale, w_main, b_main, w_small, b_small)


def _compress_kernel(kv_ref, pos_ref, w1_ref, w2_ref, o_ref):
    y = kv_ref[0, 0, 0]
    w1 = w1_ref[0]
    half = y.shape[1]
    nch = y.shape[0]
    z1 = _dot(y, w1[:half])
    z2 = _dot(y, w1[half:])
    posb = _dot(pos_ref[0], w1)[0:1]
    pre = z1 + pltpu.roll(z2, nch - 1, 0) + posb
    hid = pre * _sigmoid(pre)
    o_ref[0, 0, 0] = _dot(hid.astype(BF16), w2_ref[0]).astype(o_ref.dtype)


def _compress(kvf, posf, w1, w2):
    two, b, g, nch, half = kvf.shape
    hidden = w1.shape[2]
    dh = w2.shape[2]
    return pl.pallas_call(
        _compress_kernel,
        grid=(two, b, g),
        in_specs=[pl.BlockSpec((1, 1, 1, nch, half), lambda s, i, j: (s, i, j, 0, 0)),
                  pl.BlockSpec((1, 8, 2 * half), lambda s, i, j: (s, 0, 0)),
                  pl.BlockSpec((1, 2 * half, hidden), lambda s, i, j: (s, 0, 0)),
                  pl.BlockSpec((1, hidden, dh), lambda s, i, j: (s, 0, 0))],
        out_specs=pl.BlockSpec((1, 1, 1, nch, dh), lambda s, i, j: (s, i, j, 0, 0)),
        out_shape=jax.ShapeDtypeStruct((two, b, g, nch, dh), BF16),
        compiler_params=pltpu.CompilerParams(dimension_semantics=("parallel", "parallel", "parallel")),
        name="compress",
    )(kvf, posf, w1, w2)


def _select_blocks(imp, t_row, n_sel, three_forced):
    nbp = imp.shape[0]
    blk = lax.broadcasted_iota(jnp.int32, (nbp, 1), 0)
    cur = lax.shift_right_logical(t_row, 6)
    forced = (blk == 0) | (blk == cur) | (blk == cur - 1)
    bvalid = blk * SLC_BLOCK <= t_row
    rest = jnp.where(bvalid, imp, -1.0)
    blk_f = blk.astype(F32)
    if three_forced:
        score = jnp.where(forced, -1.0, rest)
        sel = jnp.where(forced, 1.0, 0.0)
        rounds = n_sel - 3
    else:
        score = jnp.where(forced, 1e30, rest)
        sel = jnp.zeros(imp.shape, F32)
        rounds = n_sel
    for _ in range(rounds):
        m = jnp.max(score, axis=0, keepdims=True)
        first = jnp.min(jnp.where(score == m, blk_f, float(nbp)), axis=0, keepdims=True)
        hit = blk_f == first
        sel = jnp.where(hit, 1.0, sel)
        score = jnp.where(hit, -1.0, score)
    return sel


def _softmax_step(s, vt, d, m_sc, l_sc, acc_sc, s_max=None):
    if s_max is None:
        s_max = jnp.max(s, axis=0, keepdims=True)
    m_prev = m_sc[...]
    m_new = jnp.maximum(m_prev, s_max - d)
    alpha = jnp.exp2(m_prev - m_new)
    p = jnp.exp2(s - (m_new + d))
    l_sc[...] = alpha * l_sc[...] + jnp.sum(p, axis=0, keepdims=True)
    acc_sc[...] = alpha * acc_sc[...] + _dot(vt, p.astype(BF16))
    m_sc[...] = m_new


def _stage_group(qslab_ref, kslab_ref, vslab_ref, qt_sc, ks_sc, kw_sc, vt_sc, nt, tk):
    dh = NSA_HEAD_DIM
    lane = lax.broadcasted_iota(jnp.int32, (tk, LANE), 1)
    row = lax.broadcasted_iota(jnp.int32, (tk, LANE), 0)
    pos = row.astype(F32)
    feat_sel = jnp.where((lane >= dh) & (lane < dh + N_SLOPE_TERMS), pos, 0.0)
    feat_win = jnp.where(lane < N_SLOPE_TERMS, pos, 0.0)
    blk_in_tile = lax.shift_right_logical(row, 6)

    def body(kt, carry):
        start = pl.multiple_of(kt * tk, tk)
        kk = kslab_ref[pl.ds(start, tk), :].astype(F32)
        onehot = jnp.where(kt * (tk // SLC_BLOCK) + blk_in_tile == lane, 1.0, 0.0)
        ks_sc[kt] = jnp.concatenate([jnp.where(lane < dh, kk, feat_sel), onehot], axis=1).astype(BF16)
        kw_sc[kt] = jnp.where(lane >= dh, kk, feat_win).astype(BF16)
        vt_sc[kt] = vslab_ref[pl.ds(start, tk), :].astype(F32).T.astype(BF16)
        x_t = qslab_ref[pl.ds(start, tk), :].astype(F32).T
        qt_sc[kt] = jnp.concatenate([x_t[hh * dh:(hh + 1) * dh] for hh in range(NSA_HPG)], axis=1).astype(BF16)
        return carry

    lax.fori_loop(0, nt, body, 0)


def _nsa_kernel(slopes_ref, q_ref, kslab_ref, vslab_ref, kc_ref, vct_ref, gt_ref, wimp_ref,
                o_ref, q2_sc, qw_sc, ks_sc, kw_sc, qt_sc, vt_sc, sa_sc, sb_sc, m_sc, l_sc, acc_sc, br_sc, list_sc,
                words_sc, *, tq, nt, nc, n_sel):
    g = pl.program_id(1)
    qi = pl.program_id(2)
    tk = tq
    dh = NSA_HEAD_DIM
    hq = NSA_HPG * tq

    @pl.when(qi == 0)
    def _():
        _stage_group(q_ref, kslab_ref, vslab_ref, qt_sc, ks_sc, kw_sc, vt_sc, nt, tk)

    lane = lax.broadcasted_iota(jnp.int32, (1, hq), 1)
    slope_row = jnp.zeros((1, hq), F32)
    for hh in range(NSA_HPG):
        slope_row = jnp.where(lane >= hh * tq, slopes_ref[g * NSA_HPG + hh], slope_row)
    t_one = qi * tq + lax.broadcasted_iota(jnp.int32, (1, tq), 1)
    t_row = jnp.concatenate([t_one] * NSA_HPG, axis=1)
    t_f = t_row.astype(F32)
    j_loc = lax.broadcasted_iota(jnp.int32, (tk, 1), 0)

    q_t = qt_sc[qi]
    feat_row = lax.broadcasted_iota(jnp.int32, (dh, hq), 0)
    terms = [t.astype(F32) for t in _split3(slope_row)]
    feat = jnp.zeros((dh, hq), F32)
    for i, term in enumerate(terms):
        feat = jnp.where(feat_row == i, term, feat)
    feat = feat.astype(BF16)
    q2_sc[0:dh, :] = q_t
    q2_sc[dh:QFEAT, :] = feat
    qw_sc[0:dh, :] = feat
    qw_sc[dh:QFEAT, :] = q_t

    def reset_stats():
        m_sc[...] = jnp.full(m_sc.shape, NEG, F32)
        l_sc[...] = jnp.zeros(l_sc.shape, F32)
        acc_sc[...] = jnp.zeros(acc_sc.shape, F32)

    def per_head(a):
        return [a[:, hh * tq:(hh + 1) * tq] for hh in range(NSA_HPG)]

    def shift_of(kv, off=None):
        d = slope_row * (t_f - (kv * tk).astype(F32))
        return d if off is None else d + jnp.where(off, -NEG, 0.0)

    per_tile = tk // SLC_BLOCK
    nbp = KSEL_FEAT - QFEAT
    n_words = -(-nt * per_tile // BITS_PER_WORD)

    def front(nk, nblk, three_forced):
        kc = kc_ref[0, 0, 0:nk, :]
        n_col = lax.broadcasted_iota(jnp.int32, (nk, 1), 0)
        ce = jnp.where(n_col < nc, n_col * CMP_STRIDE + (CMP_BLOCK - 1), 2 ** 30)
        ce_rel = (ce - qi * tq).astype(F32)
        s = jnp.where(ce <= t_row, _dot(kc, q_t) + slope_row * ce_rel, NEG)
        m = jnp.max(s, axis=0, keepdims=True)
        p = jnp.exp2(s - m)
        l = jnp.sum(p, axis=0, keepdims=True)
        p = p * jnp.where(t_row >= CMP_BLOCK - 1, 1.0 / l, 0.0)
        br_sc[0] = _dot(vct_ref[0, 0, :, 0:nk], p.astype(BF16))
        ps4 = per_head(p)
        psum = (ps4[0] + ps4[1]) + (ps4[2] + ps4[3])
        wimp = wimp_ref[0:nblk, 0:nk]
        ph, pm, plo = _split3(psum)
        imp = _dot(wimp, ph) + _dot(wimp, pm) + _dot(wimp, plo)

        reset_stats()
        n_back = WINDOW // tk
        qw = qw_sc[...]
        tiles = [jnp.maximum(qi - w, 0) for w in range(n_back + 1)]
        logits = [_dot(kw_sc[kv], qw) for kv in tiles]
        for w, (kv, s) in enumerate(zip(tiles, logits)):
            if w == 0:
                s = jnp.where((kv * tk + j_loc) <= t_row, s, NEG)
            elif w == n_back:
                s = jnp.where((t_row - (kv * tk + j_loc)) < WINDOW, s, NEG)
            _softmax_step(s, vt_sc[kv, dh:2 * dh, :], shift_of(kv, qi < w), m_sc, l_sc, acc_sc)
        br_sc[1] = acc_sc[...] * (1.0 / l_sc[...])

        s0 = _dot(ks_sc[0, :, 0:QFEAT], q2_sc[0:QFEAT, :])

        sel = _select_blocks(imp, t_one, n_sel, three_forced)
        penalty_f = (sel - 1.0) * SEL_MASK
        q2_sc[QFEAT:QFEAT + nblk, :] = jnp.concatenate([penalty_f.astype(BF16)] * NSA_HPG, axis=1)
        if nblk < nbp:
            q2_sc[QFEAT + nblk:KSEL_FEAT, :] = jnp.full((nbp - nblk, hq), -SEL_MASK, BF16)
        pen0 = jnp.concatenate([penalty_f[0:per_tile]] * NSA_HPG, axis=1)
        pen0 = jnp.concatenate([jnp.broadcast_to(pen0[b:b + 1], (SLC_BLOCK, hq)) for b in range(per_tile)], axis=0)
        s0 = s0 + pen0
        sa_sc[0:tk, :] = s0
        sa_sc[tk:tk + 1, :] = jnp.max(s0, axis=0, keepdims=True)

        any_tok = jnp.max(sel, axis=1, keepdims=True)
        blk_col = lax.broadcasted_iota(jnp.int32, any_tok.shape, 0)
        bit = jnp.left_shift(1, jnp.bitwise_and(blk_col, BITS_PER_WORD - 1)).astype(F32)
        bits = jnp.where(any_tok > 0.5, bit, 0.0)
        for w in range(n_words):
            lo, hi = w * BITS_PER_WORD, min((w + 1) * BITS_PER_WORD, nblk)
            words_sc[w] = jnp.sum(bits[lo:hi]).astype(jnp.int32) if lo < nblk else jnp.int32(0)

    n_var = 4
    span = nt // n_var
    ncp = kc_ref.shape[2]
    first_three = -(-2 * SLC_BLOCK // tq)
    assert first_three <= span and n_sel >= 3
    for v in range(1, n_var + 1):
        lo = (v - 1) * span
        if lo < first_three:
            @pl.when(qi < first_three)
            def _(v=v):
                front(ncp * v // n_var, (nt * per_tile) * v // n_var, False)
            lo = first_three

        @pl.when((qi >= lo) & (qi < v * span))
        def _(v=v):
            front(ncp * v // n_var, (nt * per_tile) * v // n_var, True)

    words = [words_sc[w] for w in range(n_words)]
    for kt in range(nt + 2):
        list_sc[kt] = qi
    n_act = jnp.int32(0)
    tile_mask = (1 << per_tile) - 1
    for kt in range(nt):
        list_sc[n_act] = kt
        w, sh = divmod(kt * per_tile, BITS_PER_WORD)
        hit = jnp.bitwise_and(words[w] >> sh, tile_mask) != 0
        n_act = n_act + hit.astype(jnp.int32)
    list_sc[n_act] = qi
    n_plain = n_act - 1

    reset_stats()

    def sel_logits(idx, s_ref):
        s = _dot(ks_sc[list_sc[idx]], q2_sc[...])
        s_ref[0:tk, :] = s
        s_ref[tk:tk + 1, :] = jnp.max(s, axis=0, keepdims=True)

    def sel_update(idx, s_ref):
        kv = list_sc[idx]
        _softmax_step(s_ref[0:tk, :], vt_sc[kv, 0:dh, :], shift_of(kv, idx >= n_plain), m_sc, l_sc, acc_sc,
                      s_max=s_ref[tk:tk + 1, :])

    def pair_body(i, carry):
        sel_logits(2 * i + 1, sb_sc)
        sel_update(2 * i, sa_sc)
        sel_logits(2 * i + 2, sa_sc)
        sel_update(2 * i + 1, sb_sc)
        return carry

    lax.fori_loop(0, lax.shift_right_logical(n_plain + 1, 1), pair_body, 0)
    causal = (qi * tk + j_loc) <= t_row
    _softmax_step(jnp.where(causal, sa_sc[0:tk, :], NEG), vt_sc[qi, 0:dh, :], shift_of(qi), m_sc, l_sc, acc_sc)
    o_sel = acc_sc[...] * (1.0 / l_sc[...])

    sg = _sigmoid(gt_ref[0, 0])
    gates = [jnp.concatenate([sg[3 * hh + br:3 * hh + br + 1] for hh in range(NSA_HPG)], axis=1)
             for br in range(3)]
    o_all = gates[0] * br_sc[0] + gates[1] * o_sel + gates[2] * br_sc[1]
    o_ref[...] = jnp.concatenate(per_head(o_all), axis=0).T.astype(o_ref.dtype)


def _nsa(slopes, pm, kc, vct, gt, wimp, *, B, T, nc, n_sel):
    b, g, hpg, dh = B, NSA_KV_GROUPS, NSA_HPG, NSA_HEAD_DIM
    tq = NSA_TILE
    nt = T // tq
    hq = hpg * tq
    ncp = kc.shape[2]
    nbp = wimp.shape[0]
    qb = _NEW["nsa_q"] // (hpg * dh)
    kb, vb = _NEW["sel_win_k"] // LANE, _NEW["sel_win_v"] // LANE
    kern = functools.partial(_nsa_kernel, tq=tq, nt=nt, nc=nc, n_sel=n_sel)
    return pl.pallas_call(
        kern,
        grid=(b, g, nt),
        in_specs=[pl.BlockSpec(memory_space=pltpu.SMEM),
                  pl.BlockSpec((T, hpg * dh), lambda i, j, k: (i, qb + j)),
                  pl.BlockSpec((T, LANE), lambda i, j, k: (i, kb + j)),
                  pl.BlockSpec((T, LANE), lambda i, j, k: (i, vb + j)),
                  pl.BlockSpec((1, 1, ncp, dh), lambda i, j, k: (i, j, 0, 0)),
                  pl.BlockSpec((1, 1, dh, ncp), lambda i, j, k: (i, j, 0, 0)),
                  pl.BlockSpec((1, 1, 3 * hpg, tq), lambda i, j, k: (i, j, 0, k)),
                  pl.BlockSpec((nbp, ncp), lambda i, j, k: (0, 0))],
        out_specs=pl.BlockSpec((tq, hpg * dh), lambda i, j, k: (i * nt + k, j)),
        out_shape=jax.ShapeDtypeStruct((b * T, g * hpg * dh), BF16),
        scratch_shapes=[pltpu.VMEM((KSEL_FEAT, hq), BF16), pltpu.VMEM((QFEAT, hq), BF16),
                        pltpu.VMEM((nt, tq, KSEL_FEAT), BF16), pltpu.VMEM((nt, tq, QFEAT), BF16),
                        pltpu.VMEM((nt, dh, hq), BF16), pltpu.VMEM((nt, 2 * dh, tq), BF16),
                        pltpu.VMEM((tq + 8, hq), F32), pltpu.VMEM((tq + 8, hq), F32),
                        pltpu.VMEM((1, hq), F32), pltpu.VMEM((1, hq), F32),
                        pltpu.VMEM((dh, hq), F32),
                        pltpu.VMEM((2, dh, hq), F32),
                        pltpu.SMEM((nt + 2,), jnp.int32),
                        pltpu.SMEM((-(-nt * (tq // SLC_BLOCK) // BITS_PER_WORD),), jnp.int32)],
        compiler_params=pltpu.CompilerParams(dimension_semantics=("parallel", "parallel", "arbitrary"),
                                             vmem_limit_bytes=48 * 1024 * 1024),
        name="nsa",
    )(slopes, pm, pm, pm, kc, vct, gt, wimp)


def _gla_intra_scores(q, k, bcum):
    c, dk = q.shape
    sb = GLA_SUB
    row_c = lax.broadcasted_iota(jnp.int32, (c, dk), 0)
    row_s = lax.broadcasted_iota(jnp.int32, (sb, 1), 0)
    lane_s = lax.broadcasted_iota(jnp.int32, (sb, c), 1)
    blocks = []
    for i0 in range(0, c, sb):
        bi = bcum[i0:i0 + sb]
        qi = q[i0:i0 + sb]
        a_blk = jnp.zeros((sb, c), F32)
        for jl in range(sb):
            j = i0 + jl
            w = jnp.exp2(bi - bcum[j:j + 1])
            a = jnp.sum(qi * k[j:j + 1] * w, axis=-1, keepdims=True)
            a_blk = jnp.where(lane_s == j, jnp.where(row_s >= jl, a, 0.0), a_blk)
        if i0 > 0:
            r = bcum[i0 - 1:i0]
            qh = qi * jnp.exp2(bi - r)
            kh = k * jnp.exp2(jnp.where(row_c < i0, r - bcum, NEG))
            a_blk = a_blk + _dot_nt(qh.astype(BF16), kh.astype(BF16))
        blocks.append(a_blk)
    return jnp.concatenate(blocks, axis=0)


def _gla_kernel(q_ref, k_ref, v_ref, z_ref, ps_ref, wal_ref, bal_ref, gain_ref, tri_ref, o_ref,
                st_sc, *, ts):
    ti = pl.program_id(2)
    c = GLA_CHUNK
    dk, dv = GLA_KEY_DIM, GLA_VAL_DIM

    @pl.when(ti == 0)
    def _():
        st_sc[...] = jnp.zeros(st_sc.shape, F32)

    ps = ps_ref[...]
    ph, pm, _ = _split3(ps)
    wh, wm, _ = _split3(wal_ref[...])
    zz = _dot(ph, wh) + _dot(ph, wm) + _dot(pm, wh) + bal_ref[...]
    log_a = (jnp.minimum(zz, 0.0) - jnp.log(1.0 + jnp.exp(-jnp.abs(zz)))) * (LOG2E / GLA_TAU)
    tri = tri_ref[...]
    qscale = GLA_KEY_DIM ** -0.5
    for ci in range(ts // c):
        sl = slice(ci * c, (ci + 1) * c)
        for hh in range(GLA_PAIR):
            ks, vs = slice(hh * dk, (hh + 1) * dk), slice(hh * dv, (hh + 1) * dv)
            gh, gm, gl = _split3(log_a[sl, ks])
            bcum = _dot(tri, gh) + _dot(tri, gm) + _dot(tri, gl)
            q = q_ref[sl, ks].astype(F32) * qscale
            k = k_ref[sl, ks].astype(F32)
            v = v_ref[sl, vs]
            a_mat = _gla_intra_scores(q, k, bcum)
            st = st_sc[hh]
            o = _dot(a_mat.astype(BF16), v) + _dot_nt((q * jnp.exp2(bcum)).astype(BF16), st.astype(BF16))
            b_last = bcum[c - 1:c, :]
            kd = k * jnp.exp2(b_last - bcum)
            st_sc[hh] = st * jnp.exp2(b_last) + _dot(v.astype(F32).T.astype(BF16), kd.astype(BF16))
            ms = jnp.mean(o * o, axis=-1, keepdims=True)
            y = o * lax.rsqrt(ms + EPS) * gain_ref[...]
            z = z_ref[sl, vs].astype(F32)
            o_ref[sl, vs] = (y * (z * _sigmoid(z))).astype(o_ref.dtype)


def _gla(pm, ps, wal, bal, gain, tri, *, B, T, ts=1024):
    n = pm.shape[0]
    nts = T // ts
    hp = GLA_PAIR
    dk, dv = hp * GLA_KEY_DIM, hp * GLA_VAL_DIM
    qb, kb = _NEW["gq"] // dk, _NEW["gk"] // dk
    vb, zb = _NEW["gv"] // dv, _NEW["gla_z"] // dv
    kern = functools.partial(_gla_kernel, ts=ts)
    return pl.pallas_call(
        kern,
        grid=(B, GLA_HEADS // hp, nts),
        in_specs=[pl.BlockSpec((ts, dk), lambda b, h, i: (b * nts + i, qb + h)),
                  pl.BlockSpec((ts, dk), lambda b, h, i: (b * nts + i, kb + h)),
                  pl.BlockSpec((ts, dv), lambda b, h, i: (b * nts + i, vb + h)),
                  pl.BlockSpec((ts, dv), lambda b, h, i: (b * nts + i, zb + h)),
                  pl.BlockSpec((ts, SMALL_WIDTH), lambda b, h, i: (b * nts + i, 0)),
                  pl.BlockSpec((SMALL_WIDTH, dk), lambda b, h, i: (0, h)),
                  pl.BlockSpec((1, dk), lambda b, h, i: (0, h)),
                  pl.BlockSpec((1, GLA_VAL_DIM), lambda b, h, i: (0, 0)),
                  pl.BlockSpec((GLA_CHUNK, GLA_CHUNK), lambda b, h, i: (0, 0))],
        out_specs=pl.BlockSpec((ts, dv), lambda b, h, i: (b * nts + i, h)),
        out_shape=jax.ShapeDtypeStruct((n, GLA_WIDTH), BF16),
        scratch_shapes=[pltpu.VMEM((hp, GLA_VAL_DIM, GLA_KEY_DIM), F32)],
        compiler_params=pltpu.CompilerParams(dimension_semantics=("parallel", "parallel", "arbitrary"),
                                             vmem_limit_bytes=40 * 1024 * 1024),
        name="gla",
    )(pm, pm, pm, pm, ps, wal, bal, gain, tri)


def _out_proj_kernel(x_ref, gate_ref, on_ref, nz_ref, mgn_ref, mgg_ref, yg_ref,
                     wn_ref, wg_ref, wo_ref, fg_ref, o_ref):
    nz = nz_ref[...].astype(F32)
    y_nsa = on_ref[...].astype(F32) * (nz * _sigmoid(nz))
    a = _dot(y_nsa.astype(BF16), wn_ref[...])
    bm = _dot(yg_ref[...], wg_ref[...])
    merged = _sigmoid(mgn_ref[...].astype(F32)) * a + _sigmoid(mgg_ref[...].astype(F32)) * bm
    xn = x_ref[...] + gate_ref[0] * _dot(merged.astype(BF16), wo_ref[...])
    ms = jnp.mean(xn * xn, axis=-1, keepdims=True)
    o_ref[...] = xn * lax.rsqrt(ms + EPS) * fg_ref[...]


def _out_proj(x2, gate, o_nsa, pm, yg, wn, wg, wo, fg, *, T, tm=256):
    n, d = x2.shape
    nt = T // tm
    nzb = _NEW["nsa_z"] // NSA_WIDTH
    mnb = _NEW["mg_nsa"] // D_MODEL
    mgb = _NEW["mg_gla"] // D_MODEL
    row = lambda i: (i, 0)
    const2 = lambda i: (0, 0)
    return pl.pallas_call(
        _out_proj_kernel,
        grid=(n // tm,),
        in_specs=[pl.BlockSpec((tm, d), row),
                  pl.BlockSpec((1, 1, d), lambda i: (i // nt, 0, 0)),
                  pl.BlockSpec((tm, NSA_WIDTH), row),
                  pl.BlockSpec((tm, NSA_WIDTH), lambda i: (i, nzb)),
                  pl.BlockSpec((tm, D_MODEL), lambda i: (i, mnb)),
                  pl.BlockSpec((tm, D_MODEL), lambda i: (i, mgb)),
                  pl.BlockSpec((tm, GLA_WIDTH), row),
                  pl.BlockSpec((NSA_WIDTH, d), const2),
                  pl.BlockSpec((GLA_WIDTH, d), const2),
                  pl.BlockSpec((d, d), const2),
                  pl.BlockSpec((1, d), const2)],
        out_specs=pl.BlockSpec((tm, d), row),
        out_shape=jax.ShapeDtypeStruct((n, d), F32),
        compiler_params=pltpu.CompilerParams(dimension_semantics=("parallel",),
                                             vmem_limit_bytes=60 * 1024 * 1024),
        name="out_proj",
    )(x2, gate, o_nsa, pm, pm, pm, yg, wn, wg, wo, fg)


def _importance_matrix(nbp, ncp):
    ratio = SLC_BLOCK // CMP_STRIDE
    n_sub = CMP_BLOCK // CMP_STRIDE
    w = np.zeros((nbp, ncp), np.float32)
    for blk in range(nbp):
        for m in range(ratio):
            for s in range(n_sub):
                n = ratio * blk + m - s
                if 0 <= n < ncp:
                    w[blk, n] += 1.0
    return w


def _main_cols(a):
    parts = [a[..., start:start + width] * (Q_SCALE if nm == "nsa_q" else 1.0)
             for nm, pieces in _MAIN_PIECES for start, width in pieces]
    return jnp.concatenate(parts, axis=-1)


def _small_cols(a):
    pad = SMALL_WIDTH - 3 * NSA_HEADS - GLA_GATE_RANK
    parts = [a[..., _OLD["nsa_g"][0]:_OLD["nsa_g"][0] + 3 * NSA_HEADS],
             a[..., _OLD["ga"][0]:_OLD["ga"][0] + GLA_GATE_RANK],
             jnp.zeros(a.shape[:-1] + (pad,), a.dtype)]
    return jnp.concatenate(parts, axis=-1)


def kernel(x, c, w_ada, b_ada, norm_gain, w_in, b_in, cmp_pos_k, cmp_pos_v, cmp_w1_k, cmp_w2_k, cmp_w1_v, cmp_w2_v,
           gla_w_alpha, gla_b_alpha, gla_norm_gain, w_br_nsa, w_br_gla, w_out, final_norm_gain):
    assert DEPTH == 1, "the final rmsnorm is fused into the single layer's output kernel"
    B, T, D = x.shape
    G, HPG, DH = NSA_KV_GROUPS, NSA_HPG, NSA_HEAD_DIM
    N = B * T
    nch = T // CMP_STRIDE
    nc = nch - CMP_BLOCK // CMP_STRIDE + 1
    nb = T // SLC_BLOCK
    nbp = KSEL_FEAT - QFEAT
    assert nb <= nbp and T % NSA_TILE == 0
    n_sel = min(N_SELECT, nb)
    tk = NSA_TILE
    l = 0

    slopes = 2.0 ** (-8.0 * jnp.arange(1, NSA_HEADS + 1, dtype=F32) / NSA_HEADS)
    wimp = jnp.asarray(_importance_matrix(nbp, nch), BF16)
    tri = jnp.asarray(np.tril(np.ones((GLA_CHUNK, GLA_CHUNK), np.float32)), BF16)
    bp = ((B + 7) // 8) * 8
    c8 = jnp.pad(c, ((0, bp - B), (0, 0)))
    x2 = x.reshape(N, D)

    mod = _ada(c8, w_ada[l], b_ada[l][None, :])
    shift = mod[:B, :D].reshape(B, 1, D)
    scale = mod[:B, D:2 * D].reshape(B, 1, D)
    gate = mod[:B, 2 * D:].reshape(B, 1, D)

    w_in_t = jnp.transpose(w_in[l])
    pm, ps = _in_proj(x2, norm_gain[l][None, :], shift, scale,
                      _wprep(w_in_t), _main_cols(b_in[l])[None, :],
                      _wprep_small(w_in_t), _small_cols(b_in[l])[None, :], T=T)

    def cols(name, width=NSA_KV_WIDTH):
        return pm[:, _NEW[name]:_NEW[name] + width]

    kvf = cols("ck", 2 * NSA_KV_WIDTH).reshape(B, nch, CMP_STRIDE, 2, G, DH)
    kvf = kvf.transpose(3, 0, 4, 1, 2, 5).reshape(2, B, G, nch, CMP_STRIDE * DH)
    posf = jnp.stack([cmp_pos_k[l], cmp_pos_v[l]]).reshape(2, 1, CMP_BLOCK * DH)
    posf = jnp.broadcast_to(posf, (2, 8, CMP_BLOCK * DH)).astype(BF16)
    w1 = jnp.stack([cmp_w1_k[l], cmp_w1_v[l]]).astype(BF16)
    w2 = jnp.stack([cmp_w2_k[l], cmp_w2_v[l]]).astype(BF16)
    kvc = _compress(kvf, posf, w1, w2)
    gt = ps[:, :3 * NSA_HEADS].reshape(B, T, G, 3 * HPG).transpose(0, 2, 3, 1)
    slopes2 = slopes * LOG2E
    o_nsa = _nsa(slopes2, pm, kvc[0], kvc[1].transpose(0, 1, 3, 2), gt, wimp, B=B, T=T, nc=nc, n_sel=n_sel)

    wal = jnp.pad(gla_w_alpha[l], ((GA_LANE, SMALL_WIDTH - GA_LANE - GLA_GATE_RANK), (0, 0)))
    y_gla = _gla(pm, ps, wal, gla_b_alpha[l][None, :], gla_norm_gain[l][None, :], tri, B=B, T=T)

    out = _out_proj(x2, gate, o_nsa, pm, y_gla,
                    w_br_nsa[l].astype(BF16), w_br_gla[l].astype(BF16), w_out[l].astype(BF16),
                    final_norm_gain[None, :], T=T)
    return out.reshape(B, T, D)
```

```python
import functools

import numpy as np
import jax
import jax.numpy as jnp
from jax import lax
from jax.experimental import pallas as pl
from jax.experimental.pallas import tpu as pltpu

D_MODEL = 2048
DEPTH = 1
NSA_HEADS = 16
NSA_KV_GROUPS = 4
NSA_HPG = NSA_HEADS // NSA_KV_GROUPS
NSA_HEAD_DIM = 64
CMP_BLOCK = 32
CMP_STRIDE = 16
CMP_HIDDEN = 256
SLC_BLOCK = 64
N_SELECT = 16
WINDOW = 512
NSA_WIDTH = NSA_HEADS * NSA_HEAD_DIM
NSA_KV_WIDTH = NSA_KV_GROUPS * NSA_HEAD_DIM
GLA_HEADS = 4
GLA_KEY_DIM = 128
GLA_VAL_DIM = 256
GLA_GATE_RANK = 16
GLA_TAU = 16.0
GLA_KEY_WIDTH = GLA_HEADS * GLA_KEY_DIM
GLA_WIDTH = GLA_HEADS * GLA_VAL_DIM
EPS = 1e-6
NEG = -1e30

F32 = jnp.float32
BF16 = jnp.bfloat16

_OLD = {}
_off = 0
for _name, _w in (("nsa_q", NSA_WIDTH), ("ck", NSA_KV_WIDTH), ("cv", NSA_KV_WIDTH), ("sk", NSA_KV_WIDTH),
                  ("sv", NSA_KV_WIDTH), ("wk", NSA_KV_WIDTH), ("wv", NSA_KV_WIDTH), ("nsa_g", 3 * NSA_HEADS),
                  ("nsa_z", NSA_WIDTH), ("gq", GLA_KEY_WIDTH), ("gk", GLA_KEY_WIDTH), ("gv", GLA_WIDTH),
                  ("ga", GLA_GATE_RANK), ("gla_z", GLA_WIDTH), ("mg_nsa", D_MODEL), ("mg_gla", D_MODEL)):
    _OLD[_name] = (_off, _w)
    _off += _w

def _pair_by_group(a, b):
    return [(_OLD[nm][0] + g * NSA_HEAD_DIM, NSA_HEAD_DIM) for g in range(NSA_KV_GROUPS) for nm in (a, b)]


_MAIN_PIECES = [(nm, [_OLD[nm]]) for nm in
                ("nsa_q", "nsa_z", "gla_z", "gv", "mg_nsa", "mg_gla", "gq", "gk", "ck", "cv")]
_MAIN_PIECES += [("sel_win_k", _pair_by_group("sk", "wk")), ("sel_win_v", _pair_by_group("sv", "wv"))]
_NEW = {}
_off = 0
for _name, _pieces in _MAIN_PIECES:
    _NEW[_name] = _off
    _off += sum(w for _, w in _pieces)
MAIN_WIDTH = _off
SMALL_WIDTH = 128
GA_LANE = 3 * NSA_HEADS

LANE = 128
GLA_CHUNK = 64
GLA_SUB = 16
GLA_PAIR = 2
NSA_TILE = 256
QFEAT = 2 * NSA_HEAD_DIM
KSEL_FEAT = QFEAT + LANE
N_SLOPE_TERMS = 3
SOFTMAX_STRIP = NSA_TILE
SEL_MASK = 16384.0
BITS_PER_WORD = 16
LOG2E = 1.4426950408889634
Q_SCALE = LOG2E * NSA_HEAD_DIM ** -0.5


def _dot(a, b):
    return jnp.dot(a, b, preferred_element_type=F32)


def _dot_nt(a, b):
    return lax.dot_general(a, b, (((1,), (1,)), ((), ())), preferred_element_type=F32)


def _split3(x):
    hi = x.astype(BF16)
    r = x - hi.astype(F32)
    mid = r.astype(BF16)
    lo = (r - mid.astype(F32)).astype(BF16)
    return hi, mid, lo


def _sigmoid(x):
    return 1.0 / (1.0 + jnp.exp(-x))


def _ada_kernel(c_ref, w_ref, b_ref, o_ref):
    ch, cm, cl = _split3(c_ref[...])
    wh, wm, wl = _split3(w_ref[...])
    acc = _dot(ch, wh) + _dot(ch, wm) + _dot(cm, wh) + _dot(ch, wl) + _dot(cl, wh) + _dot(cm, wm)
    o_ref[...] = acc + b_ref[...]


def _ada(c8, w, b, *, tn=768):
    m, d = c8.shape
    n = w.shape[1]
    return pl.pallas_call(
        _ada_kernel,
        grid=(n // tn,),
        in_specs=[pl.BlockSpec((m, d), lambda j: (0, 0)),
                  pl.BlockSpec((d, tn), lambda j: (0, j)),
                  pl.BlockSpec((1, tn), lambda j: (0, j))],
        out_specs=pl.BlockSpec((m, tn), lambda j: (0, j)),
        out_shape=jax.ShapeDtypeStruct((m, n), F32),
        compiler_params=pltpu.CompilerParams(dimension_semantics=("arbitrary",),
                                             vmem_limit_bytes=40 * 1024 * 1024),
        name="ada",
    )(c8, w, b)


WPREP_TILE = 512


def _wprep_tiles():
    tiles = []
    for name, pieces in _MAIN_PIECES:
        scale = Q_SCALE if name == "nsa_q" else 1.0
        if len(pieces) == 1:
            start, width = pieces[0]
            assert width % WPREP_TILE == 0 or (name in ("ck", "cv") and width * 2 == WPREP_TILE)
            if name == "cv":
                assert start == _OLD["ck"][0] + _OLD["ck"][1]
                continue
            for off in range(0, max(width, WPREP_TILE), WPREP_TILE):
                tiles.append((start + off, (start + off) % LANE, scale))
        else:
            first, second = pieces[0][0], pieces[1][0]
            assert second == first + WPREP_TILE and first % LANE == 0 and len(pieces) * NSA_HEAD_DIM == WPREP_TILE
            tiles.append((first, -1, scale))
    return tiles


def _wprep_kernel(start_ref, kind_ref, scale_ref, a_ref, b_ref, o_ref):
    j = pl.program_id(0)
    scale = scale_ref[j]

    @pl.when(kind_ref[j] >= 0)
    def _():
        o_ref[...] = (a_ref[...] * scale).astype(o_ref.dtype)

    @pl.when(kind_ref[j] < 0)
    def _():
        half = NSA_HEAD_DIM
        for g in range(NSA_KV_GROUPS):
            o_ref[2 * g * half:(2 * g + 1) * half, :] = (a_ref[g * half:(g + 1) * half, :] * scale).astype(o_ref.dtype)
            o_ref[(2 * g + 1) * half:(2 * g + 2) * half, :] = (b_ref[g * half:(g + 1) * half, :] * scale).astype(o_ref.dtype)


def _wprep(wt):
    width, d = wt.shape
    tiles = _wprep_tiles()
    half_tile = WPREP_TILE // 2
    assert len(tiles) * WPREP_TILE == MAIN_WIDTH and all(t[0] % 8 == 0 for t in tiles)
    start = jnp.asarray([t[0] for t in tiles], jnp.int32)
    kind = jnp.asarray([t[1] for t in tiles], jnp.int32)
    scale = jnp.asarray([t[2] for t in tiles], F32)
    return pl.pallas_call(
        _wprep_kernel,
        grid_spec=pltpu.PrefetchScalarGridSpec(
            num_scalar_prefetch=2,
            grid=(len(tiles),),
            in_specs=[pl.BlockSpec(memory_space=pltpu.SMEM),
                      pl.BlockSpec((pl.Element(WPREP_TILE), pl.Element(d)),
                                   lambda j, st, kd: (pl.multiple_of(st[j], 8), 0)),
                      pl.BlockSpec((pl.Element(half_tile), pl.Element(d)),
                                   lambda j, st, kd: (pl.multiple_of(
                                       jnp.minimum(st[j] + WPREP_TILE, width - half_tile), 8), 0))],
            out_specs=pl.BlockSpec((WPREP_TILE, d), lambda j, st, kd: (j, 0))),
        out_shape=jax.ShapeDtypeStruct((MAIN_WIDTH, d), BF16),
        compiler_params=pltpu.CompilerParams(dimension_semantics=("arbitrary",),
                                             vmem_limit_bytes=40 * 1024 * 1024),
        name="wprep",
    )(start, kind, scale, wt, wt)


def _wprep_small_kernel(g_ref, a_ref, o_ref):
    n_gate = 3 * NSA_HEADS
    rows = jnp.concatenate([g_ref[0:n_gate, :], a_ref[...],
                            jnp.zeros((SMALL_WIDTH - n_gate - GLA_GATE_RANK, g_ref.shape[1]), F32)], axis=0)
    o_ref[...] = rows.astype(o_ref.dtype)


def _wprep_small(wt):
    d = wt.shape[1]
    g0, a0 = _OLD["nsa_g"][0], _OLD["ga"][0]
    assert g0 % 8 == 0 and a0 % 8 == 0
    return pl.pallas_call(
        _wprep_small_kernel,
        grid=(1,),
        in_specs=[pl.BlockSpec((pl.Element(NSA_HEAD_DIM), pl.Element(d)), lambda i: (g0, 0)),
                  pl.BlockSpec((pl.Element(GLA_GATE_RANK), pl.Element(d)), lambda i: (a0, 0))],
        out_specs=pl.BlockSpec((SMALL_WIDTH, d), lambda i: (0, 0)),
        out_shape=jax.ShapeDtypeStruct((SMALL_WIDTH, d), BF16),
        name="wprep_small",
    )(wt, wt)


def _in_proj_kernel(x_ref, gain_ref, shift_ref, scale_ref, w_ref, b_ref, ws_ref, bs_ref,
                    o_ref, os_ref, h_ref):
    j = pl.program_id(1)

    @pl.when(j == 0)
    def _():
        x = x_ref[...]
        ms = jnp.mean(x * x, axis=-1, keepdims=True)
        y = x * lax.rsqrt(ms + EPS) * gain_ref[...]
        y = y * (1.0 + scale_ref[0]) + shift_ref[0]
        h = y.astype(BF16)
        h_ref[...] = h
        os_ref[...] = _dot_nt(h, ws_ref[...]) + bs_ref[...]

    o_ref[...] = (_dot_nt(h_ref[...], w_ref[...]) + b_ref[...]).astype(o_ref.dtype)


def _in_proj(x2, gain, shift, scale, w_main, b_main, w_small, b_small, *, T, tm=1024, tn=1536):
    n, d = x2.shape
    nm = w_main.shape[0]
    nt = T // tm
    return pl.pallas_call(
        _in_proj_kernel,
        grid=(n // tm, nm // tn),
        in_specs=[pl.BlockSpec((tm, d), lambda i, j: (i, 0)),
                  pl.BlockSpec((1, d), lambda i, j: (0, 0)),
                  pl.BlockSpec((1, 1, d), lambda i, j: (i // nt, 0, 0)),
                  pl.BlockSpec((1, 1, d), lambda i, j: (i // nt, 0, 0)),
                  pl.BlockSpec((tn, d), lambda i, j: (j, 0)),
                  pl.BlockSpec((1, tn), lambda i, j: (0, j)),
                  pl.BlockSpec((SMALL_WIDTH, d), lambda i, j: (0, 0)),
                  pl.BlockSpec((1, SMALL_WIDTH), lambda i, j: (0, 0))],
        out_specs=[pl.BlockSpec((tm, tn), lambda i, j: (i, j)),
                   pl.BlockSpec((tm, SMALL_WIDTH), lambda i, j: (i, 0))],
        out_shape=[jax.ShapeDtypeStruct((n, nm), BF16),
                   jax.ShapeDtypeStruct((n, SMALL_WIDTH), F32)],
        scratch_shapes=[pltpu.VMEM((tm, d), BF16)],
        compiler_params=pltpu.CompilerParams(dimension_semantics=("parallel", "arbitrary"),
                                             vmem_limit_bytes=56 * 1024 * 1024),
        name="in_proj",
    )(x2, gain, shift, scale, w_main, b_main, w_small, b_small)


def _compress_kernel(kv_ref, pos_ref, w1_ref, w2_ref, o_ref):
    y = kv_ref[0, 0, 0]
    w1 = w1_ref[0]
    half = y.shape[1]
    nch = y.shape[0]
    z1 = _dot(y, w1[:half])
    z2 = _dot(y, w1[half:])
    posb = _dot(pos_ref[0], w1)[0:1]
    pre = z1 + pltpu.roll(z2, nch - 1, 0) + posb
    hid = pre * _sigmoid(pre)
    o_ref[0, 0, 0] = _dot(hid.astype(BF16), w2_ref[0]).astype(o_ref.dtype)


def _compress(kvf, posf, w1, w2):
    two, b, g, nch, half = kvf.shape
    hidden = w1.shape[2]
    dh = w2.shape[2]
    return pl.pallas_call(
        _compress_kernel,
        grid=(two, b, g),
        in_specs=[pl.BlockSpec((1, 1, 1, nch, half), lambda s, i, j: (s, i, j, 0, 0)),
                  pl.BlockSpec((1, 8, 2 * half), lambda s, i, j: (s, 0, 0)),
                  pl.BlockSpec((1, 2 * half, hidden), lambda s, i, j: (s, 0, 0)),
                  pl.BlockSpec((1, hidden, dh), lambda s, i, j: (s, 0, 0))],
        out_specs=pl.BlockSpec((1, 1, 1, nch, dh), lambda s, i, j: (s, i, j, 0, 0)),
        out_shape=jax.ShapeDtypeStruct((two, b, g, nch, dh), BF16),
        compiler_params=pltpu.CompilerParams(dimension_semantics=("parallel", "parallel", "parallel")),
        name="compress",
    )(kvf, posf, w1, w2)


def _select_blocks(imp, t_row, n_sel, three_forced):
    nbp = imp.shape[0]
    blk = lax.broadcasted_iota(jnp.int32, (nbp, 1), 0)
    cur = lax.shift_right_logical(t_row, 6)
    forced = (blk == 0) | (blk == cur) | (blk == cur - 1)
    bvalid = blk * SLC_BLOCK <= t_row
    rest = jnp.where(bvalid, imp, -1.0)
    blk_f = blk.astype(F32)
    if three_forced:
        score = jnp.where(forced, -1.0, rest)
        sel = jnp.where(forced, 1.0, 0.0)
        rounds = n_sel - 3
    else:
        score = jnp.where(forced, 1e30, rest)
        sel = jnp.zeros(imp.shape, F32)
        rounds = n_sel
    for _ in range(rounds):
        m = jnp.max(score, axis=0, keepdims=True)
        first = jnp.min(jnp.where(score == m, blk_f, float(nbp)), axis=0, keepdims=True)
        hit = blk_f == first
        sel = jnp.where(hit, 1.0, sel)
        score = jnp.where(hit, -1.0, score)
    return sel


def _softmax_step(s_of, vt, d, m_sc, l_sc, acc_sc, s_max=None):
    for c0 in range(0, d.shape[1], SOFTMAX_STRIP):
        ls = slice(c0, c0 + SOFTMAX_STRIP)
        s = s_of(ls)
        mx = jnp.max(s, axis=0, keepdims=True) if s_max is None else s_max[:, ls]
        dd = d[:, ls]
        m_prev = m_sc[:, ls]
        m_new = jnp.maximum(m_prev, mx - dd)
        alpha = jnp.exp2(m_prev - m_new)
        p = jnp.exp2(s - (m_new + dd))
        l_sc[:, ls] = alpha * l_sc[:, ls] + jnp.sum(p, axis=0, keepdims=True)
        acc_sc[:, ls] = alpha * acc_sc[:, ls] + _dot(vt, p.astype(BF16))
        m_sc[:, ls] = m_new


def _stage_group(qslab_ref, kslab_ref, vslab_ref, qt_sc, ks_sc, kw_sc, vt_sc, nt, tk):
    dh = NSA_HEAD_DIM
    lane = lax.broadcasted_iota(jnp.int32, (tk, LANE), 1)
    row = lax.broadcasted_iota(jnp.int32, (tk, LANE), 0)
    pos = row.astype(F32)
    feat_sel = jnp.where((lane >= dh) & (lane < dh + N_SLOPE_TERMS), pos, 0.0)
    feat_win = jnp.where(lane < N_SLOPE_TERMS, pos, 0.0)
    blk_in_tile = lax.shift_right_logical(row, 6)

    def body(kt, carry):
        start = pl.multiple_of(kt * tk, tk)
        kk = kslab_ref[pl.ds(start, tk), :].astype(F32)
        onehot = jnp.where(kt * (tk // SLC_BLOCK) + blk_in_tile == lane, 1.0, 0.0)
        ks_sc[kt] = jnp.concatenate([jnp.where(lane < dh, kk, feat_sel), onehot], axis=1).astype(BF16)
        kw_sc[kt] = jnp.where(lane >= dh, kk, feat_win).astype(BF16)
        vt_sc[kt] = vslab_ref[pl.ds(start, tk), :].astype(F32).T.astype(BF16)
        x_t = qslab_ref[pl.ds(start, tk), :].astype(F32).T
        qt_sc[kt] = jnp.concatenate([x_t[hh * dh:(hh + 1) * dh] for hh in range(NSA_HPG)], axis=1).astype(BF16)
        return carry

    lax.fori_loop(0, nt, body, 0)


def _nsa_kernel(slopes_ref, q_ref, kslab_ref, vslab_ref, kc_ref, vct_ref, gt_ref, wimp_ref,
                o_ref, q2_sc, qw_sc, ks_sc, kw_sc, qt_sc, vt_sc, sa_sc, sb_sc, m_sc, l_sc, acc_sc, br_sc, list_sc,
                words_sc, *, tq, nt, nc, n_sel):
    g = pl.program_id(1)
    qi = pl.program_id(2)
    tk = tq
    dh = NSA_HEAD_DIM
    hq = NSA_HPG * tq

    @pl.when(qi == 0)
    def _():
        _stage_group(q_ref, kslab_ref, vslab_ref, qt_sc, ks_sc, kw_sc, vt_sc, nt, tk)

    lane = lax.broadcasted_iota(jnp.int32, (1, hq), 1)
    slope_row = jnp.zeros((1, hq), F32)
    for hh in range(NSA_HPG):
        slope_row = jnp.where(lane >= hh * tq, slopes_ref[g * NSA_HPG + hh], slope_row)
    t_one = qi * tq + lax.broadcasted_iota(jnp.int32, (1, tq), 1)
    t_row = jnp.concatenate([t_one] * NSA_HPG, axis=1)
    t_f = t_row.astype(F32)
    j_loc = lax.broadcasted_iota(jnp.int32, (tk, 1), 0)

    q_t = qt_sc[qi]
    feat_row = lax.broadcasted_iota(jnp.int32, (dh, hq), 0)
    terms = [t.astype(F32) for t in _split3(slope_row)]
    feat = jnp.zeros((dh, hq), F32)
    for i, term in enumerate(terms):
        feat = jnp.where(feat_row == i, term, feat)
    feat = feat.astype(BF16)
    q2_sc[0:dh, :] = q_t
    q2_sc[dh:QFEAT, :] = feat
    qw_sc[0:dh, :] = feat
    qw_sc[dh:QFEAT, :] = q_t

    def reset_stats():
        m_sc[...] = jnp.full(m_sc.shape, NEG, F32)
        l_sc[...] = jnp.zeros(l_sc.shape, F32)
        acc_sc[...] = jnp.zeros(acc_sc.shape, F32)

    def per_head(a):
        return [a[:, hh * tq:(hh + 1) * tq] for hh in range(NSA_HPG)]

    def shift_of(kv, off=None):
        d = slope_row * (t_f - (kv * tk).astype(F32))
        return d if off is None else d + jnp.where(off, -NEG, 0.0)

    per_tile = tk // SLC_BLOCK
    nbp = KSEL_FEAT - QFEAT
    n_words = -(-nt * per_tile // BITS_PER_WORD)

    def front(nk, nblk, three_forced):
        kc = kc_ref[0, 0, 0:nk, :]
        n_col = lax.broadcasted_iota(jnp.int32, (nk, 1), 0)
        ce = jnp.where(n_col < nc, n_col * CMP_STRIDE + (CMP_BLOCK - 1), 2 ** 30)
        ce_rel = (ce - qi * tq).astype(F32)
        s = jnp.where(ce <= t_row, _dot(kc, q_t) + slope_row * ce_rel, NEG)
        m = jnp.max(s, axis=0, keepdims=True)
        p = jnp.exp2(s - m)
        l = jnp.sum(p, axis=0, keepdims=True)
        p = p * jnp.where(t_row >= CMP_BLOCK - 1, 1.0 / l, 0.0)
        br_sc[0] = _dot(vct_ref[0, 0, :, 0:nk], p.astype(BF16))
        ps4 = per_head(p)
        psum = (ps4[0] + ps4[1]) + (ps4[2] + ps4[3])
        wimp = wimp_ref[0:nblk, 0:nk]
        ph, pm, plo = _split3(psum)
        imp = _dot(wimp, ph) + _dot(wimp, pm) + _dot(wimp, plo)

        reset_stats()
        n_back = WINDOW // tk
        qw = qw_sc[...]
        tiles = [jnp.maximum(qi - w, 0) for w in range(n_back + 1)]
        logits = [_dot(kw_sc[kv], qw) for kv in tiles]
        for w, (kv, s) in enumerate(zip(tiles, logits)):
            def s_of(ls, w=w, kv=kv, s=s):
                if w == 0:
                    return jnp.where((kv * tk + j_loc) <= t_row[:, ls], s[:, ls], NEG)
                if w == n_back:
                    return jnp.where((t_row[:, ls] - (kv * tk + j_loc)) < WINDOW, s[:, ls], NEG)
                return s[:, ls]
            _softmax_step(s_of, vt_sc[kv, dh:2 * dh, :], shift_of(kv, qi < w), m_sc, l_sc, acc_sc)
        br_sc[1] = acc_sc[...] * (1.0 / l_sc[...])

        s0 = _dot(ks_sc[0, :, 0:QFEAT], q2_sc[0:QFEAT, :])

        sel = _select_blocks(imp, t_one, n_sel, three_forced)
        penalty_f = (sel - 1.0) * SEL_MASK
        q2_sc[QFEAT:QFEAT + nblk, :] = jnp.concatenate([penalty_f.astype(BF16)] * NSA_HPG, axis=1)
        if nblk < nbp:
            q2_sc[QFEAT + nblk:KSEL_FEAT, :] = jnp.full((nbp - nblk, hq), -SEL_MASK, BF16)
        pen0 = jnp.concatenate([penalty_f[0:per_tile]] * NSA_HPG, axis=1)
        pen0 = jnp.concatenate([jnp.broadcast_to(pen0[b:b + 1], (SLC_BLOCK, hq)) for b in range(per_tile)], axis=0)
        s0 = s0 + pen0
        sa_sc[0:tk, :] = s0
        sa_sc[tk:tk + 1, :] = jnp.max(s0, axis=0, keepdims=True)

        any_tok = jnp.max(sel, axis=1, keepdims=True)
        blk_col = lax.broadcasted_iota(jnp.int32, any_tok.shape, 0)
        bit = jnp.left_shift(1, jnp.bitwise_and(blk_col, BITS_PER_WORD - 1)).astype(F32)
        bits = jnp.where(any_tok > 0.5, bit, 0.0)
        for w in range(n_words):
            lo, hi = w * BITS_PER_WORD, min((w + 1) * BITS_PER_WORD, nblk)
            words_sc[w] = jnp.sum(bits[lo:hi]).astype(jnp.int32) if lo < nblk else jnp.int32(0)

    n_var = 4
    span = nt // n_var
    ncp = kc_ref.shape[2]
    first_three = -(-2 * SLC_BLOCK // tq)
    assert first_three <= span and n_sel >= 3
    for v in range(1, n_var + 1):
        lo = (v - 1) * span
        if lo < first_three:
            @pl.when(qi < first_three)
            def _(v=v):
                front(ncp * v // n_var, (nt * per_tile) * v // n_var, False)
            lo = first_three

        @pl.when((qi >= lo) & (qi < v * span))
        def _(v=v):
            front(ncp * v // n_var, (nt * per_tile) * v // n_var, True)

    words = [words_sc[w] for w in range(n_words)]
    for kt in range(nt + 2):
        list_sc[kt] = qi
    n_act = jnp.int32(0)
    tile_mask = (1 << per_tile) - 1
    for kt in range(nt):
        list_sc[n_act] = kt
        w, sh = divmod(kt * per_tile, BITS_PER_WORD)
        hit = jnp.bitwise_and(words[w] >> sh, tile_mask) != 0
        n_act = n_act + hit.astype(jnp.int32)
    list_sc[n_act] = qi
    n_plain = n_act - 1

    reset_stats()

    def sel_logits(idx, s_ref):
        s = _dot(ks_sc[list_sc[idx]], q2_sc[...])
        s_ref[0:tk, :] = s
        s_ref[tk:tk + 1, :] = jnp.max(s, axis=0, keepdims=True)

    def sel_update(idx, s_ref):
        kv = list_sc[idx]
        _softmax_step(lambda ls: s_ref[0:tk, ls], vt_sc[kv, 0:dh, :], shift_of(kv, idx >= n_plain),
                      m_sc, l_sc, acc_sc, s_max=s_ref[tk:tk + 1, :])

    def pair_body(i, carry):
        sel_logits(2 * i + 1, sb_sc)
        sel_update(2 * i, sa_sc)
        sel_logits(2 * i + 2, sa_sc)
        sel_update(2 * i + 1, sb_sc)
        return carry

    lax.fori_loop(0, lax.shift_right_logical(n_plain + 1, 1), pair_body, 0)
    causal = (qi * tk + j_loc) <= t_row
    _softmax_step(lambda ls: jnp.where(causal[:, ls], sa_sc[0:tk, ls], NEG), vt_sc[qi, 0:dh, :], shift_of(qi),
                  m_sc, l_sc, acc_sc)
    o_sel = acc_sc[...] * (1.0 / l_sc[...])

    sg = _sigmoid(gt_ref[0, 0])
    gates = [jnp.concatenate([sg[3 * hh + br:3 * hh + br + 1] for hh in range(NSA_HPG)], axis=1)
             for br in range(3)]
    o_all = gates[0] * br_sc[0] + gates[1] * o_sel + gates[2] * br_sc[1]
    o_ref[...] = jnp.concatenate(per_head(o_all), axis=0).T.astype(o_ref.dtype)


def _nsa(slopes, pm, kc, vct, gt, wimp, *, B, T, nc, n_sel):
    b, g, hpg, dh = B, NSA_KV_GROUPS, NSA_HPG, NSA_HEAD_DIM
    tq = NSA_TILE
    nt = T // tq
    hq = hpg * tq
    ncp = kc.shape[2]
    nbp = wimp.shape[0]
    qb = _NEW["nsa_q"] // (hpg * dh)
    kb, vb = _NEW["sel_win_k"] // LANE, _NEW["sel_win_v"] // LANE
    kern = functools.partial(_nsa_kernel, tq=tq, nt=nt, nc=nc, n_sel=n_sel)
    return pl.pallas_call(
        kern,
        grid=(b, g, nt),
        in_specs=[pl.BlockSpec(memory_space=pltpu.SMEM),
                  pl.BlockSpec((T, hpg * dh), lambda i, j, k: (i, qb + j)),
                  pl.BlockSpec((T, LANE), lambda i, j, k: (i, kb + j)),
                  pl.BlockSpec((T, LANE), lambda i, j, k: (i, vb + j)),
                  pl.BlockSpec((1, 1, ncp, dh), lambda i, j, k: (i, j, 0, 0)),
                  pl.BlockSpec((1, 1, dh, ncp), lambda i, j, k: (i, j, 0, 0)),
                  pl.BlockSpec((1, 1, 3 * hpg, tq), lambda i, j, k: (i, j, 0, k)),
                  pl.BlockSpec((nbp, ncp), lambda i, j, k: (0, 0))],
        out_specs=pl.BlockSpec((tq, hpg * dh), lambda i, j, k: (i * nt + k, j)),
        out_shape=jax.ShapeDtypeStruct((b * T, g * hpg * dh), BF16),
        scratch_shapes=[pltpu.VMEM((KSEL_FEAT, hq), BF16), pltpu.VMEM((QFEAT, hq), BF16),
                        pltpu.VMEM((nt, tq, KSEL_FEAT), BF16), pltpu.VMEM((nt, tq, QFEAT), BF16),
                        pltpu.VMEM((nt, dh, hq), BF16), pltpu.VMEM((nt, 2 * dh, tq), BF16),
                        pltpu.VMEM((tq + 8, hq), F32), pltpu.VMEM((tq + 8, hq), F32),
                        pltpu.VMEM((1, hq), F32), pltpu.VMEM((1, hq), F32),
                        pltpu.VMEM((dh, hq), F32),
                        pltpu.VMEM((2, dh, hq), F32),
                        pltpu.SMEM((nt + 2,), jnp.int32),
                        pltpu.SMEM((-(-nt * (tq // SLC_BLOCK) // BITS_PER_WORD),), jnp.int32)],
        compiler_params=pltpu.CompilerParams(dimension_semantics=("parallel", "parallel", "arbitrary"),
                                             vmem_limit_bytes=48 * 1024 * 1024),
        name="nsa",
    )(slopes, pm, pm, pm, kc, vct, gt, wimp)


def _gla_intra_scores(q, k, bcum):
    c, dk = q.shape
    sb = GLA_SUB
    row_c = lax.broadcasted_iota(jnp.int32, (c, dk), 0)
    row_s = lax.broadcasted_iota(jnp.int32, (sb, 1), 0)
    lane_s = lax.broadcasted_iota(jnp.int32, (sb, c), 1)
    blocks = []
    for i0 in range(0, c, sb):
        bi = bcum[i0:i0 + sb]
        qi = q[i0:i0 + sb]
        a_blk = jnp.zeros((sb, c), F32)
        for jl in range(sb):
            j = i0 + jl
            w = jnp.exp2(bi - bcum[j:j + 1])
            a = jnp.sum(qi * k[j:j + 1] * w, axis=-1, keepdims=True)
            a_blk = jnp.where(lane_s == j, jnp.where(row_s >= jl, a, 0.0), a_blk)
        if i0 > 0:
            r = bcum[i0 - 1:i0]
            qh = qi * jnp.exp2(bi - r)
            kh = k * jnp.exp2(jnp.where(row_c < i0, r - bcum, NEG))
            a_blk = a_blk + _dot_nt(qh.astype(BF16), kh.astype(BF16))
        blocks.append(a_blk)
    return jnp.concatenate(blocks, axis=0)


def _gla_kernel(q_ref, k_ref, v_ref, z_ref, ps_ref, wal_ref, bal_ref, gain_ref, tri_ref, o_ref,
                st_sc, *, ts):
    ti = pl.program_id(2)
    c = GLA_CHUNK
    dk, dv = GLA_KEY_DIM, GLA_VAL_DIM

    @pl.when(ti == 0)
    def _():
        st_sc[...] = jnp.zeros(st_sc.shape, F32)

    ps = ps_ref[...]
    ph, pm, _ = _split3(ps)
    wh, wm, _ = _split3(wal_ref[...])
    zz = _dot(ph, wh) + _dot(ph, wm) + _dot(pm, wh) + bal_ref[...]
    log_a = (jnp.minimum(zz, 0.0) - jnp.log(1.0 + jnp.exp(-jnp.abs(zz)))) * (LOG2E / GLA_TAU)
    tri = tri_ref[...]
    qscale = GLA_KEY_DIM ** -0.5
    for ci in range(ts // c):
        sl = slice(ci * c, (ci + 1) * c)
        for hh in range(GLA_PAIR):
            ks, vs = slice(hh * dk, (hh + 1) * dk), slice(hh * dv, (hh + 1) * dv)
            gh, gm, gl = _split3(log_a[sl, ks])
            bcum = _dot(tri, gh) + _dot(tri, gm) + _dot(tri, gl)
            q = q_ref[sl, ks].astype(F32) * qscale
            k = k_ref[sl, ks].astype(F32)
            v = v_ref[sl, vs]
            a_mat = _gla_intra_scores(q, k, bcum)
            st = st_sc[hh]
            o = _dot(a_mat.astype(BF16), v) + _dot_nt((q * jnp.exp2(bcum)).astype(BF16), st.astype(BF16))
            b_last = bcum[c - 1:c, :]
            kd = k * jnp.exp2(b_last - bcum)
            st_sc[hh] = st * jnp.exp2(b_last) + _dot(v.astype(F32).T.astype(BF16), kd.astype(BF16))
            ms = jnp.mean(o * o, axis=-1, keepdims=True)
            y = o * lax.rsqrt(ms + EPS) * gain_ref[...]
            z = z_ref[sl, vs].astype(F32)
            o_ref[sl, vs] = (y * (z * _sigmoid(z))).astype(o_ref.dtype)


def _gla(pm, ps, wal, bal, gain, tri, *, B, T, ts=1024):
    n = pm.shape[0]
    nts = T // ts
    hp = GLA_PAIR
    dk, dv = hp * GLA_KEY_DIM, hp * GLA_VAL_DIM
    qb, kb = _NEW["gq"] // dk, _NEW["gk"] // dk
    vb, zb = _NEW["gv"] // dv, _NEW["gla_z"] // dv
    kern = functools.partial(_gla_kernel, ts=ts)
    return pl.pallas_call(
        kern,
        grid=(B, GLA_HEADS // hp, nts),
        in_specs=[pl.BlockSpec((ts, dk), lambda b, h, i: (b * nts + i, qb + h)),
                  pl.BlockSpec((ts, dk), lambda b, h, i: (b * nts + i, kb + h)),
                  pl.BlockSpec((ts, dv), lambda b, h, i: (b * nts + i, vb + h)),
                  pl.BlockSpec((ts, dv), lambda b, h, i: (b * nts + i, zb + h)),
                  pl.BlockSpec((ts, SMALL_WIDTH), lambda b, h, i: (b * nts + i, 0)),
                  pl.BlockSpec((SMALL_WIDTH, dk), lambda b, h, i: (0, h)),
                  pl.BlockSpec((1, dk), lambda b, h, i: (0, h)),
                  pl.BlockSpec((1, GLA_VAL_DIM), lambda b, h, i: (0, 0)),
                  pl.BlockSpec((GLA_CHUNK, GLA_CHUNK), lambda b, h, i: (0, 0))],
        out_specs=pl.BlockSpec((ts, dv), lambda b, h, i: (b * nts + i, h)),
        out_shape=jax.ShapeDtypeStruct((n, GLA_WIDTH), BF16),
        scratch_shapes=[pltpu.VMEM((hp, GLA_VAL_DIM, GLA_KEY_DIM), F32)],
        compiler_params=pltpu.CompilerParams(dimension_semantics=("parallel", "parallel", "arbitrary"),
                                             vmem_limit_bytes=40 * 1024 * 1024),
        name="gla",
    )(pm, pm, pm, pm, ps, wal, bal, gain, tri)


def _out_proj_kernel(x_ref, gate_ref, on_ref, nz_ref, mgn_ref, mgg_ref, yg_ref,
                     wn_ref, wg_ref, wo_ref, fg_ref, o_ref):
    nz = nz_ref[...].astype(F32)
    y_nsa = on_ref[...].astype(F32) * (nz * _sigmoid(nz))
    a = _dot(y_nsa.astype(BF16), wn_ref[...])
    bm = _dot(yg_ref[...], wg_ref[...])
    merged = _sigmoid(mgn_ref[...].astype(F32)) * a + _sigmoid(mgg_ref[...].astype(F32)) * bm
    xn = x_ref[...] + gate_ref[0] * _dot(merged.astype(BF16), wo_ref[...])
    ms = jnp.mean(xn * xn, axis=-1, keepdims=True)
    o_ref[...] = xn * lax.rsqrt(ms + EPS) * fg_ref[...]


def _out_proj(x2, gate, o_nsa, pm, yg, wn, wg, wo, fg, *, T, tm=256):
    n, d = x2.shape
    nt = T // tm
    nzb = _NEW["nsa_z"] // NSA_WIDTH
    mnb = _NEW["mg_nsa"] // D_MODEL
    mgb = _NEW["mg_gla"] // D_MODEL
    row = lambda i: (i, 0)
    const2 = lambda i: (0, 0)
    return pl.pallas_call(
        _out_proj_kernel,
        grid=(n // tm,),
        in_specs=[pl.BlockSpec((tm, d), row),
                  pl.BlockSpec((1, 1, d), lambda i: (i // nt, 0, 0)),
                  pl.BlockSpec((tm, NSA_WIDTH), row),
                  pl.BlockSpec((tm, NSA_WIDTH), lambda i: (i, nzb)),
                  pl.BlockSpec((tm, D_MODEL), lambda i: (i, mnb)),
                  pl.BlockSpec((tm, D_MODEL), lambda i: (i, mgb)),
                  pl.BlockSpec((tm, GLA_WIDTH), row),
                  pl.BlockSpec((NSA_WIDTH, d), const2),
                  pl.BlockSpec((GLA_WIDTH, d), const2),
                  pl.BlockSpec((d, d), const2),
                  pl.BlockSpec((1, d), const2)],
        out_specs=pl.BlockSpec((tm, d), row),
        out_shape=jax.ShapeDtypeStruct((n, d), F32),
        compiler_params=pltpu.CompilerParams(dimension_semantics=("parallel",),
                                             vmem_limit_bytes=60 * 1024 * 1024),
        name="out_proj",
    )(x2, gate, o_nsa, pm, pm, pm, yg, wn, wg, wo, fg)


def _importance_matrix(nbp, ncp):
    ratio = SLC_BLOCK // CMP_STRIDE
    n_sub = CMP_BLOCK // CMP_STRIDE
    w = np.zeros((nbp, ncp), np.float32)
    for blk in range(nbp):
        for m in range(ratio):
            for s in range(n_sub):
                n = ratio * blk + m - s
                if 0 <= n < ncp:
                    w[blk, n] += 1.0
    return w


def _main_cols(a):
    parts = [a[..., start:start + width] * (Q_SCALE if nm == "nsa_q" else 1.0)
             for nm, pieces in _MAIN_PIECES for start, width in pieces]
    return jnp.concatenate(parts, axis=-1)


def _small_cols(a):
    pad = SMALL_WIDTH - 3 * NSA_HEADS - GLA_GATE_RANK
    parts = [a[..., _OLD["nsa_g"][0]:_OLD["nsa_g"][0] + 3 * NSA_HEADS],
             a[..., _OLD["ga"][0]:_OLD["ga"][0] + GLA_GATE_RANK],
             jnp.zeros(a.shape[:-1] + (pad,), a.dtype)]
    return jnp.concatenate(parts, axis=-1)


def kernel(x, c, w_ada, b_ada, norm_gain, w_in, b_in, cmp_pos_k, cmp_pos_v, cmp_w1_k, cmp_w2_k, cmp_w1_v, cmp_w2_v,
           gla_w_alpha, gla_b_alpha, gla_norm_gain, w_br_nsa, w_br_gla, w_out, final_norm_gain):
    assert DEPTH == 1, "the final rmsnorm is fused into the single layer's output kernel"
    B, T, D = x.shape
    G, HPG, DH = NSA_KV_GROUPS, NSA_HPG, NSA_HEAD_DIM
    N = B * T
    nch = T // CMP_STRIDE
    nc = nch - CMP_BLOCK // CMP_STRIDE + 1
    nb = T // SLC_BLOCK
    nbp = KSEL_FEAT - QFEAT
    assert nb <= nbp and T % NSA_TILE == 0
    n_sel = min(N_SELECT, nb)
    tk = NSA_TILE
    l = 0

    slopes = 2.0 ** (-8.0 * jnp.arange(1, NSA_HEADS + 1, dtype=F32) / NSA_HEADS)
    wimp = jnp.asarray(_importance_matrix(nbp, nch), BF16)
    tri = jnp.asarray(np.tril(np.ones((GLA_CHUNK, GLA_CHUNK), np.float32)), BF16)
    bp = ((B + 7) // 8) * 8
    c8 = jnp.pad(c, ((0, bp - B), (0, 0)))
    x2 = x.reshape(N, D)

    mod = _ada(c8, w_ada[l], b_ada[l][None, :])
    shift = mod[:B, :D].reshape(B, 1, D)
    scale = mod[:B, D:2 * D].reshape(B, 1, D)
    gate = mod[:B, 2 * D:].reshape(B, 1, D)

    w_in_t = jnp.transpose(w_in[l])
    pm, ps = _in_proj(x2, norm_gain[l][None, :], shift, scale,
                      _wprep(w_in_t), _main_cols(b_in[l])[None, :],
                      _wprep_small(w_in_t), _small_cols(b_in[l])[None, :], T=T)

    def cols(name, width=NSA_KV_WIDTH):
        return pm[:, _NEW[name]:_NEW[name] + width]

    kvf = cols("ck", 2 * NSA_KV_WIDTH).reshape(B, nch, CMP_STRIDE, 2, G, DH)
    kvf = kvf.transpose(3, 0, 4, 1, 2, 5).reshape(2, B, G, nch, CMP_STRIDE * DH)
    posf = jnp.stack([cmp_pos_k[l], cmp_pos_v[l]]).reshape(2, 1, CMP_BLOCK * DH)
    posf = jnp.broadcast_to(posf, (2, 8, CMP_BLOCK * DH)).astype(BF16)
    w1 = jnp.stack([cmp_w1_k[l], cmp_w1_v[l]]).astype(BF16)
    w2 = jnp.stack([cmp_w2_k[l], cmp_w2_v[l]]).astype(BF16)
    kvc = _compress(kvf, posf, w1, w2)
    gt = ps[:, :3 * NSA_HEADS].reshape(B, T, G, 3 * HPG).transpose(0, 2, 3, 1)
    slopes2 = slopes * LOG2E
    o_nsa = _nsa(slopes2, pm, kvc[0], kvc[1].transpose(0, 1, 3, 2), gt, wimp, B=B, T=T, nc=nc, n_sel=n_sel)

    wal = jnp.pad(gla_w_alpha[l], ((GA_LANE, SMALL_WIDTH - GA_LANE - GLA_GATE_RANK), (0, 0)))
    y_gla = _gla(pm, ps, wal, gla_b_alpha[l][None, :], gla_norm_gain[l][None, :], tri, B=B, T=T)

    out = _out_proj(x2, gate, o_nsa, pm, y_gla,
                    w_br_nsa[l].astype(BF16), w_br_gla[l].astype(BF16), w_out[l].astype(BF16),
                    final_norm_gain[None, :], T=T)
    return out.reshape(B, T, D)
```

```python
import functools

import numpy as np
import jax
import jax.numpy as jnp
from jax import lax
from jax.experimental import pallas as pl
from jax.experimental.pallas import tpu as pltpu

D_MODEL = 2048
DEPTH = 1
NSA_HEADS = 16
NSA_KV_GROUPS = 4
NSA_HPG = NSA_HEADS // NSA_KV_GROUPS
NSA_HEAD_DIM = 64
CMP_BLOCK = 32
CMP_STRIDE = 16
CMP_HIDDEN = 256
SLC_BLOCK = 64
N_SELECT = 16
WINDOW = 512
NSA_WIDTH = NSA_HEADS * NSA_HEAD_DIM
NSA_KV_WIDTH = NSA_KV_GROUPS * NSA_HEAD_DIM
GLA_HEADS = 4
GLA_KEY_DIM = 128
GLA_VAL_DIM = 256
GLA_GATE_RANK = 16
GLA_TAU = 16.0
GLA_KEY_WIDTH = GLA_HEADS * GLA_KEY_DIM
GLA_WIDTH = GLA_HEADS * GLA_VAL_DIM
EPS = 1e-6
NEG = -1e30

F32 = jnp.float32
BF16 = jnp.bfloat16

_OLD = {}
_off = 0
for _name, _w in (("nsa_q", NSA_WIDTH), ("ck", NSA_KV_WIDTH), ("cv", NSA_KV_WIDTH), ("sk", NSA_KV_WIDTH),
                  ("sv", NSA_KV_WIDTH), ("wk", NSA_KV_WIDTH), ("wv", NSA_KV_WIDTH), ("nsa_g", 3 * NSA_HEADS),
                  ("nsa_z", NSA_WIDTH), ("gq", GLA_KEY_WIDTH), ("gk", GLA_KEY_WIDTH), ("gv", GLA_WIDTH),
                  ("ga", GLA_GATE_RANK), ("gla_z", GLA_WIDTH), ("mg_nsa", D_MODEL), ("mg_gla", D_MODEL)):
    _OLD[_name] = (_off, _w)
    _off += _w

def _pair_by_group(a, b):
    return [(_OLD[nm][0] + g * NSA_HEAD_DIM, NSA_HEAD_DIM) for g in range(NSA_KV_GROUPS) for nm in (a, b)]


_MAIN_PIECES = [(nm, [_OLD[nm]]) for nm in
                ("nsa_q", "nsa_z", "gla_z", "gv", "mg_nsa", "mg_gla", "gq", "gk", "ck", "cv")]
_MAIN_PIECES += [("sel_win_k", _pair_by_group("sk", "wk")), ("sel_win_v", _pair_by_group("sv", "wv"))]
_NEW = {}
_off = 0
for _name, _pieces in _MAIN_PIECES:
    _NEW[_name] = _off
    _off += sum(w for _, w in _pieces)
MAIN_WIDTH = _off
SMALL_WIDTH = 128
GA_LANE = 3 * NSA_HEADS

LANE = 128
GLA_CHUNK = 64
GLA_SUB = 16
GLA_PAIR = 2
NSA_TILE = 256
QFEAT = 2 * NSA_HEAD_DIM
KSEL_FEAT = QFEAT + LANE
N_SLOPE_TERMS = 3
SOFTMAX_STRIP = NSA_TILE
SEL_MASK = 16384.0
BITS_PER_WORD = 16
LOG2E = 1.4426950408889634
Q_SCALE = LOG2E * NSA_HEAD_DIM ** -0.5


def _dot(a, b):
    return jnp.dot(a, b, preferred_element_type=F32)


def _dot_nt(a, b):
    return lax.dot_general(a, b, (((1,), (1,)), ((), ())), preferred_element_type=F32)


def _split3(x):
    hi = x.astype(BF16)
    r = x - hi.astype(F32)
    mid = r.astype(BF16)
    lo = (r - mid.astype(F32)).astype(BF16)
    return hi, mid, lo


def _sigmoid(x):
    return 1.0 / (1.0 + jnp.exp(-x))


def _ada_kernel(c_ref, w_ref, b_ref, o_ref):
    ch, cm, cl = _split3(c_ref[...])
    wh, wm, wl = _split3(w_ref[...])
    acc = _dot(ch, wh) + _dot(ch, wm) + _dot(cm, wh) + _dot(ch, wl) + _dot(cl, wh) + _dot(cm, wm)
    o_ref[...] = acc + b_ref[...]


def _ada(c8, w, b, *, tn=768):
    m, d = c8.shape
    n = w.shape[1]
    return pl.pallas_call(
        _ada_kernel,
        grid=(n // tn,),
        in_specs=[pl.BlockSpec((m, d), lambda j: (0, 0)),
                  pl.BlockSpec((d, tn), lambda j: (0, j)),
                  pl.BlockSpec((1, tn), lambda j: (0, j))],
        out_specs=pl.BlockSpec((m, tn), lambda j: (0, j)),
        out_shape=jax.ShapeDtypeStruct((m, n), F32),
        compiler_params=pltpu.CompilerParams(dimension_semantics=("arbitrary",),
                                             vmem_limit_bytes=40 * 1024 * 1024),
        name="ada",
    )(c8, w, b)


WPREP_TILE = 512


def _wprep_tiles():
    tiles = []
    for name, pieces in _MAIN_PIECES:
        scale = Q_SCALE if name == "nsa_q" else 1.0
        if len(pieces) == 1:
            start, width = pieces[0]
            assert width % WPREP_TILE == 0 or (name in ("ck", "cv") and width * 2 == WPREP_TILE)
            if name == "cv":
                assert start == _OLD["ck"][0] + _OLD["ck"][1]
                continue
            for off in range(0, max(width, WPREP_TILE), WPREP_TILE):
                tiles.append((start + off, (start + off) % LANE, scale))
        else:
            first, second = pieces[0][0], pieces[1][0]
            assert second == first + WPREP_TILE and first % LANE == 0 and len(pieces) * NSA_HEAD_DIM == WPREP_TILE
            tiles.append((first, -1, scale))
    return tiles


def _wprep_kernel(start_ref, kind_ref, scale_ref, a_ref, b_ref, o_ref):
    j = pl.program_id(0)
    scale = scale_ref[j]

    @pl.when(kind_ref[j] >= 0)
    def _():
        o_ref[...] = (a_ref[...] * scale).astype(o_ref.dtype)

    @pl.when(kind_ref[j] < 0)
    def _():
        half = NSA_HEAD_DIM
        for g in range(NSA_KV_GROUPS):
            o_ref[2 * g * half:(2 * g + 1) * half, :] = (a_ref[g * half:(g + 1) * half, :] * scale).astype(o_ref.dtype)
            o_ref[(2 * g + 1) * half:(2 * g + 2) * half, :] = (b_ref[g * half:(g + 1) * half, :] * scale).astype(o_ref.dtype)


def _wprep(wt):
    width, d = wt.shape
    tiles = _wprep_tiles()
    half_tile = WPREP_TILE // 2
    assert len(tiles) * WPREP_TILE == MAIN_WIDTH and all(t[0] % 8 == 0 for t in tiles)
    start = jnp.asarray([t[0] for t in tiles], jnp.int32)
    kind = jnp.asarray([t[1] for t in tiles], jnp.int32)
    scale = jnp.asarray([t[2] for t in tiles], F32)
    return pl.pallas_call(
        _wprep_kernel,
        grid_spec=pltpu.PrefetchScalarGridSpec(
            num_scalar_prefetch=2,
            grid=(len(tiles),),
            in_specs=[pl.BlockSpec(memory_space=pltpu.SMEM),
                      pl.BlockSpec((pl.Element(WPREP_TILE), pl.Element(d)),
                                   lambda j, st, kd: (pl.multiple_of(st[j], 8), 0)),
                      pl.BlockSpec((pl.Element(half_tile), pl.Element(d)),
                                   lambda j, st, kd: (pl.multiple_of(
                                       jnp.minimum(st[j] + WPREP_TILE, width - half_tile), 8), 0))],
            out_specs=pl.BlockSpec((WPREP_TILE, d), lambda j, st, kd: (j, 0))),
        out_shape=jax.ShapeDtypeStruct((MAIN_WIDTH, d), BF16),
        compiler_params=pltpu.CompilerParams(dimension_semantics=("arbitrary",),
                                             vmem_limit_bytes=40 * 1024 * 1024),
        name="wprep",
    )(start, kind, scale, wt, wt)


def _wprep_small_kernel(g_ref, a_ref, o_ref):
    n_gate = 3 * NSA_HEADS
    rows = jnp.concatenate([g_ref[0:n_gate, :], a_ref[...],
                            jnp.zeros((SMALL_WIDTH - n_gate - GLA_GATE_RANK, g_ref.shape[1]), F32)], axis=0)
    o_ref[...] = rows.astype(o_ref.dtype)


def _wprep_small(wt):
    d = wt.shape[1]
    g0, a0 = _OLD["nsa_g"][0], _OLD["ga"][0]
    assert g0 % 8 == 0 and a0 % 8 == 0
    return pl.pallas_call(
        _wprep_small_kernel,
        grid=(1,),
        in_specs=[pl.BlockSpec((pl.Element(NSA_HEAD_DIM), pl.Element(d)), lambda i: (g0, 0)),
                  pl.BlockSpec((pl.Element(GLA_GATE_RANK), pl.Element(d)), lambda i: (a0, 0))],
        out_specs=pl.BlockSpec((SMALL_WIDTH, d), lambda i: (0, 0)),
        out_shape=jax.ShapeDtypeStruct((SMALL_WIDTH, d), BF16),
        name="wprep_small",
    )(wt, wt)


def _in_proj_kernel(x_ref, gain_ref, shift_ref, scale_ref, w_ref, b_ref, ws_ref, bs_ref,
                    o_ref, os_ref, h_ref):
    j = pl.program_id(1)

    @pl.when(j == 0)
    def _():
        x = x_ref[...]
        ms = jnp.mean(x * x, axis=-1, keepdims=True)
        y = x * lax.rsqrt(ms + EPS) * gain_ref[...]
        y = y * (1.0 + scale_ref[0]) + shift_ref[0]
        h = y.astype(BF16)
        h_ref[...] = h
        os_ref[...] = _dot_nt(h, ws_ref[...]) + bs_ref[...]

    o_ref[...] = (_dot_nt(h_ref[...], w_ref[...]) + b_ref[...]).astype(o_ref.dtype)


def _in_proj(x2, gain, shift, scale, w_main, b_main, w_small, b_small, *, T, tm=1024, tn=1536):
    n, d = x2.shape
    nm = w_main.shape[0]
    nt = T // tm
    return pl.pallas_call(
        _in_proj_kernel,
        grid=(n // tm, nm // tn),
        in_specs=[pl.BlockSpec((tm, d), lambda i, j: (i, 0)),
                  pl.BlockSpec((1, d), lambda i, j: (0, 0)),
                  pl.BlockSpec((1, 1, d), lambda i, j: (i // nt, 0, 0)),
                  pl.BlockSpec((1, 1, d), lambda i, j: (i // nt, 0, 0)),
                  pl.BlockSpec((tn, d), lambda i, j: (j, 0)),
                  pl.BlockSpec((1, tn), lambda i, j: (0, j)),
                  pl.BlockSpec((SMALL_WIDTH, d), lambda i, j: (0, 0)),
                  pl.BlockSpec((1, SMALL_WIDTH), lambda i, j: (0, 0))],
        out_specs=[pl.BlockSpec((tm, tn), lambda i, j: (i, j)),
                   pl.BlockSpec((tm, SMALL_WIDTH), lambda i, j: (i, 0))],
        out_shape=[jax.ShapeDtypeStruct((n, nm), BF16),
                   jax.ShapeDtypeStruct((n, SMALL_WIDTH), F32)],
        scratch_shapes=[pltpu.VMEM((tm, d), BF16)],
        compiler_params=pltpu.CompilerParams(dimension_semantics=("parallel", "arbitrary"),
                                             vmem_limit_bytes=56 * 1024 * 1024),
        name="in_proj",
    )(x2, gain, shift, scale, w_main, b_main, w_small, b_small)


def _compress_kernel(kv_ref, pos_ref, w1_ref, w2_ref, o_ref):
    y = kv_ref[0, 0, 0]
    w1 = w1_ref[0]
    half = y.shape[1]
    nch = y.shape[0]
    z1 = _dot(y, w1[:half])
    z2 = _dot(y, w1[half:])
    posb = _dot(pos_ref[0], w1)[0:1]
    pre = z1 + pltpu.roll(z2, nch - 1, 0) + posb
    hid = pre * _sigmoid(pre)
    o_ref[0, 0, 0] = _dot(hid.astype(BF16), w2_ref[0]).astype(o_ref.dtype)


def _compress(kvf, posf, w1, w2):
    two, b, g, nch, half = kvf.shape
    hidden = w1.shape[2]
    dh = w2.shape[2]
    return pl.pallas_call(
        _compress_kernel,
        grid=(two, b, g),
        in_specs=[pl.BlockSpec((1, 1, 1, nch, half), lambda s, i, j: (s, i, j, 0, 0)),
                  pl.BlockSpec((1, 8, 2 * half), lambda s, i, j: (s, 0, 0)),
                  pl.BlockSpec((1, 2 * half, hidden), lambda s, i, j: (s, 0, 0)),
                  pl.BlockSpec((1, hidden, dh), lambda s, i, j: (s, 0, 0))],
        out_specs=pl.BlockSpec((1, 1, 1, nch, dh), lambda s, i, j: (s, i, j, 0, 0)),
        out_shape=jax.ShapeDtypeStruct((two, b, g, nch, dh), BF16),
        compiler_params=pltpu.CompilerParams(dimension_semantics=("parallel", "parallel", "parallel")),
        name="compress",
    )(kvf, posf, w1, w2)


def _select_blocks(imp, t_row, n_sel, three_forced):
    nbp = imp.shape[0]
    blk = lax.broadcasted_iota(jnp.int32, (nbp, 1), 0)
    cur = lax.shift_right_logical(t_row, 6)
    forced = (blk == 0) | (blk == cur) | (blk == cur - 1)
    bvalid = blk * SLC_BLOCK <= t_row
    rest = jnp.where(bvalid, imp, -1.0)
    blk_f = blk.astype(F32)
    if three_forced:
        score = jnp.where(forced, -1.0, rest)
        sel = jnp.where(forced, 1.0, 0.0)
        rounds = n_sel - 3
    else:
        score = jnp.where(forced, 1e30, rest)
        sel = jnp.zeros(imp.shape, F32)
        rounds = n_sel
    for _ in range(rounds):
        m = jnp.max(score, axis=0, keepdims=True)
        first = jnp.min(jnp.where(score == m, blk_f, float(nbp)), axis=0, keepdims=True)
        hit = blk_f == first
        sel = jnp.where(hit, 1.0, sel)
        score = jnp.where(hit, -1.0, score)
    return sel


def _softmax_step(s_of, vt, d, m_sc, l_sc, acc_sc, s_max=None):
    for c0 in range(0, d.shape[1], SOFTMAX_STRIP):
        ls = slice(c0, c0 + SOFTMAX_STRIP)
        s = s_of(ls)
        mx = jnp.max(s, axis=0, keepdims=True) if s_max is None else s_max[:, ls]
        dd = d[:, ls]
        m_prev = m_sc[:, ls]
        m_new = jnp.maximum(m_prev, mx - dd)
        alpha = jnp.exp2(m_prev - m_new)
        p = jnp.exp2(s - (m_new + dd))
        l_sc[:, ls] = alpha * l_sc[:, ls] + jnp.sum(p, axis=0, keepdims=True)
        acc_sc[:, ls] = alpha * acc_sc[:, ls] + _dot(vt, p.astype(BF16))
        m_sc[:, ls] = m_new


def _stage_group(qslab_ref, kslab_ref, vslab_ref, qt_sc, ks_sc, kw_sc, vt_sc, nt, tk):
    dh = NSA_HEAD_DIM
    lane = lax.broadcasted_iota(jnp.int32, (tk, LANE), 1)
    row = lax.broadcasted_iota(jnp.int32, (tk, LANE), 0)
    pos = row.astype(F32)
    feat_sel = jnp.where((lane >= dh) & (lane < dh + N_SLOPE_TERMS), pos, 0.0)
    feat_win = jnp.where(lane < N_SLOPE_TERMS, pos, 0.0)
    blk_in_tile = lax.shift_right_logical(row, 6)

    def body(kt, carry):
        start = pl.multiple_of(kt * tk, tk)
        kk = kslab_ref[pl.ds(start, tk), :].astype(F32)
        onehot = jnp.where(kt * (tk // SLC_BLOCK) + blk_in_tile == lane, 1.0, 0.0)
        ks_sc[kt] = jnp.concatenate([jnp.where(lane < dh, kk, feat_sel), onehot], axis=1).astype(BF16)
        kw_sc[kt] = jnp.where(lane >= dh, kk, feat_win).astype(BF16)
        vt_sc[kt] = vslab_ref[pl.ds(start, tk), :].astype(F32).T.astype(BF16)
        x_t = qslab_ref[pl.ds(start, tk), :].astype(F32).T
        qt_sc[kt] = jnp.concatenate([x_t[hh * dh:(hh + 1) * dh] for hh in range(NSA_HPG)], axis=1).astype(BF16)
        return carry

    lax.fori_loop(0, nt, body, 0)


def _nsa_kernel(slopes_ref, q_ref, kslab_ref, vslab_ref, kc_ref, vct_ref, gt_ref, wimp_ref,
                o_ref, q2_sc, qw_sc, ks_sc, kw_sc, qt_sc, vt_sc, sa_sc, sb_sc, m_sc, l_sc, acc_sc, br_sc, list_sc,
                words_sc, *, tq, nt, nc, n_sel):
    g = pl.program_id(1)
    qi = pl.program_id(2)
    tk = tq
    dh = NSA_HEAD_DIM
    hq = NSA_HPG * tq

    @pl.when(qi == 0)
    def _():
        _stage_group(q_ref, kslab_ref, vslab_ref, qt_sc, ks_sc, kw_sc, vt_sc, nt, tk)

    lane = lax.broadcasted_iota(jnp.int32, (1, hq), 1)
    slope_row = jnp.zeros((1, hq), F32)
    for hh in range(NSA_HPG):
        slope_row = jnp.where(lane >= hh * tq, slopes_ref[g * NSA_HPG + hh], slope_row)
    t_one = qi * tq + lax.broadcasted_iota(jnp.int32, (1, tq), 1)
    t_row = jnp.concatenate([t_one] * NSA_HPG, axis=1)
    t_f = t_row.astype(F32)
    j_loc = lax.broadcasted_iota(jnp.int32, (tk, 1), 0)

    q_t = qt_sc[qi]
    feat_row = lax.broadcasted_iota(jnp.int32, (dh, hq), 0)
    terms = [t.astype(F32) for t in _split3(slope_row)]
    feat = jnp.zeros((dh, hq), F32)
    for i, term in enumerate(terms):
        feat = jnp.where(feat_row == i, term, feat)
    feat = feat.astype(BF16)
    q2_sc[0:dh, :] = q_t
    q2_sc[dh:QFEAT, :] = feat
    qw_sc[0:dh, :] = feat
    qw_sc[dh:QFEAT, :] = q_t

    def reset_stats():
        m_sc[...] = jnp.full(m_sc.shape, NEG, F32)
        l_sc[...] = jnp.zeros(l_sc.shape, F32)
        acc_sc[...] = jnp.zeros(acc_sc.shape, F32)

    def per_head(a):
        return [a[:, hh * tq:(hh + 1) * tq] for hh in range(NSA_HPG)]

    def shift_of(kv, off=None):
        d = slope_row * (t_f - (kv * tk).astype(F32))
        return d if off is None else d + jnp.where(off, -NEG, 0.0)

    per_tile = tk // SLC_BLOCK
    nbp = KSEL_FEAT - QFEAT
    n_words = -(-nt * per_tile // BITS_PER_WORD)

    def front(nk, nblk, three_forced):
        kc = kc_ref[0, 0, 0:nk, :]
        n_col = lax.broadcasted_iota(jnp.int32, (nk, 1), 0)
        ce = jnp.where(n_col < nc, n_col * CMP_STRIDE + (CMP_BLOCK - 1), 2 ** 30)
        ce_rel = (ce - qi * tq).astype(F32)
        s = jnp.where(ce <= t_row, _dot(kc, q_t) + slope_row * ce_rel, NEG)
        m = jnp.max(s, axis=0, keepdims=True)
        p = jnp.exp2(s - m)
        l = jnp.sum(p, axis=0, keepdims=True)
        p = p * jnp.where(t_row >= CMP_BLOCK - 1, 1.0 / l, 0.0)
        br_sc[0] = _dot(vct_ref[0, 0, :, 0:nk], p.astype(BF16))
        ps4 = per_head(p)
        psum = (ps4[0] + ps4[1]) + (ps4[2] + ps4[3])
        wimp = wimp_ref[0:nblk, 0:nk]
        ph, pm, plo = _split3(psum)
        imp = _dot(wimp, ph) + _dot(wimp, pm) + _dot(wimp, plo)

        reset_stats()
        n_back = WINDOW // tk
        qw = qw_sc[...]
        tiles = [jnp.maximum(qi - w, 0) for w in range(n_back + 1)]
        logits = [_dot(kw_sc[kv], qw) for kv in tiles]
        for w, (kv, s) in enumerate(zip(tiles, logits)):
            def s_of(ls, w=w, kv=kv, s=s):
                if w == 0:
                    return jnp.where((kv * tk + j_loc) <= t_row[:, ls], s[:, ls], NEG)
                if w == n_back:
                    return jnp.where((t_row[:, ls] - (kv * tk + j_loc)) < WINDOW, s[:, ls], NEG)
                return s[:, ls]
            _softmax_step(s_of, vt_sc[kv, dh:2 * dh, :], shift_of(kv, qi < w), m_sc, l_sc, acc_sc)
        br_sc[1] = acc_sc[...] * (1.0 / l_sc[...])

        s0 = _dot(ks_sc[0, :, 0:QFEAT], q2_sc[0:QFEAT, :])

        sel = _select_blocks(imp, t_one, n_sel, three_forced)
        penalty_f = (sel - 1.0) * SEL_MASK
        q2_sc[QFEAT:QFEAT + nblk, :] = jnp.concatenate([penalty_f.astype(BF16)] * NSA_HPG, axis=1)
        if nblk < nbp:
            q2_sc[QFEAT + nblk:KSEL_FEAT, :] = jnp.full((nbp - nblk, hq), -SEL_MASK, BF16)
        pen0 = jnp.concatenate([penalty_f[0:per_tile]] * NSA_HPG, axis=1)
        pen0 = jnp.concatenate([jnp.broadcast_to(pen0[b:b + 1], (SLC_BLOCK, hq)) for b in range(per_tile)], axis=0)
        s0 = s0 + pen0
        sa_sc[0:tk, :] = s0
        sa_sc[tk:tk + 1, :] = jnp.max(s0, axis=0, keepdims=True)

        any_tok = jnp.max(sel, axis=1, keepdims=True)
        blk_col = lax.broadcasted_iota(jnp.int32, any_tok.shape, 0)
        bit = jnp.left_shift(1, jnp.bitwise_and(blk_col, BITS_PER_WORD - 1)).astype(F32)
        bits = jnp.where(any_tok > 0.5, bit, 0.0)
        for w in range(n_words):
            lo, hi = w * BITS_PER_WORD, min((w + 1) * BITS_PER_WORD, nblk)
            words_sc[w] = jnp.sum(bits[lo:hi]).astype(jnp.int32) if lo < nblk else jnp.int32(0)

    n_var = 4
    span = nt // n_var
    ncp = kc_ref.shape[2]
    first_three = -(-2 * SLC_BLOCK // tq)
    assert first_three <= span and n_sel >= 3
    for v in range(1, n_var + 1):
        lo = (v - 1) * span
        if lo < first_three:
            @pl.when(qi < first_three)
            def _(v=v):
                front(ncp * v // n_var, (nt * per_tile) * v // n_var, False)
            lo = first_three

        @pl.when((qi >= lo) & (qi < v * span))
        def _(v=v):
            front(ncp * v // n_var, (nt * per_tile) * v // n_var, True)

    words = [words_sc[w] for w in range(n_words)]
    for kt in range(nt + 2):
        list_sc[kt] = qi
    n_act = jnp.int32(0)
    tile_mask = (1 << per_tile) - 1
    for kt in range(nt):
        list_sc[n_act] = kt
        w, sh = divmod(kt * per_tile, BITS_PER_WORD)
        hit = jnp.bitwise_and(words[w] >> sh, tile_mask) != 0
        n_act = n_act + hit.astype(jnp.int32)
    list_sc[n_act] = qi
    n_plain = n_act - 1

    reset_stats()

    def sel_logits(idx, s_ref):
        s = _dot(ks_sc[list_sc[idx]], q2_sc[...])
        s_ref[0:tk, :] = s
        s_ref[tk:tk + 1, :] = jnp.max(s, axis=0, keepdims=True)

    def sel_update(idx, s_ref):
        kv = list_sc[idx]
        _softmax_step(lambda ls: s_ref[0:tk, ls], vt_sc[kv, 0:dh, :], shift_of(kv),
                      m_sc, l_sc, acc_sc, s_max=s_ref[tk:tk + 1, :])

    causal = (qi * tk + j_loc) <= t_row

    def diag_update(s_ref):
        _softmax_step(lambda ls: jnp.where(causal[:, ls], s_ref[0:tk, ls], NEG), vt_sc[qi, 0:dh, :], shift_of(qi),
                      m_sc, l_sc, acc_sc)

    def pair_body(i, carry):
        sel_logits(2 * i + 1, sb_sc)
        sel_update(2 * i, sa_sc)
        sel_logits(2 * i + 2, sa_sc)
        sel_update(2 * i + 1, sb_sc)
        return carry

    lax.fori_loop(0, lax.shift_right_logical(n_plain, 1), pair_body, 0)
    odd = jnp.bitwise_and(n_plain, 1) == 1

    @pl.when(jnp.logical_not(odd))
    def _():
        diag_update(sa_sc)

    @pl.when(odd)
    def _():
        sel_logits(n_plain, sb_sc)
        sel_update(n_plain - 1, sa_sc)
        diag_update(sb_sc)

    o_sel = acc_sc[...] * (1.0 / l_sc[...])

    sg = _sigmoid(gt_ref[0, 0])
    gates = [jnp.concatenate([sg[3 * hh + br:3 * hh + br + 1] for hh in range(NSA_HPG)], axis=1)
             for br in range(3)]
    o_all = gates[0] * br_sc[0] + gates[1] * o_sel + gates[2] * br_sc[1]
    o_ref[...] = jnp.concatenate(per_head(o_all), axis=0).T.astype(o_ref.dtype)


def _nsa(slopes, pm, kc, vct, gt, wimp, *, B, T, nc, n_sel):
    b, g, hpg, dh = B, NSA_KV_GROUPS, NSA_HPG, NSA_HEAD_DIM
    tq = NSA_TILE
    nt = T // tq
    hq = hpg * tq
    ncp = kc.shape[2]
    nbp = wimp.shape[0]
    qb = _NEW["nsa_q"] // (hpg * dh)
    kb, vb = _NEW["sel_win_k"] // LANE, _NEW["sel_win_v"] // LANE
    kern = functools.partial(_nsa_kernel, tq=tq, nt=nt, nc=nc, n_sel=n_sel)
    return pl.pallas_call(
        kern,
        grid=(b, g, nt),
        in_specs=[pl.BlockSpec(memory_space=pltpu.SMEM),
                  pl.BlockSpec((T, hpg * dh), lambda i, j, k: (i, qb + j)),
                  pl.BlockSpec((T, LANE), lambda i, j, k: (i, kb + j)),
                  pl.BlockSpec((T, LANE), lambda i, j, k: (i, vb + j)),
                  pl.BlockSpec((1, 1, ncp, dh), lambda i, j, k: (i, j, 0, 0)),
                  pl.BlockSpec((1, 1, dh, ncp), lambda i, j, k: (i, j, 0, 0)),
                  pl.BlockSpec((1, 1, 3 * hpg, tq), lambda i, j, k: (i, j, 0, k)),
                  pl.BlockSpec((nbp, ncp), lambda i, j, k: (0, 0))],
        out_specs=pl.BlockSpec((tq, hpg * dh), lambda i, j, k: (i * nt + k, j)),
        out_shape=jax.ShapeDtypeStruct((b * T, g * hpg * dh), BF16),
        scratch_shapes=[pltpu.VMEM((KSEL_FEAT, hq), BF16), pltpu.VMEM((QFEAT, hq), BF16),
                        pltpu.VMEM((nt, tq, KSEL_FEAT), BF16), pltpu.VMEM((nt, tq, QFEAT), BF16),
                        pltpu.VMEM((nt, dh, hq), BF16), pltpu.VMEM((nt, 2 * dh, tq), BF16),
                        pltpu.VMEM((tq + 8, hq), F32), pltpu.VMEM((tq + 8, hq), F32),
                        pltpu.VMEM((1, hq), F32), pltpu.VMEM((1, hq), F32),
                        pltpu.VMEM((dh, hq), F32),
                        pltpu.VMEM((2, dh, hq), F32),
                        pltpu.SMEM((nt + 2,), jnp.int32),
                        pltpu.SMEM((-(-nt * (tq // SLC_BLOCK) // BITS_PER_WORD),), jnp.int32)],
        compiler_params=pltpu.CompilerParams(dimension_semantics=("parallel", "parallel", "arbitrary"),
                                             vmem_limit_bytes=48 * 1024 * 1024),
        name="nsa",
    )(slopes, pm, pm, pm, kc, vct, gt, wimp)


def _gla_intra_scores(q, k, bcum):
    c, dk = q.shape
    sb = GLA_SUB
    row_c = lax.broadcasted_iota(jnp.int32, (c, dk), 0)
    row_s = lax.broadcasted_iota(jnp.int32, (sb, 1), 0)
    lane_s = lax.broadcasted_iota(jnp.int32, (sb, c), 1)
    blocks = []
    for i0 in range(0, c, sb):
        bi = bcum[i0:i0 + sb]
        qi = q[i0:i0 + sb]
        a_blk = jnp.zeros((sb, c), F32)
        for jl in range(sb):
            j = i0 + jl
            w = jnp.exp2(bi - bcum[j:j + 1])
            a = jnp.sum(qi * k[j:j + 1] * w, axis=-1, keepdims=True)
            a_blk = jnp.where(lane_s == j, jnp.where(row_s >= jl, a, 0.0), a_blk)
        if i0 > 0:
            r = bcum[i0 - 1:i0]
            qh = qi * jnp.exp2(bi - r)
            kh = k * jnp.exp2(jnp.where(row_c < i0, r - bcum, NEG))
            a_blk = a_blk + _dot_nt(qh.astype(BF16), kh.astype(BF16))
        blocks.append(a_blk)
    return jnp.concatenate(blocks, axis=0)


def _gla_kernel(q_ref, k_ref, v_ref, z_ref, ps_ref, wal_ref, bal_ref, gain_ref, tri_ref, o_ref,
                st_sc, *, ts):
    ti = pl.program_id(2)
    c = GLA_CHUNK
    dk, dv = GLA_KEY_DIM, GLA_VAL_DIM

    @pl.when(ti == 0)
    def _():
        st_sc[...] = jnp.zeros(st_sc.shape, F32)

    ps = ps_ref[...]
    ph, pm, _ = _split3(ps)
    wh, wm, _ = _split3(wal_ref[...])
    zz = _dot(ph, wh) + _dot(ph, wm) + _dot(pm, wh) + bal_ref[...]
    log_a = (jnp.minimum(zz, 0.0) - jnp.log(1.0 + jnp.exp(-jnp.abs(zz)))) * (LOG2E / GLA_TAU)
    tri = tri_ref[...]
    qscale = GLA_KEY_DIM ** -0.5
    for ci in range(ts // c):
        sl = slice(ci * c, (ci + 1) * c)
        for hh in range(GLA_PAIR):
            ks, vs = slice(hh * dk, (hh + 1) * dk), slice(hh * dv, (hh + 1) * dv)
            gh, gm, gl = _split3(log_a[sl, ks])
            bcum = _dot(tri, gh) + _dot(tri, gm) + _dot(tri, gl)
            q = q_ref[sl, ks].astype(F32) * qscale
            k = k_ref[sl, ks].astype(F32)
            v = v_ref[sl, vs]
            a_mat = _gla_intra_scores(q, k, bcum)
            st = st_sc[hh]
            o = _dot(a_mat.astype(BF16), v) + _dot_nt((q * jnp.exp2(bcum)).astype(BF16), st.astype(BF16))
            b_last = bcum[c - 1:c, :]
            kd = k * jnp.exp2(b_last - bcum)
            st_sc[hh] = st * jnp.exp2(b_last) + _dot(v.astype(F32).T.astype(BF16), kd.astype(BF16))
            ms = jnp.mean(o * o, axis=-1, keepdims=True)
            y = o * lax.rsqrt(ms + EPS) * gain_ref[...]
            z = z_ref[sl, vs].astype(F32)
            o_ref[sl, vs] = (y * (z * _sigmoid(z))).astype(o_ref.dtype)


def _gla(pm, ps, wal, bal, gain, tri, *, B, T, ts=1024):
    n = pm.shape[0]
    nts = T // ts
    hp = GLA_PAIR
    dk, dv = hp * GLA_KEY_DIM, hp * GLA_VAL_DIM
    qb, kb = _NEW["gq"] // dk, _NEW["gk"] // dk
    vb, zb = _NEW["gv"] // dv, _NEW["gla_z"] // dv
    kern = functools.partial(_gla_kernel, ts=ts)
    return pl.pallas_call(
        kern,
        grid=(B, GLA_HEADS // hp, nts),
        in_specs=[pl.BlockSpec((ts, dk), lambda b, h, i: (b * nts + i, qb + h)),
                  pl.BlockSpec((ts, dk), lambda b, h, i: (b * nts + i, kb + h)),
                  pl.BlockSpec((ts, dv), lambda b, h, i: (b * nts + i, vb + h)),
                  pl.BlockSpec((ts, dv), lambda b, h, i: (b * nts + i, zb + h)),
                  pl.BlockSpec((ts, SMALL_WIDTH), lambda b, h, i: (b * nts + i, 0)),
                  pl.BlockSpec((SMALL_WIDTH, dk), lambda b, h, i: (0, h)),
                  pl.BlockSpec((1, dk), lambda b, h, i: (0, h)),
                  pl.BlockSpec((1, GLA_VAL_DIM), lambda b, h, i: (0, 0)),
                  pl.BlockSpec((GLA_CHUNK, GLA_CHUNK), lambda b, h, i: (0, 0))],
        out_specs=pl.BlockSpec((ts, dv), lambda b, h, i: (b * nts + i, h)),
        out_shape=jax.ShapeDtypeStruct((n, GLA_WIDTH), BF16),
        scratch_shapes=[pltpu.VMEM((hp, GLA_VAL_DIM, GLA_KEY_DIM), F32)],
        compiler_params=pltpu.CompilerParams(dimension_semantics=("parallel", "parallel", "arbitrary"),
                                             vmem_limit_bytes=40 * 1024 * 1024),
        name="gla",
    )(pm, pm, pm, pm, ps, wal, bal, gain, tri)


def _out_proj_kernel(x_ref, gate_ref, on_ref, nz_ref, mgn_ref, mgg_ref, yg_ref,
                     wn_ref, wg_ref, wo_ref, fg_ref, o_ref):
    nz = nz_ref[...].astype(F32)
    y_nsa = on_ref[...].astype(F32) * (nz * _sigmoid(nz))
    a = _dot(y_nsa.astype(BF16), wn_ref[...])
    bm = _dot(yg_ref[...], wg_ref[...])
    merged = _sigmoid(mgn_ref[...].astype(F32)) * a + _sigmoid(mgg_ref[...].astype(F32)) * bm
    xn = x_ref[...] + gate_ref[0] * _dot(merged.astype(BF16), wo_ref[...])
    ms = jnp.mean(xn * xn, axis=-1, keepdims=True)
    o_ref[...] = xn * lax.rsqrt(ms + EPS) * fg_ref[...]


def _out_proj(x2, gate, o_nsa, pm, yg, wn, wg, wo, fg, *, T, tm=256):
    n, d = x2.shape
    nt = T // tm
    nzb = _NEW["nsa_z"] // NSA_WIDTH
    mnb = _NEW["mg_nsa"] // D_MODEL
    mgb = _NEW["mg_gla"] // D_MODEL
    row = lambda i: (i, 0)
    const2 = lambda i: (0, 0)
    return pl.pallas_call(
        _out_proj_kernel,
        grid=(n // tm,),
        in_specs=[pl.BlockSpec((tm, d), row),
                  pl.BlockSpec((1, 1, d), lambda i: (i // nt, 0, 0)),
                  pl.BlockSpec((tm, NSA_WIDTH), row),
                  pl.BlockSpec((tm, NSA_WIDTH), lambda i: (i, nzb)),
                  pl.BlockSpec((tm, D_MODEL), lambda i: (i, mnb)),
                  pl.BlockSpec((tm, D_MODEL), lambda i: (i, mgb)),
                  pl.BlockSpec((tm, GLA_WIDTH), row),
                  pl.BlockSpec((NSA_WIDTH, d), const2),
                  pl.BlockSpec((GLA_WIDTH, d), const2),
                  pl.BlockSpec((d, d), const2),
                  pl.BlockSpec((1, d), const2)],
        out_specs=pl.BlockSpec((tm, d), row),
        out_shape=jax.ShapeDtypeStruct((n, d), F32),
        compiler_params=pltpu.CompilerParams(dimension_semantics=("parallel",),
                                             vmem_limit_bytes=60 * 1024 * 1024),
        name="out_proj",
    )(x2, gate, o_nsa, pm, pm, pm, yg, wn, wg, wo, fg)


def _importance_matrix(nbp, ncp):
    ratio = SLC_BLOCK // CMP_STRIDE
    n_sub = CMP_BLOCK // CMP_STRIDE
    w = np.zeros((nbp, ncp), np.float32)
    for blk in range(nbp):
        for m in range(ratio):
            for s in range(n_sub):
                n = ratio * blk + m - s
                if 0 <= n < ncp:
                    w[blk, n] += 1.0
    return w


def _main_cols(a):
    parts = [a[..., start:start + width] * (Q_SCALE if nm == "nsa_q" else 1.0)
             for nm, pieces in _MAIN_PIECES for start, width in pieces]
    return jnp.concatenate(parts, axis=-1)


def _small_cols(a):
    pad = SMALL_WIDTH - 3 * NSA_HEADS - GLA_GATE_RANK
    parts = [a[..., _OLD["nsa_g"][0]:_OLD["nsa_g"][0] + 3 * NSA_HEADS],
             a[..., _OLD["ga"][0]:_OLD["ga"][0] + GLA_GATE_RANK],
             jnp.zeros(a.shape[:-1] + (pad,), a.dtype)]
    return jnp.concatenate(parts, axis=-1)


def kernel(x, c, w_ada, b_ada, norm_gain, w_in, b_in, cmp_pos_k, cmp_pos_v, cmp_w1_k, cmp_w2_k, cmp_w1_v, cmp_w2_v,
           gla_w_alpha, gla_b_alpha, gla_norm_gain, w_br_nsa, w_br_gla, w_out, final_norm_gain):
    assert DEPTH == 1, "the final rmsnorm is fused into the single layer's output kernel"
    B, T, D = x.shape
    G, HPG, DH = NSA_KV_GROUPS, NSA_HPG, NSA_HEAD_DIM
    N = B * T
    nch = T // CMP_STRIDE
    nc = nch - CMP_BLOCK // CMP_STRIDE + 1
    nb = T // SLC_BLOCK
    nbp = KSEL_FEAT - QFEAT
    assert nb <= nbp and T % NSA_TILE == 0
    n_sel = min(N_SELECT, nb)
    tk = NSA_TILE
    l = 0

    slopes = 2.0 ** (-8.0 * jnp.arange(1, NSA_HEADS + 1, dtype=F32) / NSA_HEADS)
    wimp = jnp.asarray(_importance_matrix(nbp, nch), BF16)
    tri = jnp.asarray(np.tril(np.ones((GLA_CHUNK, GLA_CHUNK), np.float32)), BF16)
    bp = ((B + 7) // 8) * 8
    c8 = jnp.pad(c, ((0, bp - B), (0, 0)))
    x2 = x.reshape(N, D)

    mod = _ada(c8, w_ada[l], b_ada[l][None, :])
    shift = mod[:B, :D].reshape(B, 1, D)
    scale = mod[:B, D:2 * D].reshape(B, 1, D)
    gate = mod[:B, 2 * D:].reshape(B, 1, D)

    w_in_t = jnp.transpose(w_in[l])
    pm, ps = _in_proj(x2, norm_gain[l][None, :], shift, scale,
                      _wprep(w_in_t), _main_cols(b_in[l])[None, :],
                      _wprep_small(w_in_t), _small_cols(b_in[l])[None, :], T=T)

    def cols(name, width=NSA_KV_WIDTH):
        return pm[:, _NEW[name]:_NEW[name] + width]

    kvf = cols("ck", 2 * NSA_KV_WIDTH).reshape(B, nch, CMP_STRIDE, 2, G, DH)
    kvf = kvf.transpose(3, 0, 4, 1, 2, 5).reshape(2, B, G, nch, CMP_STRIDE * DH)
    posf = jnp.stack([cmp_pos_k[l], cmp_pos_v[l]]).reshape(2, 1, CMP_BLOCK * DH)
    posf = jnp.broadcast_to(posf, (2, 8, CMP_BLOCK * DH)).astype(BF16)
    w1 = jnp.stack([cmp_w1_k[l], cmp_w1_v[l]]).astype(BF16)
    w2 = jnp.stack([cmp_w2_k[l], cmp_w2_v[l]]).astype(BF16)
    kvc = _compress(kvf, posf, w1, w2)
    gt = ps[:, :3 * NSA_HEADS].reshape(B, T, G, 3 * HPG).transpose(0, 2, 3, 1)
    slopes2 = slopes * LOG2E
    o_nsa = _nsa(slopes2, pm, kvc[0], kvc[1].transpose(0, 1, 3, 2), gt, wimp, B=B, T=T, nc=nc, n_sel=n_sel)

    wal = jnp.pad(gla_w_alpha[l], ((GA_LANE, SMALL_WIDTH - GA_LANE - GLA_GATE_RANK), (0, 0)))
    y_gla = _gla(pm, ps, wal, gla_b_alpha[l][None, :], gla_norm_gain[l][None, :], tri, B=B, T=T)

    out = _out_proj(x2, gate, o_nsa, pm, y_gla,
                    w_br_nsa[l].astype(BF16), w_br_gla[l].astype(BF16), w_out[l].astype(BF16),
                    final_norm_gain[None, :], T=T)
    return out.reshape(B, T, D)
```

```python
import functools

import numpy as np
import jax
import jax.numpy as jnp
from jax import lax
from jax.experimental import pallas as pl
from jax.experimental.pallas import tpu as pltpu

D_MODEL = 2048
DEPTH = 1
NSA_HEADS = 16
NSA_KV_GROUPS = 4
NSA_HPG = NSA_HEADS // NSA_KV_GROUPS
NSA_HEAD_DIM = 64
CMP_BLOCK = 32
CMP_STRIDE = 16
CMP_HIDDEN = 256
SLC_BLOCK = 64
N_SELECT = 16
WINDOW = 512
NSA_WIDTH = NSA_HEADS * NSA_HEAD_DIM
NSA_KV_WIDTH = NSA_KV_GROUPS * NSA_HEAD_DIM
GLA_HEADS = 4
GLA_KEY_DIM = 128
GLA_VAL_DIM = 256
GLA_GATE_RANK = 16
GLA_TAU = 16.0
GLA_KEY_WIDTH = GLA_HEADS * GLA_KEY_DIM
GLA_WIDTH = GLA_HEADS * GLA_VAL_DIM
EPS = 1e-6
NEG = -1e30

F32 = jnp.float32
BF16 = jnp.bfloat16

_OLD = {}
_off = 0
for _name, _w in (("nsa_q", NSA_WIDTH), ("ck", NSA_KV_WIDTH), ("cv", NSA_KV_WIDTH), ("sk", NSA_KV_WIDTH),
                  ("sv", NSA_KV_WIDTH), ("wk", NSA_KV_WIDTH), ("wv", NSA_KV_WIDTH), ("nsa_g", 3 * NSA_HEADS),
                  ("nsa_z", NSA_WIDTH), ("gq", GLA_KEY_WIDTH), ("gk", GLA_KEY_WIDTH), ("gv", GLA_WIDTH),
                  ("ga", GLA_GATE_RANK), ("gla_z", GLA_WIDTH), ("mg_nsa", D_MODEL), ("mg_gla", D_MODEL)):
    _OLD[_name] = (_off, _w)
    _off += _w

def _pair_by_group(a, b):
    return [(_OLD[nm][0] + g * NSA_HEAD_DIM, NSA_HEAD_DIM) for g in range(NSA_KV_GROUPS) for nm in (a, b)]


_MAIN_PIECES = [(nm, [_OLD[nm]]) for nm in
                ("nsa_q", "nsa_z", "gla_z", "gv", "mg_nsa", "mg_gla", "gq", "gk", "ck", "cv")]
_MAIN_PIECES += [("sel_win_k", _pair_by_group("sk", "wk")), ("sel_win_v", _pair_by_group("sv", "wv"))]
_NEW = {}
_off = 0
for _name, _pieces in _MAIN_PIECES:
    _NEW[_name] = _off
    _off += sum(w for _, w in _pieces)
MAIN_WIDTH = _off
SMALL_WIDTH = 128
GA_LANE = 3 * NSA_HEADS

LANE = 128
GLA_CHUNK = 64
GLA_SUB = 16
GLA_PAIR = 4
NSA_TILE = 256
QFEAT = 2 * NSA_HEAD_DIM
KSEL_FEAT = QFEAT + LANE
N_SLOPE_TERMS = 3
SOFTMAX_STRIP = NSA_TILE
SEL_MASK = 16384.0
BITS_PER_WORD = 16
LOG2E = 1.4426950408889634
Q_SCALE = LOG2E * NSA_HEAD_DIM ** -0.5


def _dot(a, b):
    return jnp.dot(a, b, preferred_element_type=F32)


def _dot_nt(a, b):
    return lax.dot_general(a, b, (((1,), (1,)), ((), ())), preferred_element_type=F32)


def _split3(x):
    hi = x.astype(BF16)
    r = x - hi.astype(F32)
    mid = r.astype(BF16)
    lo = (r - mid.astype(F32)).astype(BF16)
    return hi, mid, lo


def _sigmoid(x):
    return 1.0 / (1.0 + jnp.exp(-x))


def _ada_kernel(c_ref, w_ref, b_ref, o_ref):
    ch, cm, cl = _split3(c_ref[...])
    wh, wm, wl = _split3(w_ref[...])
    acc = _dot(ch, wh) + _dot(ch, wm) + _dot(cm, wh) + _dot(ch, wl) + _dot(cl, wh) + _dot(cm, wm)
    o_ref[...] = acc + b_ref[...]


def _ada(c8, w, b, *, tn=768):
    m, d = c8.shape
    n = w.shape[1]
    return pl.pallas_call(
        _ada_kernel,
        grid=(n // tn,),
        in_specs=[pl.BlockSpec((m, d), lambda j: (0, 0)),
                  pl.BlockSpec((d, tn), lambda j: (0, j)),
                  pl.BlockSpec((1, tn), lambda j: (0, j))],
        out_specs=pl.BlockSpec((m, tn), lambda j: (0, j)),
        out_shape=jax.ShapeDtypeStruct((m, n), F32),
        compiler_params=pltpu.CompilerParams(dimension_semantics=("arbitrary",),
                                             vmem_limit_bytes=40 * 1024 * 1024),
        name="ada",
    )(c8, w, b)


WPREP_TILE = 512


def _wprep_tiles():
    tiles = []
    for name, pieces in _MAIN_PIECES:
        scale = Q_SCALE if name == "nsa_q" else 1.0
        if len(pieces) == 1:
            start, width = pieces[0]
            assert width % WPREP_TILE == 0 or (name in ("ck", "cv") and width * 2 == WPREP_TILE)
            if name == "cv":
                assert start == _OLD["ck"][0] + _OLD["ck"][1]
                continue
            for off in range(0, max(width, WPREP_TILE), WPREP_TILE):
                tiles.append((start + off, (start + off) % LANE, scale))
        else:
            first, second = pieces[0][0], pieces[1][0]
            assert second == first + WPREP_TILE and first % LANE == 0 and len(pieces) * NSA_HEAD_DIM == WPREP_TILE
            tiles.append((first, -1, scale))
    return tiles


def _wprep_kernel(start_ref, kind_ref, scale_ref, a_ref, b_ref, o_ref):
    j = pl.program_id(0)
    scale = scale_ref[j]

    @pl.when(kind_ref[j] >= 0)
    def _():
        o_ref[...] = (a_ref[...] * scale).astype(o_ref.dtype)

    @pl.when(kind_ref[j] < 0)
    def _():
        half = NSA_HEAD_DIM
        for g in range(NSA_KV_GROUPS):
            o_ref[2 * g * half:(2 * g + 1) * half, :] = (a_ref[g * half:(g + 1) * half, :] * scale).astype(o_ref.dtype)
            o_ref[(2 * g + 1) * half:(2 * g + 2) * half, :] = (b_ref[g * half:(g + 1) * half, :] * scale).astype(o_ref.dtype)


def _wprep(wt):
    width, d = wt.shape
    tiles = _wprep_tiles()
    half_tile = WPREP_TILE // 2
    assert len(tiles) * WPREP_TILE == MAIN_WIDTH and all(t[0] % 8 == 0 for t in tiles)
    start = jnp.asarray([t[0] for t in tiles], jnp.int32)
    kind = jnp.asarray([t[1] for t in tiles], jnp.int32)
    scale = jnp.asarray([t[2] for t in tiles], F32)
    return pl.pallas_call(
        _wprep_kernel,
        grid_spec=pltpu.PrefetchScalarGridSpec(
            num_scalar_prefetch=2,
            grid=(len(tiles),),
            in_specs=[pl.BlockSpec(memory_space=pltpu.SMEM),
                      pl.BlockSpec((pl.Element(WPREP_TILE), pl.Element(d)),
                                   lambda j, st, kd: (pl.multiple_of(st[j], 8), 0)),
                      pl.BlockSpec((pl.Element(half_tile), pl.Element(d)),
                                   lambda j, st, kd: (pl.multiple_of(
                                       jnp.minimum(st[j] + WPREP_TILE, width - half_tile), 8), 0))],
            out_specs=pl.BlockSpec((WPREP_TILE, d), lambda j, st, kd: (j, 0))),
        out_shape=jax.ShapeDtypeStruct((MAIN_WIDTH, d), BF16),
        compiler_params=pltpu.CompilerParams(dimension_semantics=("arbitrary",),
                                             vmem_limit_bytes=40 * 1024 * 1024),
        name="wprep",
    )(start, kind, scale, wt, wt)


def _wprep_small_kernel(g_ref, a_ref, o_ref):
    n_gate = 3 * NSA_HEADS
    rows = jnp.concatenate([g_ref[0:n_gate, :], a_ref[...],
                            jnp.zeros((SMALL_WIDTH - n_gate - GLA_GATE_RANK, g_ref.shape[1]), F32)], axis=0)
    o_ref[...] = rows.astype(o_ref.dtype)


def _wprep_small(wt):
    d = wt.shape[1]
    g0, a0 = _OLD["nsa_g"][0], _OLD["ga"][0]
    assert g0 % 8 == 0 and a0 % 8 == 0
    return pl.pallas_call(
        _wprep_small_kernel,
        grid=(1,),
        in_specs=[pl.BlockSpec((pl.Element(NSA_HEAD_DIM), pl.Element(d)), lambda i: (g0, 0)),
                  pl.BlockSpec((pl.Element(GLA_GATE_RANK), pl.Element(d)), lambda i: (a0, 0))],
        out_specs=pl.BlockSpec((SMALL_WIDTH, d), lambda i: (0, 0)),
        out_shape=jax.ShapeDtypeStruct((SMALL_WIDTH, d), BF16),
        name="wprep_small",
    )(wt, wt)


def _in_proj_kernel(x_ref, gain_ref, shift_ref, scale_ref, w_ref, b_ref, ws_ref, bs_ref,
                    o_ref, os_ref, h_ref):
    j = pl.program_id(1)

    @pl.when(j == 0)
    def _():
        x = x_ref[...]
        ms = jnp.mean(x * x, axis=-1, keepdims=True)
        y = x * lax.rsqrt(ms + EPS) * gain_ref[...]
        y = y * (1.0 + scale_ref[0]) + shift_ref[0]
        h = y.astype(BF16)
        h_ref[...] = h
        os_ref[...] = _dot_nt(h, ws_ref[...]) + bs_ref[...]

    o_ref[...] = (_dot_nt(h_ref[...], w_ref[...]) + b_ref[...]).astype(o_ref.dtype)


def _in_proj(x2, gain, shift, scale, w_main, b_main, w_small, b_small, *, T, tm=1024, tn=1792):
    n, d = x2.shape
    nm = w_main.shape[0]
    nt = T // tm
    return pl.pallas_call(
        _in_proj_kernel,
        grid=(n // tm, nm // tn),
        in_specs=[pl.BlockSpec((tm, d), lambda i, j: (i, 0)),
                  pl.BlockSpec((1, d), lambda i, j: (0, 0)),
                  pl.BlockSpec((1, 1, d), lambda i, j: (i // nt, 0, 0)),
                  pl.BlockSpec((1, 1, d), lambda i, j: (i // nt, 0, 0)),
                  pl.BlockSpec((tn, d), lambda i, j: (j, 0)),
                  pl.BlockSpec((1, tn), lambda i, j: (0, j)),
                  pl.BlockSpec((SMALL_WIDTH, d), lambda i, j: (0, 0)),
                  pl.BlockSpec((1, SMALL_WIDTH), lambda i, j: (0, 0))],
        out_specs=[pl.BlockSpec((tm, tn), lambda i, j: (i, j)),
                   pl.BlockSpec((tm, SMALL_WIDTH), lambda i, j: (i, 0))],
        out_shape=[jax.ShapeDtypeStruct((n, nm), BF16),
                   jax.ShapeDtypeStruct((n, SMALL_WIDTH), F32)],
        scratch_shapes=[pltpu.VMEM((tm, d), BF16)],
        compiler_params=pltpu.CompilerParams(dimension_semantics=("parallel", "arbitrary"),
                                             vmem_limit_bytes=56 * 1024 * 1024),
        name="in_proj",
    )(x2, gain, shift, scale, w_main, b_main, w_small, b_small)


def _compress_kernel(kv_ref, pos_ref, w1_ref, w2_ref, o_ref):
    y = kv_ref[0, 0, 0]
    w1 = w1_ref[0]
    half = y.shape[1]
    nch = y.shape[0]
    z1 = _dot(y, w1[:half])
    z2 = _dot(y, w1[half:])
    posb = _dot(pos_ref[0], w1)[0:1]
    pre = z1 + pltpu.roll(z2, nch - 1, 0) + posb
    hid = pre * _sigmoid(pre)
    o_ref[0, 0, 0] = _dot(hid.astype(BF16), w2_ref[0]).astype(o_ref.dtype)


def _compress(kvf, posf, w1, w2):
    two, b, g, nch, half = kvf.shape
    hidden = w1.shape[2]
    dh = w2.shape[2]
    return pl.pallas_call(
        _compress_kernel,
        grid=(two, b, g),
        in_specs=[pl.BlockSpec((1, 1, 1, nch, half), lambda s, i, j: (s, i, j, 0, 0)),
                  pl.BlockSpec((1, 8, 2 * half), lambda s, i, j: (s, 0, 0)),
                  pl.BlockSpec((1, 2 * half, hidden), lambda s, i, j: (s, 0, 0)),
                  pl.BlockSpec((1, hidden, dh), lambda s, i, j: (s, 0, 0))],
        out_specs=pl.BlockSpec((1, 1, 1, nch, dh), lambda s, i, j: (s, i, j, 0, 0)),
        out_shape=jax.ShapeDtypeStruct((two, b, g, nch, dh), BF16),
        compiler_params=pltpu.CompilerParams(dimension_semantics=("parallel", "parallel", "parallel")),
        name="compress",
    )(kvf, posf, w1, w2)


def _select_blocks(imp, t_row, n_sel, three_forced):
    nbp = imp.shape[0]
    blk = lax.broadcasted_iota(jnp.int32, (nbp, 1), 0)
    cur = lax.shift_right_logical(t_row, 6)
    forced = (blk == 0) | (blk == cur) | (blk == cur - 1)
    bvalid = blk * SLC_BLOCK <= t_row
    rest = jnp.where(bvalid, imp, -1.0)
    blk_f = blk.astype(F32)
    if three_forced:
        score = jnp.where(forced, -1.0, rest)
        sel = jnp.where(forced, 1.0, 0.0)
        rounds = n_sel - 3
    else:
        score = jnp.where(forced, 1e30, rest)
        sel = jnp.zeros(imp.shape, F32)
        rounds = n_sel
    for _ in range(rounds):
        m = jnp.max(score, axis=0, keepdims=True)
        first = jnp.min(jnp.where(score == m, blk_f, float(nbp)), axis=0, keepdims=True)
        hit = blk_f == first
        sel = jnp.where(hit, 1.0, sel)
        score = jnp.where(hit, -1.0, score)
    return sel


def _softmax_step(s_of, vt, d, m_sc, l_sc, acc_sc, s_max=None):
    for c0 in range(0, d.shape[1], SOFTMAX_STRIP):
        ls = slice(c0, c0 + SOFTMAX_STRIP)
        s = s_of(ls)
        mx = jnp.max(s, axis=0, keepdims=True) if s_max is None else s_max[:, ls]
        dd = d[:, ls]
        m_prev = m_sc[:, ls]
        m_new = jnp.maximum(m_prev, mx - dd)
        alpha = jnp.exp2(m_prev - m_new)
        p = jnp.exp2(s - (m_new + dd))
        l_sc[:, ls] = alpha * l_sc[:, ls] + jnp.sum(p, axis=0, keepdims=True)
        acc_sc[:, ls] = alpha * acc_sc[:, ls] + _dot(vt, p.astype(BF16))
        m_sc[:, ls] = m_new


def _stage_group(qslab_ref, kslab_ref, vslab_ref, qt_sc, ks_sc, kw_sc, vt_sc, nt, tk):
    dh = NSA_HEAD_DIM
    lane = lax.broadcasted_iota(jnp.int32, (tk, LANE), 1)
    row = lax.broadcasted_iota(jnp.int32, (tk, LANE), 0)
    pos = row.astype(F32)
    feat_sel = jnp.where((lane >= dh) & (lane < dh + N_SLOPE_TERMS), pos, 0.0)
    feat_win = jnp.where(lane < N_SLOPE_TERMS, pos, 0.0)
    blk_in_tile = lax.shift_right_logical(row, 6)

    def body(kt, carry):
        start = pl.multiple_of(kt * tk, tk)
        kk = kslab_ref[pl.ds(start, tk), :].astype(F32)
        onehot = jnp.where(kt * (tk // SLC_BLOCK) + blk_in_tile == lane, 1.0, 0.0)
        ks_sc[kt] = jnp.concatenate([jnp.where(lane < dh, kk, feat_sel), onehot], axis=1).astype(BF16)
        kw_sc[kt] = jnp.where(lane >= dh, kk, feat_win).astype(BF16)
        vt_sc[kt] = vslab_ref[pl.ds(start, tk), :].astype(F32).T.astype(BF16)
        x_t = qslab_ref[pl.ds(start, tk), :].astype(F32).T
        qt_sc[kt] = jnp.concatenate([x_t[hh * dh:(hh + 1) * dh] for hh in range(NSA_HPG)], axis=1).astype(BF16)
        return carry

    lax.fori_loop(0, nt, body, 0)


def _nsa_kernel(slopes_ref, q_ref, kslab_ref, vslab_ref, kc_ref, vct_ref, gt_ref, wimp_ref,
                o_ref, q2_sc, qw_sc, ks_sc, kw_sc, qt_sc, vt_sc, sa_sc, sb_sc, m_sc, l_sc, acc_sc, br_sc, list_sc,
                words_sc, *, tq, nt, nc, n_sel):
    g = pl.program_id(1)
    qi = pl.program_id(2)
    tk = tq
    dh = NSA_HEAD_DIM
    hq = NSA_HPG * tq

    @pl.when(qi == 0)
    def _():
        _stage_group(q_ref, kslab_ref, vslab_ref, qt_sc, ks_sc, kw_sc, vt_sc, nt, tk)

    lane = lax.broadcasted_iota(jnp.int32, (1, hq), 1)
    slope_row = jnp.zeros((1, hq), F32)
    for hh in range(NSA_HPG):
        slope_row = jnp.where(lane >= hh * tq, slopes_ref[g * NSA_HPG + hh], slope_row)
    t_one = qi * tq + lax.broadcasted_iota(jnp.int32, (1, tq), 1)
    t_row = jnp.concatenate([t_one] * NSA_HPG, axis=1)
    t_f = t_row.astype(F32)
    j_loc = lax.broadcasted_iota(jnp.int32, (tk, 1), 0)

    q_t = qt_sc[qi]
    feat_row = lax.broadcasted_iota(jnp.int32, (dh, hq), 0)
    terms = [t.astype(F32) for t in _split3(slope_row)]
    feat = jnp.zeros((dh, hq), F32)
    for i, term in enumerate(terms):
        feat = jnp.where(feat_row == i, term, feat)
    feat = feat.astype(BF16)
    q2_sc[0:dh, :] = q_t
    q2_sc[dh:QFEAT, :] = feat
    qw_sc[0:dh, :] = feat
    qw_sc[dh:QFEAT, :] = q_t

    def reset_stats():
        m_sc[...] = jnp.full(m_sc.shape, NEG, F32)
        l_sc[...] = jnp.zeros(l_sc.shape, F32)
        acc_sc[...] = jnp.zeros(acc_sc.shape, F32)

    def per_head(a):
        return [a[:, hh * tq:(hh + 1) * tq] for hh in range(NSA_HPG)]

    def shift_of(kv, off=None):
        d = slope_row * (t_f - (kv * tk).astype(F32))
        return d if off is None else d + jnp.where(off, -NEG, 0.0)

    per_tile = tk // SLC_BLOCK
    nbp = KSEL_FEAT - QFEAT
    n_words = -(-nt * per_tile // BITS_PER_WORD)

    def front(nk, nblk, three_forced):
        kc = kc_ref[0, 0, 0:nk, :]
        n_col = lax.broadcasted_iota(jnp.int32, (nk, 1), 0)
        ce = jnp.where(n_col < nc, n_col * CMP_STRIDE + (CMP_BLOCK - 1), 2 ** 30)
        ce_rel = (ce - qi * tq).astype(F32)
        s = jnp.where(ce <= t_row, _dot(kc, q_t) + slope_row * ce_rel, NEG)
        m = jnp.max(s, axis=0, keepdims=True)
        p = jnp.exp2(s - m)
        l = jnp.sum(p, axis=0, keepdims=True)
        p = p * jnp.where(t_row >= CMP_BLOCK - 1, 1.0 / l, 0.0)
        br_sc[0] = _dot(vct_ref[0, 0, :, 0:nk], p.astype(BF16))
        ps4 = per_head(p)
        psum = (ps4[0] + ps4[1]) + (ps4[2] + ps4[3])
        wimp = wimp_ref[0:nblk, 0:nk]
        ph, pm, plo = _split3(psum)
        imp = _dot(wimp, ph) + _dot(wimp, pm) + _dot(wimp, plo)

        reset_stats()
        n_back = WINDOW // tk
        qw = qw_sc[...]
        tiles = [jnp.maximum(qi - w, 0) for w in range(n_back + 1)]
        logits = [_dot(kw_sc[kv], qw) for kv in tiles]
        for w, (kv, s) in enumerate(zip(tiles, logits)):
            def s_of(ls, w=w, kv=kv, s=s):
                if w == 0:
                    return jnp.where((kv * tk + j_loc) <= t_row[:, ls], s[:, ls], NEG)
                if w == n_back:
                    return jnp.where((t_row[:, ls] - (kv * tk + j_loc)) < WINDOW, s[:, ls], NEG)
                return s[:, ls]
            _softmax_step(s_of, vt_sc[kv, dh:2 * dh, :], shift_of(kv, qi < w), m_sc, l_sc, acc_sc)
        br_sc[1] = acc_sc[...] * (1.0 / l_sc[...])

        s0 = _dot(ks_sc[0, :, 0:QFEAT], q2_sc[0:QFEAT, :])

        sel = _select_blocks(imp, t_one, n_sel, three_forced)
        penalty_f = (sel - 1.0) * SEL_MASK
        q2_sc[QFEAT:QFEAT + nblk, :] = jnp.concatenate([penalty_f.astype(BF16)] * NSA_HPG, axis=1)
        if nblk < nbp:
            q2_sc[QFEAT + nblk:KSEL_FEAT, :] = jnp.full((nbp - nblk, hq), -SEL_MASK, BF16)
        pen0 = jnp.concatenate([penalty_f[0:per_tile]] * NSA_HPG, axis=1)
        pen0 = jnp.concatenate([jnp.broadcast_to(pen0[b:b + 1], (SLC_BLOCK, hq)) for b in range(per_tile)], axis=0)
        s0 = s0 + pen0
        sa_sc[0:tk, :] = s0
        sa_sc[tk:tk + 1, :] = jnp.max(s0, axis=0, keepdims=True)

        any_tok = jnp.max(sel, axis=1, keepdims=True)
        blk_col = lax.broadcasted_iota(jnp.int32, any_tok.shape, 0)
        bit = jnp.left_shift(1, jnp.bitwise_and(blk_col, BITS_PER_WORD - 1)).astype(F32)
        bits = jnp.where(any_tok > 0.5, bit, 0.0)
        for w in range(n_words):
            lo, hi = w * BITS_PER_WORD, min((w + 1) * BITS_PER_WORD, nblk)
            words_sc[w] = jnp.sum(bits[lo:hi]).astype(jnp.int32) if lo < nblk else jnp.int32(0)

    n_var = 4
    span = nt // n_var
    ncp = kc_ref.shape[2]
    first_three = -(-2 * SLC_BLOCK // tq)
    assert first_three <= span and n_sel >= 3
    for v in range(1, n_var + 1):
        lo = (v - 1) * span
        if lo < first_three:
            @pl.when(qi < first_three)
            def _(v=v):
                front(ncp * v // n_var, (nt * per_tile) * v // n_var, False)
            lo = first_three

        @pl.when((qi >= lo) & (qi < v * span))
        def _(v=v):
            front(ncp * v // n_var, (nt * per_tile) * v // n_var, True)

    words = [words_sc[w] for w in range(n_words)]
    for kt in range(nt + 2):
        list_sc[kt] = qi
    n_act = jnp.int32(0)
    tile_mask = (1 << per_tile) - 1
    for kt in range(nt):
        list_sc[n_act] = kt
        w, sh = divmod(kt * per_tile, BITS_PER_WORD)
        hit = jnp.bitwise_and(words[w] >> sh, tile_mask) != 0
        n_act = n_act + hit.astype(jnp.int32)
    list_sc[n_act] = qi
    n_plain = n_act - 1

    reset_stats()

    def sel_logits(idx, s_ref):
        s = _dot(ks_sc[list_sc[idx]], q2_sc[...])
        s_ref[0:tk, :] = s
        s_ref[tk:tk + 1, :] = jnp.max(s, axis=0, keepdims=True)

    def sel_update(idx, s_ref):
        kv = list_sc[idx]
        _softmax_step(lambda ls: s_ref[0:tk, ls], vt_sc[kv, 0:dh, :], shift_of(kv),
                      m_sc, l_sc, acc_sc, s_max=s_ref[tk:tk + 1, :])

    causal = (qi * tk + j_loc) <= t_row

    def diag_update(s_ref):
        _softmax_step(lambda ls: jnp.where(causal[:, ls], s_ref[0:tk, ls], NEG), vt_sc[qi, 0:dh, :], shift_of(qi),
                      m_sc, l_sc, acc_sc)

    def pair_body(i, carry):
        sel_logits(2 * i + 1, sb_sc)
        sel_update(2 * i, sa_sc)
        sel_logits(2 * i + 2, sa_sc)
        sel_update(2 * i + 1, sb_sc)
        return carry

    lax.fori_loop(0, lax.shift_right_logical(n_plain, 1), pair_body, 0)
    odd = jnp.bitwise_and(n_plain, 1) == 1

    @pl.when(jnp.logical_not(odd))
    def _():
        diag_update(sa_sc)

    @pl.when(odd)
    def _():
        sel_logits(n_plain, sb_sc)
        sel_update(n_plain - 1, sa_sc)
        diag_update(sb_sc)

    o_sel = acc_sc[...] * (1.0 / l_sc[...])

    sg = _sigmoid(gt_ref[0, 0])
    gates = [jnp.concatenate([sg[3 * hh + br:3 * hh + br + 1] for hh in range(NSA_HPG)], axis=1)
             for br in range(3)]
    o_all = gates[0] * br_sc[0] + gates[1] * o_sel + gates[2] * br_sc[1]
    o_ref[...] = jnp.concatenate(per_head(o_all), axis=0).T.astype(o_ref.dtype)


def _nsa(slopes, pm, kc, vct, gt, wimp, *, B, T, nc, n_sel):
    b, g, hpg, dh = B, NSA_KV_GROUPS, NSA_HPG, NSA_HEAD_DIM
    tq = NSA_TILE
    nt = T // tq
    hq = hpg * tq
    ncp = kc.shape[2]
    nbp = wimp.shape[0]
    qb = _NEW["nsa_q"] // (hpg * dh)
    kb, vb = _NEW["sel_win_k"] // LANE, _NEW["sel_win_v"] // LANE
    kern = functools.partial(_nsa_kernel, tq=tq, nt=nt, nc=nc, n_sel=n_sel)
    return pl.pallas_call(
        kern,
        grid=(b, g, nt),
        in_specs=[pl.BlockSpec(memory_space=pltpu.SMEM),
                  pl.BlockSpec((T, hpg * dh), lambda i, j, k: (i, qb + j)),
                  pl.BlockSpec((T, LANE), lambda i, j, k: (i, kb + j)),
                  pl.BlockSpec((T, LANE), lambda i, j, k: (i, vb + j)),
                  pl.BlockSpec((1, 1, ncp, dh), lambda i, j, k: (i, j, 0, 0)),
                  pl.BlockSpec((1, 1, dh, ncp), lambda i, j, k: (i, j, 0, 0)),
                  pl.BlockSpec((1, 1, 3 * hpg, tq), lambda i, j, k: (i, j, 0, k)),
                  pl.BlockSpec((nbp, ncp), lambda i, j, k: (0, 0))],
        out_specs=pl.BlockSpec((tq, hpg * dh), lambda i, j, k: (i * nt + k, j)),
        out_shape=jax.ShapeDtypeStruct((b * T, g * hpg * dh), BF16),
        scratch_shapes=[pltpu.VMEM((KSEL_FEAT, hq), BF16), pltpu.VMEM((QFEAT, hq), BF16),
                        pltpu.VMEM((nt, tq, KSEL_FEAT), BF16), pltpu.VMEM((nt, tq, QFEAT), BF16),
                        pltpu.VMEM((nt, dh, hq), BF16), pltpu.VMEM((nt, 2 * dh, tq), BF16),
                        pltpu.VMEM((tq + 8, hq), F32), pltpu.VMEM((tq + 8, hq), F32),
                        pltpu.VMEM((1, hq), F32), pltpu.VMEM((1, hq), F32),
                        pltpu.VMEM((dh, hq), F32),
                        pltpu.VMEM((2, dh, hq), F32),
                        pltpu.SMEM((nt + 2,), jnp.int32),
                        pltpu.SMEM((-(-nt * (tq // SLC_BLOCK) // BITS_PER_WORD),), jnp.int32)],
        compiler_params=pltpu.CompilerParams(dimension_semantics=("parallel", "parallel", "arbitrary"),
                                             vmem_limit_bytes=48 * 1024 * 1024),
        name="nsa",
    )(slopes, pm, pm, pm, kc, vct, gt, wimp)


def _gla_intra_scores(q, k, bcum):
    c, dk = q.shape
    sb = GLA_SUB
    row_c = lax.broadcasted_iota(jnp.int32, (c, dk), 0)
    row_s = lax.broadcasted_iota(jnp.int32, (sb, 1), 0)
    lane_s = lax.broadcasted_iota(jnp.int32, (sb, c), 1)
    blocks = []
    for i0 in range(0, c, sb):
        bi = bcum[i0:i0 + sb]
        qi = q[i0:i0 + sb]
        a_blk = jnp.zeros((sb, c), F32)
        for jl in range(sb):
            j = i0 + jl
            w = jnp.exp2(bi - bcum[j:j + 1])
            a = jnp.sum(qi * k[j:j + 1] * w, axis=-1, keepdims=True)
            a_blk = jnp.where(lane_s == j, jnp.where(row_s >= jl, a, 0.0), a_blk)
        if i0 > 0:
            r = bcum[i0 - 1:i0]
            qh = qi * jnp.exp2(bi - r)
            kh = k * jnp.exp2(jnp.where(row_c < i0, r - bcum, NEG))
            a_blk = a_blk + _dot_nt(qh.astype(BF16), kh.astype(BF16))
        blocks.append(a_blk)
    return jnp.concatenate(blocks, axis=0)


def _gla_kernel(q_ref, k_ref, v_ref, z_ref, ps_ref, wal_ref, bal_ref, gain_ref, tri_ref, o_ref,
                st_sc, *, ts):
    ti = pl.program_id(2)
    c = GLA_CHUNK
    dk, dv = GLA_KEY_DIM, GLA_VAL_DIM

    @pl.when(ti == 0)
    def _():
        st_sc[...] = jnp.zeros(st_sc.shape, F32)

    ps = ps_ref[...]
    ph, pm, _ = _split3(ps)
    wh, wm, _ = _split3(wal_ref[...])
    zz = _dot(ph, wh) + _dot(ph, wm) + _dot(pm, wh) + bal_ref[...]
    log_a = (jnp.minimum(zz, 0.0) - jnp.log(1.0 + jnp.exp(-jnp.abs(zz)))) * (LOG2E / GLA_TAU)
    tri = tri_ref[...]
    qscale = GLA_KEY_DIM ** -0.5
    for ci in range(ts // c):
        sl = slice(ci * c, (ci + 1) * c)
        for hh in range(GLA_PAIR):
            ks, vs = slice(hh * dk, (hh + 1) * dk), slice(hh * dv, (hh + 1) * dv)
            gh, gm, gl = _split3(log_a[sl, ks])
            bcum = _dot(tri, gh) + _dot(tri, gm) + _dot(tri, gl)
            q = q_ref[sl, ks].astype(F32) * qscale
            k = k_ref[sl, ks].astype(F32)
            v = v_ref[sl, vs]
            a_mat = _gla_intra_scores(q, k, bcum)
            st = st_sc[hh]
            o = _dot(a_mat.astype(BF16), v) + _dot_nt((q * jnp.exp2(bcum)).astype(BF16), st.astype(BF16))
            b_last = bcum[c - 1:c, :]
            kd = k * jnp.exp2(b_last - bcum)
            st_sc[hh] = st * jnp.exp2(b_last) + _dot(v.astype(F32).T.astype(BF16), kd.astype(BF16))
            ms = jnp.mean(o * o, axis=-1, keepdims=True)
            y = o * lax.rsqrt(ms + EPS) * gain_ref[...]
            z = z_ref[sl, vs].astype(F32)
            o_ref[sl, vs] = (y * (z * _sigmoid(z))).astype(o_ref.dtype)


def _gla(pm, ps, wal, bal, gain, tri, *, B, T, ts=1024):
    n = pm.shape[0]
    nts = T // ts
    hp = GLA_PAIR
    dk, dv = hp * GLA_KEY_DIM, hp * GLA_VAL_DIM
    qb, kb = _NEW["gq"] // dk, _NEW["gk"] // dk
    vb, zb = _NEW["gv"] // dv, _NEW["gla_z"] // dv
    kern = functools.partial(_gla_kernel, ts=ts)
    return pl.pallas_call(
        kern,
        grid=(B, GLA_HEADS // hp, nts),
        in_specs=[pl.BlockSpec((ts, dk), lambda b, h, i: (b * nts + i, qb + h)),
                  pl.BlockSpec((ts, dk), lambda b, h, i: (b * nts + i, kb + h)),
                  pl.BlockSpec((ts, dv), lambda b, h, i: (b * nts + i, vb + h)),
                  pl.BlockSpec((ts, dv), lambda b, h, i: (b * nts + i, zb + h)),
                  pl.BlockSpec((ts, SMALL_WIDTH), lambda b, h, i: (b * nts + i, 0)),
                  pl.BlockSpec((SMALL_WIDTH, dk), lambda b, h, i: (0, h)),
                  pl.BlockSpec((1, dk), lambda b, h, i: (0, h)),
                  pl.BlockSpec((1, GLA_VAL_DIM), lambda b, h, i: (0, 0)),
                  pl.BlockSpec((GLA_CHUNK, GLA_CHUNK), lambda b, h, i: (0, 0))],
        out_specs=pl.BlockSpec((ts, dv), lambda b, h, i: (b * nts + i, h)),
        out_shape=jax.ShapeDtypeStruct((n, GLA_WIDTH), BF16),
        scratch_shapes=[pltpu.VMEM((hp, GLA_VAL_DIM, GLA_KEY_DIM), F32)],
        compiler_params=pltpu.CompilerParams(dimension_semantics=("parallel", "parallel", "arbitrary"),
                                             vmem_limit_bytes=40 * 1024 * 1024),
        name="gla",
    )(pm, pm, pm, pm, ps, wal, bal, gain, tri)


def _out_proj_kernel(x_ref, gate_ref, on_ref, nz_ref, mgn_ref, mgg_ref, yg_ref,
                     wn_ref, wg_ref, wo_ref, fg_ref, o_ref):
    nz = nz_ref[...].astype(F32)
    y_nsa = on_ref[...].astype(F32) * (nz * _sigmoid(nz))
    a = _dot(y_nsa.astype(BF16), wn_ref[...])
    bm = _dot(yg_ref[...], wg_ref[...])
    merged = _sigmoid(mgn_ref[...].astype(F32)) * a + _sigmoid(mgg_ref[...].astype(F32)) * bm
    xn = x_ref[...] + gate_ref[0] * _dot(merged.astype(BF16), wo_ref[...])
    ms = jnp.mean(xn * xn, axis=-1, keepdims=True)
    o_ref[...] = xn * lax.rsqrt(ms + EPS) * fg_ref[...]


def _out_proj(x2, gate, o_nsa, pm, yg, wn, wg, wo, fg, *, T, tm=256):
    n, d = x2.shape
    nt = T // tm
    nzb = _NEW["nsa_z"] // NSA_WIDTH
    mnb = _NEW["mg_nsa"] // D_MODEL
    mgb = _NEW["mg_gla"] // D_MODEL
    row = lambda i: (i, 0)
    const2 = lambda i: (0, 0)
    return pl.pallas_call(
        _out_proj_kernel,
        grid=(n // tm,),
        in_specs=[pl.BlockSpec((tm, d), row),
                  pl.BlockSpec((1, 1, d), lambda i: (i // nt, 0, 0)),
                  pl.BlockSpec((tm, NSA_WIDTH), row),
                  pl.BlockSpec((tm, NSA_WIDTH), lambda i: (i, nzb)),
                  pl.BlockSpec((tm, D_MODEL), lambda i: (i, mnb)),
                  pl.BlockSpec((tm, D_MODEL), lambda i: (i, mgb)),
                  pl.BlockSpec((tm, GLA_WIDTH), row),
                  pl.BlockSpec((NSA_WIDTH, d), const2),
                  pl.BlockSpec((GLA_WIDTH, d), const2),
                  pl.BlockSpec((d, d), const2),
                  pl.BlockSpec((1, d), const2)],
        out_specs=pl.BlockSpec((tm, d), row),
        out_shape=jax.ShapeDtypeStruct((n, d), F32),
        compiler_params=pltpu.CompilerParams(dimension_semantics=("parallel",),
                                             vmem_limit_bytes=60 * 1024 * 1024),
        name="out_proj",
    )(x2, gate, o_nsa, pm, pm, pm, yg, wn, wg, wo, fg)


def _importance_matrix(nbp, ncp):
    ratio = SLC_BLOCK // CMP_STRIDE
    n_sub = CMP_BLOCK // CMP_STRIDE
    w = np.zeros((nbp, ncp), np.float32)
    for blk in range(nbp):
        for m in range(ratio):
            for s in range(n_sub):
                n = ratio * blk + m - s
                if 0 <= n < ncp:
                    w[blk, n] += 1.0
    return w


def _main_cols(a):
    parts = [a[..., start:start + width] * (Q_SCALE if nm == "nsa_q" else 1.0)
             for nm, pieces in _MAIN_PIECES for start, width in pieces]
    return jnp.concatenate(parts, axis=-1)


def _small_cols(a):
    pad = SMALL_WIDTH - 3 * NSA_HEADS - GLA_GATE_RANK
    parts = [a[..., _OLD["nsa_g"][0]:_OLD["nsa_g"][0] + 3 * NSA_HEADS],
             a[..., _OLD["ga"][0]:_OLD["ga"][0] + GLA_GATE_RANK],
             jnp.zeros(a.shape[:-1] + (pad,), a.dtype)]
    return jnp.concatenate(parts, axis=-1)


def kernel(x, c, w_ada, b_ada, norm_gain, w_in, b_in, cmp_pos_k, cmp_pos_v, cmp_w1_k, cmp_w2_k, cmp_w1_v, cmp_w2_v,
           gla_w_alpha, gla_b_alpha, gla_norm_gain, w_br_nsa, w_br_gla, w_out, final_norm_gain):
    assert DEPTH == 1, "the final rmsnorm is fused into the single layer's output kernel"
    B, T, D = x.shape
    G, HPG, DH = NSA_KV_GROUPS, NSA_HPG, NSA_HEAD_DIM
    N = B * T
    nch = T // CMP_STRIDE
    nc = nch - CMP_BLOCK // CMP_STRIDE + 1
    nb = T // SLC_BLOCK
    nbp = KSEL_FEAT - QFEAT
    assert nb <= nbp and T % NSA_TILE == 0
    n_sel = min(N_SELECT, nb)
    tk = NSA_TILE
    l = 0

    slopes = 2.0 ** (-8.0 * jnp.arange(1, NSA_HEADS + 1, dtype=F32) / NSA_HEADS)
    wimp = jnp.asarray(_importance_matrix(nbp, nch), BF16)
    tri = jnp.asarray(np.tril(np.ones((GLA_CHUNK, GLA_CHUNK), np.float32)), BF16)
    bp = ((B + 7) // 8) * 8
    c8 = jnp.pad(c, ((0, bp - B), (0, 0)))
    x2 = x.reshape(N, D)

    mod = _ada(c8, w_ada[l], b_ada[l][None, :])
    shift = mod[:B, :D].reshape(B, 1, D)
    scale = mod[:B, D:2 * D].reshape(B, 1, D)
    gate = mod[:B, 2 * D:].reshape(B, 1, D)

    w_in_t = jnp.transpose(w_in[l])
    pm, ps = _in_proj(x2, norm_gain[l][None, :], shift, scale,
                      _wprep(w_in_t), _main_cols(b_in[l])[None, :],
                      _wprep_small(w_in_t), _small_cols(b_in[l])[None, :], T=T)

    def cols(name, width=NSA_KV_WIDTH):
        return pm[:, _NEW[name]:_NEW[name] + width]

    kvf = cols("ck", 2 * NSA_KV_WIDTH).reshape(B, nch, CMP_STRIDE, 2, G, DH)
    kvf = kvf.transpose(3, 0, 4, 1, 2, 5).reshape(2, B, G, nch, CMP_STRIDE * DH)
    posf = jnp.stack([cmp_pos_k[l], cmp_pos_v[l]]).reshape(2, 1, CMP_BLOCK * DH)
    posf = jnp.broadcast_to(posf, (2, 8, CMP_BLOCK * DH)).astype(BF16)
    w1 = jnp.stack([cmp_w1_k[l], cmp_w1_v[l]]).astype(BF16)
    w2 = jnp.stack([cmp_w2_k[l], cmp_w2_v[l]]).astype(BF16)
    kvc = _compress(kvf, posf, w1, w2)
    gt = ps[:, :3 * NSA_HEADS].reshape(B, T, G, 3 * HPG).transpose(0, 2, 3, 1)
    slopes2 = slopes * LOG2E
    o_nsa = _nsa(slopes2, pm, kvc[0], kvc[1].transpose(0, 1, 3, 2), gt, wimp, B=B, T=T, nc=nc, n_sel=n_sel)

    wal = jnp.pad(gla_w_alpha[l], ((GA_LANE, SMALL_WIDTH - GA_LANE - GLA_GATE_RANK), (0, 0)))
    y_gla = _gla(pm, ps, wal, gla_b_alpha[l][None, :], gla_norm_gain[l][None, :], tri, B=B, T=T)

    out = _out_proj(x2, gate, o_nsa, pm, y_gla,
                    w_br_nsa[l].astype(BF16), w_br_gla[l].astype(BF16), w_out[l].astype(BF16),
                    final_norm_gain[None, :], T=T)
    return out.reshape(B, T, D)
```

```python
import functools

import numpy as np
import jax
import jax.numpy as jnp
from jax import lax
from jax.experimental import pallas as pl
from jax.experimental.pallas import tpu as pltpu

D_MODEL = 2048
DEPTH = 1
NSA_HEADS = 16
NSA_KV_GROUPS = 4
NSA_HPG = NSA_HEADS // NSA_KV_GROUPS
NSA_HEAD_DIM = 64
CMP_BLOCK = 32
CMP_STRIDE = 16
CMP_HIDDEN = 256
SLC_BLOCK = 64
N_SELECT = 16
WINDOW = 512
NSA_WIDTH = NSA_HEADS * NSA_HEAD_DIM
NSA_KV_WIDTH = NSA_KV_GROUPS * NSA_HEAD_DIM
GLA_HEADS = 4
GLA_KEY_DIM = 128
GLA_VAL_DIM = 256
GLA_GATE_RANK = 16
GLA_TAU = 16.0
GLA_KEY_WIDTH = GLA_HEADS * GLA_KEY_DIM
GLA_WIDTH = GLA_HEADS * GLA_VAL_DIM
EPS = 1e-6
NEG = -1e30

F32 = jnp.float32
BF16 = jnp.bfloat16

_OLD = {}
_off = 0
for _name, _w in (("nsa_q", NSA_WIDTH), ("ck", NSA_KV_WIDTH), ("cv", NSA_KV_WIDTH), ("sk", NSA_KV_WIDTH),
                  ("sv", NSA_KV_WIDTH), ("wk", NSA_KV_WIDTH), ("wv", NSA_KV_WIDTH), ("nsa_g", 3 * NSA_HEADS),
                  ("nsa_z", NSA_WIDTH), ("gq", GLA_KEY_WIDTH), ("gk", GLA_KEY_WIDTH), ("gv", GLA_WIDTH),
                  ("ga", GLA_GATE_RANK), ("gla_z", GLA_WIDTH), ("mg_nsa", D_MODEL), ("mg_gla", D_MODEL)):
    _OLD[_name] = (_off, _w)
    _off += _w

def _pair_by_group(a, b):
    return [(_OLD[nm][0] + g * NSA_HEAD_DIM, NSA_HEAD_DIM) for g in range(NSA_KV_GROUPS) for nm in (a, b)]


_MAIN_PIECES = [(nm, [_OLD[nm]]) for nm in
                ("nsa_q", "nsa_z", "gla_z", "gv", "mg_nsa", "mg_gla", "gq", "gk", "ck", "cv")]
_MAIN_PIECES += [("sel_win_k", _pair_by_group("sk", "wk")), ("sel_win_v", _pair_by_group("sv", "wv"))]
_NEW = {}
_off = 0
for _name, _pieces in _MAIN_PIECES:
    _NEW[_name] = _off
    _off += sum(w for _, w in _pieces)
MAIN_WIDTH = _off
SMALL_WIDTH = 128
GA_LANE = 3 * NSA_HEADS

LANE = 128
GLA_CHUNK = 64
GLA_SUB = 16
GLA_PAIR = 4
NSA_TILE = 256
QFEAT = 2 * NSA_HEAD_DIM
KSEL_FEAT = QFEAT + LANE
N_SLOPE_TERMS = 3
SOFTMAX_STRIP = NSA_TILE
SEL_MASK = 16384.0
BITS_PER_WORD = 16
LOG2E = 1.4426950408889634
Q_SCALE = LOG2E * NSA_HEAD_DIM ** -0.5


def _dot(a, b):
    return jnp.dot(a, b, preferred_element_type=F32)


def _dot_nt(a, b):
    return lax.dot_general(a, b, (((1,), (1,)), ((), ())), preferred_element_type=F32)


def _split3(x):
    hi = x.astype(BF16)
    r = x - hi.astype(F32)
    mid = r.astype(BF16)
    lo = (r - mid.astype(F32)).astype(BF16)
    return hi, mid, lo


def _sigmoid(x):
    return 1.0 / (1.0 + jnp.exp(-x))


def _ada_kernel(c_ref, w_ref, b_ref, o_ref):
    ch, cm, cl = _split3(c_ref[...])
    wh, wm, wl = _split3(w_ref[...])
    acc = _dot(ch, wh) + _dot(ch, wm) + _dot(cm, wh) + _dot(ch, wl) + _dot(cl, wh) + _dot(cm, wm)
    o_ref[...] = acc + b_ref[...]


def _ada(c8, w, b, *, tn=768):
    m, d = c8.shape
    n = w.shape[1]
    return pl.pallas_call(
        _ada_kernel,
        grid=(n // tn,),
        in_specs=[pl.BlockSpec((m, d), lambda j: (0, 0)),
                  pl.BlockSpec((d, tn), lambda j: (0, j)),
                  pl.BlockSpec((1, tn), lambda j: (0, j))],
        out_specs=pl.BlockSpec((m, tn), lambda j: (0, j)),
        out_shape=jax.ShapeDtypeStruct((m, n), F32),
        compiler_params=pltpu.CompilerParams(dimension_semantics=("arbitrary",),
                                             vmem_limit_bytes=40 * 1024 * 1024),
        name="ada",
    )(c8, w, b)


WPREP_TILE = 512


def _wprep_tiles():
    tiles = []
    for name, pieces in _MAIN_PIECES:
        scale = Q_SCALE if name == "nsa_q" else 1.0
        if len(pieces) == 1:
            start, width = pieces[0]
            assert width % WPREP_TILE == 0 or (name in ("ck", "cv") and width * 2 == WPREP_TILE)
            if name == "cv":
                assert start == _OLD["ck"][0] + _OLD["ck"][1]
                continue
            for off in range(0, max(width, WPREP_TILE), WPREP_TILE):
                tiles.append((start + off, (start + off) % LANE, scale))
        else:
            first, second = pieces[0][0], pieces[1][0]
            assert second == first + WPREP_TILE and first % LANE == 0 and len(pieces) * NSA_HEAD_DIM == WPREP_TILE
            tiles.append((first, -1, scale))
    return tiles


def _wprep_kernel(start_ref, kind_ref, scale_ref, a_ref, b_ref, o_ref):
    j = pl.program_id(0)
    scale = scale_ref[j]

    @pl.when(kind_ref[j] >= 0)
    def _():
        o_ref[...] = (a_ref[...] * scale).astype(o_ref.dtype)

    @pl.when(kind_ref[j] < 0)
    def _():
        half = NSA_HEAD_DIM
        for g in range(NSA_KV_GROUPS):
            o_ref[2 * g * half:(2 * g + 1) * half, :] = (a_ref[g * half:(g + 1) * half, :] * scale).astype(o_ref.dtype)
            o_ref[(2 * g + 1) * half:(2 * g + 2) * half, :] = (b_ref[g * half:(g + 1) * half, :] * scale).astype(o_ref.dtype)


def _wprep(wt):
    width, d = wt.shape
    tiles = _wprep_tiles()
    half_tile = WPREP_TILE // 2
    assert len(tiles) * WPREP_TILE == MAIN_WIDTH and all(t[0] % 8 == 0 for t in tiles)
    start = jnp.asarray([t[0] for t in tiles], jnp.int32)
    kind = jnp.asarray([t[1] for t in tiles], jnp.int32)
    scale = jnp.asarray([t[2] for t in tiles], F32)
    return pl.pallas_call(
        _wprep_kernel,
        grid_spec=pltpu.PrefetchScalarGridSpec(
            num_scalar_prefetch=2,
            grid=(len(tiles),),
            in_specs=[pl.BlockSpec(memory_space=pltpu.SMEM),
                      pl.BlockSpec((pl.Element(WPREP_TILE), pl.Element(d)),
                                   lambda j, st, kd: (pl.multiple_of(st[j], 8), 0)),
                      pl.BlockSpec((pl.Element(half_tile), pl.Element(d)),
                                   lambda j, st, kd: (pl.multiple_of(
                                       jnp.minimum(st[j] + WPREP_TILE, width - half_tile), 8), 0))],
            out_specs=pl.BlockSpec((WPREP_TILE, d), lambda j, st, kd: (j, 0))),
        out_shape=jax.ShapeDtypeStruct((MAIN_WIDTH, d), BF16),
        compiler_params=pltpu.CompilerParams(dimension_semantics=("arbitrary",),
                                             vmem_limit_bytes=40 * 1024 * 1024),
        name="wprep",
    )(start, kind, scale, wt, wt)


def _wprep_small_kernel(g_ref, a_ref, o_ref):
    n_gate = 3 * NSA_HEADS
    rows = jnp.concatenate([g_ref[0:n_gate, :], a_ref[...],
                            jnp.zeros((SMALL_WIDTH - n_gate - GLA_GATE_RANK, g_ref.shape[1]), F32)], axis=0)
    o_ref[...] = rows.astype(o_ref.dtype)


def _wprep_small(wt):
    d = wt.shape[1]
    g0, a0 = _OLD["nsa_g"][0], _OLD["ga"][0]
    assert g0 % 8 == 0 and a0 % 8 == 0
    return pl.pallas_call(
        _wprep_small_kernel,
        grid=(1,),
        in_specs=[pl.BlockSpec((pl.Element(NSA_HEAD_DIM), pl.Element(d)), lambda i: (g0, 0)),
                  pl.BlockSpec((pl.Element(GLA_GATE_RANK), pl.Element(d)), lambda i: (a0, 0))],
        out_specs=pl.BlockSpec((SMALL_WIDTH, d), lambda i: (0, 0)),
        out_shape=jax.ShapeDtypeStruct((SMALL_WIDTH, d), BF16),
        name="wprep_small",
    )(wt, wt)


def _in_proj_kernel(x_ref, gain_ref, shift_ref, scale_ref, w_ref, b_ref, ws_ref, bs_ref,
                    o_ref, os_ref, h_ref):
    j = pl.program_id(1)

    @pl.when(j == 0)
    def _():
        x = x_ref[...]
        ms = jnp.mean(x * x, axis=-1, keepdims=True)
        y = x * lax.rsqrt(ms + EPS) * gain_ref[...]
        y = y * (1.0 + scale_ref[0]) + shift_ref[0]
        h = y.astype(BF16)
        h_ref[...] = h
        os_ref[...] = _dot_nt(h, ws_ref[...]) + bs_ref[...]

    o_ref[...] = (_dot_nt(h_ref[...], w_ref[...]) + b_ref[...]).astype(o_ref.dtype)


def _in_proj(x2, gain, shift, scale, w_main, b_main, w_small, b_small, *, T, tm=1024, tn=1792):
    n, d = x2.shape
    nm = w_main.shape[0]
    nt = T // tm
    return pl.pallas_call(
        _in_proj_kernel,
        grid=(n // tm, nm // tn),
        in_specs=[pl.BlockSpec((tm, d), lambda i, j: (i, 0)),
                  pl.BlockSpec((1, d), lambda i, j: (0, 0)),
                  pl.BlockSpec((1, 1, d), lambda i, j: (i // nt, 0, 0)),
                  pl.BlockSpec((1, 1, d), lambda i, j: (i // nt, 0, 0)),
                  pl.BlockSpec((tn, d), lambda i, j: (j, 0)),
                  pl.BlockSpec((1, tn), lambda i, j: (0, j)),
                  pl.BlockSpec((SMALL_WIDTH, d), lambda i, j: (0, 0)),
                  pl.BlockSpec((1, SMALL_WIDTH), lambda i, j: (0, 0))],
        out_specs=[pl.BlockSpec((tm, tn), lambda i, j: (i, j)),
                   pl.BlockSpec((tm, SMALL_WIDTH), lambda i, j: (i, 0))],
        out_shape=[jax.ShapeDtypeStruct((n, nm), BF16),
                   jax.ShapeDtypeStruct((n, SMALL_WIDTH), F32)],
        scratch_shapes=[pltpu.VMEM((tm, d), BF16)],
        compiler_params=pltpu.CompilerParams(dimension_semantics=("parallel", "arbitrary"),
                                             vmem_limit_bytes=56 * 1024 * 1024),
        name="in_proj",
    )(x2, gain, shift, scale, w_main, b_main, w_small, b_small)


def _compress_kernel(kv_ref, pos_ref, w1_ref, w2_ref, o_ref):
    y = kv_ref[0, 0, 0]
    w1 = w1_ref[0]
    half = y.shape[1]
    nch = y.shape[0]
    z1 = _dot(y, w1[:half])
    z2 = _dot(y, w1[half:])
    posb = _dot(pos_ref[0], w1)[0:1]
    pre = z1 + pltpu.roll(z2, nch - 1, 0) + posb
    hid = pre * _sigmoid(pre)
    o_ref[0, 0, 0] = _dot(hid.astype(BF16), w2_ref[0]).astype(o_ref.dtype)


def _compress(kvf, posf, w1, w2):
    two, b, g, nch, half = kvf.shape
    hidden = w1.shape[2]
    dh = w2.shape[2]
    return pl.pallas_call(
        _compress_kernel,
        grid=(two, b, g),
        in_specs=[pl.BlockSpec((1, 1, 1, nch, half), lambda s, i, j: (s, i, j, 0, 0)),
                  pl.BlockSpec((1, 8, 2 * half), lambda s, i, j: (s, 0, 0)),
                  pl.BlockSpec((1, 2 * half, hidden), lambda s, i, j: (s, 0, 0)),
                  pl.BlockSpec((1, hidden, dh), lambda s, i, j: (s, 0, 0))],
        out_specs=pl.BlockSpec((1, 1, 1, nch, dh), lambda s, i, j: (s, i, j, 0, 0)),
        out_shape=jax.ShapeDtypeStruct((two, b, g, nch, dh), BF16),
        compiler_params=pltpu.CompilerParams(dimension_semantics=("parallel", "parallel", "parallel")),
        name="compress",
    )(kvf, posf, w1, w2)


def _select_blocks(imp, t_row, n_sel, three_forced):
    nbp = imp.shape[0]
    blk = lax.broadcasted_iota(jnp.int32, (nbp, 1), 0)
    cur = lax.shift_right_logical(t_row, 6)
    forced = (blk == 0) | (blk == cur) | (blk == cur - 1)
    bvalid = blk * SLC_BLOCK <= t_row
    rest = jnp.where(bvalid, imp, -1.0)
    blk_f = blk.astype(F32)
    if three_forced:
        score = jnp.where(forced, -1.0, rest)
        sel = jnp.where(forced, 1.0, 0.0)
        rounds = n_sel - 3
    else:
        score = jnp.where(forced, 1e30, rest)
        sel = jnp.zeros(imp.shape, F32)
        rounds = n_sel
    for _ in range(rounds):
        m = jnp.max(score, axis=0, keepdims=True)
        first = jnp.min(jnp.where(score == m, blk_f, float(nbp)), axis=0, keepdims=True)
        hit = blk_f == first
        sel = jnp.where(hit, 1.0, sel)
        score = jnp.where(hit, -1.0, score)
    return sel


def _softmax_step(s_of, vt, d, m_sc, l_sc, acc_sc, s_max=None, edge=None):
    half = SOFTMAX_STRIP // 2
    if edge is not None:
        r_i = lax.broadcasted_iota(jnp.int32, (half, half), 0)
        c_i = lax.broadcasted_iota(jnp.int32, (half, half), 1)
        tri = (r_i <= c_i) if edge == "diag" else (r_i > c_i)
    for c0 in range(0, d.shape[1], SOFTMAX_STRIP):
        ls = slice(c0, c0 + SOFTMAX_STRIP)
        s = s_of(ls)
        dd = d[:, ls]
        m_prev = m_sc[:, ls]
        if edge is None:
            mx = jnp.max(s, axis=0, keepdims=True) if s_max is None else s_max[:, ls]
            m_new = jnp.maximum(m_prev, mx - dd)
            p = jnp.exp2(s - (m_new + dd))
            p_sum = jnp.sum(p, axis=0, keepdims=True)
        else:
            top_left = jnp.where(tri, s[:half, :half], NEG)
            bot_right = jnp.where(tri, s[half:, half:], NEG)
            full = s[:half, half:] if edge == "diag" else s[half:, :half]
            mx_tl = jnp.max(top_left, axis=0, keepdims=True)
            mx_br = jnp.max(bot_right, axis=0, keepdims=True)
            mx_full = jnp.max(full, axis=0, keepdims=True)
            if edge == "diag":
                mx = jnp.concatenate([mx_tl, jnp.maximum(mx_full, mx_br)], axis=1)
            else:
                mx = jnp.concatenate([jnp.maximum(mx_tl, mx_full), mx_br], axis=1)
            m_new = jnp.maximum(m_prev, mx - dd)
            shift = m_new + dd
            p_tl = jnp.exp2(top_left - shift[:, :half])
            p_br = jnp.exp2(bot_right - shift[:, half:])
            dead = jnp.zeros((half, half), F32)
            if edge == "diag":
                p_full = jnp.exp2(full - shift[:, half:])
                p = jnp.concatenate([jnp.concatenate([p_tl, p_full], axis=1),
                                     jnp.concatenate([dead, p_br], axis=1)], axis=0)
                p_sum = jnp.concatenate([jnp.sum(p_tl, axis=0, keepdims=True),
                                         jnp.sum(p_full, axis=0, keepdims=True)
                                         + jnp.sum(p_br, axis=0, keepdims=True)], axis=1)
            else:
                p_full = jnp.exp2(full - shift[:, :half])
                p = jnp.concatenate([jnp.concatenate([p_tl, dead], axis=1),
                                     jnp.concatenate([p_full, p_br], axis=1)], axis=0)
                p_sum = jnp.concatenate([jnp.sum(p_tl, axis=0, keepdims=True)
                                         + jnp.sum(p_full, axis=0, keepdims=True),
                                         jnp.sum(p_br, axis=0, keepdims=True)], axis=1)
        alpha = jnp.exp2(m_prev - m_new)
        l_sc[:, ls] = alpha * l_sc[:, ls] + p_sum
        acc_sc[:, ls] = alpha * acc_sc[:, ls] + _dot(vt, p.astype(BF16))
        m_sc[:, ls] = m_new


def _stage_group(qslab_ref, kslab_ref, vslab_ref, qt_sc, ks_sc, kw_sc, vt_sc, nt, tk):
    dh = NSA_HEAD_DIM
    lane = lax.broadcasted_iota(jnp.int32, (tk, LANE), 1)
    row = lax.broadcasted_iota(jnp.int32, (tk, LANE), 0)
    pos = row.astype(F32)
    feat_sel = jnp.where((lane >= dh) & (lane < dh + N_SLOPE_TERMS), pos, 0.0)
    feat_win = jnp.where(lane < N_SLOPE_TERMS, pos, 0.0)
    blk_in_tile = lax.shift_right_logical(row, 6)

    def body(kt, carry):
        start = pl.multiple_of(kt * tk, tk)
        kk = kslab_ref[pl.ds(start, tk), :].astype(F32)
        onehot = jnp.where(kt * (tk // SLC_BLOCK) + blk_in_tile == lane, 1.0, 0.0)
        ks_sc[kt] = jnp.concatenate([jnp.where(lane < dh, kk, feat_sel), onehot], axis=1).astype(BF16)
        kw_sc[kt] = jnp.where(lane >= dh, kk, feat_win).astype(BF16)
        vt_sc[kt] = vslab_ref[pl.ds(start, tk), :].astype(F32).T.astype(BF16)
        x_t = qslab_ref[pl.ds(start, tk), :].astype(F32).T
        qt_sc[kt] = jnp.concatenate([x_t[hh * dh:(hh + 1) * dh] for hh in range(NSA_HPG)], axis=1).astype(BF16)
        return carry

    lax.fori_loop(0, nt, body, 0)


def _nsa_kernel(slopes_ref, q_ref, kslab_ref, vslab_ref, kc_ref, vct_ref, gt_ref, wimp_ref,
                o_ref, q2_sc, qw_sc, ks_sc, kw_sc, qt_sc, vt_sc, sa_sc, sb_sc, m_sc, l_sc, acc_sc, br_sc, list_sc,
                words_sc, *, tq, nt, nc, n_sel):
    g = pl.program_id(1)
    qi = pl.program_id(2)
    tk = tq
    dh = NSA_HEAD_DIM
    hq = NSA_HPG * tq

    @pl.when(qi == 0)
    def _():
        _stage_group(q_ref, kslab_ref, vslab_ref, qt_sc, ks_sc, kw_sc, vt_sc, nt, tk)

    lane = lax.broadcasted_iota(jnp.int32, (1, hq), 1)
    slope_row = jnp.zeros((1, hq), F32)
    for hh in range(NSA_HPG):
        slope_row = jnp.where(lane >= hh * tq, slopes_ref[g * NSA_HPG + hh], slope_row)
    t_one = qi * tq + lax.broadcasted_iota(jnp.int32, (1, tq), 1)
    t_row = jnp.concatenate([t_one] * NSA_HPG, axis=1)
    t_f = t_row.astype(F32)

    q_t = qt_sc[qi]
    feat_row = lax.broadcasted_iota(jnp.int32, (dh, hq), 0)
    terms = [t.astype(F32) for t in _split3(slope_row)]
    feat = jnp.zeros((dh, hq), F32)
    for i, term in enumerate(terms):
        feat = jnp.where(feat_row == i, term, feat)
    feat = feat.astype(BF16)
    q2_sc[0:dh, :] = q_t
    q2_sc[dh:QFEAT, :] = feat
    qw_sc[0:dh, :] = feat
    qw_sc[dh:QFEAT, :] = q_t

    def reset_stats():
        m_sc[...] = jnp.full(m_sc.shape, NEG, F32)
        l_sc[...] = jnp.zeros(l_sc.shape, F32)
        acc_sc[...] = jnp.zeros(acc_sc.shape, F32)

    def per_head(a):
        return [a[:, hh * tq:(hh + 1) * tq] for hh in range(NSA_HPG)]

    def shift_of(kv, off=None):
        d = slope_row * (t_f - (kv * tk).astype(F32))
        return d if off is None else d + jnp.where(off, -NEG, 0.0)

    per_tile = tk // SLC_BLOCK
    nbp = KSEL_FEAT - QFEAT
    n_words = -(-nt * per_tile // BITS_PER_WORD)

    def front(nk, nblk, three_forced):
        kc = kc_ref[0, 0, 0:nk, :]
        n_col = lax.broadcasted_iota(jnp.int32, (nk, 1), 0)
        ce = jnp.where(n_col < nc, n_col * CMP_STRIDE + (CMP_BLOCK - 1), 2 ** 30)
        ce_rel = (ce - qi * tq).astype(F32)
        s = jnp.where(ce <= t_row, _dot(kc, q_t) + slope_row * ce_rel, NEG)
        m = jnp.max(s, axis=0, keepdims=True)
        p = jnp.exp2(s - m)
        l = jnp.sum(p, axis=0, keepdims=True)
        p = p * jnp.where(t_row >= CMP_BLOCK - 1, 1.0 / l, 0.0)
        br_sc[0] = _dot(vct_ref[0, 0, :, 0:nk], p.astype(BF16))
        ps4 = per_head(p)
        psum = (ps4[0] + ps4[1]) + (ps4[2] + ps4[3])
        wimp = wimp_ref[0:nblk, 0:nk]
        ph, pm, plo = _split3(psum)
        imp = _dot(wimp, ph) + _dot(wimp, pm) + _dot(wimp, plo)

        reset_stats()
        n_back = WINDOW // tk
        qw = qw_sc[...]
        tiles = [jnp.maximum(qi - w, 0) for w in range(n_back + 1)]
        logits = [_dot(kw_sc[kv], qw) for kv in tiles]
        assert WINDOW == n_back * tk and tk == SOFTMAX_STRIP
        for w, (kv, s) in enumerate(zip(tiles, logits)):
            edge = "diag" if w == 0 else ("far" if w == n_back else None)
            _softmax_step(lambda ls, s=s: s[:, ls], vt_sc[kv, dh:2 * dh, :], shift_of(kv, qi < w),
                          m_sc, l_sc, acc_sc, edge=edge)
        br_sc[1] = acc_sc[...] * (1.0 / l_sc[...])

        s0 = _dot(ks_sc[0, :, 0:QFEAT], q2_sc[0:QFEAT, :])

        sel = _select_blocks(imp, t_one, n_sel, three_forced)
        penalty_f = (sel - 1.0) * SEL_MASK
        q2_sc[QFEAT:QFEAT + nblk, :] = jnp.concatenate([penalty_f.astype(BF16)] * NSA_HPG, axis=1)
        if nblk < nbp:
            q2_sc[QFEAT + nblk:KSEL_FEAT, :] = jnp.full((nbp - nblk, hq), -SEL_MASK, BF16)
        pen0 = jnp.concatenate([penalty_f[0:per_tile]] * NSA_HPG, axis=1)
        pen0 = jnp.concatenate([jnp.broadcast_to(pen0[b:b + 1], (SLC_BLOCK, hq)) for b in range(per_tile)], axis=0)
        s0 = s0 + pen0
        sa_sc[0:tk, :] = s0
        sa_sc[tk:tk + 1, :] = jnp.max(s0, axis=0, keepdims=True)

        any_tok = jnp.max(sel, axis=1, keepdims=True)
        blk_col = lax.broadcasted_iota(jnp.int32, any_tok.shape, 0)
        bit = jnp.left_shift(1, jnp.bitwise_and(blk_col, BITS_PER_WORD - 1)).astype(F32)
        bits = jnp.where(any_tok > 0.5, bit, 0.0)
        for w in range(n_words):
            lo, hi = w * BITS_PER_WORD, min((w + 1) * BITS_PER_WORD, nblk)
            words_sc[w] = jnp.sum(bits[lo:hi]).astype(jnp.int32) if lo < nblk else jnp.int32(0)

    n_var = 4
    span = nt // n_var
    ncp = kc_ref.shape[2]
    first_three = -(-2 * SLC_BLOCK // tq)
    assert first_three <= span and n_sel >= 3
    for v in range(1, n_var + 1):
        lo = (v - 1) * span
        if lo < first_three:
            @pl.when(qi < first_three)
            def _(v=v):
                front(ncp * v // n_var, (nt * per_tile) * v // n_var, False)
            lo = first_three

        @pl.when((qi >= lo) & (qi < v * span))
        def _(v=v):
            front(ncp * v // n_var, (nt * per_tile) * v // n_var, True)

    words = [words_sc[w] for w in range(n_words)]
    for kt in range(nt + 2):
        list_sc[kt] = qi
    n_act = jnp.int32(0)
    tile_mask = (1 << per_tile) - 1
    for kt in range(nt):
        list_sc[n_act] = kt
        w, sh = divmod(kt * per_tile, BITS_PER_WORD)
        hit = jnp.bitwise_and(words[w] >> sh, tile_mask) != 0
        n_act = n_act + hit.astype(jnp.int32)
    list_sc[n_act] = qi
    n_plain = n_act - 1

    reset_stats()

    def sel_logits(idx, s_ref):
        s = _dot(ks_sc[list_sc[idx]], q2_sc[...])
        s_ref[0:tk, :] = s
        s_ref[tk:tk + 1, :] = jnp.max(s, axis=0, keepdims=True)

    def sel_update(idx, s_ref):
        kv = list_sc[idx]
        _softmax_step(lambda ls: s_ref[0:tk, ls], vt_sc[kv, 0:dh, :], shift_of(kv),
                      m_sc, l_sc, acc_sc, s_max=s_ref[tk:tk + 1, :])

    def diag_update(s_ref):
        _softmax_step(lambda ls: s_ref[0:tk, ls], vt_sc[qi, 0:dh, :], shift_of(qi), m_sc, l_sc, acc_sc, edge="diag")

    def pair_body(i, carry):
        sel_logits(2 * i + 1, sb_sc)
        sel_update(2 * i, sa_sc)
        sel_logits(2 * i + 2, sa_sc)
        sel_update(2 * i + 1, sb_sc)
        return carry

    lax.fori_loop(0, lax.shift_right_logical(n_plain, 1), pair_body, 0)
    odd = jnp.bitwise_and(n_plain, 1) == 1

    @pl.when(jnp.logical_not(odd))
    def _():
        diag_update(sa_sc)

    @pl.when(odd)
    def _():
        sel_logits(n_plain, sb_sc)
        sel_update(n_plain - 1, sa_sc)
        diag_update(sb_sc)

    o_sel = acc_sc[...] * (1.0 / l_sc[...])

    sg = _sigmoid(gt_ref[0, 0])
    gates = [jnp.concatenate([sg[3 * hh + br:3 * hh + br + 1] for hh in range(NSA_HPG)], axis=1)
             for br in range(3)]
    o_all = gates[0] * br_sc[0] + gates[1] * o_sel + gates[2] * br_sc[1]
    o_ref[...] = jnp.concatenate(per_head(o_all), axis=0).T.astype(o_ref.dtype)


def _nsa(slopes, pm, kc, vct, gt, wimp, *, B, T, nc, n_sel):
    b, g, hpg, dh = B, NSA_KV_GROUPS, NSA_HPG, NSA_HEAD_DIM
    tq = NSA_TILE
    nt = T // tq
    hq = hpg * tq
    ncp = kc.shape[2]
    nbp = wimp.shape[0]
    qb = _NEW["nsa_q"] // (hpg * dh)
    kb, vb = _NEW["sel_win_k"] // LANE, _NEW["sel_win_v"] // LANE
    kern = functools.partial(_nsa_kernel, tq=tq, nt=nt, nc=nc, n_sel=n_sel)
    return pl.pallas_call(
        kern,
        grid=(b, g, nt),
        in_specs=[pl.BlockSpec(memory_space=pltpu.SMEM),
                  pl.BlockSpec((T, hpg * dh), lambda i, j, k: (i, qb + j)),
                  pl.BlockSpec((T, LANE), lambda i, j, k: (i, kb + j)),
                  pl.BlockSpec((T, LANE), lambda i, j, k: (i, vb + j)),
                  pl.BlockSpec((1, 1, ncp, dh), lambda i, j, k: (i, j, 0, 0)),
                  pl.BlockSpec((1, 1, dh, ncp), lambda i, j, k: (i, j, 0, 0)),
                  pl.BlockSpec((1, 1, 3 * hpg, tq), lambda i, j, k: (i, j, 0, k)),
                  pl.BlockSpec((nbp, ncp), lambda i, j, k: (0, 0))],
        out_specs=pl.BlockSpec((tq, hpg * dh), lambda i, j, k: (i * nt + k, j)),
        out_shape=jax.ShapeDtypeStruct((b * T, g * hpg * dh), BF16),
        scratch_shapes=[pltpu.VMEM((KSEL_FEAT, hq), BF16), pltpu.VMEM((QFEAT, hq), BF16),
                        pltpu.VMEM((nt, tq, KSEL_FEAT), BF16), pltpu.VMEM((nt, tq, QFEAT), BF16),
                        pltpu.VMEM((nt, dh, hq), BF16), pltpu.VMEM((nt, 2 * dh, tq), BF16),
                        pltpu.VMEM((tq + 8, hq), F32), pltpu.VMEM((tq + 8, hq), F32),
                        pltpu.VMEM((1, hq), F32), pltpu.VMEM((1, hq), F32),
                        pltpu.VMEM((dh, hq), F32),
                        pltpu.VMEM((2, dh, hq), F32),
                        pltpu.SMEM((nt + 2,), jnp.int32),
                        pltpu.SMEM((-(-nt * (tq // SLC_BLOCK) // BITS_PER_WORD),), jnp.int32)],
        compiler_params=pltpu.CompilerParams(dimension_semantics=("parallel", "parallel", "arbitrary"),
                                             vmem_limit_bytes=48 * 1024 * 1024),
        name="nsa",
    )(slopes, pm, pm, pm, kc, vct, gt, wimp)


def _gla_intra_scores(q, k, bcum):
    c, dk = q.shape
    sb = GLA_SUB
    row_c = lax.broadcasted_iota(jnp.int32, (c, dk), 0)
    row_s = lax.broadcasted_iota(jnp.int32, (sb, 1), 0)
    lane_s = lax.broadcasted_iota(jnp.int32, (sb, c), 1)
    blocks = []
    for i0 in range(0, c, sb):
        bi = bcum[i0:i0 + sb]
        qi = q[i0:i0 + sb]
        a_blk = jnp.zeros((sb, c), F32)
        for jl in range(sb):
            j = i0 + jl
            w = jnp.exp2(bi - bcum[j:j + 1])
            a = jnp.sum(qi * k[j:j + 1] * w, axis=-1, keepdims=True)
            a_blk = jnp.where(lane_s == j, jnp.where(row_s >= jl, a, 0.0), a_blk)
        if i0 > 0:
            r = bcum[i0 - 1:i0]
            qh = qi * jnp.exp2(bi - r)
            kh = k * jnp.exp2(jnp.where(row_c < i0, r - bcum, NEG))
            a_blk = a_blk + _dot_nt(qh.astype(BF16), kh.astype(BF16))
        blocks.append(a_blk)
    return jnp.concatenate(blocks, axis=0)


def _gla_kernel(q_ref, k_ref, v_ref, z_ref, ps_ref, wal_ref, bal_ref, gain_ref, tri_ref, o_ref,
                st_sc, *, ts):
    ti = pl.program_id(2)
    c = GLA_CHUNK
    dk, dv = GLA_KEY_DIM, GLA_VAL_DIM

    @pl.when(ti == 0)
    def _():
        st_sc[...] = jnp.zeros(st_sc.shape, F32)

    ps = ps_ref[...]
    ph, pm, _ = _split3(ps)
    wh, wm, _ = _split3(wal_ref[...])
    zz = _dot(ph, wh) + _dot(ph, wm) + _dot(pm, wh) + bal_ref[...]
    log_a = (jnp.minimum(zz, 0.0) - jnp.log(1.0 + jnp.exp(-jnp.abs(zz)))) * (LOG2E / GLA_TAU)
    tri = tri_ref[...]
    qscale = GLA_KEY_DIM ** -0.5
    for ci in range(ts // c):
        sl = slice(ci * c, (ci + 1) * c)
        for hh in range(GLA_PAIR):
            ks, vs = slice(hh * dk, (hh + 1) * dk), slice(hh * dv, (hh + 1) * dv)
            gh, gm, gl = _split3(log_a[sl, ks])
            bcum = _dot(tri, gh) + _dot(tri, gm) + _dot(tri, gl)
            q = q_ref[sl, ks].astype(F32) * qscale
            k = k_ref[sl, ks].astype(F32)
            v = v_ref[sl, vs]
            a_mat = _gla_intra_scores(q, k, bcum)
            st = st_sc[hh]
            o = _dot(a_mat.astype(BF16), v) + _dot_nt((q * jnp.exp2(bcum)).astype(BF16), st.astype(BF16))
            b_last = bcum[c - 1:c, :]
            kd = k * jnp.exp2(b_last - bcum)
            st_sc[hh] = st * jnp.exp2(b_last) + _dot(v.astype(F32).T.astype(BF16), kd.astype(BF16))
            ms = jnp.mean(o * o, axis=-1, keepdims=True)
            y = o * lax.rsqrt(ms + EPS) * gain_ref[...]
            z = z_ref[sl, vs].astype(F32)
            o_ref[sl, vs] = (y * (z * _sigmoid(z))).astype(o_ref.dtype)


def _gla(pm, ps, wal, bal, gain, tri, *, B, T, ts=1024):
    n = pm.shape[0]
    nts = T // ts
    hp = GLA_PAIR
    dk, dv = hp * GLA_KEY_DIM, hp * GLA_VAL_DIM
    qb, kb = _NEW["gq"] // dk, _NEW["gk"] // dk
    vb, zb = _NEW["gv"] // dv, _NEW["gla_z"] // dv
    kern = functools.partial(_gla_kernel, ts=ts)
    return pl.pallas_call(
        kern,
        grid=(B, GLA_HEADS // hp, nts),
        in_specs=[pl.BlockSpec((ts, dk), lambda b, h, i: (b * nts + i, qb + h)),
                  pl.BlockSpec((ts, dk), lambda b, h, i: (b * nts + i, kb + h)),
                  pl.BlockSpec((ts, dv), lambda b, h, i: (b * nts + i, vb + h)),
                  pl.BlockSpec((ts, dv), lambda b, h, i: (b * nts + i, zb + h)),
                  pl.BlockSpec((ts, SMALL_WIDTH), lambda b, h, i: (b * nts + i, 0)),
                  pl.BlockSpec((SMALL_WIDTH, dk), lambda b, h, i: (0, h)),
                  pl.BlockSpec((1, dk), lambda b, h, i: (0, h)),
                  pl.BlockSpec((1, GLA_VAL_DIM), lambda b, h, i: (0, 0)),
                  pl.BlockSpec((GLA_CHUNK, GLA_CHUNK), lambda b, h, i: (0, 0))],
        out_specs=pl.BlockSpec((ts, dv), lambda b, h, i: (b * nts + i, h)),
        out_shape=jax.ShapeDtypeStruct((n, GLA_WIDTH), BF16),
        scratch_shapes=[pltpu.VMEM((hp, GLA_VAL_DIM, GLA_KEY_DIM), F32)],
        compiler_params=pltpu.CompilerParams(dimension_semantics=("parallel", "parallel", "arbitrary"),
                                             vmem_limit_bytes=40 * 1024 * 1024),
        name="gla",
    )(pm, pm, pm, pm, ps, wal, bal, gain, tri)


def _out_proj_kernel(x_ref, gate_ref, on_ref, nz_ref, mgn_ref, mgg_ref, yg_ref,
                     wn_ref, wg_ref, wo_ref, fg_ref, o_ref):
    nz = nz_ref[...].astype(F32)
    y_nsa = on_ref[...].astype(F32) * (nz * _sigmoid(nz))
    a = _dot(y_nsa.astype(BF16), wn_ref[...])
    bm = _dot(yg_ref[...], wg_ref[...])
    merged = _sigmoid(mgn_ref[...].astype(F32)) * a + _sigmoid(mgg_ref[...].astype(F32)) * bm
    xn = x_ref[...] + gate_ref[0] * _dot(merged.astype(BF16), wo_ref[...])
    ms = jnp.mean(xn * xn, axis=-1, keepdims=True)
    o_ref[...] = xn * lax.rsqrt(ms + EPS) * fg_ref[...]


def _out_proj(x2, gate, o_nsa, pm, yg, wn, wg, wo, fg, *, T, tm=256):
    n, d = x2.shape
    nt = T // tm
    nzb = _NEW["nsa_z"] // NSA_WIDTH
    mnb = _NEW["mg_nsa"] // D_MODEL
    mgb = _NEW["mg_gla"] // D_MODEL
    row = lambda i: (i, 0)
    const2 = lambda i: (0, 0)
    return pl.pallas_call(
        _out_proj_kernel,
        grid=(n // tm,),
        in_specs=[pl.BlockSpec((tm, d), row),
                  pl.BlockSpec((1, 1, d), lambda i: (i // nt, 0, 0)),
                  pl.BlockSpec((tm, NSA_WIDTH), row),
                  pl.BlockSpec((tm, NSA_WIDTH), lambda i: (i, nzb)),
                  pl.BlockSpec((tm, D_MODEL), lambda i: (i, mnb)),
                  pl.BlockSpec((tm, D_MODEL), lambda i: (i, mgb)),
                  pl.BlockSpec((tm, GLA_WIDTH), row),
                  pl.BlockSpec((NSA_WIDTH, d), const2),
                  pl.BlockSpec((GLA_WIDTH, d), const2),
                  pl.BlockSpec((d, d), const2),
                  pl.BlockSpec((1, d), const2)],
        out_specs=pl.BlockSpec((tm, d), row),
        out_shape=jax.ShapeDtypeStruct((n, d), F32),
        compiler_params=pltpu.CompilerParams(dimension_semantics=("parallel",),
                                             vmem_limit_bytes=60 * 1024 * 1024),
        name="out_proj",
    )(x2, gate, o_nsa, pm, pm, pm, yg, wn, wg, wo, fg)


def _importance_matrix(nbp, ncp):
    ratio = SLC_BLOCK // CMP_STRIDE
    n_sub = CMP_BLOCK // CMP_STRIDE
    w = np.zeros((nbp, ncp), np.float32)
    for blk in range(nbp):
        for m in range(ratio):
            for s in range(n_sub):
                n = ratio * blk + m - s
                if 0 <= n < ncp:
                    w[blk, n] += 1.0
    return w


def _main_cols(a):
    parts = [a[..., start:start + width] * (Q_SCALE if nm == "nsa_q" else 1.0)
             for nm, pieces in _MAIN_PIECES for start, width in pieces]
    return jnp.concatenate(parts, axis=-1)


def _small_cols(a):
    pad = SMALL_WIDTH - 3 * NSA_HEADS - GLA_GATE_RANK
    parts = [a[..., _OLD["nsa_g"][0]:_OLD["nsa_g"][0] + 3 * NSA_HEADS],
             a[..., _OLD["ga"][0]:_OLD["ga"][0] + GLA_GATE_RANK],
             jnp.zeros(a.shape[:-1] + (pad,), a.dtype)]
    return jnp.concatenate(parts, axis=-1)


def kernel(x, c, w_ada, b_ada, norm_gain, w_in, b_in, cmp_pos_k, cmp_pos_v, cmp_w1_k, cmp_w2_k, cmp_w1_v, cmp_w2_v,
           gla_w_alpha, gla_b_alpha, gla_norm_gain, w_br_nsa, w_br_gla, w_out, final_norm_gain):
    assert DEPTH == 1, "the final rmsnorm is fused into the single layer's output kernel"
    B, T, D = x.shape
    G, HPG, DH = NSA_KV_GROUPS, NSA_HPG, NSA_HEAD_DIM
    N = B * T
    nch = T // CMP_STRIDE
    nc = nch - CMP_BLOCK // CMP_STRIDE + 1
    nb = T // SLC_BLOCK
    nbp = KSEL_FEAT - QFEAT
    assert nb <= nbp and T % NSA_TILE == 0
    n_sel = min(N_SELECT, nb)
    tk = NSA_TILE
    l = 0

    slopes = 2.0 ** (-8.0 * jnp.arange(1, NSA_HEADS + 1, dtype=F32) / NSA_HEADS)
    wimp = jnp.asarray(_importance_matrix(nbp, nch), BF16)
    tri = jnp.asarray(np.tril(np.ones((GLA_CHUNK, GLA_CHUNK), np.float32)), BF16)
    bp = ((B + 7) // 8) * 8
    c8 = jnp.pad(c, ((0, bp - B), (0, 0)))
    x2 = x.reshape(N, D)

    mod = _ada(c8, w_ada[l], b_ada[l][None, :])
    shift = mod[:B, :D].reshape(B, 1, D)
    scale = mod[:B, D:2 * D].reshape(B, 1, D)
    gate = mod[:B, 2 * D:].reshape(B, 1, D)

    w_in_t = jnp.transpose(w_in[l])
    pm, ps = _in_proj(x2, norm_gain[l][None, :], shift, scale,
                      _wprep(w_in_t), _main_cols(b_in[l])[None, :],
                      _wprep_small(w_in_t), _small_cols(b_in[l])[None, :], T=T)

    def cols(name, width=NSA_KV_WIDTH):
        return pm[:, _NEW[name]:_NEW[name] + width]

    kvf = cols("ck", 2 * NSA_KV_WIDTH).reshape(B, nch, CMP_STRIDE, 2, G, DH)
    kvf = kvf.transpose(3, 0, 4, 1, 2, 5).reshape(2, B, G, nch, CMP_STRIDE * DH)
    posf = jnp.stack([cmp_pos_k[l], cmp_pos_v[l]]).reshape(2, 1, CMP_BLOCK * DH)
    posf = jnp.broadcast_to(posf, (2, 8, CMP_BLOCK * DH)).astype(BF16)
    w1 = jnp.stack([cmp_w1_k[l], cmp_w1_v[l]]).astype(BF16)
    w2 = jnp.stack([cmp_w2_k[l], cmp_w2_v[l]]).astype(BF16)
    kvc = _compress(kvf, posf, w1, w2)
    gt = ps[:, :3 * NSA_HEADS].reshape(B, T, G, 3 * HPG).transpose(0, 2, 3, 1)
    slopes2 = slopes * LOG2E
    o_nsa = _nsa(slopes2, pm, kvc[0], kvc[1].transpose(0, 1, 3, 2), gt, wimp, B=B, T=T, nc=nc, n_sel=n_sel)

    wal = jnp.pad(gla_w_alpha[l], ((GA_LANE, SMALL_WIDTH - GA_LANE - GLA_GATE_RANK), (0, 0)))
    y_gla = _gla(pm, ps, wal, gla_b_alpha[l][None, :], gla_norm_gain[l][None, :], tri, B=B, T=T)

    out = _out_proj(x2, gate, o_nsa, pm, y_gla,
                    w_br_nsa[l].astype(BF16), w_br_gla[l].astype(BF16), w_out[l].astype(BF16),
                    final_norm_gain[None, :], T=T)
    return out.reshape(B, T, D)
```

```python
import functools

import numpy as np
import jax
import jax.numpy as jnp
from jax import lax
from jax.experimental import pallas as pl
from jax.experimental.pallas import tpu as pltpu

D_MODEL = 2048
DEPTH = 1
NSA_HEADS = 16
NSA_KV_GROUPS = 4
NSA_HPG = NSA_HEADS // NSA_KV_GROUPS
NSA_HEAD_DIM = 64
CMP_BLOCK = 32
CMP_STRIDE = 16
CMP_HIDDEN = 256
SLC_BLOCK = 64
N_SELECT = 16
WINDOW = 512
NSA_WIDTH = NSA_HEADS * NSA_HEAD_DIM
NSA_KV_WIDTH = NSA_KV_GROUPS * NSA_HEAD_DIM
GLA_HEADS = 4
GLA_KEY_DIM = 128
GLA_VAL_DIM = 256
GLA_GATE_RANK = 16
GLA_TAU = 16.0
GLA_KEY_WIDTH = GLA_HEADS * GLA_KEY_DIM
GLA_WIDTH = GLA_HEADS * GLA_VAL_DIM
EPS = 1e-6
NEG = -1e30

F32 = jnp.float32
BF16 = jnp.bfloat16

_OLD = {}
_off = 0
for _name, _w in (("nsa_q", NSA_WIDTH), ("ck", NSA_KV_WIDTH), ("cv", NSA_KV_WIDTH), ("sk", NSA_KV_WIDTH),
                  ("sv", NSA_KV_WIDTH), ("wk", NSA_KV_WIDTH), ("wv", NSA_KV_WIDTH), ("nsa_g", 3 * NSA_HEADS),
                  ("nsa_z", NSA_WIDTH), ("gq", GLA_KEY_WIDTH), ("gk", GLA_KEY_WIDTH), ("gv", GLA_WIDTH),
                  ("ga", GLA_GATE_RANK), ("gla_z", GLA_WIDTH), ("mg_nsa", D_MODEL), ("mg_gla", D_MODEL)):
    _OLD[_name] = (_off, _w)
    _off += _w

def _pair_by_group(a, b):
    return [(_OLD[nm][0] + g * NSA_HEAD_DIM, NSA_HEAD_DIM) for g in range(NSA_KV_GROUPS) for nm in (a, b)]


_MAIN_PIECES = [(nm, [_OLD[nm]]) for nm in
                ("nsa_q", "nsa_z", "gla_z", "gv", "mg_nsa", "mg_gla", "gq", "gk", "ck", "cv")]
_MAIN_PIECES += [("sel_win_k", _pair_by_group("sk", "wk")), ("sel_win_v", _pair_by_group("sv", "wv"))]
_NEW = {}
_off = 0
for _name, _pieces in _MAIN_PIECES:
    _NEW[_name] = _off
    _off += sum(w for _, w in _pieces)
MAIN_WIDTH = _off
SMALL_WIDTH = 128
GA_LANE = 3 * NSA_HEADS

LANE = 128
GLA_CHUNK = 64
GLA_SUB = 16
GLA_PAIR = 4
NSA_TILE = 256
QFEAT = 2 * NSA_HEAD_DIM
KSEL_FEAT = QFEAT + LANE
N_SLOPE_TERMS = 3
SOFTMAX_STRIP = NSA_TILE
SEL_MASK = 16384.0
BITS_PER_WORD = 16
LOG2E = 1.4426950408889634
Q_SCALE = LOG2E * NSA_HEAD_DIM ** -0.5


def _dot(a, b):
    return jnp.dot(a, b, preferred_element_type=F32)


def _dot_nt(a, b):
    return lax.dot_general(a, b, (((1,), (1,)), ((), ())), preferred_element_type=F32)


def _split3(x):
    hi = x.astype(BF16)
    r = x - hi.astype(F32)
    mid = r.astype(BF16)
    lo = (r - mid.astype(F32)).astype(BF16)
    return hi, mid, lo


def _sigmoid(x):
    return 1.0 / (1.0 + jnp.exp(-x))


def _ada_kernel(c_ref, w_ref, b_ref, o_ref):
    ch, cm, cl = _split3(c_ref[...])
    wh, wm, wl = _split3(w_ref[...])
    acc = _dot(ch, wh) + _dot(ch, wm) + _dot(cm, wh) + _dot(ch, wl) + _dot(cl, wh) + _dot(cm, wm)
    o_ref[...] = acc + b_ref[...]


def _ada(c8, w, b, *, tn=768):
    m, d = c8.shape
    n = w.shape[1]
    return pl.pallas_call(
        _ada_kernel,
        grid=(n // tn,),
        in_specs=[pl.BlockSpec((m, d), lambda j: (0, 0)),
                  pl.BlockSpec((d, tn), lambda j: (0, j)),
                  pl.BlockSpec((1, tn), lambda j: (0, j))],
        out_specs=pl.BlockSpec((m, tn), lambda j: (0, j)),
        out_shape=jax.ShapeDtypeStruct((m, n), F32),
        compiler_params=pltpu.CompilerParams(dimension_semantics=("arbitrary",),
                                             vmem_limit_bytes=40 * 1024 * 1024),
        name="ada",
    )(c8, w, b)


WPREP_TILE = 512


def _wprep_tiles():
    tiles = []
    for name, pieces in _MAIN_PIECES:
        scale = Q_SCALE if name == "nsa_q" else 1.0
        if len(pieces) == 1:
            start, width = pieces[0]
            assert width % WPREP_TILE == 0 or (name in ("ck", "cv") and width * 2 == WPREP_TILE)
            if name == "cv":
                assert start == _OLD["ck"][0] + _OLD["ck"][1]
                continue
            for off in range(0, max(width, WPREP_TILE), WPREP_TILE):
                tiles.append((start + off, (start + off) % LANE, scale))
        else:
            first, second = pieces[0][0], pieces[1][0]
            assert second == first + WPREP_TILE and first % LANE == 0 and len(pieces) * NSA_HEAD_DIM == WPREP_TILE
            tiles.append((first, -1, scale))
    return tiles


def _wprep_kernel(start_ref, kind_ref, scale_ref, a_ref, b_ref, o_ref):
    j = pl.program_id(0)
    scale = scale_ref[j]

    @pl.when(kind_ref[j] >= 0)
    def _():
        o_ref[...] = (a_ref[...] * scale).astype(o_ref.dtype)

    @pl.when(kind_ref[j] < 0)
    def _():
        half = NSA_HEAD_DIM
        for g in range(NSA_KV_GROUPS):
            o_ref[2 * g * half:(2 * g + 1) * half, :] = (a_ref[g * half:(g + 1) * half, :] * scale).astype(o_ref.dtype)
            o_ref[(2 * g + 1) * half:(2 * g + 2) * half, :] = (b_ref[g * half:(g + 1) * half, :] * scale).astype(o_ref.dtype)


def _wprep(wt):
    width, d = wt.shape
    tiles = _wprep_tiles()
    half_tile = WPREP_TILE // 2
    assert len(tiles) * WPREP_TILE == MAIN_WIDTH and all(t[0] % 8 == 0 for t in tiles)
    start = jnp.asarray([t[0] for t in tiles], jnp.int32)
    kind = jnp.asarray([t[1] for t in tiles], jnp.int32)
    scale = jnp.asarray([t[2] for t in tiles], F32)
    return pl.pallas_call(
        _wprep_kernel,
        grid_spec=pltpu.PrefetchScalarGridSpec(
            num_scalar_prefetch=2,
            grid=(len(tiles),),
            in_specs=[pl.BlockSpec(memory_space=pltpu.SMEM),
                      pl.BlockSpec((pl.Element(WPREP_TILE), pl.Element(d)),
                                   lambda j, st, kd: (pl.multiple_of(st[j], 8), 0)),
                      pl.BlockSpec((pl.Element(half_tile), pl.Element(d)),
                                   lambda j, st, kd: (pl.multiple_of(
                                       jnp.minimum(st[j] + WPREP_TILE, width - half_tile), 8), 0))],
            out_specs=pl.BlockSpec((WPREP_TILE, d), lambda j, st, kd: (j, 0))),
        out_shape=jax.ShapeDtypeStruct((MAIN_WIDTH, d), BF16),
        compiler_params=pltpu.CompilerParams(dimension_semantics=("arbitrary",),
                                             vmem_limit_bytes=40 * 1024 * 1024),
        name="wprep",
    )(start, kind, scale, wt, wt)


def _wprep_small_kernel(g_ref, a_ref, o_ref):
    n_gate = 3 * NSA_HEADS
    rows = jnp.concatenate([g_ref[0:n_gate, :], a_ref[...],
                            jnp.zeros((SMALL_WIDTH - n_gate - GLA_GATE_RANK, g_ref.shape[1]), F32)], axis=0)
    o_ref[...] = rows.astype(o_ref.dtype)


def _wprep_small(wt):
    d = wt.shape[1]
    g0, a0 = _OLD["nsa_g"][0], _OLD["ga"][0]
    assert g0 % 8 == 0 and a0 % 8 == 0
    return pl.pallas_call(
        _wprep_small_kernel,
        grid=(1,),
        in_specs=[pl.BlockSpec((pl.Element(NSA_HEAD_DIM), pl.Element(d)), lambda i: (g0, 0)),
                  pl.BlockSpec((pl.Element(GLA_GATE_RANK), pl.Element(d)), lambda i: (a0, 0))],
        out_specs=pl.BlockSpec((SMALL_WIDTH, d), lambda i: (0, 0)),
        out_shape=jax.ShapeDtypeStruct((SMALL_WIDTH, d), BF16),
        name="wprep_small",
    )(wt, wt)


def _in_proj_kernel(x_ref, gain_ref, shift_ref, scale_ref, w_ref, b_ref, ws_ref, bs_ref,
                    o_ref, os_ref, h_ref):
    j = pl.program_id(1)

    @pl.when(j == 0)
    def _():
        x = x_ref[...]
        ms = jnp.mean(x * x, axis=-1, keepdims=True)
        y = x * lax.rsqrt(ms + EPS) * gain_ref[...]
        y = y * (1.0 + scale_ref[0]) + shift_ref[0]
        h = y.astype(BF16)
        h_ref[...] = h
        os_ref[...] = _dot_nt(h, ws_ref[...]) + bs_ref[...]

    o_ref[...] = (_dot_nt(h_ref[...], w_ref[...]) + b_ref[...]).astype(o_ref.dtype)


def _in_proj(x2, gain, shift, scale, w_main, b_main, w_small, b_small, *, T, tm=1024, tn=1792):
    n, d = x2.shape
    nm = w_main.shape[0]
    nt = T // tm
    return pl.pallas_call(
        _in_proj_kernel,
        grid=(n // tm, nm // tn),
        in_specs=[pl.BlockSpec((tm, d), lambda i, j: (i, 0)),
                  pl.BlockSpec((1, d), lambda i, j: (0, 0)),
                  pl.BlockSpec((1, 1, d), lambda i, j: (i // nt, 0, 0)),
                  pl.BlockSpec((1, 1, d), lambda i, j: (i // nt, 0, 0)),
                  pl.BlockSpec((tn, d), lambda i, j: (j, 0)),
                  pl.BlockSpec((1, tn), lambda i, j: (0, j)),
                  pl.BlockSpec((SMALL_WIDTH, d), lambda i, j: (0, 0)),
                  pl.BlockSpec((1, SMALL_WIDTH), lambda i, j: (0, 0))],
        out_specs=[pl.BlockSpec((tm, tn), lambda i, j: (i, j)),
                   pl.BlockSpec((tm, SMALL_WIDTH), lambda i, j: (i, 0))],
        out_shape=[jax.ShapeDtypeStruct((n, nm), BF16),
                   jax.ShapeDtypeStruct((n, SMALL_WIDTH), F32)],
        scratch_shapes=[pltpu.VMEM((tm, d), BF16)],
        compiler_params=pltpu.CompilerParams(dimension_semantics=("parallel", "arbitrary"),
                                             vmem_limit_bytes=56 * 1024 * 1024),
        name="in_proj",
    )(x2, gain, shift, scale, w_main, b_main, w_small, b_small)


def _compress_kernel(kv_ref, pos_ref, w1_ref, w2_ref, o_ref):
    y = kv_ref[0, 0, 0]
    w1 = w1_ref[0]
    half = y.shape[1]
    nch = y.shape[0]
    z1 = _dot(y, w1[:half])
    z2 = _dot(y, w1[half:])
    posb = _dot(pos_ref[0], w1)[0:1]
    pre = z1 + pltpu.roll(z2, nch - 1, 0) + posb
    hid = pre * _sigmoid(pre)
    o_ref[0, 0, 0] = _dot(hid.astype(BF16), w2_ref[0]).astype(o_ref.dtype)


def _compress(kvf, posf, w1, w2):
    two, b, g, nch, half = kvf.shape
    hidden = w1.shape[2]
    dh = w2.shape[2]
    return pl.pallas_call(
        _compress_kernel,
        grid=(two, b, g),
        in_specs=[pl.BlockSpec((1, 1, 1, nch, half), lambda s, i, j: (s, i, j, 0, 0)),
                  pl.BlockSpec((1, 8, 2 * half), lambda s, i, j: (s, 0, 0)),
                  pl.BlockSpec((1, 2 * half, hidden), lambda s, i, j: (s, 0, 0)),
                  pl.BlockSpec((1, hidden, dh), lambda s, i, j: (s, 0, 0))],
        out_specs=pl.BlockSpec((1, 1, 1, nch, dh), lambda s, i, j: (s, i, j, 0, 0)),
        out_shape=jax.ShapeDtypeStruct((two, b, g, nch, dh), BF16),
        compiler_params=pltpu.CompilerParams(dimension_semantics=("parallel", "parallel", "parallel")),
        name="compress",
    )(kvf, posf, w1, w2)


def _select_blocks(imp, t_row, n_sel, three_forced):
    nbp = imp.shape[0]
    blk = lax.broadcasted_iota(jnp.int32, (nbp, 1), 0)
    cur = lax.shift_right_logical(t_row, 6)
    forced = (blk == 0) | (blk == cur) | (blk == cur - 1)
    bvalid = blk * SLC_BLOCK <= t_row
    rest = jnp.where(bvalid, imp, -1.0)
    blk_f = blk.astype(F32)
    if three_forced:
        score = jnp.where(forced, -1.0, rest)
        sel = jnp.where(forced, 1.0, 0.0)
        rounds = n_sel - 3
    else:
        score = jnp.where(forced, 1e30, rest)
        sel = jnp.zeros(imp.shape, F32)
        rounds = n_sel
    for _ in range(rounds):
        m = jnp.max(score, axis=0, keepdims=True)
        first = jnp.min(jnp.where(score == m, blk_f, float(nbp)), axis=0, keepdims=True)
        hit = blk_f == first
        sel = jnp.where(hit, 1.0, sel)
        score = jnp.where(hit, -1.0, score)
    return sel


def _softmax_step(s_of, vt, d, m_sc, l_sc, acc_sc, s_max=None, edge=None):
    half = SOFTMAX_STRIP // 2
    if edge is not None:
        r_i = lax.broadcasted_iota(jnp.int32, (half, half), 0)
        c_i = lax.broadcasted_iota(jnp.int32, (half, half), 1)
        tri = (r_i <= c_i) if edge == "diag" else (r_i > c_i)
    for c0 in range(0, d.shape[1], SOFTMAX_STRIP):
        ls = slice(c0, c0 + SOFTMAX_STRIP)
        s = s_of(ls)
        dd = d[:, ls]
        m_prev = m_sc[:, ls]
        if edge is None:
            mx = jnp.max(s, axis=0, keepdims=True) if s_max is None else s_max[:, ls]
            m_new = jnp.maximum(m_prev, mx - dd)
            p = jnp.exp2(s - (m_new + dd))
            p_sum = jnp.sum(p, axis=0, keepdims=True)
        else:
            top_left = jnp.where(tri, s[:half, :half], NEG)
            bot_right = jnp.where(tri, s[half:, half:], NEG)
            full = s[:half, half:] if edge == "diag" else s[half:, :half]
            mx_tl = jnp.max(top_left, axis=0, keepdims=True)
            mx_br = jnp.max(bot_right, axis=0, keepdims=True)
            mx_full = jnp.max(full, axis=0, keepdims=True)
            if edge == "diag":
                mx = jnp.concatenate([mx_tl, jnp.maximum(mx_full, mx_br)], axis=1)
            else:
                mx = jnp.concatenate([jnp.maximum(mx_tl, mx_full), mx_br], axis=1)
            m_new = jnp.maximum(m_prev, mx - dd)
            shift = m_new + dd
            p_tl = jnp.exp2(top_left - shift[:, :half])
            p_br = jnp.exp2(bot_right - shift[:, half:])
            dead = jnp.zeros((half, half), F32)
            if edge == "diag":
                p_full = jnp.exp2(full - shift[:, half:])
                p = jnp.concatenate([jnp.concatenate([p_tl, p_full], axis=1),
                                     jnp.concatenate([dead, p_br], axis=1)], axis=0)
                p_sum = jnp.concatenate([jnp.sum(p_tl, axis=0, keepdims=True),
                                         jnp.sum(p_full, axis=0, keepdims=True)
                                         + jnp.sum(p_br, axis=0, keepdims=True)], axis=1)
            else:
                p_full = jnp.exp2(full - shift[:, :half])
                p = jnp.concatenate([jnp.concatenate([p_tl, dead], axis=1),
                                     jnp.concatenate([p_full, p_br], axis=1)], axis=0)
                p_sum = jnp.concatenate([jnp.sum(p_tl, axis=0, keepdims=True)
                                         + jnp.sum(p_full, axis=0, keepdims=True),
                                         jnp.sum(p_br, axis=0, keepdims=True)], axis=1)
        alpha = jnp.exp2(m_prev - m_new)
        l_sc[:, ls] = alpha * l_sc[:, ls] + p_sum
        acc_sc[:, ls] = alpha * acc_sc[:, ls] + _dot(vt, p.astype(BF16))
        m_sc[:, ls] = m_new


def _stage_group(qslab_ref, kslab_ref, vslab_ref, qt_sc, ks_sc, kw_sc, vt_sc, nt, tk):
    dh = NSA_HEAD_DIM
    lane = lax.broadcasted_iota(jnp.int32, (tk, LANE), 1)
    row = lax.broadcasted_iota(jnp.int32, (tk, LANE), 0)
    pos = row.astype(F32)
    feat_sel = jnp.where((lane >= dh) & (lane < dh + N_SLOPE_TERMS), pos, 0.0)
    feat_win = jnp.where(lane < N_SLOPE_TERMS, pos, 0.0)
    blk_in_tile = lax.shift_right_logical(row, 6)

    def body(kt, carry):
        start = pl.multiple_of(kt * tk, tk)
        kk = kslab_ref[pl.ds(start, tk), :].astype(F32)
        onehot = jnp.where(kt * (tk // SLC_BLOCK) + blk_in_tile == lane, 1.0, 0.0)
        ks_sc[kt] = jnp.concatenate([jnp.where(lane < dh, kk, feat_sel), onehot], axis=1).astype(BF16)
        kw_sc[kt] = jnp.where(lane >= dh, kk, feat_win).astype(BF16)
        vt_sc[kt] = vslab_ref[pl.ds(start, tk), :].astype(F32).T.astype(BF16)
        x_t = qslab_ref[pl.ds(start, tk), :].astype(F32).T
        qt_sc[kt] = jnp.concatenate([x_t[hh * dh:(hh + 1) * dh] for hh in range(NSA_HPG)], axis=1).astype(BF16)
        return carry

    lax.fori_loop(0, nt, body, 0)


def _nsa_kernel(slopes_ref, q_ref, kslab_ref, vslab_ref, kc_ref, vct_ref, gt_ref, wimp_ref,
                o_ref, q2_sc, qw_sc, ks_sc, kw_sc, qt_sc, vt_sc, sa_sc, sb_sc, m_sc, l_sc, acc_sc, br_sc, list_sc,
                words_sc, *, tq, nt, nc, n_sel):
    g = pl.program_id(1)
    qi = pl.program_id(2)
    tk = tq
    dh = NSA_HEAD_DIM
    hq = NSA_HPG * tq

    @pl.when(qi == 0)
    def _():
        _stage_group(q_ref, kslab_ref, vslab_ref, qt_sc, ks_sc, kw_sc, vt_sc, nt, tk)

    lane = lax.broadcasted_iota(jnp.int32, (1, hq), 1)
    slope_row = jnp.zeros((1, hq), F32)
    for hh in range(NSA_HPG):
        slope_row = jnp.where(lane >= hh * tq, slopes_ref[g * NSA_HPG + hh], slope_row)
    t_one = qi * tq + lax.broadcasted_iota(jnp.int32, (1, tq), 1)
    t_row = jnp.concatenate([t_one] * NSA_HPG, axis=1)
    t_f = t_row.astype(F32)

    q_t = qt_sc[qi]
    feat_row = lax.broadcasted_iota(jnp.int32, (dh, hq), 0)
    terms = [t.astype(F32) for t in _split3(slope_row)]
    feat = jnp.zeros((dh, hq), F32)
    for i, term in enumerate(terms):
        feat = jnp.where(feat_row == i, term, feat)
    feat = feat.astype(BF16)
    q2_sc[0:dh, :] = q_t
    q2_sc[dh:QFEAT, :] = feat
    qw_sc[0:dh, :] = feat
    qw_sc[dh:QFEAT, :] = q_t

    def reset_stats():
        m_sc[...] = jnp.full(m_sc.shape, NEG, F32)
        l_sc[...] = jnp.zeros(l_sc.shape, F32)
        acc_sc[...] = jnp.zeros(acc_sc.shape, F32)

    def per_head(a):
        return [a[:, hh * tq:(hh + 1) * tq] for hh in range(NSA_HPG)]

    def shift_of(kv, off=None):
        d = slope_row * (t_f - (kv * tk).astype(F32))
        return d if off is None else d + jnp.where(off, -NEG, 0.0)

    per_tile = tk // SLC_BLOCK
    nbp = KSEL_FEAT - QFEAT
    n_words = -(-nt * per_tile // BITS_PER_WORD)

    def front(nk, nblk, three_forced):
        kc = kc_ref[0, 0, 0:nk, :]
        n_col = lax.broadcasted_iota(jnp.int32, (nk, 1), 0)
        ce = jnp.where(n_col < nc, n_col * CMP_STRIDE + (CMP_BLOCK - 1), 2 ** 30)
        ce_rel = (ce - qi * tq).astype(F32)
        s = jnp.where(ce <= t_row, _dot(kc, q_t) + slope_row * ce_rel, NEG)
        m = jnp.max(s, axis=0, keepdims=True)
        p = jnp.exp2(s - m)
        l = jnp.sum(p, axis=0, keepdims=True)
        p = p * jnp.where(t_row >= CMP_BLOCK - 1, 1.0 / l, 0.0)
        br_sc[0] = _dot(vct_ref[0, 0, :, 0:nk], p.astype(BF16))
        ps4 = per_head(p)
        psum = (ps4[0] + ps4[1]) + (ps4[2] + ps4[3])
        wimp = wimp_ref[0:nblk, 0:nk]
        ph, pm, plo = _split3(psum)
        imp = _dot(wimp, ph) + _dot(wimp, pm) + _dot(wimp, plo)

        reset_stats()
        n_back = WINDOW // tk
        qw = qw_sc[...]
        tiles = [jnp.maximum(qi - w, 0) for w in range(n_back + 1)]
        logits = [_dot(kw_sc[kv], qw) for kv in tiles]
        assert WINDOW == n_back * tk and tk == SOFTMAX_STRIP
        for w, (kv, s) in enumerate(zip(tiles, logits)):
            edge = "diag" if w == 0 else ("far" if w == n_back else None)
            _softmax_step(lambda ls, s=s: s[:, ls], vt_sc[kv, dh:2 * dh, :], shift_of(kv, qi < w),
                          m_sc, l_sc, acc_sc, edge=edge)
        br_sc[1] = acc_sc[...] * (1.0 / l_sc[...])

        s0 = _dot(ks_sc[0, :, 0:QFEAT], q2_sc[0:QFEAT, :])

        sel = _select_blocks(imp, t_one, n_sel, three_forced)
        penalty_f = (sel - 1.0) * SEL_MASK
        q2_sc[QFEAT:QFEAT + nblk, :] = jnp.concatenate([penalty_f.astype(BF16)] * NSA_HPG, axis=1)
        if nblk < nbp:
            q2_sc[QFEAT + nblk:KSEL_FEAT, :] = jnp.full((nbp - nblk, hq), -SEL_MASK, BF16)
        pen0 = jnp.concatenate([penalty_f[0:per_tile]] * NSA_HPG, axis=1)
        pen0 = jnp.concatenate([jnp.broadcast_to(pen0[b:b + 1], (SLC_BLOCK, hq)) for b in range(per_tile)], axis=0)
        s0 = s0 + pen0
        sa_sc[0:tk, :] = s0
        sa_sc[tk:tk + 1, :] = jnp.max(s0, axis=0, keepdims=True)

        any_tok = jnp.max(sel, axis=1, keepdims=True)
        blk_col = lax.broadcasted_iota(jnp.int32, any_tok.shape, 0)
        bit = jnp.left_shift(1, jnp.bitwise_and(blk_col, BITS_PER_WORD - 1)).astype(F32)
        bits = jnp.where(any_tok > 0.5, bit, 0.0)
        for w in range(n_words):
            lo, hi = w * BITS_PER_WORD, min((w + 1) * BITS_PER_WORD, nblk)
            words_sc[w] = jnp.sum(bits[lo:hi]).astype(jnp.int32) if lo < nblk else jnp.int32(0)

    n_var = 4
    span = nt // n_var
    ncp = kc_ref.shape[2]
    first_three = -(-2 * SLC_BLOCK // tq)
    assert first_three <= span and n_sel >= 3
    for v in range(1, n_var + 1):
        lo = (v - 1) * span
        if lo < first_three:
            @pl.when(qi < first_three)
            def _(v=v):
                front(ncp * v // n_var, (nt * per_tile) * v // n_var, False)
            lo = first_three

        @pl.when((qi >= lo) & (qi < v * span))
        def _(v=v):
            front(ncp * v // n_var, (nt * per_tile) * v // n_var, True)

    words = [words_sc[w] for w in range(n_words)]
    for kt in range(nt + 2):
        list_sc[kt] = qi
    n_act = jnp.int32(0)
    tile_mask = (1 << per_tile) - 1
    for kt in range(nt):
        list_sc[n_act] = kt
        w, sh = divmod(kt * per_tile, BITS_PER_WORD)
        hit = jnp.bitwise_and(words[w] >> sh, tile_mask) != 0
        n_act = n_act + hit.astype(jnp.int32)
    list_sc[n_act] = qi
    n_plain = n_act - 1

    reset_stats()

    def sel_logits(idx, s_ref):
        s = _dot(ks_sc[list_sc[idx]], q2_sc[...])
        s_ref[0:tk, :] = s
        s_ref[tk:tk + 1, :] = jnp.max(s, axis=0, keepdims=True)

    def sel_update(idx, s_ref):
        kv = list_sc[idx]
        _softmax_step(lambda ls: s_ref[0:tk, ls], vt_sc[kv, 0:dh, :], shift_of(kv),
                      m_sc, l_sc, acc_sc, s_max=s_ref[tk:tk + 1, :])

    def diag_update(s_ref):
        _softmax_step(lambda ls: s_ref[0:tk, ls], vt_sc[qi, 0:dh, :], shift_of(qi), m_sc, l_sc, acc_sc, edge="diag")

    def pair_body(i, carry):
        sel_logits(2 * i + 1, sb_sc)
        sel_update(2 * i, sa_sc)
        sel_logits(2 * i + 2, sa_sc)
        sel_update(2 * i + 1, sb_sc)
        return carry

    lax.fori_loop(0, lax.shift_right_logical(n_plain, 1), pair_body, 0)
    odd = jnp.bitwise_and(n_plain, 1) == 1

    @pl.when(jnp.logical_not(odd))
    def _():
        diag_update(sa_sc)

    @pl.when(odd)
    def _():
        sel_logits(n_plain, sb_sc)
        sel_update(n_plain - 1, sa_sc)
        diag_update(sb_sc)

    o_sel = acc_sc[...] * (1.0 / l_sc[...])

    sg = _sigmoid(gt_ref[0, 0])
    gates = [jnp.concatenate([sg[3 * hh + br:3 * hh + br + 1] for hh in range(NSA_HPG)], axis=1)
             for br in range(3)]
    o_all = gates[0] * br_sc[0] + gates[1] * o_sel + gates[2] * br_sc[1]
    o_ref[...] = jnp.concatenate(per_head(o_all), axis=0).T.astype(o_ref.dtype)


def _nsa(slopes, pm, kc, vct, gt, wimp, *, B, T, nc, n_sel):
    b, g, hpg, dh = B, NSA_KV_GROUPS, NSA_HPG, NSA_HEAD_DIM
    tq = NSA_TILE
    nt = T // tq
    hq = hpg * tq
    ncp = kc.shape[2]
    nbp = wimp.shape[0]
    qb = _NEW["nsa_q"] // (hpg * dh)
    kb, vb = _NEW["sel_win_k"] // LANE, _NEW["sel_win_v"] // LANE
    kern = functools.partial(_nsa_kernel, tq=tq, nt=nt, nc=nc, n_sel=n_sel)
    return pl.pallas_call(
        kern,
        grid=(b, g, nt),
        in_specs=[pl.BlockSpec(memory_space=pltpu.SMEM),
                  pl.BlockSpec((T, hpg * dh), lambda i, j, k: (i, qb + j)),
                  pl.BlockSpec((T, LANE), lambda i, j, k: (i, kb + j)),
                  pl.BlockSpec((T, LANE), lambda i, j, k: (i, vb + j)),
                  pl.BlockSpec((1, 1, ncp, dh), lambda i, j, k: (i, j, 0, 0)),
                  pl.BlockSpec((1, 1, dh, ncp), lambda i, j, k: (i, j, 0, 0)),
                  pl.BlockSpec((1, 1, 3 * hpg, tq), lambda i, j, k: (i, j, 0, k)),
                  pl.BlockSpec((nbp, ncp), lambda i, j, k: (0, 0))],
        out_specs=pl.BlockSpec((tq, hpg * dh), lambda i, j, k: (i * nt + k, j)),
        out_shape=jax.ShapeDtypeStruct((b * T, g * hpg * dh), BF16),
        scratch_shapes=[pltpu.VMEM((KSEL_FEAT, hq), BF16), pltpu.VMEM((QFEAT, hq), BF16),
                        pltpu.VMEM((nt, tq, KSEL_FEAT), BF16), pltpu.VMEM((nt, tq, QFEAT), BF16),
                        pltpu.VMEM((nt, dh, hq), BF16), pltpu.VMEM((nt, 2 * dh, tq), BF16),
                        pltpu.VMEM((tq + 8, hq), F32), pltpu.VMEM((tq + 8, hq), F32),
                        pltpu.VMEM((1, hq), F32), pltpu.VMEM((1, hq), F32),
                        pltpu.VMEM((dh, hq), F32),
                        pltpu.VMEM((2, dh, hq), F32),
                        pltpu.SMEM((nt + 2,), jnp.int32),
                        pltpu.SMEM((-(-nt * (tq // SLC_BLOCK) // BITS_PER_WORD),), jnp.int32)],
        compiler_params=pltpu.CompilerParams(dimension_semantics=("parallel", "parallel", "arbitrary"),
                                             vmem_limit_bytes=48 * 1024 * 1024),
        name="nsa",
    )(slopes, pm, pm, pm, kc, vct, gt, wimp)


def _gla_intra_scores(q, k, bcum):
    c, dk = q.shape
    sb = GLA_SUB
    row_c = lax.broadcasted_iota(jnp.int32, (c, dk), 0)
    row_s = lax.broadcasted_iota(jnp.int32, (sb, 1), 0)
    lane_s = lax.broadcasted_iota(jnp.int32, (sb, c), 1)
    blocks = []
    for i0 in range(0, c, sb):
        bi = bcum[i0:i0 + sb]
        qi = q[i0:i0 + sb]
        a_blk = jnp.zeros((sb, c), F32)
        for jl in range(sb):
            j = i0 + jl
            w = jnp.exp2(bi - bcum[j:j + 1])
            a = jnp.sum(qi * k[j:j + 1] * w, axis=-1, keepdims=True)
            a_blk = jnp.where(lane_s == j, jnp.where(row_s >= jl, a, 0.0), a_blk)
        if i0 > 0:
            r = bcum[i0 - 1:i0]
            qh = qi * jnp.exp2(bi - r)
            kh = k * jnp.exp2(jnp.where(row_c < i0, r - bcum, NEG))
            a_blk = a_blk + _dot_nt(qh.astype(BF16), kh.astype(BF16))
        blocks.append(a_blk)
    return jnp.concatenate(blocks, axis=0)


def _gla_kernel(q_ref, k_ref, v_ref, z_ref, ps_ref, wal_ref, bal_ref, gain_ref, tri_ref, o_ref,
                st_sc, *, ts):
    ti = pl.program_id(2)
    c = GLA_CHUNK
    dk, dv = GLA_KEY_DIM, GLA_VAL_DIM

    @pl.when(ti == 0)
    def _():
        st_sc[...] = jnp.zeros(st_sc.shape, F32)

    ps = ps_ref[...]
    ph, pm, _ = _split3(ps)
    wh, wm, _ = _split3(wal_ref[...])
    zz = _dot(ph, wh) + _dot(ph, wm) + _dot(pm, wh) + bal_ref[...]
    log_a = (jnp.minimum(zz, 0.0) - jnp.log(1.0 + jnp.exp(-jnp.abs(zz)))) * (LOG2E / GLA_TAU)
    tri = tri_ref[...]
    qscale = GLA_KEY_DIM ** -0.5
    for ci in range(ts // c):
        sl = slice(ci * c, (ci + 1) * c)
        for hh in range(GLA_PAIR):
            ks, vs = slice(hh * dk, (hh + 1) * dk), slice(hh * dv, (hh + 1) * dv)
            gh, gm, gl = _split3(log_a[sl, ks])
            bcum = _dot(tri, gh) + _dot(tri, gm) + _dot(tri, gl)
            q = q_ref[sl, ks].astype(F32) * qscale
            k = k_ref[sl, ks].astype(F32)
            v = v_ref[sl, vs]
            a_mat = _gla_intra_scores(q, k, bcum)
            st = st_sc[hh]
            o = _dot(a_mat.astype(BF16), v) + _dot_nt((q * jnp.exp2(bcum)).astype(BF16), st.astype(BF16))
            b_last = bcum[c - 1:c, :]
            kd = k * jnp.exp2(b_last - bcum)
            st_sc[hh] = st * jnp.exp2(b_last) + _dot(v.astype(F32).T.astype(BF16), kd.astype(BF16))
            ms = jnp.mean(o * o, axis=-1, keepdims=True)
            y = o * lax.rsqrt(ms + EPS) * gain_ref[...]
            z = z_ref[sl, vs].astype(F32)
            o_ref[sl, vs] = (y * (z * _sigmoid(z))).astype(o_ref.dtype)


def _gla(pm, ps, wal, bal, gain, tri, *, B, T, ts=1024):
    n = pm.shape[0]
    nts = T // ts
    hp = GLA_PAIR
    dk, dv = hp * GLA_KEY_DIM, hp * GLA_VAL_DIM
    qb, kb = _NEW["gq"] // dk, _NEW["gk"] // dk
    vb, zb = _NEW["gv"] // dv, _NEW["gla_z"] // dv
    kern = functools.partial(_gla_kernel, ts=ts)
    return pl.pallas_call(
        kern,
        grid=(B, GLA_HEADS // hp, nts),
        in_specs=[pl.BlockSpec((ts, dk), lambda b, h, i: (b * nts + i, qb + h)),
                  pl.BlockSpec((ts, dk), lambda b, h, i: (b * nts + i, kb + h)),
                  pl.BlockSpec((ts, dv), lambda b, h, i: (b * nts + i, vb + h)),
                  pl.BlockSpec((ts, dv), lambda b, h, i: (b * nts + i, zb + h)),
                  pl.BlockSpec((ts, SMALL_WIDTH), lambda b, h, i: (b * nts + i, 0)),
                  pl.BlockSpec((SMALL_WIDTH, dk), lambda b, h, i: (0, h)),
                  pl.BlockSpec((1, dk), lambda b, h, i: (0, h)),
                  pl.BlockSpec((1, GLA_VAL_DIM), lambda b, h, i: (0, 0)),
                  pl.BlockSpec((GLA_CHUNK, GLA_CHUNK), lambda b, h, i: (0, 0))],
        out_specs=pl.BlockSpec((ts, dv), lambda b, h, i: (b * nts + i, h)),
        out_shape=jax.ShapeDtypeStruct((n, GLA_WIDTH), BF16),
        scratch_shapes=[pltpu.VMEM((hp, GLA_VAL_DIM, GLA_KEY_DIM), F32)],
        compiler_params=pltpu.CompilerParams(dimension_semantics=("parallel", "parallel", "arbitrary"),
                                             vmem_limit_bytes=40 * 1024 * 1024),
        name="gla",
    )(pm, pm, pm, pm, ps, wal, bal, gain, tri)


def _out_proj_kernel(x_ref, gate_ref, on_ref, nz_ref, mgn_ref, mgg_ref, yg_ref,
                     wn_ref, wg_ref, wo_ref, fg_ref, o_ref):
    nz = nz_ref[...].astype(F32)
    y_nsa = on_ref[...].astype(F32) * (nz * _sigmoid(nz))
    a = _dot(y_nsa.astype(BF16), wn_ref[...])
    bm = _dot(yg_ref[...], wg_ref[...])
    merged = _sigmoid(mgn_ref[...].astype(F32)) * a + _sigmoid(mgg_ref[...].astype(F32)) * bm
    xn = x_ref[...] + gate_ref[0] * _dot(merged.astype(BF16), wo_ref[...])
    ms = jnp.mean(xn * xn, axis=-1, keepdims=True)
    o_ref[...] = xn * lax.rsqrt(ms + EPS) * fg_ref[...]


def _out_proj(x2, gate, o_nsa, pm, yg, wn, wg, wo, fg, *, T, tm=512):
    n, d = x2.shape
    nt = T // tm
    nzb = _NEW["nsa_z"] // NSA_WIDTH
    mnb = _NEW["mg_nsa"] // D_MODEL
    mgb = _NEW["mg_gla"] // D_MODEL
    row = lambda i: (i, 0)
    const2 = lambda i: (0, 0)
    resident = dict(pipeline_mode=pl.Buffered(1))
    return pl.pallas_call(
        _out_proj_kernel,
        grid=(n // tm,),
        in_specs=[pl.BlockSpec((tm, d), row),
                  pl.BlockSpec((1, 1, d), lambda i: (i // nt, 0, 0)),
                  pl.BlockSpec((tm, NSA_WIDTH), row),
                  pl.BlockSpec((tm, NSA_WIDTH), lambda i: (i, nzb)),
                  pl.BlockSpec((tm, D_MODEL), lambda i: (i, mnb)),
                  pl.BlockSpec((tm, D_MODEL), lambda i: (i, mgb)),
                  pl.BlockSpec((tm, GLA_WIDTH), row),
                  pl.BlockSpec((NSA_WIDTH, d), const2, **resident),
                  pl.BlockSpec((GLA_WIDTH, d), const2, **resident),
                  pl.BlockSpec((d, d), const2, **resident),
                  pl.BlockSpec((1, d), const2)],
        out_specs=pl.BlockSpec((tm, d), row),
        out_shape=jax.ShapeDtypeStruct((n, d), F32),
        compiler_params=pltpu.CompilerParams(dimension_semantics=("parallel",),
                                             vmem_limit_bytes=60 * 1024 * 1024),
        name="out_proj",
    )(x2, gate, o_nsa, pm, pm, pm, yg, wn, wg, wo, fg)


def _importance_matrix(nbp, ncp):
    ratio = SLC_BLOCK // CMP_STRIDE
    n_sub = CMP_BLOCK // CMP_STRIDE
    w = np.zeros((nbp, ncp), np.float32)
    for blk in range(nbp):
        for m in range(ratio):
            for s in range(n_sub):
                n = ratio * blk + m - s
                if 0 <= n < ncp:
                    w[blk, n] += 1.0
    return w


def _main_cols(a):
    parts = [a[..., start:start + width] * (Q_SCALE if nm == "nsa_q" else 1.0)
             for nm, pieces in _MAIN_PIECES for start, width in pieces]
    return jnp.concatenate(parts, axis=-1)


def _small_cols(a):
    pad = SMALL_WIDTH - 3 * NSA_HEADS - GLA_GATE_RANK
    parts = [a[..., _OLD["nsa_g"][0]:_OLD["nsa_g"][0] + 3 * NSA_HEADS],
             a[..., _OLD["ga"][0]:_OLD["ga"][0] + GLA_GATE_RANK],
             jnp.zeros(a.shape[:-1] + (pad,), a.dtype)]
    return jnp.concatenate(parts, axis=-1)


def kernel(x, c, w_ada, b_ada, norm_gain, w_in, b_in, cmp_pos_k, cmp_pos_v, cmp_w1_k, cmp_w2_k, cmp_w1_v, cmp_w2_v,
           gla_w_alpha, gla_b_alpha, gla_norm_gain, w_br_nsa, w_br_gla, w_out, final_norm_gain):
    assert DEPTH == 1, "the final rmsnorm is fused into the single layer's output kernel"
    B, T, D = x.shape
    G, HPG, DH = NSA_KV_GROUPS, NSA_HPG, NSA_HEAD_DIM
    N = B * T
    nch = T // CMP_STRIDE
    nc = nch - CMP_BLOCK // CMP_STRIDE + 1
    nb = T // SLC_BLOCK
    nbp = KSEL_FEAT - QFEAT
    assert nb <= nbp and T % NSA_TILE == 0
    n_sel = min(N_SELECT, nb)
    tk = NSA_TILE
    l = 0

    slopes = 2.0 ** (-8.0 * jnp.arange(1, NSA_HEADS + 1, dtype=F32) / NSA_HEADS)
    wimp = jnp.asarray(_importance_matrix(nbp, nch), BF16)
    tri = jnp.asarray(np.tril(np.ones((GLA_CHUNK, GLA_CHUNK), np.float32)), BF16)
    bp = ((B + 7) // 8) * 8
    c8 = jnp.pad(c, ((0, bp - B), (0, 0)))
    x2 = x.reshape(N, D)

    mod = _ada(c8, w_ada[l], b_ada[l][None, :])
    shift = mod[:B, :D].reshape(B, 1, D)
    scale = mod[:B, D:2 * D].reshape(B, 1, D)
    gate = mod[:B, 2 * D:].reshape(B, 1, D)

    w_in_t = jnp.transpose(w_in[l])
    pm, ps = _in_proj(x2, norm_gain[l][None, :], shift, scale,
                      _wprep(w_in_t), _main_cols(b_in[l])[None, :],
                      _wprep_small(w_in_t), _small_cols(b_in[l])[None, :], T=T)

    def cols(name, width=NSA_KV_WIDTH):
        return pm[:, _NEW[name]:_NEW[name] + width]

    kvf = cols("ck", 2 * NSA_KV_WIDTH).reshape(B, nch, CMP_STRIDE, 2, G, DH)
    kvf = kvf.transpose(3, 0, 4, 1, 2, 5).reshape(2, B, G, nch, CMP_STRIDE * DH)
    posf = jnp.stack([cmp_pos_k[l], cmp_pos_v[l]]).reshape(2, 1, CMP_BLOCK * DH)
    posf = jnp.broadcast_to(posf, (2, 8, CMP_BLOCK * DH)).astype(BF16)
    w1 = jnp.stack([cmp_w1_k[l], cmp_w1_v[l]]).astype(BF16)
    w2 = jnp.stack([cmp_w2_k[l], cmp_w2_v[l]]).astype(BF16)
    kvc = _compress(kvf, posf, w1, w2)
    gt = ps[:, :3 * NSA_HEADS].reshape(B, T, G, 3 * HPG).transpose(0, 2, 3, 1)
    slopes2 = slopes * LOG2E
    o_nsa = _nsa(slopes2, pm, kvc[0], kvc[1].transpose(0, 1, 3, 2), gt, wimp, B=B, T=T, nc=nc, n_sel=n_sel)

    wal = jnp.pad(gla_w_alpha[l], ((GA_LANE, SMALL_WIDTH - GA_LANE - GLA_GATE_RANK), (0, 0)))
    y_gla = _gla(pm, ps, wal, gla_b_alpha[l][None, :], gla_norm_gain[l][None, :], tri, B=B, T=T)

    out = _out_proj(x2, gate, o_nsa, pm, y_gla,
                    w_br_nsa[l].astype(BF16), w_br_gla[l].astype(BF16), w_out[l].astype(BF16),
                    final_norm_gain[None, :], T=T)
    return out.reshape(B, T, D)
```
